```python
import math
import jax, jax.numpy as jnp
from jax import lax
import numpy as np

D_MODEL = 2048
BATCH = 2
SEQ = 8192
DEPTH = 1

GRID_W = 64
CTX_LEN = 256
D_MIX = D_MODEL
D_S5 = D_MIX // 2
S5_GROUP = 16
S5_GROUPS = D_S5 // S5_GROUP
S5_STATE = 64
S5_CHUNK = 128
D_ML = D_MIX - D_S5
ML_HEADS = 4
ML_HEAD_DIM = D_ML // ML_HEADS
ML_CHUNK = 64
CONV_K = 5
D_IN = D_S5 + 4 * D_ML + 4 * ML_HEADS
N_EXPERTS = 16
EC_FACTOR = 2
EXPERT_FF = 2048
N_MOD = 6
EPS = 1e-6

kernel_name = "hymba_s5_mlstm_ec_flow_block"

F32 = jnp.float32


def rms_norm(x, g):
    xf = x.astype(F32)
    y = xf * lax.rsqrt(jnp.mean(xf * xf, axis=-1, keepdims=True) + EPS)
    return (y * g.astype(F32)).astype(x.dtype)


def _modulation(cond, ada_w, ada_b):
    m = jax.nn.silu(cond) @ ada_w + ada_b
    return jnp.split(m[..., None, :], N_MOD, axis=-1)


def centred_dwconv(u, w, b):
    ch = u.shape[-1]
    out = lax.conv_general_dilated(
        u, w[:, None, :].astype(u.dtype), window_strides=(1,),
        padding=[(CONV_K // 2, CONV_K // 2)],
        dimension_numbers=("NWC", "WIO", "NWC"), feature_group_count=ch)
    return out + b


def to_colmajor(t, rows):
    b, L, ch = t.shape
    return t.reshape(b, rows, GRID_W, ch).swapaxes(1, 2).reshape(b, L, ch)


def from_colmajor(t, rows):
    b, L, ch = t.shape
    return t.reshape(b, GRID_W, rows, ch).swapaxes(1, 2).reshape(b, L, ch)


def s5_discretise(a_re, a_im, log_dt, b_re, b_im, c_re, c_im):
    lam = lax.complex(jnp.minimum(a_re.astype(F32), -1e-4), a_im.astype(F32))
    dt = jnp.exp(log_dt.astype(F32))[..., None]
    a_bar = jnp.exp(lam * dt)
    bmat = lax.complex(b_re.astype(F32), b_im.astype(F32))
    b_bar = ((a_bar - 1.0) / lam)[..., None] * bmat[None]
    c_mat = lax.complex(c_re.astype(F32), c_im.astype(F32))
    return a_bar, b_bar, c_mat


def _lin_combine(e1, e2):
    a1, b1 = e1
    a2, b2 = e2
    return a1 * a2, a2 * b1 + b2


def s5_scan(u, a_bar, b_bar, c_mat, h0):
    bsz, L = u.shape[:2]
    ub = u.reshape(bsz, L // S5_CHUNK, S5_CHUNK, S5_GROUPS, S5_GROUP).swapaxes(0, 1)

    def step(h, u_blk):
        bu = jnp.einsum("btgc,gpc->btgp", u_blk.astype(jnp.complex64), b_bar)
        bu = bu.at[:, 0].add(a_bar * h)
        a = jnp.broadcast_to(a_bar, bu.shape)
        _, hs = lax.associative_scan(_lin_combine, (a, bu), axis=1)
        y = jnp.einsum("btgp,gcp->btgc", hs, c_mat).real
        return hs[:, -1], y

    h_last, ys = lax.scan(step, h0, ub)
    return ys.swapaxes(0, 1).reshape(u.shape), h_last


def s5_bidir(u, a_bar, b_bar, c_mat, d_skip, h0f, h0b):
    bsz, L, _ = u.shape
    uf = u.astype(F32)
    ug = uf.reshape(bsz, L, S5_GROUPS, S5_GROUP)
    yf, hf = s5_scan(ug, a_bar[0], b_bar[0], c_mat, h0f)
    yb, hb = s5_scan(ug[:, ::-1], a_bar[1], b_bar[1], c_mat, h0b)
    y = (yf + yb[:, ::-1]).reshape(bsz, L, D_S5) + d_skip.astype(F32) * uf
    return y, hf, hb


def s5_glu(y, w, b):
    y = jax.nn.gelu(y)
    return y * jax.nn.sigmoid(y @ w + b)


def mlstm_zero_state(bsz):
    return (jnp.zeros((bsz, ML_HEADS, ML_HEAD_DIM, ML_HEAD_DIM), F32),
            jnp.zeros((bsz, ML_HEADS, ML_HEAD_DIM), F32),
            jnp.zeros((bsz, ML_HEADS), F32))


def mlstm_scan(q, k, v, i_pre, log_f, state):
    bsz, nh, L, dh = q.shape
    nc = L // ML_CHUNK

    def blk(t):
        return jnp.moveaxis(t.reshape(bsz, nh, nc, ML_CHUNK, *t.shape[3:]), 2, 0)

    lower = jnp.tril(jnp.ones((ML_CHUNK, ML_CHUNK), dtype=bool))

    def step(carry, inp):
        c_st, n_st, m_st = carry
        qb, kb, vb, ib, fb = inp
        b = jnp.cumsum(fb, axis=-1)
        d = b[..., :, None] - b[..., None, :] + ib[..., None, :]
        d = jnp.where(lower, d, -jnp.inf)
        inter = b + m_st[..., None]
        m_t = jnp.maximum(inter, jnp.max(d, axis=-1))
        w = jnp.exp(d - m_t[..., None])
        s_inter = jnp.exp(inter - m_t)
        s = jnp.einsum("bhtd,bhsd->bhts", qb, kb) * w
        num = jnp.einsum("bhts,bhsd->bhtd", s, vb) + s_inter[..., None] * jnp.einsum("bhvd,bhtd->bhtv", c_st, qb)
        den = jnp.sum(s, axis=-1) + s_inter * jnp.einsum("bhd,bhtd->bht", n_st, qb)
        h = num / jnp.maximum(jnp.abs(den), jnp.exp(-m_t))[..., None]
        b_T = b[..., -1]
        dT = b_T[..., None] - b + ib
        m_new = jnp.maximum(b_T + m_st, jnp.max(dT, axis=-1))
        wT = jnp.exp(dT - m_new[..., None])
        dec = jnp.exp(b_T + m_st - m_new)
        c_new = dec[..., None, None] * c_st + jnp.einsum("bhsv,bhsd->bhvd", vb * wT[..., None], kb)
        n_new = dec[..., None] * n_st + jnp.einsum("bhs,bhsd->bhd", wT, kb)
        return (c_new, n_new, m_new), h

    state, hs = lax.scan(step, state, (blk(q), blk(k), blk(v), blk(i_pre), blk(log_f)))
    return jnp.moveaxis(hs, 0, 2).reshape(bsz, nh, L, dh), state


def mlstm_inputs(p, conv_w, conv_b, gate_b):
    bsz, L, _ = p.shape
    qk = jax.nn.silu(centred_dwconv(p[..., :2 * D_ML], conv_w, conv_b))

    def heads(t):
        return t.astype(F32).reshape(bsz, L, ML_HEADS, ML_HEAD_DIM).transpose(0, 2, 1, 3)

    q = heads(qk[..., :D_ML]) * ML_HEAD_DIM ** -0.5
    k = heads(qk[..., D_ML:])
    v = heads(p[..., 2 * D_ML:3 * D_ML])
    o = p[..., 3 * D_ML:4 * D_ML]
    g = (p[..., 4 * D_ML:].reshape(bsz, L, 4, ML_HEADS) + gate_b).astype(F32).transpose(0, 2, 3, 1)
    return q, k, v, o, g


def mlstm_bidir(p, conv_w, conv_b, gate_b, state_f, state_b):
    q, k, v, o, g = mlstm_inputs(p, conv_w, conv_b, gate_b)
    hf, sf = mlstm_scan(q, k, v, g[:, 0], jax.nn.log_sigmoid(g[:, 1]), state_f)

    def rev(t):
        return jnp.flip(t, axis=2)

    hb, sb = mlstm_scan(rev(q), rev(k), rev(v), g[:, 2, :, ::-1],
                        jax.nn.log_sigmoid(g[:, 3, :, ::-1]), state_b)
    return hf + rev(hb), o, sf, sb


def mlstm_out(h, o, norm_g):
    bsz, nh, L, dh = h.shape
    hn = h * lax.rsqrt(jnp.mean(h * h, axis=-1, keepdims=True) + EPS)
    hn = hn.transpose(0, 2, 1, 3).reshape(bsz, L, D_ML) * norm_g.astype(F32)
    return hn * jax.nn.sigmoid(o.astype(F32))


def token_mixer(hc, hx, with_ctx_out, w_in, s5_a_re, s5_a_im, s5_log_dt, s5_b_re, s5_b_im,
                s5_c_re, s5_c_im, s5_d, s5_glu_w, s5_glu_b, ml_conv_w, ml_conv_b, ml_gate_b,
                ml_norm_g, w_out):
    bsz, L, _ = hx.shape
    rows = L // GRID_W
    pc = hc @ w_in
    px = hx @ w_in
    a_bar, b_bar, c_mat = s5_discretise(s5_a_re, s5_a_im, s5_log_dt, s5_b_re, s5_b_im, s5_c_re, s5_c_im)
    z5 = jnp.zeros((bsz, S5_GROUPS, S5_STATE), jnp.complex64)
    yc_s5, s5f, s5b = s5_bidir(pc[..., :D_S5], a_bar, b_bar, c_mat, s5_d, z5, z5)
    yx_s5, _, _ = s5_bidir(px[..., :D_S5], a_bar, b_bar, c_mat, s5_d, s5f, s5b)
    zm = mlstm_zero_state(bsz)
    hc_ml, oc, mlf, mlb = mlstm_bidir(pc[..., D_S5:], ml_conv_w, ml_conv_b, ml_gate_b, zm, zm)
    hx_ml, ox, _, _ = mlstm_bidir(to_colmajor(px[..., D_S5:], rows), ml_conv_w, ml_conv_b,
                                  ml_gate_b, mlf, mlb)
    yx = jnp.concatenate([s5_glu(yx_s5, s5_glu_w, s5_glu_b),
                          from_colmajor(mlstm_out(hx_ml, ox, ml_norm_g), rows)], axis=-1)
    yx = yx.astype(hx.dtype) @ w_out
    yc = None
    if with_ctx_out:
        yc = jnp.concatenate([s5_glu(yc_s5, s5_glu_w, s5_glu_b),
                              mlstm_out(hc_ml, oc, ml_norm_g)], axis=-1).astype(hc.dtype) @ w_out
    return yc, yx


def ec_moe(h, router_w, w_gate, w_up, w_down):
    bsz, n, d = h.shape
    cap = EC_FACTOR * n // N_EXPERTS
    aff = jax.nn.softmax((h @ router_w).astype(F32), axis=-1)
    g, idx = lax.top_k(aff.swapaxes(1, 2), cap)
    xs = jax.vmap(lambda hb, ib: hb[ib])(h, idx)
    a = jnp.einsum("becd,edf->becf", xs, w_gate)
    u = jnp.einsum("becd,edf->becf", xs, w_up)
    y = jnp.einsum("becf,efd->becd", jax.nn.silu(a) * u, w_down) * g[..., None].astype(h.dtype)
    return jax.vmap(lambda yb, ib: jnp.zeros((n, d), yb.dtype).at[ib.reshape(-1)].add(yb.reshape(-1, d)))(y, idx)


def setup_inputs(seed: int = 0) -> dict:
    key = jax.random.key(seed)
    ks = jax.random.split(key, 32)

    def nrm(k, shape, s):
        return jax.random.normal(k, shape, F32) * s

    n_idx = jnp.arange(S5_STATE, dtype=F32)
    log_lo, log_hi = math.log(1e-3), math.log(1e-1)
    f_bias = jnp.linspace(3.0, 6.0, ML_HEADS, dtype=F32)
    gate_base = jnp.stack([jnp.zeros((ML_HEADS,), F32), f_bias, jnp.zeros((ML_HEADS,), F32), f_bias])
    return {
        "x": nrm(ks[0], (BATCH, SEQ, D_MODEL), 1.0),
        "c": nrm(ks[1], (BATCH, D_MODEL), 1.0),
        "ctx": nrm(ks[2], (BATCH, CTX_LEN, D_MODEL), 1.0),
        "c_ctx": nrm(ks[3], (D_MODEL,), 1.0),
        "ada_w": nrm(ks[4], (DEPTH, D_MODEL, N_MOD * D_MODEL), 0.5 * D_MODEL ** -0.5),
        "ada_b": nrm(ks[5], (DEPTH, N_MOD * D_MODEL), 0.02),
        "norm_g": 1.0 + nrm(ks[6], (DEPTH, 4, D_MODEL), 0.05),
        "w_in": nrm(ks[7], (DEPTH, D_MODEL, D_IN), D_MODEL ** -0.5),
        "s5_a_re": -0.5 + nrm(ks[8], (DEPTH, 2, S5_GROUPS, S5_STATE), 0.01),
        "s5_a_im": math.pi * n_idx + nrm(ks[9], (DEPTH, 2, S5_GROUPS, S5_STATE), 0.01),
        "s5_log_dt": log_lo + (log_hi - log_lo) * jax.random.uniform(ks[10], (DEPTH, 2, S5_GROUPS), F32),
        "s5_b_re": nrm(ks[11], (DEPTH, S5_GROUPS, S5_STATE, S5_GROUP), (2 * S5_GROUP) ** -0.5),
        "s5_b_im": nrm(ks[12], (DEPTH, S5_GROUPS, S5_STATE, S5_GROUP), (2 * S5_GROUP) ** -0.5),
        "s5_c_re": nrm(ks[13], (DEPTH, S5_GROUPS, S5_GROUP, S5_STATE), 0.5),
        "s5_c_im": nrm(ks[14], (DEPTH, S5_GROUPS, S5_GROUP, S5_STATE), 0.5),
        "s5_d": nrm(ks[15], (DEPTH, D_S5), 0.5),
        "s5_glu_w": nrm(ks[16], (DEPTH, D_S5, D_S5), D_S5 ** -0.5),
        "s5_glu_b": nrm(ks[17], (DEPTH, D_S5), 0.02),
        "ml_conv_w": nrm(ks[18], (DEPTH, CONV_K, 2 * D_ML), CONV_K ** -0.5),
        "ml_conv_b": nrm(ks[19], (DEPTH, 2 * D_ML), 0.02),
        "ml_gate_b": gate_base[None] + nrm(ks[20], (DEPTH, 4, ML_HEADS), 0.1),
        "ml_norm_g": 1.0 + nrm(ks[21], (DEPTH, D_ML), 0.05),
        "w_out": nrm(ks[22], (DEPTH, D_MIX, D_MODEL), D_MIX ** -0.5),
        "router_w": nrm(ks[23], (DEPTH, D_MODEL, N_EXPERTS), D_MODEL ** -0.5),
        "exp_w_gate": nrm(ks[24], (DEPTH, N_EXPERTS, D_MODEL, EXPERT_FF), D_MODEL ** -0.5),
        "exp_w_up": nrm(ks[25], (DEPTH, N_EXPERTS, D_MODEL, EXPERT_FF), D_MODEL ** -0.5),
        "exp_w_down": nrm(ks[26], (DEPTH, N_EXPERTS, EXPERT_FF, D_MODEL), EXPERT_FF ** -0.5),
    }


def reference(x, c, ctx, c_ctx, ada_w, ada_b, norm_g, w_in, s5_a_re, s5_a_im, s5_log_dt,
              s5_b_re, s5_b_im, s5_c_re, s5_c_im, s5_d, s5_glu_w, s5_glu_b, ml_conv_w,
              ml_conv_b, ml_gate_b, ml_norm_g, w_out, router_w, exp_w_gate, exp_w_up, exp_w_down):
    for li in range(DEPTH):
        with_ctx_out = li + 1 < DEPTH
        sh1, sc1, g1, sh2, sc2, g2 = _modulation(c, ada_w[li], ada_b[li])
        csh1, csc1, cg1, csh2, csc2, cg2 = _modulation(c_ctx, ada_w[li], ada_b[li])
        hx = rms_norm(x, norm_g[li, 0]) * (1.0 + sc1) + sh1
        hc = rms_norm(ctx, norm_g[li, 0]) * (1.0 + csc1) + csh1
        yc, yx = token_mixer(hc, hx, with_ctx_out, w_in[li], s5_a_re[li], s5_a_im[li], s5_log_dt[li],
                             s5_b_re[li], s5_b_im[li], s5_c_re[li], s5_c_im[li], s5_d[li],
                             s5_glu_w[li], s5_glu_b[li], ml_conv_w[li], ml_conv_b[li],
                             ml_gate_b[li], ml_norm_g[li], w_out[li])
        x = x + g1 * rms_norm(yx, norm_g[li, 1])
        hx = rms_norm(x, norm_g[li, 2]) * (1.0 + sc2) + sh2
        x = x + g2 * rms_norm(ec_moe(hx, router_w[li], exp_w_gate[li], exp_w_up[li], exp_w_down[li]), norm_g[li, 3])
        if with_ctx_out:
            ctx = ctx + cg1 * rms_norm(yc, norm_g[li, 1])
            hc = rms_norm(ctx, norm_g[li, 2]) * (1.0 + csc2) + csh2
            ctx = ctx + cg2 * rms_norm(ec_moe(hc, router_w[li], exp_w_gate[li], exp_w_up[li], exp_w_down[li]), norm_g[li, 3])
    return x
```

```python
import functools
import math

import jax
import jax.numpy as jnp
from jax import lax
from jax.experimental import pallas as pl
from jax.experimental.pallas import tpu as pltpu

F32 = jnp.float32
BF16 = jnp.bfloat16
I32 = jnp.int32

EPS = 1e-6
GRID_W = 64
S5_GROUP = 16
S5_STATE = 64
S5_T = 16
ML_HEADS = 4
ML_T = 256
CONV_K = 5
N_EXPERTS = 16
EC_FACTOR = 2
N_MOD = 6
LANES = 128
TOK_BLK = 128
DISP_W = 136
COMB_W = 144
VMEM_LIMIT = 56 * 1024 * 1024


def _cparams(sem):
    return pltpu.CompilerParams(dimension_semantics=sem, vmem_limit_bytes=VMEM_LIMIT)


def _sigmoid(x):
    return 1.0 / (1.0 + jnp.exp(-x))


def _silu(x):
    return x * _sigmoid(x)


def _log_sigmoid(x):
    return jnp.minimum(x, 0.0) - jnp.log(1.0 + jnp.exp(-jnp.abs(x)))


def _split3(x):
    x1 = x.astype(BF16)
    r = x - x1.astype(F32)
    x2 = r.astype(BF16)
    x3 = (r - x2.astype(F32)).astype(BF16)
    return x1, x2, x3


def _dot(a, b):
    return jnp.dot(a, b, preferred_element_type=F32)


def _dot_nt(a, b):
    return lax.dot_general(a, b, (((1,), (1,)), ((), ())), preferred_element_type=F32)


def _dot_tn(a, b):
    return lax.dot_general(a, b, (((0,), (0,)), ((), ())), preferred_element_type=F32)


def _dot3_left(a_f32, b_bf16):
    a1, a2, a3 = _split3(a_f32)
    return _dot(a1, b_bf16) + _dot(a2, b_bf16) + _dot(a3, b_bf16)


def _dot3_right(a_bf16, b_f32):
    b1, b2, b3 = _split3(b_f32)
    return _dot(a_bf16, b1) + _dot(a_bf16, b2) + _dot(a_bf16, b3)


def _mod_kernel(c_ref, w_ref, b_ref, o_ref):
    c = c_ref[...]
    s = _silu(c).astype(BF16)
    o_ref[...] = _dot(s, w_ref[...].astype(BF16)) + b_ref[...]


def _modulation(cond8, ada_w, ada_b):
    d, n = ada_w.shape
    tn = 1024
    return pl.pallas_call(
        _mod_kernel,
        grid=(n // tn,),
        in_specs=[pl.BlockSpec((8, d), lambda j: (0, 0)),
                  pl.BlockSpec((d, tn), lambda j: (0, j)),
                  pl.BlockSpec((1, tn), lambda j: (0, j))],
        out_specs=pl.BlockSpec((8, tn), lambda j: (0, j)),
        out_shape=jax.ShapeDtypeStruct((8, n), F32),
        compiler_params=_cparams(("parallel",)),
        name="modulation",
    )(cond8, ada_w, ada_b.reshape(1, n))


def _inproj_kernel(x_ref, g_ref, sc_ref, sh_ref, w_ref, wg_ref, o_ref, og_ref, h_scr):
    @pl.when(pl.program_id(1) == 0)
    def _():
        x = x_ref[...]
        ms = jnp.mean(x * x, axis=-1, keepdims=True)
        h = x * lax.rsqrt(ms + EPS) * g_ref[...]
        h = h * (1.0 + sc_ref[0]) + sh_ref[0]
        hb = h.astype(BF16)
        h_scr[...] = hb
        og_ref[...] = _dot(hb, wg_ref[...])

    o_ref[...] = _dot(h_scr[...], w_ref[...]).astype(o_ref.dtype)


def _in_projection(x2d, rows_per_mod, g0, sc, sh, w_main, w_gate):
    n, d = x2d.shape
    nmain = w_main.shape[1]
    tm = min(512, rows_per_mod)
    tn = 1024
    tiles_per_mod = rows_per_mod // tm
    return pl.pallas_call(
        _inproj_kernel,
        grid=(n // tm, nmain // tn),
        in_specs=[pl.BlockSpec((tm, d), lambda i, j: (i, 0)),
                  pl.BlockSpec((1, d), lambda i, j: (0, 0)),
                  pl.BlockSpec((1, 1, d), lambda i, j: (i // tiles_per_mod, 0, 0)),
                  pl.BlockSpec((1, 1, d), lambda i, j: (i // tiles_per_mod, 0, 0)),
                  pl.BlockSpec((d, tn), lambda i, j: (0, j)),
                  pl.BlockSpec((d, LANES), lambda i, j: (0, 0))],
        out_specs=[pl.BlockSpec((tm, tn), lambda i, j: (i, j)),
                   pl.BlockSpec((tm, LANES), lambda i, j: (i, 0))],
        out_shape=[jax.ShapeDtypeStruct((n, nmain), BF16),
                   jax.ShapeDtypeStruct((n, LANES), F32)],
        scratch_shapes=[pltpu.VMEM((tm, d), BF16)],
        compiler_params=_cparams(("parallel", "arbitrary")),
        name="in_projection",
    )(x2d, g0, sc, sh, w_main, w_gate)


def _s5_matrices(a_re, a_im, log_dt, b_re, b_im, c_re, c_im):
    t = S5_T
    g = a_re.shape[1]
    lam = lax.complex(jnp.minimum(a_re.astype(F32), -1e-4), a_im.astype(F32))
    dt = jnp.exp(log_dt.astype(F32))[..., None]
    a_bar = jnp.exp(lam * dt)
    bmat = lax.complex(b_re.astype(F32), b_im.astype(F32))
    b_bar = ((a_bar - 1.0) / lam)[..., None] * bmat[None]
    c_mat = lax.complex(c_re.astype(F32), c_im.astype(F32))
    pw = jnp.exp((lam * dt)[:, None] * jnp.arange(t + 1, dtype=F32)[None, :, None, None])
    kern = jnp.einsum("gcp,dtgp,dgpk->dtgck", c_mat, pw[:, :t], b_bar,
                      precision=lax.Precision.HIGHEST).real
    ti = jnp.arange(t)
    lag_f = ti[None, :] - ti[:, None]
    kf = kern[0][jnp.clip(lag_f, 0, t - 1)]
    kf = jnp.where((lag_f >= 0)[:, :, None, None, None], kf, 0.0)
    kb = kern[1][jnp.clip(-lag_f, 0, t - 1)]
    kb = jnp.where((lag_f <= 0)[:, :, None, None, None], kb, 0.0)
    m = (kf + kb).transpose(2, 0, 4, 1, 3).reshape(g, t * S5_GROUP, t * S5_GROUP)
    inj_f = pw[0][t - 1 - ti][:, :, :, None] * b_bar[0][None]
    inj_b = pw[1][ti][:, :, :, None] * b_bar[1][None]
    out_f = c_mat[None] * pw[0][ti + 1][:, :, None, :]
    out_b = c_mat[None] * pw[1][t - ti][:, :, None, :]

    def inj_cols(z):
        return z.transpose(1, 0, 3, 2).reshape(g, t * S5_GROUP, S5_STATE)

    def out_rows(z):
        return z.transpose(1, 3, 0, 2).reshape(g, S5_STATE, t * S5_GROUP)

    npair = g // 2
    zi = jnp.zeros((npair, t * S5_GROUP, S5_STATE), F32)
    zo = jnp.zeros((npair, S5_STATE, t * S5_GROUP), F32)

    def pair_inj(z):
        z = z.reshape(npair, 2, t * S5_GROUP, S5_STATE)
        top = jnp.concatenate([z[:, 0], zi], axis=-1)
        bot = jnp.concatenate([zi, z[:, 1]], axis=-1)
        return jnp.concatenate([top, bot], axis=1)

    def pair_out(z):
        z = z.reshape(npair, 2, S5_STATE, t * S5_GROUP)
        top = jnp.concatenate([z[:, 0], zo], axis=-1)
        bot = jnp.concatenate([zo, z[:, 1]], axis=-1)
        return jnp.concatenate([top, bot], axis=1)

    jf, jb = inj_cols(inj_f), inj_cols(inj_b)
    bp = jnp.concatenate([pair_inj(jf.real), pair_inj(jf.imag),
                          pair_inj(jb.real), pair_inj(jb.imag)], axis=-1)
    of, ob = out_rows(out_f), out_rows(out_b)
    cp = jnp.concatenate([pair_out(of.real), pair_out(-of.imag),
                          pair_out(ob.real), pair_out(-ob.imag)], axis=1)
    at = pw[:, t]
    avec = jnp.stack([at[0].real, at[0].imag, at[1].real, at[1].imag], axis=1)
    avec = avec.reshape(npair, 2, 4, S5_STATE).transpose(0, 2, 1, 3).reshape(npair, 4, 2 * S5_STATE)
    avec = jnp.concatenate([avec, jnp.zeros((npair, 4, 2 * S5_STATE), F32)], axis=1)
    return (m.reshape(npair, 2, t * S5_GROUP, t * S5_GROUP).astype(BF16),
            bp.astype(BF16), cp.astype(BF16), avec)


def _s5_kernel(u_ref, m_ref, bp_ref, cp_ref, av_ref, y_ref, s_scr, hp_scr, *, bsz, cc, xc):
    nc = cc + xc
    u = u_ref[0]
    s_scr[...] = _dot(u, bp_ref[0])
    av = av_ref[0]
    ar_f, ai_f, ar_b, ai_b = av[0:1], av[1:2], av[2:3], av[3:4]

    sub = lax.broadcasted_iota(I32, (8, 128), 0)

    def scan8(s_re, s_im, h_re, h_im, ar, ai, order):
        p_re = jnp.zeros((8, 128), F32)
        p_im = jnp.zeros((8, 128), F32)
        for r in order:
            p_re = jnp.where(sub == r, h_re, p_re)
            p_im = jnp.where(sub == r, h_im, p_im)
            h_re, h_im = (ar * h_re - ai * h_im + s_re[r:r + 1],
                          ar * h_im + ai * h_re + s_im[r:r + 1])
        return p_re, p_im, h_re, h_im

    def step(rows_f, rows_b, carry):
        new = []
        for b in range(bsz):
            hr, hi, gr, gi = carry[4 * b:4 * b + 4]
            rf = pl.multiple_of(rows_f[b], 8)
            rb = pl.multiple_of(rows_b[b], 8)
            p_re, p_im, hr, hi = scan8(s_scr[pl.ds(rf, 8), 0:128], s_scr[pl.ds(rf, 8), 128:256],
                                       hr, hi, ar_f, ai_f, range(8))
            hp_scr[pl.ds(rf, 8), 0:128] = p_re
            hp_scr[pl.ds(rf, 8), 128:256] = p_im
            p_re, p_im, gr, gi = scan8(s_scr[pl.ds(rb, 8), 256:384], s_scr[pl.ds(rb, 8), 384:512],
                                       gr, gi, ar_b, ai_b, range(7, -1, -1))
            hp_scr[pl.ds(rb, 8), 256:384] = p_re
            hp_scr[pl.ds(rb, 8), 384:512] = p_im
            new += [hr, hi, gr, gi]
        return tuple(new)

    def ctx_body(i, carry):
        return step([b * nc + 8 * i for b in range(bsz)],
                    [b * nc + cc - 8 - 8 * i for b in range(bsz)], carry)

    def x_body(i, carry):
        return step([b * nc + cc + 8 * i for b in range(bsz)],
                    [b * nc + nc - 8 - 8 * i for b in range(bsz)], carry)

    z = jnp.zeros((1, 128), F32)
    carry = lax.fori_loop(0, cc // 8, ctx_body, (z,) * (4 * bsz))
    lax.fori_loop(0, xc // 8, x_body, carry)

    for b in range(bsz):
        lo = b * nc + cc
        ux = u_ref[0, lo:lo + xc, :]
        hp = hp_scr[lo:lo + xc, :].astype(BF16)
        y = _dot(hp, cp_ref[0])
        y0 = _dot(ux[:, 0:256], m_ref[0, 0])
        y1 = _dot(ux[:, 256:512], m_ref[0, 1])
        y_ref[0, b * xc:(b + 1) * xc, :] = (y + jnp.concatenate([y0, y1], axis=-1)).astype(y_ref.dtype)


def _s5_scan(u_pairs, m, bp, cp, avec, bsz, cc, xc):
    npair = u_pairs.shape[0]
    nc = cc + xc
    kern = functools.partial(_s5_kernel, bsz=bsz, cc=cc, xc=xc)
    return pl.pallas_call(
        kern,
        grid=(npair,),
        in_specs=[pl.BlockSpec((1, bsz * nc, 512), lambda p: (p, 0, 0)),
                  pl.BlockSpec((1, 2, 256, 256), lambda p: (p, 0, 0, 0)),
                  pl.BlockSpec((1, 512, 512), lambda p: (p, 0, 0)),
                  pl.BlockSpec((1, 512, 512), lambda p: (p, 0, 0)),
                  pl.BlockSpec((1, 8, 128), lambda p: (p, 0, 0))],
        out_specs=pl.BlockSpec((1, bsz * xc, 512), lambda p: (p, 0, 0)),
        out_shape=jax.ShapeDtypeStruct((npair, bsz * xc, 512), BF16),
        scratch_shapes=[pltpu.VMEM((bsz * nc, 512), F32), pltpu.VMEM((bsz * nc, 512), F32)],
        compiler_params=_cparams(("parallel",)),
        name="s5_scan",
    )(u_pairs, m, bp, cp, avec)


def _conv_kernel(x_ref, p_ref, n_ref, w_ref, b_ref, sc_ref, o_ref, scr, *, rows, seq_tiles):
    i = pl.program_id(1)
    first = functools.reduce(jnp.logical_or, [i == s for s in seq_tiles[0]])
    last = functools.reduce(jnp.logical_or, [i == s for s in seq_tiles[1]])
    pm = jnp.where(first, 0.0, 1.0)
    nm = jnp.where(last, 0.0, 1.0)
    scr[0:8, :] = p_ref[0, 8:16, :].astype(F32) * pm
    scr[8:8 + rows, :] = x_ref[0].astype(F32)
    scr[8 + rows:16 + rows, :] = n_ref[0, 0:8, :].astype(F32) * nm
    w = w_ref[...]
    acc = scr[6:6 + rows, :] * w[0:1]
    for j in range(1, CONV_K):
        acc = acc + scr[6 + j:6 + j + rows, :] * w[j:j + 1]
    z = acc + b_ref[...]
    o_ref[0] = (_silu(z) * sc_ref[...]).astype(o_ref.dtype)


def _conv_silu(ml, conv_w8, conv_b, scale, seq_starts, seq_ends):
    bsz, ltot, _ = ml.shape
    rows = 256
    ct = 512
    nch = conv_b.shape[-1]
    nt = ltot // rows
    hb = rows // 16
    nhb = ltot // 16
    seq_tiles = (tuple(s // rows for s in seq_starts), tuple(e // rows - 1 for e in seq_ends))
    kern = functools.partial(_conv_kernel, rows=rows, seq_tiles=seq_tiles)
    return pl.pallas_call(
        kern,
        grid=(bsz, nt, nch // ct),
        in_specs=[pl.BlockSpec((1, rows, ct), lambda b, i, c: (b, i, c)),
                  pl.BlockSpec((1, 16, ct), lambda b, i, c: (b, jnp.maximum(i * hb - 1, 0), c)),
                  pl.BlockSpec((1, 16, ct), lambda b, i, c: (b, jnp.minimum((i + 1) * hb, nhb - 1), c)),
                  pl.BlockSpec((8, ct), lambda b, i, c: (0, c)),
                  pl.BlockSpec((1, ct), lambda b, i, c: (0, c)),
                  pl.BlockSpec((1, ct), lambda b, i, c: (0, c))],
        out_specs=pl.BlockSpec((1, rows, ct), lambda b, i, c: (b, i, c)),
        out_shape=jax.ShapeDtypeStruct((bsz, ltot, nch), BF16),
        scratch_shapes=[pltpu.VMEM((rows + 16, ct), F32)],
        compiler_params=_cparams(("parallel", "parallel", "parallel")),
        name="conv_silu",
    )(ml, ml, ml, conv_w8, conv_b, scale)


def _mlstm_chain_step(q, k, v, i_col, f_col, i_row, f_row, b_col, b_row, c_ref, n_ref, m_ref,
                      idx, backward):
    t = q.shape[0]
    rid = lax.broadcasted_iota(I32, (t, t), 0)
    cid = lax.broadcasted_iota(I32, (t, t), 1)
    mask = (cid >= rid) if backward else (rid >= cid)
    m_st = m_ref[idx]
    d = jnp.where(mask, b_col - b_row + i_row, -jnp.inf)
    inter = b_col + m_st
    m_t = jnp.maximum(inter, jnp.max(d, axis=-1, keepdims=True))
    w = jnp.exp(d - m_t)
    s_inter = jnp.exp(inter - m_t)
    s = _dot_nt(q, k) * w
    c_st = c_ref[idx]
    n_st = n_ref[idx]
    num = _dot(s.astype(BF16), v) + s_inter * _dot_nt(q, c_st.astype(BF16))
    qf = q.astype(F32)
    den = jnp.sum(s, axis=-1, keepdims=True) + s_inter * jnp.sum(qf * n_st, axis=-1, keepdims=True)
    h = num / jnp.maximum(jnp.abs(den), jnp.exp(-m_t))
    total = jnp.sum(f_col, axis=0, keepdims=True)
    d_end = total - b_col + i_col
    m_new = jnp.maximum(total + m_st, jnp.max(d_end, axis=0, keepdims=True))
    w_end = jnp.exp(d_end - m_new)
    dec = jnp.exp(total + m_st - m_new)
    vw = (v.astype(F32) * w_end).astype(BF16)
    c_ref[idx] = dec * c_st + _dot_tn(vw, k)
    n_ref[idx] = dec * n_st + jnp.sum(k.astype(F32) * w_end, axis=0, keepdims=True)
    m_ref[idx] = m_new
    return h


def _mlstm_kernel(qf_ref, kf_ref, vf_ref, gf_ref, gtf_ref, qb_ref, kb_ref, vb_ref, gb_ref, gtb_ref,
                  gbias_ref, gbias_t_ref, hf_ref, hb_ref, c_scr, n_scr, m_scr, *, dh):
    @pl.when(pl.program_id(1) == 0)
    def _():
        c_scr[...] = jnp.zeros_like(c_scr)
        n_scr[...] = jnp.zeros_like(n_scr)
        m_scr[...] = jnp.zeros_like(m_scr)

    t = qf_ref.shape[1]
    rid = lax.broadcasted_iota(I32, (t, t), 0)
    cid = lax.broadcasted_iota(I32, (t, t), 1)
    lower = jnp.where(rid >= cid, 1.0, 0.0).astype(BF16)
    upper = jnp.where(cid >= rid, 1.0, 0.0).astype(BF16)

    for d, (q_ref, k_ref, v_ref, g_ref, gt_ref, h_ref) in enumerate(
            [(qf_ref, kf_ref, vf_ref, gf_ref, gtf_ref, hf_ref),
             (qb_ref, kb_ref, vb_ref, gb_ref, gtb_ref, hb_ref)]):
        backward = d == 1
        g = g_ref[0] + gbias_ref[...]
        gt = gt_ref[0] + gbias_t_ref[...]
        lf = _log_sigmoid(g)
        lft = _log_sigmoid(gt)
        cum_col = _dot3_right(upper if backward else lower, lf)
        cum_row = _dot3_left(lft, lower if backward else upper)
        for hd in range(ML_HEADS):
            ci = 2 * ML_HEADS * d + hd
            cf = ci + ML_HEADS
            sl = slice(hd * dh, (hd + 1) * dh)
            h = _mlstm_chain_step(
                q_ref[0, :, sl], k_ref[0, :, sl], v_ref[0, :, sl],
                g[:, ci:ci + 1], lf[:, cf:cf + 1], gt[ci:ci + 1, :], lft[cf:cf + 1, :],
                cum_col[:, cf:cf + 1], cum_row[cf:cf + 1, :],
                c_scr, n_scr, m_scr, d * ML_HEADS + hd, backward)
            h_ref[0, :, sl] = h.astype(h_ref.dtype)


def _mlstm_scan(qk, ml, gates, gates_t, gbias, gbias_t, n_ctx_chunks):
    bsz, ltot, dml2 = qk.shape
    dml = dml2 // 2
    dh = dml // ML_HEADS
    t = ML_T
    nchunk = ltot // t
    cc = n_ctx_chunks

    def fwd(b, s):
        return s

    def bwd(b, s):
        return jnp.where(s < cc, cc - 1 - s, nchunk - 1 - (s - cc))

    def specs(cmap):
        return [pl.BlockSpec((1, t, dml), lambda b, s: (b, cmap(b, s), 0)),
                pl.BlockSpec((1, t, dml), lambda b, s: (b, cmap(b, s), 1)),
                pl.BlockSpec((1, t, dml), lambda b, s: (b, cmap(b, s), 2)),
                pl.BlockSpec((1, t, 16), lambda b, s: (b, cmap(b, s), 0)),
                pl.BlockSpec((1, 16, t), lambda b, s: (b, 0, cmap(b, s)))]

    kern = functools.partial(_mlstm_kernel, dh=dh)
    nchain = 2 * ML_HEADS
    return pl.pallas_call(
        kern,
        grid=(bsz, nchunk),
        in_specs=specs(fwd) + specs(bwd) + [pl.BlockSpec((1, 16), lambda b, s: (0, 0)),
                                            pl.BlockSpec((16, 1), lambda b, s: (0, 0))],
        out_specs=[pl.BlockSpec((1, t, dml), lambda b, s: (b, fwd(b, s), 0)),
                   pl.BlockSpec((1, t, dml), lambda b, s: (b, bwd(b, s), 0))],
        out_shape=[jax.ShapeDtypeStruct((bsz, ltot, dml), BF16)] * 2,
        scratch_shapes=[pltpu.VMEM((nchain, dh, dh), F32),
                        pltpu.VMEM((nchain, 1, dh), F32),
                        pltpu.VMEM((nchain, 1, 1), F32)],
        compiler_params=_cparams(("parallel", "arbitrary")),
        name="mlstm_scan",
    )(qk, qk, ml, gates, gates_t, qk, qk, ml, gates, gates_t, gbias, gbias_t)


def _gelu_tanh(y):
    return 0.5 * y * (1.0 + jnp.tanh(math.sqrt(2.0 / math.pi) * (y + 0.044715 * (y * y * y))))


def _mixout_kernel(ys_ref, u_ref, hf_ref, hb_ref, o_ref, x_ref, dsk_ref, gw_ref, gb_ref, mg_ref,
                   wo_ref, ng_ref, mod_ref, rw_ref, x1_ref, hx_ref, aff_ref, *, dh, n_exp):
    y = ys_ref[...].astype(F32) + dsk_ref[...] * u_ref[...].astype(F32)
    ge = _gelu_tanh(y)
    z = _dot(ge.astype(BF16), gw_ref[...]) + gb_ref[...]
    s5 = (ge * _sigmoid(z)).astype(BF16)
    h = hf_ref[...].astype(F32) + hb_ref[...].astype(F32)
    parts = []
    for hd in range(ML_HEADS):
        hh = h[:, hd * dh:(hd + 1) * dh]
        parts.append(hh * lax.rsqrt(jnp.mean(hh * hh, axis=-1, keepdims=True) + EPS))
    hn = jnp.concatenate(parts, axis=-1) * mg_ref[...]
    ml = (hn * _sigmoid(o_ref[...].astype(F32))).astype(BF16)
    yx = _dot(jnp.concatenate([s5, ml], axis=-1), wo_ref[...])
    ng = ng_ref[...]
    mod = mod_ref[0]
    r1 = yx * lax.rsqrt(jnp.mean(yx * yx, axis=-1, keepdims=True) + EPS) * ng[1:2]
    x1 = x_ref[...] + mod[0:1] * r1
    x1_ref[...] = x1
    hx = x1 * lax.rsqrt(jnp.mean(x1 * x1, axis=-1, keepdims=True) + EPS) * ng[2:3]
    hx = hx * (1.0 + mod[1:2]) + mod[2:3]
    hx_ref[...] = hx.astype(hx_ref.dtype)
    logits = _dot3_left(hx, rw_ref[...])
    lane = lax.broadcasted_iota(I32, logits.shape, 1)
    logits = jnp.where(lane < n_exp, logits, -jnp.inf)
    e = jnp.exp(logits - jnp.max(logits, axis=-1, keepdims=True))
    aff_ref[...] = e / jnp.sum(e, axis=-1, keepdims=True)


def _mixer_out(ys, px, hf, hb, x2d, rows_per_batch, dsk, glu_w, glu_b, ml_g, w_out, norm_g, mod3,
               router_w):
    n, d = x2d.shape
    ds5 = ys.shape[1]
    dml = hf.shape[1]
    tm = 256
    tiles_per_batch = rows_per_batch // tm
    o_blk = (ds5 + 3 * dml) // dml
    kern = functools.partial(_mixout_kernel, dh=dml // ML_HEADS, n_exp=N_EXPERTS)
    row = lambda i: (i, 0)
    const = lambda i: (0, 0)
    return pl.pallas_call(
        kern,
        grid=(n // tm,),
        in_specs=[pl.BlockSpec((tm, ds5), row),
                  pl.BlockSpec((tm, ds5), row),
                  pl.BlockSpec((tm, dml), row),
                  pl.BlockSpec((tm, dml), row),
                  pl.BlockSpec((tm, dml), lambda i: (i, o_blk)),
                  pl.BlockSpec((tm, d), row),
                  pl.BlockSpec((1, ds5), const),
                  pl.BlockSpec((ds5, ds5), const),
                  pl.BlockSpec((1, ds5), const),
                  pl.BlockSpec((1, dml), const),
                  pl.BlockSpec((d, d), const),
                  pl.BlockSpec((4, d), const),
                  pl.BlockSpec((1, 8, d), lambda i: (i // tiles_per_batch, 0, 0)),
                  pl.BlockSpec((d, LANES), const)],
        out_specs=[pl.BlockSpec((tm, d), row),
                   pl.BlockSpec((tm, d), row),
                   pl.BlockSpec((tm, LANES), row)],
        out_shape=[jax.ShapeDtypeStruct((n, d), F32),
                   jax.ShapeDtypeStruct((n, d), BF16),
                   jax.ShapeDtypeStruct((n, LANES), F32)],
        compiler_params=_cparams(("parallel",)),
        name="mixer_out",
    )(ys, px, hf, hb, px, x2d, dsk, glu_w, glu_b, ml_g, w_out, norm_g, mod3, router_w)


def _count(mask):
    c = jnp.sum(jnp.where(mask, 1.0, 0.0), axis=2, keepdims=True)
    return jnp.sum(c, axis=1, keepdims=True)


def _select_kernel(aff_ref, pos_ref, off_ref, *, cap, n_tok):
    a = aff_ref[0]
    e, nb, _ = a.shape
    v = pltpu.bitcast(a, I32)
    thr = jnp.zeros((e, 1, 1), I32)
    for bit in range(30, -1, -1):
        cand = thr | (1 << bit)
        thr = jnp.where(_count(v >= cand) >= cap, cand, thr)
    gt = v > thr
    tie = v == thr
    need = cap - _count(gt)
    tok = (lax.broadcasted_iota(I32, a.shape, 1) * LANES + lax.broadcasted_iota(I32, a.shape, 2))
    cut = jnp.zeros((e, 1, 1), I32)
    for bit in range(n_tok.bit_length() - 1, -1, -1):
        cand = cut | (1 << bit)
        cut = jnp.where(_count(tie & (tok < cand)) < need, cand, cut)
    sel = gt | (tie & (tok <= cut))
    self32 = jnp.where(sel, 1.0, 0.0).reshape(e * nb, LANES)
    r = lax.broadcasted_iota(I32, (LANES, LANES), 0)
    c = lax.broadcasted_iota(I32, (LANES, LANES), 1)
    incl = _dot(self32.astype(BF16), jnp.where(r <= c, 1.0, 0.0).astype(BF16))
    tot = jnp.broadcast_to(incl[:, LANES - 1:LANES], (e * nb, LANES)).astype(BF16)
    rr = lax.broadcasted_iota(I32, (e * nb, e * nb), 0)
    cc = lax.broadcasted_iota(I32, (e * nb, e * nb), 1)
    same = (rr // nb) == (cc // nb)
    before = jnp.where(same & (cc < rr), 1.0, 0.0).astype(BF16)
    off = _dot(before, tot)
    pos = (off + incl - 1.0).astype(I32)
    pos_ref[0] = jnp.where(self32 > 0.5, pos, -1).reshape(e, nb, LANES)
    off_ref[0] = off.astype(I32).reshape(e, nb, LANES)


def _select(aff_t, cap):
    bsz, e, nb, _ = aff_t.shape
    kern = functools.partial(_select_kernel, cap=cap, n_tok=nb * LANES)
    blk = pl.BlockSpec((1, e, nb, LANES), lambda b: (b, 0, 0, 0))
    return pl.pallas_call(
        kern,
        grid=(bsz,),
        in_specs=[blk],
        out_specs=[blk, blk],
        out_shape=[jax.ShapeDtypeStruct(aff_t.shape, I32)] * 2,
        compiler_params=_cparams(("parallel",)),
        name="ec_select",
    )(aff_t)


def _dispatch_kernel(off_ref, hx_ref, pos_ref, o_ref, acc, *, cap, nb, n_exp):
    b = pl.program_id(0)
    ex = pl.program_id(2)
    acc[...] = jnp.zeros_like(acc)
    kiota = lax.broadcasted_iota(I32, (DISP_W, TOK_BLK), 0)

    def body(j, _):
        start = pl.multiple_of((off_ref[(b * n_exp + ex) * nb + j] // 8) * 8, 8)
        prow = pos_ref[0, 0, pl.ds(j, 1), :]
        onehot = jnp.where(prow - start == kiota, 1.0, 0.0).astype(BF16)
        xj = hx_ref[0, pl.ds(pl.multiple_of(j * TOK_BLK, TOK_BLK), TOK_BLK), :]
        acc[pl.ds(start, DISP_W), :] += _dot(onehot, xj)
        return 0

    lax.fori_loop(0, nb, body, 0)
    o_ref[0, 0] = acc[0:cap, :].astype(o_ref.dtype)


def _dispatch(off_flat, hx, pos, cap):
    bsz, n_tok, d = hx.shape
    n_exp, nb = pos.shape[1], pos.shape[2]
    dblk = 1024
    kern = functools.partial(_dispatch_kernel, cap=cap, nb=nb, n_exp=n_exp)
    return pl.pallas_call(
        kern,
        grid_spec=pltpu.PrefetchScalarGridSpec(
            num_scalar_prefetch=1,
            grid=(bsz, d // dblk, n_exp),
            in_specs=[pl.BlockSpec((1, n_tok, dblk), lambda b, c, e, off: (b, 0, c)),
                      pl.BlockSpec((1, 1, nb, LANES), lambda b, c, e, off: (b, e, 0, 0))],
            out_specs=pl.BlockSpec((1, 1, cap, dblk), lambda b, c, e, off: (b, e, 0, c)),
            scratch_shapes=[pltpu.VMEM((cap + DISP_W, dblk), F32)]),
        out_shape=jax.ShapeDtypeStruct((bsz, n_exp, cap, d), BF16),
        compiler_params=_cparams(("parallel", "parallel", "arbitrary")),
        name="ec_dispatch",
    )(off_flat, hx, pos)


def _ffn_kernel(x_ref, wg_ref, wu_ref, wd_ref, o_ref, acc):
    f = pl.program_id(2)

    @pl.when(f == 0)
    def _():
        acc[...] = jnp.zeros_like(acc)

    x = x_ref[0, 0]
    a = _dot(x, wg_ref[0].astype(BF16))
    u = _dot(x, wu_ref[0].astype(BF16))
    hmid = (_silu(a) * u).astype(BF16)
    acc[...] += _dot(hmid, wd_ref[0].astype(BF16))

    @pl.when(f == pl.num_programs(2) - 1)
    def _():
        o_ref[0, 0] = acc[...].astype(o_ref.dtype)


def _expert_ffn(xs, w_gate, w_up, w_down):
    bsz, n_exp, cap, d = xs.shape
    ff = w_gate.shape[2]
    tf = 256
    return pl.pallas_call(
        _ffn_kernel,
        grid=(n_exp, bsz, ff // tf),
        in_specs=[pl.BlockSpec((1, 1, cap, d), lambda e, b, f: (b, e, 0, 0)),
                  pl.BlockSpec((1, d, tf), lambda e, b, f: (e, 0, f)),
                  pl.BlockSpec((1, d, tf), lambda e, b, f: (e, 0, f)),
                  pl.BlockSpec((1, tf, d), lambda e, b, f: (e, f, 0))],
        out_specs=pl.BlockSpec((1, 1, cap, d), lambda e, b, f: (b, e, 0, 0)),
        out_shape=jax.ShapeDtypeStruct((bsz, n_exp, cap, d), BF16),
        scratch_shapes=[pltpu.VMEM((cap, d), F32)],
        compiler_params=_cparams(("parallel", "parallel", "arbitrary")),
        name="expert_ffn",
    )(xs, w_gate, w_up, w_down)


def _combine_kernel(off_ref, y_ref, pos_ref, aff_ref, o_ref, *, cap, nb, n_exp):
    b = pl.program_id(0)
    j = pl.program_id(2)
    kiota = lax.broadcasted_iota(I32, (TOK_BLK, COMB_W), 1)
    pos = pos_ref[0]
    aff = aff_ref[0]
    acc = jnp.zeros(o_ref.shape[1:], F32)
    for ex in range(n_exp):
        start = (off_ref[(b * n_exp + ex) * nb + j] // 16) * 16
        start = pl.multiple_of(jnp.minimum(start, cap - COMB_W), 16)
        onehot = jnp.where(pos[:, ex:ex + 1] - start == kiota, 1.0, 0.0).astype(BF16)
        ywin = y_ref[0, ex, pl.ds(start, COMB_W), :]
        acc = acc + aff[:, ex:ex + 1] * _dot(onehot, ywin)
    o_ref[0] = acc


def _combine(off_flat, y, pos_t, aff, cap):
    bsz, n_exp, _, d = y.shape
    n_tok = pos_t.shape[1]
    nb = n_tok // TOK_BLK
    dblk = 256
    kern = functools.partial(_combine_kernel, cap=cap, nb=nb, n_exp=n_exp)
    return pl.pallas_call(
        kern,
        grid_spec=pltpu.PrefetchScalarGridSpec(
            num_scalar_prefetch=1,
            grid=(bsz, d // dblk, nb),
            in_specs=[pl.BlockSpec((1, n_exp, cap, dblk), lambda b, c, j, off: (b, 0, 0, c)),
                      pl.BlockSpec((1, TOK_BLK, n_exp), lambda b, c, j, off: (b, j, 0)),
                      pl.BlockSpec((1, TOK_BLK, n_exp), lambda b, c, j, off: (b, j, 0))],
            out_specs=pl.BlockSpec((1, TOK_BLK, dblk), lambda b, c, j, off: (b, j, c))),
        out_shape=jax.ShapeDtypeStruct((bsz, n_tok, d), F32),
        compiler_params=_cparams(("parallel", "parallel", "arbitrary")),
        name="ec_combine",
    )(off_flat, y, pos_t, aff)


def _final_kernel(x1_ref, moe_ref, g_ref, mod_ref, o_ref):
    m = moe_ref[...]
    r = m * lax.rsqrt(jnp.mean(m * m, axis=-1, keepdims=True) + EPS) * g_ref[...]
    o_ref[...] = x1_ref[...] + mod_ref[0] * r


def _final(x1, moe, g3, g2, rows_per_batch):
    n, d = x1.shape
    tm = 512
    tiles_per_batch = rows_per_batch // tm
    return pl.pallas_call(
        _final_kernel,
        grid=(n // tm,),
        in_specs=[pl.BlockSpec((tm, d), lambda i: (i, 0)),
                  pl.BlockSpec((tm, d), lambda i: (i, 0)),
                  pl.BlockSpec((1, d), lambda i: (0, 0)),
                  pl.BlockSpec((1, 1, d), lambda i: (i // tiles_per_batch, 0, 0))],
        out_specs=pl.BlockSpec((tm, d), lambda i: (i, 0)),
        out_shape=jax.ShapeDtypeStruct((n, d), F32),
        compiler_params=_cparams(("parallel",)),
        name="final_residual",
    )(x1, moe, g3, g2)


def _layer(x, c, ctx, c_ctx, ada_w, ada_b, norm_g, w_in, s5_a_re, s5_a_im, s5_log_dt, s5_b_re,
           s5_b_im, s5_c_re, s5_c_im, s5_d, s5_glu_w, s5_glu_b, ml_conv_w, ml_conv_b, ml_gate_b,
           ml_norm_g, w_out, router_w, exp_w_gate, exp_w_up, exp_w_down):
    bsz, seq, d = x.shape
    lctx = ctx.shape[1]
    ds5 = s5_d.shape[0]
    dml = ml_norm_g.shape[0]
    ngroups = ds5 // S5_GROUP
    rows = seq // GRID_W
    n = bsz * seq
    cap = EC_FACTOR * seq // N_EXPERTS
    assert cap >= COMB_W and seq % TOK_BLK == 0 and lctx % ML_T == 0 and seq % ML_T == 0

    cond = jnp.concatenate([c, c_ctx[None], jnp.zeros((8 - bsz - 1, d), F32)], axis=0)
    mod = _modulation(cond, ada_w, ada_b).reshape(8, N_MOD, d)
    sh1, sc1, g1, sh2, sc2, g2 = [mod[:, i] for i in range(N_MOD)]

    nmain = ds5 + 4 * dml
    w_main = w_in[:, :nmain].astype(BF16)
    w_gate = jnp.pad(w_in[:, nmain:], ((0, 0), (0, LANES - 4 * ML_HEADS))).astype(BF16)
    g0 = norm_g[0:1]
    px, gx = _in_projection(x.reshape(n, d), seq, g0, sc1[:bsz, None], sh1[:bsz, None], w_main, w_gate)
    pc, gc = _in_projection(ctx.reshape(bsz * lctx, d), bsz * lctx, g0, sc1[bsz:bsz + 1, None],
                            sh1[bsz:bsz + 1, None], w_main, w_gate)

    m, bp, cp, avec = _s5_matrices(s5_a_re, s5_a_im, s5_log_dt, s5_b_re, s5_b_im, s5_c_re, s5_c_im)
    npair = ngroups // 2

    def to_pairs(p, length):
        u = p[:, :ds5].reshape(bsz, length // S5_T, S5_T, npair, 2, S5_GROUP)
        return u.transpose(3, 0, 1, 4, 2, 5).reshape(npair, bsz, length // S5_T, 2 * S5_T * S5_GROUP)

    cc, xc = lctx // S5_T, seq // S5_T
    u_pairs = jnp.concatenate([to_pairs(pc, lctx), to_pairs(px, seq)], axis=2)
    u_pairs = u_pairs.reshape(npair, bsz * (cc + xc), 2 * S5_T * S5_GROUP)
    y_pairs = _s5_scan(u_pairs, m, bp, cp, avec, bsz, cc, xc)
    ys = y_pairs.reshape(npair, bsz, xc, 2, S5_T, S5_GROUP).transpose(1, 2, 4, 0, 3, 5).reshape(n, ds5)

    def colmajor(t):
        return t.reshape(bsz, rows, GRID_W, -1).swapaxes(1, 2).reshape(bsz, seq, -1)

    ml = jnp.concatenate([pc[:, ds5:].reshape(bsz, lctx, 4 * dml), colmajor(px[:, ds5:])], axis=1)
    gates = jnp.concatenate([gc[:, :16].reshape(bsz, lctx, 16), colmajor(gx[:, :16])], axis=1)
    ltot = lctx + seq
    conv_w8 = jnp.pad(ml_conv_w, ((0, 8 - CONV_K), (0, 0)))
    qscale = jnp.concatenate([jnp.full((1, dml), (dml // ML_HEADS) ** -0.5, F32),
                              jnp.ones((1, dml), F32)], axis=1)
    qk = _conv_silu(ml, conv_w8, ml_conv_b[None], qscale, (0, lctx), (lctx, ltot))
    gbias = ml_gate_b.reshape(1, 4 * ML_HEADS)
    hf, hb = _mlstm_scan(qk, ml, gates, gates.swapaxes(1, 2), gbias, gbias.reshape(4 * ML_HEADS, 1),
                         lctx // ML_T)

    def rowmajor(t):
        return t[:, lctx:].reshape(bsz, GRID_W, rows, -1).swapaxes(1, 2).reshape(n, -1)

    mod3 = jnp.stack([g1[:bsz], sc2[:bsz], sh2[:bsz]], axis=1)
    mod3 = jnp.concatenate([mod3, jnp.zeros((bsz, 5, d), F32)], axis=1)
    rw = jnp.pad(router_w, ((0, 0), (0, LANES - N_EXPERTS))).astype(BF16)
    x1, hx2, aff = _mixer_out(ys, px, rowmajor(hf), rowmajor(hb), x.reshape(n, d), seq, s5_d[None],
                              s5_glu_w.astype(BF16), s5_glu_b[None], ml_norm_g[None],
                              w_out.astype(BF16), norm_g, mod3, rw)

    nb = seq // TOK_BLK
    aff16 = aff[:, :N_EXPERTS].reshape(bsz, seq, N_EXPERTS)
    aff_t = aff16.swapaxes(1, 2).reshape(bsz, N_EXPERTS, nb, LANES)
    pos, off = _select(aff_t, cap)
    off_flat = off[..., 0].reshape(-1)
    pos_t = pos.reshape(bsz, N_EXPERTS, seq).swapaxes(1, 2)
    xs = _dispatch(off_flat, hx2.reshape(bsz, seq, d), pos, cap)
    ye = _expert_ffn(xs, exp_w_gate, exp_w_up, exp_w_down)
    moe = _combine(off_flat, ye, pos_t, aff16, cap)
    out = _final(x1, moe.reshape(n, d), norm_g[3:4], g2[:bsz, None], seq)
    return out.reshape(bsz, seq, d)


def kernel(x, c, ctx, c_ctx, ada_w, ada_b, norm_g, w_in, s5_a_re, s5_a_im, s5_log_dt, s5_b_re, s5_b_im, s5_c_re, s5_c_im, s5_d, s5_glu_w, s5_glu_b, ml_conv_w, ml_conv_b, ml_gate_b, ml_norm_g, w_out, router_w, exp_w_gate, exp_w_up, exp_w_down):
    depth = ada_w.shape[0]
    assert depth == 1, "context outputs are only produced when another layer follows"
    li = 0
    return _layer(x, c, ctx, c_ctx, ada_w[li], ada_b[li], norm_g[li], w_in[li], s5_a_re[li],
                  s5_a_im[li], s5_log_dt[li], s5_b_re[li], s5_b_im[li], s5_c_re[li], s5_c_im[li],
                  s5_d[li], s5_glu_w[li], s5_glu_b[li], ml_conv_w[li], ml_conv_b[li], ml_gate_b[li],
                  ml_norm_g[li], w_out[li], router_w[li], exp_w_gate[li], exp_w_up[li],
                  exp_w_down[li])
```

```python
import functools
import math

import jax
import jax.numpy as jnp
from jax import lax
from jax.experimental import pallas as pl
from jax.experimental.pallas import tpu as pltpu

F32 = jnp.float32
BF16 = jnp.bfloat16
I32 = jnp.int32

EPS = 1e-6
GRID_W = 64
S5_GROUP = 16
S5_STATE = 64
S5_T = 16
ML_HEADS = 4
ML_T = 256
CONV_K = 5
N_EXPERTS = 16
EC_FACTOR = 2
N_MOD = 6
LANES = 128
TOK_BLK = 128
DISP_W = 136
COMB_W = 144
VMEM_LIMIT = 56 * 1024 * 1024


def _cparams(sem):
    return pltpu.CompilerParams(dimension_semantics=sem, vmem_limit_bytes=VMEM_LIMIT)


def _sigmoid(x):
    return 1.0 / (1.0 + jnp.exp(-x))


def _silu(x):
    return x * _sigmoid(x)


def _log_sigmoid(x):
    return jnp.minimum(x, 0.0) - jnp.log(1.0 + jnp.exp(-jnp.abs(x)))


def _split3(x):
    x1 = x.astype(BF16)
    r = x - x1.astype(F32)
    x2 = r.astype(BF16)
    x3 = (r - x2.astype(F32)).astype(BF16)
    return x1, x2, x3


def _dot(a, b):
    return jnp.dot(a, b, preferred_element_type=F32)


def _dot_nt(a, b):
    return lax.dot_general(a, b, (((1,), (1,)), ((), ())), preferred_element_type=F32)


def _dot_tn(a, b):
    return lax.dot_general(a, b, (((0,), (0,)), ((), ())), preferred_element_type=F32)


def _dot3_left(a_f32, b_bf16):
    a1, a2, a3 = _split3(a_f32)
    return _dot(a1, b_bf16) + _dot(a2, b_bf16) + _dot(a3, b_bf16)


def _dot3_right(a_bf16, b_f32):
    b1, b2, b3 = _split3(b_f32)
    return _dot(a_bf16, b1) + _dot(a_bf16, b2) + _dot(a_bf16, b3)


def _mod_kernel(c_ref, w_ref, b_ref, o_ref):
    c = c_ref[...]
    s = _silu(c).astype(BF16)
    o_ref[...] = _dot(s, w_ref[...].astype(BF16)) + b_ref[...]


def _modulation(cond8, ada_w, ada_b):
    d, n = ada_w.shape
    tn = 1024
    return pl.pallas_call(
        _mod_kernel,
        grid=(n // tn,),
        in_specs=[pl.BlockSpec((8, d), lambda j: (0, 0)),
                  pl.BlockSpec((d, tn), lambda j: (0, j)),
                  pl.BlockSpec((1, tn), lambda j: (0, j))],
        out_specs=pl.BlockSpec((8, tn), lambda j: (0, j)),
        out_shape=jax.ShapeDtypeStruct((8, n), F32),
        compiler_params=_cparams(("parallel",)),
        name="modulation",
    )(cond8, ada_w, ada_b.reshape(1, n))


def _norm_mod(x, g, sc, sh):
    ms = jnp.mean(x * x, axis=-1, keepdims=True)
    h = x * lax.rsqrt(ms + EPS) * g
    return (h * (1.0 + sc) + sh).astype(BF16)


def _inproj_s5_kernel(x_ref, g_ref, sc_ref, sh_ref, w_ref, o_ref, h_scr):
    @pl.when(pl.program_id(2) == 0)
    def _():
        h_scr[...] = _norm_mod(x_ref[0], g_ref[...], sc_ref[0], sh_ref[0])

    o_ref[0] = _dot(h_scr[...], w_ref[...])


def _in_projection_s5(x, g0, sc, sh, mod_of_batch, w_s5):
    bsz, length, d = x.shape
    ds5 = w_s5.shape[1]
    tm = min(1024, length)
    tn = 512
    return pl.pallas_call(
        _inproj_s5_kernel,
        grid=(bsz, length // tm, ds5 // tn),
        in_specs=[pl.BlockSpec((1, tm, d), lambda b, i, j: (b, i, 0)),
                  pl.BlockSpec((1, d), lambda b, i, j: (0, 0)),
                  pl.BlockSpec((1, 1, d), lambda b, i, j: (mod_of_batch(b), 0, 0)),
                  pl.BlockSpec((1, 1, d), lambda b, i, j: (mod_of_batch(b), 0, 0)),
                  pl.BlockSpec((d, tn), lambda b, i, j: (0, j))],
        out_specs=pl.BlockSpec((1, tm, tn), lambda b, i, j: (b, i, j)),
        out_shape=jax.ShapeDtypeStruct((bsz, length, ds5), F32),
        scratch_shapes=[pltpu.VMEM((tm, d), BF16)],
        compiler_params=_cparams(("parallel", "parallel", "arbitrary")),
        name="in_projection_s5",
    )(x, g0, sc, sh, w_s5)


def _inproj_ml_kernel(x_ref, g_ref, sc_ref, sh_ref, w_ref, wg_ref, *rest, colmajor):
    o_ref, og_ref, h_scr, x_scr = rest[-4:]
    rb = o_ref.shape[3]
    nlb = x_ref.shape[2] // LANES

    @pl.when(pl.program_id(2) == 0)
    def _():
        if colmajor:
            for c in range(nlb):
                for r in range(rb):
                    x_scr[c, r * ROW_PITCH:r * ROW_PITCH + GRID_W, :] = (
                        x_ref[0, r * GRID_W:(r + 1) * GRID_W, c * LANES:(c + 1) * LANES])
            for w in range(GRID_W):
                xw = jnp.concatenate([x_scr[c, pl.ds(w, rb, stride=ROW_PITCH), :] for c in range(nlb)],
                                     axis=-1)
                h_scr[w * rb:(w + 1) * rb, :] = _norm_mod(xw, g_ref[...], sc_ref[0], sh_ref[0])
        else:
            h_scr[...] = _norm_mod(x_ref[0], g_ref[...], sc_ref[0], sh_ref[0])
        og_ref[0] = _dot(h_scr[...], wg_ref[...]).reshape(og_ref.shape[1:])

    o_ref[0] = _dot(h_scr[...], w_ref[...]).reshape(o_ref.shape[1:]).astype(o_ref.dtype)


ML_RB = 16
ROW_PITCH = GRID_W + 8


def _in_projection_ml(x, g0, sc, sh, mod_of_batch, w_ml, w_gate, n_slots, ctx_into=None):
    bsz, length, d = x.shape
    nml = w_ml.shape[1]
    tn = 512
    if ctx_into is None:
        rows = length // GRID_W
        rblocks = rows // ML_RB
        tm = ML_RB * GRID_W
        grid = (bsz, rblocks, nml // tn)
        oblk = (1, GRID_W, 1, ML_RB, tn)
        gblk = (1, GRID_W, 1, ML_RB, LANES)
        omap = lambda b, i, j: (b, 0, i, 0, j)
        gmap = lambda b, i, j: (b, 0, i, 0, 0)
        extra_in, extra_specs, aliases = [], [], {}
    else:
        ml_arr, g_arr, first_slot = ctx_into
        rblocks = ml_arr.shape[2]
        tm = length
        nslot = length // (rblocks * ML_RB)
        assert nslot * rblocks * ML_RB == length and first_slot % nslot == 0
        grid = (bsz, 1, nml // tn)
        oblk = (1, nslot, rblocks, ML_RB, tn)
        gblk = (1, nslot, rblocks, ML_RB, LANES)
        omap = lambda b, i, j: (b, first_slot // nslot, 0, 0, j)
        gmap = lambda b, i, j: (b, first_slot // nslot, 0, 0, 0)
        extra_in = [ml_arr, g_arr]
        extra_specs = [pl.BlockSpec(memory_space=pl.ANY)] * 2
        aliases = {6: 0, 7: 1}
    kern = functools.partial(_inproj_ml_kernel, colmajor=ctx_into is None)
    return pl.pallas_call(
        kern,
        grid=grid,
        in_specs=[pl.BlockSpec((1, tm, d), lambda b, i, j: (b, i, 0)),
                  pl.BlockSpec((1, d), lambda b, i, j: (0, 0)),
                  pl.BlockSpec((1, 1, d), lambda b, i, j: (mod_of_batch(b), 0, 0)),
                  pl.BlockSpec((1, 1, d), lambda b, i, j: (mod_of_batch(b), 0, 0)),
                  pl.BlockSpec((d, tn), lambda b, i, j: (0, j)),
                  pl.BlockSpec((d, LANES), lambda b, i, j: (0, 0))] + extra_specs,
        out_specs=[pl.BlockSpec(oblk, omap), pl.BlockSpec(gblk, gmap)],
        out_shape=[jax.ShapeDtypeStruct((bsz, n_slots, rblocks, ML_RB, nml), BF16),
                   jax.ShapeDtypeStruct((bsz, n_slots, rblocks, ML_RB, LANES), F32)],
        scratch_shapes=[pltpu.VMEM((tm, d), BF16),
                        pltpu.VMEM((d // LANES, ML_RB * ROW_PITCH, LANES) if ctx_into is None
                                   else (1, 8, LANES), F32)],
        input_output_aliases=aliases,
        compiler_params=_cparams(("parallel", "parallel", "arbitrary")),
        name="in_projection_ml" if ctx_into is None else "in_projection_ml_ctx",
    )(x, g0, sc, sh, w_ml, w_gate, *extra_in)


def _s5_matrices(a_re, a_im, log_dt, b_re, b_im, c_re, c_im):
    t = S5_T
    g = a_re.shape[1]
    hi = lax.Precision.HIGHEST
    lr, li = jnp.minimum(a_re.astype(F32), -1e-4), a_im.astype(F32)
    dt = jnp.exp(log_dt.astype(F32))[..., None]
    zr, zi = lr * dt, li * dt
    tau = jnp.arange(t + 1, dtype=F32)[None, :, None, None]
    mag, ang = jnp.exp(zr[:, None] * tau), zi[:, None] * tau
    pr, pi = mag * jnp.cos(ang), mag * jnp.sin(ang)
    xr, xi = pr[:, 1] - 1.0, pi[:, 1]
    den = lr * lr + li * li
    qr, qi = (xr * lr + xi * li) / den, (xi * lr - xr * li) / den
    br, bi = b_re.astype(F32)[None], b_im.astype(F32)[None]
    bbr = qr[..., None] * br - qi[..., None] * bi
    bbi = qr[..., None] * bi + qi[..., None] * br
    cr, ci = c_re.astype(F32)[None, None], c_im.astype(F32)[None, None]
    cpr = cr * pr[:, :, :, None, :] - ci * pi[:, :, :, None, :]
    cpi = cr * pi[:, :, :, None, :] + ci * pr[:, :, :, None, :]
    kern = (jnp.einsum("dtgcp,dgpk->dtgck", cpr[:, :t], bbr, precision=hi)
            - jnp.einsum("dtgcp,dgpk->dtgck", cpi[:, :t], bbi, precision=hi))
    ti = jnp.arange(t)
    lag_f = ti[None, :] - ti[:, None]
    kf = kern[0][jnp.clip(lag_f, 0, t - 1)]
    kf = jnp.where((lag_f >= 0)[:, :, None, None, None], kf, 0.0)
    kb = kern[1][jnp.clip(-lag_f, 0, t - 1)]
    kb = jnp.where((lag_f <= 0)[:, :, None, None, None], kb, 0.0)
    m = (kf + kb).transpose(2, 0, 4, 1, 3).reshape(g, t * S5_GROUP, t * S5_GROUP)

    def inject(d, expo):
        er, ei = pr[d][expo][..., None], pi[d][expo][..., None]
        return er * bbr[d][None] - ei * bbi[d][None], er * bbi[d][None] + ei * bbr[d][None]

    def readout(d, expo):
        return cpr[d][expo], -cpi[d][expo]

    def inj_cols(z):
        return z.transpose(1, 0, 3, 2).reshape(g, t * S5_GROUP, S5_STATE)

    def out_rows(z):
        return z.transpose(1, 3, 0, 2).reshape(g, S5_STATE, t * S5_GROUP)

    npair = g // 2
    zi_ = jnp.zeros((npair, t * S5_GROUP, S5_STATE), F32)
    zo_ = jnp.zeros((npair, S5_STATE, t * S5_GROUP), F32)

    def pair_inj(z):
        z = z.reshape(npair, 2, t * S5_GROUP, S5_STATE)
        return jnp.concatenate([jnp.concatenate([z[:, 0], zi_], axis=-1),
                                jnp.concatenate([zi_, z[:, 1]], axis=-1)], axis=1)

    def pair_out(z):
        z = z.reshape(npair, 2, S5_STATE, t * S5_GROUP)
        return jnp.concatenate([jnp.concatenate([z[:, 0], zo_], axis=-1),
                                jnp.concatenate([zo_, z[:, 1]], axis=-1)], axis=1)

    jfr, jfi = inject(0, t - 1 - ti)
    jbr, jbi = inject(1, ti)
    bp = jnp.concatenate([pair_inj(inj_cols(z)) for z in (jfr, jfi, jbr, jbi)], axis=-1)
    ofr, ofi = readout(0, ti + 1)
    obr, obi = readout(1, t - ti)
    cp = jnp.concatenate([pair_out(out_rows(z)) for z in (ofr, ofi, obr, obi)], axis=1)
    avec = jnp.stack([pr[0, t], pi[0, t], pr[1, t], pi[1, t]], axis=1)
    avec = avec.reshape(npair, 2, 4, S5_STATE).transpose(0, 2, 1, 3).reshape(npair, 4, 2 * S5_STATE)
    avec = jnp.concatenate([avec, jnp.zeros((npair, 4, 2 * S5_STATE), F32)], axis=1)
    return (m.reshape(npair, 2, t * S5_GROUP, t * S5_GROUP).astype(BF16),
            bp.astype(BF16), cp.astype(BF16), avec)


def _s5_permutation():
    i = jnp.arange(8 * 8 * S5_GROUP)
    t8, j, c = i // (8 * S5_GROUP), (i // S5_GROUP) % 8, i % S5_GROUP
    dst = j * (8 * S5_GROUP) + t8 * S5_GROUP + c
    e1 = (dst[:, None] == i[None, :]).astype(BF16)
    return e1, e1.T


S5_PAIRS = 4


def _s5_kernel(ux_ref, uc_ref, e1_ref, e1t_ref, m_ref, bp_ref, cp_ref, av_ref, y_ref, s_scr, hp_scr,
               *, cc, xc):
    nc = cc + xc

    def chunk_rows(ref, nchunks):
        cols = [ref[0, pl.ds(t, nchunks, stride=S5_T), :].astype(BF16) for t in range(S5_T)]
        return [_dot(jnp.concatenate(cols[8 * h:8 * h + 8], axis=-1), e1_ref[...]).astype(BF16)
                for h in range(2)]

    def pair_lanes(halves, p):
        return jnp.concatenate([halves[h][:, j * LANES:(j + 1) * LANES]
                                for j in (2 * p, 2 * p + 1) for h in range(2)], axis=-1)

    pc = chunk_rows(uc_ref, cc)
    px = chunk_rows(ux_ref, xc)
    for p in range(S5_PAIRS):
        s_scr[0:cc, p * 512:(p + 1) * 512] = _dot(pair_lanes(pc, p), bp_ref[p])
        s_scr[cc:nc, p * 512:(p + 1) * 512] = _dot(pair_lanes(px, p), bp_ref[p])

    av = av_ref[...]
    sub = lax.broadcasted_iota(I32, (8, 128), 0)

    def scan8(s_re, s_im, h_re, h_im, ar, ai, order):
        p_re = jnp.zeros((8, 128), F32)
        p_im = jnp.zeros((8, 128), F32)
        for r in order:
            p_re = jnp.where(sub == r, h_re, p_re)
            p_im = jnp.where(sub == r, h_im, p_im)
            h_re, h_im = (ar * h_re - ai * h_im + s_re[r:r + 1],
                          ar * h_im + ai * h_re + s_im[r:r + 1])
        return p_re, p_im, h_re, h_im

    def step(row_f, row_b, carry):
        rf = pl.multiple_of(row_f, 8)
        rb = pl.multiple_of(row_b, 8)
        new = []
        for p in range(S5_PAIRS):
            hr, hi, gr, gi = carry[4 * p:4 * p + 4]
            c0 = p * 512
            p_re, p_im, hr, hi = scan8(s_scr[pl.ds(rf, 8), c0:c0 + 128], s_scr[pl.ds(rf, 8), c0 + 128:c0 + 256],
                                       hr, hi, av[p, 0:1], av[p, 1:2], range(8))
            hp_scr[pl.ds(rf, 8), c0:c0 + 128] = p_re
            hp_scr[pl.ds(rf, 8), c0 + 128:c0 + 256] = p_im
            p_re, p_im, gr, gi = scan8(s_scr[pl.ds(rb, 8), c0 + 256:c0 + 384], s_scr[pl.ds(rb, 8), c0 + 384:c0 + 512],
                                       gr, gi, av[p, 2:3], av[p, 3:4], range(7, -1, -1))
            hp_scr[pl.ds(rb, 8), c0 + 256:c0 + 384] = p_re
            hp_scr[pl.ds(rb, 8), c0 + 384:c0 + 512] = p_im
            new += [hr, hi, gr, gi]
        return tuple(new)

    z = jnp.zeros((1, 128), F32)
    carry = lax.fori_loop(0, cc // 8, lambda i, c: step(8 * i, cc - 8 - 8 * i, c), (z,) * (4 * S5_PAIRS))
    lax.fori_loop(0, xc // 8, lambda i, c: step(cc + 8 * i, nc - 8 - 8 * i, c), carry)

    ys = []
    for p in range(S5_PAIRS):
        ux = pair_lanes(px, p)
        y = _dot(hp_scr[cc:nc, p * 512:(p + 1) * 512].astype(BF16), cp_ref[p])
        ys.append(y + jnp.concatenate([_dot(ux[:, 0:256], m_ref[p, 0]), _dot(ux[:, 256:512], m_ref[p, 1])],
                                      axis=-1))
    for h in range(2):
        yh = jnp.concatenate([ys[j // 2][:, (j % 2) * 256 + h * LANES:(j % 2) * 256 + (h + 1) * LANES]
                              for j in range(8)], axis=-1).astype(BF16)
        r = _dot(yh, e1t_ref[...])
        for t8 in range(8):
            y_ref[0, pl.ds(8 * h + t8, xc, stride=S5_T), :] = r[:, t8 * LANES:(t8 + 1) * LANES]


def _s5_scan(ux, uc, m, bp, cp, avec):
    bsz, seq, ds5 = ux.shape
    lctx = uc.shape[1]
    cc, xc = lctx // S5_T, seq // S5_T
    assert cc % 8 == 0 and xc % 8 == 0
    e1, e1t = _s5_permutation()
    kern = functools.partial(_s5_kernel, cc=cc, xc=xc)
    nblk = ds5 // LANES
    const = lambda j, b: (0, 0)
    return pl.pallas_call(
        kern,
        grid=(nblk, bsz),
        in_specs=[pl.BlockSpec((1, seq, LANES), lambda j, b: (b, 0, j)),
                  pl.BlockSpec((1, lctx, LANES), lambda j, b: (b, 0, j)),
                  pl.BlockSpec(e1.shape, const),
                  pl.BlockSpec(e1.shape, const),
                  pl.BlockSpec((S5_PAIRS, 2, 256, 256), lambda j, b: (j, 0, 0, 0)),
                  pl.BlockSpec((S5_PAIRS, 512, 512), lambda j, b: (j, 0, 0)),
                  pl.BlockSpec((S5_PAIRS, 512, 512), lambda j, b: (j, 0, 0)),
                  pl.BlockSpec((S5_PAIRS, 8, 128), lambda j, b: (j, 0, 0))],
        out_specs=pl.BlockSpec((1, seq, LANES), lambda j, b: (b, 0, j)),
        out_shape=jax.ShapeDtypeStruct((bsz, seq, ds5), F32),
        scratch_shapes=[pltpu.VMEM((cc + xc, S5_PAIRS * 512), F32),
                        pltpu.VMEM((cc + xc, S5_PAIRS * 512), F32)],
        compiler_params=_cparams(("parallel", "parallel")),
        name="s5_scan",
    )(ux, uc, e1, e1t, m, bp, cp, avec)


def _conv_kernel(x_ref, p_ref, n_ref, w_ref, b_ref, sc_ref, o_ref, scr, *, rows, seq_tiles):
    i = pl.program_id(1)
    first = functools.reduce(jnp.logical_or, [i == s for s in seq_tiles[0]])
    last = functools.reduce(jnp.logical_or, [i == s for s in seq_tiles[1]])
    pm = jnp.where(first, 0.0, 1.0)
    nm = jnp.where(last, 0.0, 1.0)
    scr[0:8, :] = p_ref[0, 8:16, :].astype(F32) * pm
    scr[8:8 + rows, :] = x_ref[0].astype(F32)
    scr[8 + rows:16 + rows, :] = n_ref[0, 0:8, :].astype(F32) * nm
    w = w_ref[...]
    acc = scr[6:6 + rows, :] * w[0:1]
    for j in range(1, CONV_K):
        acc = acc + scr[6 + j:6 + j + rows, :] * w[j:j + 1]
    z = acc + b_ref[...]
    o_ref[0] = (_silu(z) * sc_ref[...]).astype(o_ref.dtype)


def _conv_silu(ml, conv_w8, conv_b, scale, seq_starts, seq_ends):
    bsz, ltot, _ = ml.shape
    rows = 256
    ct = 512
    nch = conv_b.shape[-1]
    nt = ltot // rows
    hb = rows // 16
    nhb = ltot // 16
    seq_tiles = (tuple(s // rows for s in seq_starts), tuple(e // rows - 1 for e in seq_ends))
    kern = functools.partial(_conv_kernel, rows=rows, seq_tiles=seq_tiles)
    return pl.pallas_call(
        kern,
        grid=(bsz, nt, nch // ct),
        in_specs=[pl.BlockSpec((1, rows, ct), lambda b, i, c: (b, i, c)),
                  pl.BlockSpec((1, 16, ct), lambda b, i, c: (b, jnp.maximum(i * hb - 1, 0), c)),
                  pl.BlockSpec((1, 16, ct), lambda b, i, c: (b, jnp.minimum((i + 1) * hb, nhb - 1), c)),
                  pl.BlockSpec((8, ct), lambda b, i, c: (0, c)),
                  pl.BlockSpec((1, ct), lambda b, i, c: (0, c)),
                  pl.BlockSpec((1, ct), lambda b, i, c: (0, c))],
        out_specs=pl.BlockSpec((1, rows, ct), lambda b, i, c: (b, i, c)),
        out_shape=jax.ShapeDtypeStruct((bsz, ltot, nch), BF16),
        scratch_shapes=[pltpu.VMEM((rows + 16, ct), F32)],
        compiler_params=_cparams(("parallel", "parallel", "parallel")),
        name="conv_silu",
    )(ml, ml, ml, conv_w8, conv_b, scale)


def _mlstm_chain_step(q, k, v, i_col, f_col, i_row, f_row, b_col, b_row, c_ref, n_ref, m_ref,
                      idx, backward):
    t = q.shape[0]
    rid = lax.broadcasted_iota(I32, (t, t), 0)
    cid = lax.broadcasted_iota(I32, (t, t), 1)
    mask = (cid >= rid) if backward else (rid >= cid)
    m_st = m_ref[idx]
    d = jnp.where(mask, b_col - b_row + i_row, -jnp.inf)
    inter = b_col + m_st
    m_t = jnp.maximum(inter, jnp.max(d, axis=-1, keepdims=True))
    w = jnp.exp(d - m_t)
    s_inter = jnp.exp(inter - m_t)
    s = _dot_nt(q, k) * w
    c_st = c_ref[idx]
    n_st = n_ref[idx]
    num = _dot(s.astype(BF16), v) + s_inter * _dot_nt(q, c_st.astype(BF16))
    qf = q.astype(F32)
    den = jnp.sum(s, axis=-1, keepdims=True) + s_inter * jnp.sum(qf * n_st, axis=-1, keepdims=True)
    h = num / jnp.maximum(jnp.abs(den), jnp.exp(-m_t))
    total = jnp.sum(f_col, axis=0, keepdims=True)
    d_end = total - b_col + i_col
    m_new = jnp.maximum(total + m_st, jnp.max(d_end, axis=0, keepdims=True))
    w_end = jnp.exp(d_end - m_new)
    dec = jnp.exp(total + m_st - m_new)
    vw = (v.astype(F32) * w_end).astype(BF16)
    c_ref[idx] = dec * c_st + _dot_tn(vw, k)
    n_ref[idx] = dec * n_st + jnp.sum(k.astype(F32) * w_end, axis=0, keepdims=True)
    m_ref[idx] = m_new
    return h


def _mlstm_kernel(qf_ref, kf_ref, vf_ref, gf_ref, gtf_ref, qb_ref, kb_ref, vb_ref, gb_ref, gtb_ref,
                  gbias_ref, gbias_t_ref, hf_ref, hb_ref, c_scr, n_scr, m_scr, *, dh):
    @pl.when(pl.program_id(1) == 0)
    def _():
        c_scr[...] = jnp.zeros_like(c_scr)
        n_scr[...] = jnp.zeros_like(n_scr)
        m_scr[...] = jnp.zeros_like(m_scr)

    t = qf_ref.shape[1]
    rid = lax.broadcasted_iota(I32, (t, t), 0)
    cid = lax.broadcasted_iota(I32, (t, t), 1)
    lower = jnp.where(rid >= cid, 1.0, 0.0).astype(BF16)
    upper = jnp.where(cid >= rid, 1.0, 0.0).astype(BF16)

    for d, (q_ref, k_ref, v_ref, g_ref, gt_ref, h_ref) in enumerate(
            [(qf_ref, kf_ref, vf_ref, gf_ref, gtf_ref, hf_ref),
             (qb_ref, kb_ref, vb_ref, gb_ref, gtb_ref, hb_ref)]):
        backward = d == 1
        g = g_ref[0, :, 0:16] + gbias_ref[...]
        gt = gt_ref[0] + gbias_t_ref[...]
        lf = _log_sigmoid(g)
        lft = _log_sigmoid(gt)
        cum_col = _dot3_right(upper if backward else lower, lf)
        cum_row = _dot3_left(lft, lower if backward else upper)
        for hd in range(ML_HEADS):
            ci = 2 * ML_HEADS * d + hd
            cf = ci + ML_HEADS
            sl = slice(hd * dh, (hd + 1) * dh)
            h = _mlstm_chain_step(
                q_ref[0, :, sl], k_ref[0, :, sl], v_ref[0, :, sl],
                g[:, ci:ci + 1], lf[:, cf:cf + 1], gt[ci:ci + 1, :], lft[cf:cf + 1, :],
                cum_col[:, cf:cf + 1], cum_row[cf:cf + 1, :],
                c_scr, n_scr, m_scr, d * ML_HEADS + hd, backward)
            h_ref[0, :, sl] = h.astype(h_ref.dtype)


def _mlstm_scan(qk, ml, gates, gates_t, gbias, gbias_t, n_ctx_chunks):
    bsz, ltot, dml2 = qk.shape
    dml = dml2 // 2
    dh = dml // ML_HEADS
    t = ML_T
    nchunk = ltot // t
    cc = n_ctx_chunks
    nx = nchunk - cc

    def fwd(s):
        return jnp.where(s < cc, nx + s, s - cc)

    def bwd(s):
        return jnp.where(s < cc, nx + cc - 1 - s, nx - 1 - (s - cc))

    def specs(cmap):
        return [pl.BlockSpec((1, t, dml), lambda b, s: (b, cmap(s), 0)),
                pl.BlockSpec((1, t, dml), lambda b, s: (b, cmap(s), 1)),
                pl.BlockSpec((1, t, dml), lambda b, s: (b, cmap(s), 2)),
                pl.BlockSpec((1, t, LANES), lambda b, s: (b, cmap(s), 0)),
                pl.BlockSpec((1, 16, t), lambda b, s: (b, 0, cmap(s)))]

    kern = functools.partial(_mlstm_kernel, dh=dh)
    nchain = 2 * ML_HEADS
    return pl.pallas_call(
        kern,
        grid=(bsz, nchunk),
        in_specs=specs(fwd) + specs(bwd) + [pl.BlockSpec((1, 16), lambda b, s: (0, 0)),
                                            pl.BlockSpec((16, 1), lambda b, s: (0, 0))],
        out_specs=[pl.BlockSpec((1, t, dml), lambda b, s: (b, fwd(s), 0)),
                   pl.BlockSpec((1, t, dml), lambda b, s: (b, bwd(s), 0))],
        out_shape=[jax.ShapeDtypeStruct((bsz, ltot, dml), BF16)] * 2,
        scratch_shapes=[pltpu.VMEM((nchain, dh, dh), F32),
                        pltpu.VMEM((nchain, 1, dh), F32),
                        pltpu.VMEM((nchain, 1, 1), F32)],
        compiler_params=_cparams(("parallel", "arbitrary")),
        name="mlstm_scan",
    )(qk, qk, ml, gates, gates_t, qk, qk, ml, gates, gates_t, gbias, gbias_t)


COL_PITCH = ML_RB + 8


def _mlpost_kernel(hf_ref, hb_ref, o_ref, mg_ref, out_ref, scr, *, dh):
    nw, _, rb, dml = hf_ref.shape[1:]
    h = (hf_ref[0].astype(F32) + hb_ref[0].astype(F32)).reshape(nw * rb, dml)
    parts = []
    for hd in range(ML_HEADS):
        hh = h[:, hd * dh:(hd + 1) * dh]
        parts.append(hh * lax.rsqrt(jnp.mean(hh * hh, axis=-1, keepdims=True) + EPS))
    hn = jnp.concatenate(parts, axis=-1) * mg_ref[...]
    res = hn * _sigmoid(o_ref[0].astype(F32).reshape(nw * rb, dml))
    nlb = dml // LANES
    for c in range(nlb):
        for w in range(nw):
            scr[c, w * COL_PITCH:w * COL_PITCH + rb, :] = res[w * rb:(w + 1) * rb, c * LANES:(c + 1) * LANES]
    for r in range(rb):
        row = jnp.concatenate([scr[c, pl.ds(r, nw, stride=COL_PITCH), :] for c in range(nlb)], axis=-1)
        out_ref[0, r * nw:(r + 1) * nw, :] = row.astype(out_ref.dtype)


def _ml_post(hf5, hb5, ml5, ml_g, seq):
    bsz, _, rblocks, rb, dml = hf5.shape
    kern = functools.partial(_mlpost_kernel, dh=dml // ML_HEADS)
    blk = (1, GRID_W, 1, rb, dml)
    return pl.pallas_call(
        kern,
        grid=(bsz, rblocks),
        in_specs=[pl.BlockSpec(blk, lambda b, i: (b, 0, i, 0, 0)),
                  pl.BlockSpec(blk, lambda b, i: (b, 0, i, 0, 0)),
                  pl.BlockSpec(blk, lambda b, i: (b, 0, i, 0, 3)),
                  pl.BlockSpec((1, dml), lambda b, i: (0, 0))],
        out_specs=pl.BlockSpec((1, rb * GRID_W, dml), lambda b, i: (b, i, 0)),
        out_shape=jax.ShapeDtypeStruct((bsz, seq, dml), BF16),
        scratch_shapes=[pltpu.VMEM((dml // LANES, GRID_W * COL_PITCH, LANES), F32)],
        compiler_params=_cparams(("parallel", "parallel")),
        name="mlstm_post",
    )(hf5, hb5, ml5, ml_g)


def _gelu_tanh(y):
    return 0.5 * y * (1.0 + jnp.tanh(math.sqrt(2.0 / math.pi) * (y + 0.044715 * (y * y * y))))


def _mixout_kernel(ys_ref, u_ref, ml_ref, x_ref, dsk_ref, gw_ref, gb_ref, wo_ref, ng_ref, mod_ref,
                   rwh_ref, rwl_ref, x1_ref, hx_ref, aff_ref, *, n_exp):
    y = ys_ref[...] + dsk_ref[...] * u_ref[...]
    ge = _gelu_tanh(y)
    z = _dot(ge.astype(BF16), gw_ref[...]) + gb_ref[...]
    s5 = (ge * _sigmoid(z)).astype(BF16)
    yx = _dot(jnp.concatenate([s5, ml_ref[...]], axis=-1), wo_ref[...])
    ng = ng_ref[...]
    mod = mod_ref[0]
    r1 = yx * lax.rsqrt(jnp.mean(yx * yx, axis=-1, keepdims=True) + EPS) * ng[1:2]
    x1 = x_ref[...] + mod[0:1] * r1
    x1_ref[...] = x1
    hx = x1 * lax.rsqrt(jnp.mean(x1 * x1, axis=-1, keepdims=True) + EPS) * ng[2:3]
    hx = hx * (1.0 + mod[1:2]) + mod[2:3]
    hx_ref[...] = hx.astype(hx_ref.dtype)
    h1, h2, h3 = _split3(hx)
    rwh = rwh_ref[...]
    logits = _dot(h1, rwh) + _dot(h2, rwh) + _dot(h3, rwh) + _dot(h1, rwl_ref[...])
    lane = lax.broadcasted_iota(I32, logits.shape, 1)
    logits = jnp.where(lane < n_exp, logits, -jnp.inf)
    e = jnp.exp(logits - jnp.max(logits, axis=-1, keepdims=True))
    aff_ref[...] = e / jnp.sum(e, axis=-1, keepdims=True)


def _mixer_out(ys, u, mlo, x2d, rows_per_batch, dsk, glu_w, glu_b, w_out, norm_g, mod3, rw_hi, rw_lo):
    n, d = x2d.shape
    ds5 = ys.shape[1]
    dml = mlo.shape[1]
    tm = 256
    tiles_per_batch = rows_per_batch // tm
    kern = functools.partial(_mixout_kernel, n_exp=N_EXPERTS)
    row = lambda i: (i, 0)
    const = lambda i: (0, 0)
    return pl.pallas_call(
        kern,
        grid=(n // tm,),
        in_specs=[pl.BlockSpec((tm, ds5), row),
                  pl.BlockSpec((tm, ds5), row),
                  pl.BlockSpec((tm, dml), row),
                  pl.BlockSpec((tm, d), row),
                  pl.BlockSpec((1, ds5), const),
                  pl.BlockSpec((ds5, ds5), const),
                  pl.BlockSpec((1, ds5), const),
                  pl.BlockSpec((d, d), const),
                  pl.BlockSpec((4, d), const),
                  pl.BlockSpec((1, 8, d), lambda i: (i // tiles_per_batch, 0, 0)),
                  pl.BlockSpec((d, LANES), const),
                  pl.BlockSpec((d, LANES), const)],
        out_specs=[pl.BlockSpec((tm, d), row),
                   pl.BlockSpec((tm, d), row),
                   pl.BlockSpec((tm, LANES), row)],
        out_shape=[jax.ShapeDtypeStruct((n, d), F32),
                   jax.ShapeDtypeStruct((n, d), BF16),
                   jax.ShapeDtypeStruct((n, LANES), F32)],
        compiler_params=_cparams(("parallel",)),
        name="mixer_out",
    )(ys, u, mlo, x2d, dsk, glu_w, glu_b, w_out, norm_g, mod3, rw_hi, rw_lo)


def _count(mask):
    c = jnp.sum(jnp.where(mask, 1.0, 0.0), axis=2, keepdims=True)
    return jnp.sum(c, axis=1, keepdims=True)


SELECT_BISECTIONS = 48


def _pow2_neg(n):
    p = jnp.ones(n.shape, F32)
    for i in range(7):
        p = p * jnp.where(((n >> i) & 1) == 1, 2.0 ** -(2 ** i), 1.0)
    return p


def _select_kernel(aff_ref, pos_ref, off_ref, *, cap, n_tok):
    a = aff_ref[0]
    e, nb, _ = a.shape

    def enough(thr):
        return _count(a >= thr) >= cap

    ex_lo = jnp.full((e, 1, 1), -127, I32)
    ex_hi = jnp.full((e, 1, 1), 1, I32)
    for _ in range(7):
        ex_mid = (ex_lo + ex_hi) >> 1
        ok = enough(_pow2_neg(-ex_mid))
        ex_lo = jnp.where(ok, ex_mid, ex_lo)
        ex_hi = jnp.where(ok, ex_hi, ex_mid)
    lo = jnp.where(ex_lo == -127, 0.0, _pow2_neg(jnp.minimum(-ex_lo, 126)))
    hi = jnp.where(ex_hi == 1, 2.0, _pow2_neg(jnp.maximum(-ex_hi, 0)))
    for _ in range(SELECT_BISECTIONS):
        mid = lo + 0.5 * (hi - lo)
        ok = enough(mid)
        lo = jnp.where(ok, mid, lo)
        hi = jnp.where(ok, hi, mid)
    gt = a >= hi
    tie = (a >= lo) & (a < hi)
    need = cap - _count(gt)
    tok = (lax.broadcasted_iota(I32, a.shape, 1) * LANES + lax.broadcasted_iota(I32, a.shape, 2))
    cut = jnp.zeros((e, 1, 1), I32)
    for bit in range(n_tok.bit_length() - 1, -1, -1):
        cand = cut | (1 << bit)
        cut = jnp.where(_count(tie & (tok < cand)) < need, cand, cut)
    sel = gt | (tie & (tok <= cut))
    self32 = jnp.where(sel, 1.0, 0.0).reshape(e * nb, LANES)
    r = lax.broadcasted_iota(I32, (LANES, LANES), 0)
    c = lax.broadcasted_iota(I32, (LANES, LANES), 1)
    incl = _dot(self32.astype(BF16), jnp.where(r <= c, 1.0, 0.0).astype(BF16))
    tot = jnp.broadcast_to(incl[:, LANES - 1:LANES], (e * nb, LANES)).astype(BF16)
    rr = lax.broadcasted_iota(I32, (e * nb, e * nb), 0)
    cc = lax.broadcasted_iota(I32, (e * nb, e * nb), 1)
    same = (rr // nb) == (cc // nb)
    before = jnp.where(same & (cc < rr), 1.0, 0.0).astype(BF16)
    off = _dot(before, tot)
    pos = (off + incl - 1.0).astype(I32)
    pos_ref[0] = jnp.where(self32 > 0.5, pos, -1).reshape(e, nb, LANES)
    off_ref[0] = off.astype(I32).reshape(e, nb, LANES)


def _select(aff_t, cap):
    bsz, e, nb, _ = aff_t.shape
    kern = functools.partial(_select_kernel, cap=cap, n_tok=nb * LANES)
    blk = pl.BlockSpec((1, e, nb, LANES), lambda b: (b, 0, 0, 0))
    return pl.pallas_call(
        kern,
        grid=(bsz,),
        in_specs=[blk],
        out_specs=[blk, blk],
        out_shape=[jax.ShapeDtypeStruct(aff_t.shape, I32)] * 2,
        compiler_params=_cparams(("parallel",)),
        name="ec_select",
    )(aff_t)


def _dispatch_kernel(off_ref, hx_ref, pos_ref, o_ref, acc, *, cap, nb, n_exp):
    b = pl.program_id(0)
    ex = pl.program_id(2)
    acc[...] = jnp.zeros_like(acc)
    kiota = lax.broadcasted_iota(I32, (DISP_W, TOK_BLK), 0)

    def body(j, _):
        start = pl.multiple_of((off_ref[(b * n_exp + ex) * nb + j] // 8) * 8, 8)
        prow = pos_ref[0, 0, pl.ds(j, 1), :]
        onehot = jnp.where(prow - start == kiota, 1.0, 0.0).astype(BF16)
        xj = hx_ref[0, pl.ds(pl.multiple_of(j * TOK_BLK, TOK_BLK), TOK_BLK), :]
        acc[pl.ds(start, DISP_W), :] += _dot(onehot, xj)
        return 0

    lax.fori_loop(0, nb, body, 0)
    o_ref[0, 0] = acc[0:cap, :].astype(o_ref.dtype)


def _dispatch(off_flat, hx, pos, cap):
    bsz, n_tok, d = hx.shape
    n_exp, nb = pos.shape[1], pos.shape[2]
    dblk = 1024
    kern = functools.partial(_dispatch_kernel, cap=cap, nb=nb, n_exp=n_exp)
    return pl.pallas_call(
        kern,
        grid_spec=pltpu.PrefetchScalarGridSpec(
            num_scalar_prefetch=1,
            grid=(bsz, d // dblk, n_exp),
            in_specs=[pl.BlockSpec((1, n_tok, dblk), lambda b, c, e, off: (b, 0, c)),
                      pl.BlockSpec((1, 1, nb, LANES), lambda b, c, e, off: (b, e, 0, 0))],
            out_specs=pl.BlockSpec((1, 1, cap, dblk), lambda b, c, e, off: (b, e, 0, c)),
            scratch_shapes=[pltpu.VMEM((cap + DISP_W, dblk), F32)]),
        out_shape=jax.ShapeDtypeStruct((bsz, n_exp, cap, d), BF16),
        compiler_params=_cparams(("parallel", "parallel", "arbitrary")),
        name="ec_dispatch",
    )(off_flat, hx, pos)


def _ffn_kernel(x_ref, wg_ref, wu_ref, wd_ref, o_ref, acc):
    f = pl.program_id(2)

    @pl.when(f == 0)
    def _():
        acc[...] = jnp.zeros_like(acc)

    x = x_ref[0, 0]
    a = _dot(x, wg_ref[0].astype(BF16))
    u = _dot(x, wu_ref[0].astype(BF16))
    hmid = (_silu(a) * u).astype(BF16)
    acc[...] += _dot(hmid, wd_ref[0].astype(BF16))

    @pl.when(f == pl.num_programs(2) - 1)
    def _():
        o_ref[0, 0] = acc[...].astype(o_ref.dtype)


def _expert_ffn(xs, w_gate, w_up, w_down):
    bsz, n_exp, cap, d = xs.shape
    ff = w_gate.shape[2]
    tf = 256
    return pl.pallas_call(
        _ffn_kernel,
        grid=(n_exp, bsz, ff // tf),
        in_specs=[pl.BlockSpec((1, 1, cap, d), lambda e, b, f: (b, e, 0, 0)),
                  pl.BlockSpec((1, d, tf), lambda e, b, f: (e, 0, f)),
                  pl.BlockSpec((1, d, tf), lambda e, b, f: (e, 0, f)),
                  pl.BlockSpec((1, tf, d), lambda e, b, f: (e, f, 0))],
        out_specs=pl.BlockSpec((1, 1, cap, d), lambda e, b, f: (b, e, 0, 0)),
        out_shape=jax.ShapeDtypeStruct((bsz, n_exp, cap, d), BF16),
        scratch_shapes=[pltpu.VMEM((cap, d), F32)],
        compiler_params=_cparams(("parallel", "parallel", "arbitrary")),
        name="expert_ffn",
    )(xs, w_gate, w_up, w_down)


def _combine_kernel(off_ref, y_ref, pos_ref, aff_ref, o_ref, *, cap, nb, n_exp):
    b = pl.program_id(0)
    j = pl.program_id(2)
    kiota = lax.broadcasted_iota(I32, (TOK_BLK, COMB_W), 1)
    pos = pos_ref[0]
    aff = aff_ref[0]
    acc = jnp.zeros(o_ref.shape[1:], F32)
    for ex in range(n_exp):
        start = (off_ref[(b * n_exp + ex) * nb + j] // 16) * 16
        start = pl.multiple_of(jnp.minimum(start, cap - COMB_W), 16)
        onehot = jnp.where(pos[:, ex:ex + 1] - start == kiota, 1.0, 0.0).astype(BF16)
        ywin = y_ref[0, ex, pl.ds(start, COMB_W), :]
        acc = acc + aff[:, ex:ex + 1] * _dot(onehot, ywin)
    o_ref[0] = acc


def _combine(off_flat, y, pos_t, aff, cap):
    bsz, n_exp, _, d = y.shape
    n_tok = pos_t.shape[1]
    nb = n_tok // TOK_BLK
    dblk = 256
    kern = functools.partial(_combine_kernel, cap=cap, nb=nb, n_exp=n_exp)
    return pl.pallas_call(
        kern,
        grid_spec=pltpu.PrefetchScalarGridSpec(
            num_scalar_prefetch=1,
            grid=(bsz, d // dblk, nb),
            in_specs=[pl.BlockSpec((1, n_exp, cap, dblk), lambda b, c, j, off: (b, 0, 0, c)),
                      pl.BlockSpec((1, TOK_BLK, n_exp), lambda b, c, j, off: (b, j, 0)),
                      pl.BlockSpec((1, TOK_BLK, n_exp), lambda b, c, j, off: (b, j, 0))],
            out_specs=pl.BlockSpec((1, TOK_BLK, dblk), lambda b, c, j, off: (b, j, c))),
        out_shape=jax.ShapeDtypeStruct((bsz, n_tok, d), F32),
        compiler_params=_cparams(("parallel", "parallel", "arbitrary")),
        name="ec_combine",
    )(off_flat, y, pos_t, aff)


def _final_kernel(x1_ref, moe_ref, g_ref, mod_ref, o_ref):
    m = moe_ref[...]
    r = m * lax.rsqrt(jnp.mean(m * m, axis=-1, keepdims=True) + EPS) * g_ref[...]
    o_ref[...] = x1_ref[...] + mod_ref[0] * r


def _final(x1, moe, g3, g2, rows_per_batch):
    n, d = x1.shape
    tm = 512
    tiles_per_batch = rows_per_batch // tm
    return pl.pallas_call(
        _final_kernel,
        grid=(n // tm,),
        in_specs=[pl.BlockSpec((tm, d), lambda i: (i, 0)),
                  pl.BlockSpec((tm, d), lambda i: (i, 0)),
                  pl.BlockSpec((1, d), lambda i: (0, 0)),
                  pl.BlockSpec((1, 1, d), lambda i: (i // tiles_per_batch, 0, 0))],
        out_specs=pl.BlockSpec((tm, d), lambda i: (i, 0)),
        out_shape=jax.ShapeDtypeStruct((n, d), F32),
        compiler_params=_cparams(("parallel",)),
        name="final_residual",
    )(x1, moe, g3, g2)


def _layer(x, c, ctx, c_ctx, ada_w, ada_b, norm_g, w_in, s5_a_re, s5_a_im, s5_log_dt, s5_b_re,
           s5_b_im, s5_c_re, s5_c_im, s5_d, s5_glu_w, s5_glu_b, ml_conv_w, ml_conv_b, ml_gate_b,
           ml_norm_g, w_out, router_w, exp_w_gate, exp_w_up, exp_w_down):
    bsz, seq, d = x.shape
    lctx = ctx.shape[1]
    ds5 = s5_d.shape[0]
    dml = ml_norm_g.shape[0]
    ngroups = ds5 // S5_GROUP
    rows = seq // GRID_W
    n = bsz * seq
    cap = EC_FACTOR * seq // N_EXPERTS
    assert cap >= COMB_W and seq % TOK_BLK == 0 and lctx % ML_T == 0 and seq % ML_T == 0

    cond = jnp.concatenate([c, c_ctx[None], jnp.zeros((8 - bsz - 1, d), F32)], axis=0)
    mod = _modulation(cond, ada_w, ada_b).reshape(8, N_MOD, d)
    sh1, sc1, g1, sh2, sc2, g2 = [mod[:, i] for i in range(N_MOD)]

    nmain = ds5 + 4 * dml
    w_s5 = w_in[:, :ds5].astype(BF16)
    w_ml = w_in[:, ds5:nmain].astype(BF16)
    w_gate = jnp.pad(w_in[:, nmain:], ((0, 0), (0, LANES - 4 * ML_HEADS))).astype(BF16)
    g0 = norm_g[0:1]
    sc, sh = sc1[:, None], sh1[:, None]
    of_sample = lambda b: b
    of_ctx = lambda b: bsz
    ux = _in_projection_s5(x, g0, sc, sh, of_sample, w_s5)
    uc = _in_projection_s5(ctx, g0, sc, sh, of_ctx, w_s5)
    ltot = seq + lctx
    n_slots = ltot // rows
    ml5, gates5 = _in_projection_ml(x, g0, sc, sh, of_sample, w_ml, w_gate, n_slots)
    ml5, gates5 = _in_projection_ml(ctx, g0, sc, sh, of_ctx, w_ml, w_gate, n_slots,
                                    ctx_into=(ml5, gates5, GRID_W))

    m, bp, cp, avec = _s5_matrices(s5_a_re, s5_a_im, s5_log_dt, s5_b_re, s5_b_im, s5_c_re, s5_c_im)
    ys = _s5_scan(ux, uc, m, bp, cp, avec).reshape(n, ds5)

    ml = ml5.reshape(bsz, ltot, 4 * dml)
    gates = gates5.reshape(bsz, ltot, LANES)
    conv_w8 = jnp.pad(ml_conv_w, ((0, 8 - CONV_K), (0, 0)))
    qscale = jnp.concatenate([jnp.full((1, dml), (dml // ML_HEADS) ** -0.5, F32),
                              jnp.ones((1, dml), F32)], axis=1)
    qk = _conv_silu(ml, conv_w8, ml_conv_b[None], qscale, (0, seq), (seq, ltot))
    gbias = ml_gate_b.reshape(1, 4 * ML_HEADS)
    gates_t = gates[..., :4 * ML_HEADS].swapaxes(1, 2)
    hf, hb = _mlstm_scan(qk, ml, gates, gates_t, gbias, gbias.reshape(4 * ML_HEADS, 1), lctx // ML_T)
    mlo = _ml_post(hf.reshape(ml5.shape[:4] + (dml,)), hb.reshape(ml5.shape[:4] + (dml,)), ml5,
                   ml_norm_g[None], seq).reshape(n, dml)

    mod3 = jnp.stack([g1[:bsz], sc2[:bsz], sh2[:bsz]], axis=1)
    mod3 = jnp.concatenate([mod3, jnp.zeros((bsz, 5, d), F32)], axis=1)
    rw = jnp.pad(router_w, ((0, 0), (0, LANES - N_EXPERTS)))
    rw_hi = rw.astype(BF16)
    rw_lo = (rw - rw_hi.astype(F32)).astype(BF16)
    x1, hx2, aff = _mixer_out(ys, ux.reshape(n, ds5), mlo, x.reshape(n, d), seq, s5_d[None],
                              s5_glu_w.astype(BF16), s5_glu_b[None], w_out.astype(BF16), norm_g, mod3,
                              rw_hi, rw_lo)

    nb = seq // TOK_BLK
    aff16 = aff[:, :N_EXPERTS].reshape(bsz, seq, N_EXPERTS)
    aff_t = aff16.swapaxes(1, 2).reshape(bsz, N_EXPERTS, nb, LANES)
    pos, off = _select(aff_t, cap)
    off_flat = off[..., 0].reshape(-1)
    pos_t = pos.reshape(bsz, N_EXPERTS, seq).swapaxes(1, 2)
    xs = _dispatch(off_flat, hx2.reshape(bsz, seq, d), pos, cap)
    ye = _expert_ffn(xs, exp_w_gate, exp_w_up, exp_w_down)
    moe = _combine(off_flat, ye, pos_t, aff16, cap)
    out = _final(x1, moe.reshape(n, d), norm_g[3:4], g2[:bsz, None], seq)
    return out.reshape(bsz, seq, d)


def kernel(x, c, ctx, c_ctx, ada_w, ada_b, norm_g, w_in, s5_a_re, s5_a_im, s5_log_dt, s5_b_re, s5_b_im, s5_c_re, s5_c_im, s5_d, s5_glu_w, s5_glu_b, ml_conv_w, ml_conv_b, ml_gate_b, ml_norm_g, w_out, router_w, exp_w_gate, exp_w_up, exp_w_down):
    depth = ada_w.shape[0]
    assert depth == 1, "context outputs are only produced when another layer follows"
    li = 0
    return _layer(x, c, ctx, c_ctx, ada_w[li], ada_b[li], norm_g[li], w_in[li], s5_a_re[li],
                  s5_a_im[li], s5_log_dt[li], s5_b_re[li], s5_b_im[li], s5_c_re[li], s5_c_im[li],
                  s5_d[li], s5_glu_w[li], s5_glu_b[li], ml_conv_w[li], ml_conv_b[li], ml_gate_b[li],
                  ml_norm_g[li], w_out[li], router_w[li], exp_w_gate[li], exp_w_up[li],
                  exp_w_down[li])
```

```python
import functools
import math

import jax
import jax.numpy as jnp
from jax import lax
from jax.experimental import pallas as pl
from jax.experimental.pallas import tpu as pltpu

F32 = jnp.float32
BF16 = jnp.bfloat16
I32 = jnp.int32

EPS = 1e-6
GRID_W = 64
S5_GROUP = 16
S5_STATE = 64
S5_T = 16
ML_HEADS = 4
ML_T = 256
CONV_K = 5
N_EXPERTS = 16
EC_FACTOR = 2
N_MOD = 6
LANES = 128
TOK_BLK = 128
DISP_TOK = 512
DISP_W = DISP_TOK + 8
VMEM_LIMIT = 56 * 1024 * 1024


def _cparams(sem):
    return pltpu.CompilerParams(dimension_semantics=sem, vmem_limit_bytes=VMEM_LIMIT)


def _sigmoid(x):
    return 1.0 / (1.0 + jnp.exp(-x))


def _silu(x):
    return x * _sigmoid(x)


def _log_sigmoid(x):
    return jnp.minimum(x, 0.0) - jnp.log(1.0 + jnp.exp(-jnp.abs(x)))


def _split3(x):
    x1 = x.astype(BF16)
    r = x - x1.astype(F32)
    x2 = r.astype(BF16)
    x3 = (r - x2.astype(F32)).astype(BF16)
    return x1, x2, x3


def _dot(a, b):
    return jnp.dot(a, b, preferred_element_type=F32)


def _dot_nt(a, b):
    return lax.dot_general(a, b, (((1,), (1,)), ((), ())), preferred_element_type=F32)


def _dot_tn(a, b):
    return lax.dot_general(a, b, (((0,), (0,)), ((), ())), preferred_element_type=F32)


def _dot3_left(a_f32, b_bf16):
    a1, a2, a3 = _split3(a_f32)
    return _dot(a1, b_bf16) + _dot(a2, b_bf16) + _dot(a3, b_bf16)


def _dot3_right(a_bf16, b_f32):
    b1, b2, b3 = _split3(b_f32)
    return _dot(a_bf16, b1) + _dot(a_bf16, b2) + _dot(a_bf16, b3)


def _mod_kernel(c_ref, w_ref, b_ref, o_ref):
    c = c_ref[...]
    s = _silu(c).astype(BF16)
    o_ref[...] = _dot(s, w_ref[...].astype(BF16)) + b_ref[...]


def _modulation(cond8, ada_w, ada_b):
    d, n = ada_w.shape
    tn = 1024
    return pl.pallas_call(
        _mod_kernel,
        grid=(n // tn,),
        in_specs=[pl.BlockSpec((8, d), lambda j: (0, 0)),
                  pl.BlockSpec((d, tn), lambda j: (0, j)),
                  pl.BlockSpec((1, tn), lambda j: (0, j))],
        out_specs=pl.BlockSpec((8, tn), lambda j: (0, j)),
        out_shape=jax.ShapeDtypeStruct((8, n), F32),
        compiler_params=_cparams(("parallel",)),
        name="modulation",
    )(cond8, ada_w, ada_b.reshape(1, n))


def _norm_mod(x, g, sc, sh):
    ms = jnp.mean(x * x, axis=-1, keepdims=True)
    h = x * lax.rsqrt(ms + EPS) * g
    return (h * (1.0 + sc) + sh).astype(BF16)


def _inproj_s5_kernel(x_ref, g_ref, sc_ref, sh_ref, w_ref, o_ref, h_scr):
    @pl.when(pl.program_id(2) == 0)
    def _():
        h_scr[...] = _norm_mod(x_ref[0], g_ref[...], sc_ref[0], sh_ref[0])

    o_ref[0] = _dot(h_scr[...], w_ref[...])


def _in_projection_s5(x, g0, sc, sh, mod_of_batch, w_s5):
    bsz, length, d = x.shape
    ds5 = w_s5.shape[1]
    tm = min(1024, length)
    tn = 512
    return pl.pallas_call(
        _inproj_s5_kernel,
        grid=(bsz, length // tm, ds5 // tn),
        in_specs=[pl.BlockSpec((1, tm, d), lambda b, i, j: (b, i, 0)),
                  pl.BlockSpec((1, d), lambda b, i, j: (0, 0)),
                  pl.BlockSpec((1, 1, d), lambda b, i, j: (mod_of_batch(b), 0, 0)),
                  pl.BlockSpec((1, 1, d), lambda b, i, j: (mod_of_batch(b), 0, 0)),
                  pl.BlockSpec((d, tn), lambda b, i, j: (0, j))],
        out_specs=pl.BlockSpec((1, tm, tn), lambda b, i, j: (b, i, j)),
        out_shape=jax.ShapeDtypeStruct((bsz, length, ds5), F32),
        scratch_shapes=[pltpu.VMEM((tm, d), BF16)],
        compiler_params=_cparams(("parallel", "parallel", "arbitrary")),
        name="in_projection_s5",
    )(x, g0, sc, sh, w_s5)


def _inproj_ml_kernel(x_ref, g_ref, sc_ref, sh_ref, w_ref, wg_ref, *rest, colmajor):
    o_ref, og_ref, h_scr, x_scr = rest[-4:]
    rb = o_ref.shape[3]
    nlb = x_ref.shape[2] // LANES

    @pl.when(pl.program_id(2) == 0)
    def _():
        if colmajor:
            for c in range(nlb):
                for r in range(rb):
                    x_scr[c, r * ROW_PITCH:r * ROW_PITCH + GRID_W, :] = (
                        x_ref[0, r * GRID_W:(r + 1) * GRID_W, c * LANES:(c + 1) * LANES])
            for w in range(GRID_W):
                xw = jnp.concatenate([x_scr[c, pl.ds(w, rb, stride=ROW_PITCH), :] for c in range(nlb)],
                                     axis=-1)
                h_scr[w * rb:(w + 1) * rb, :] = _norm_mod(xw, g_ref[...], sc_ref[0], sh_ref[0])
        else:
            h_scr[...] = _norm_mod(x_ref[0], g_ref[...], sc_ref[0], sh_ref[0])
        og_ref[0] = _dot(h_scr[...], wg_ref[...]).reshape(og_ref.shape[1:])

    o_ref[0] = _dot(h_scr[...], w_ref[...]).reshape(o_ref.shape[1:]).astype(o_ref.dtype)


ML_RB = 16
ROW_PITCH = GRID_W + 8


def _in_projection_ml(x, g0, sc, sh, mod_of_batch, w_ml, w_gate, n_slots, ctx_into=None):
    bsz, length, d = x.shape
    nml = w_ml.shape[1]
    tn = 512
    if ctx_into is None:
        rows = length // GRID_W
        rblocks = rows // ML_RB
        tm = ML_RB * GRID_W
        grid = (bsz, rblocks, nml // tn)
        oblk = (1, GRID_W, 1, ML_RB, tn)
        gblk = (1, GRID_W, 1, ML_RB, LANES)
        omap = lambda b, i, j: (b, 0, i, 0, j)
        gmap = lambda b, i, j: (b, 0, i, 0, 0)
        extra_in, extra_specs, aliases = [], [], {}
    else:
        ml_arr, g_arr, first_slot = ctx_into
        rblocks = ml_arr.shape[2]
        tm = length
        nslot = length // (rblocks * ML_RB)
        assert nslot * rblocks * ML_RB == length and first_slot % nslot == 0
        grid = (bsz, 1, nml // tn)
        oblk = (1, nslot, rblocks, ML_RB, tn)
        gblk = (1, nslot, rblocks, ML_RB, LANES)
        omap = lambda b, i, j: (b, first_slot // nslot, 0, 0, j)
        gmap = lambda b, i, j: (b, first_slot // nslot, 0, 0, 0)
        extra_in = [ml_arr, g_arr]
        extra_specs = [pl.BlockSpec(memory_space=pl.ANY)] * 2
        aliases = {6: 0, 7: 1}
    kern = functools.partial(_inproj_ml_kernel, colmajor=ctx_into is None)
    return pl.pallas_call(
        kern,
        grid=grid,
        in_specs=[pl.BlockSpec((1, tm, d), lambda b, i, j: (b, i, 0)),
                  pl.BlockSpec((1, d), lambda b, i, j: (0, 0)),
                  pl.BlockSpec((1, 1, d), lambda b, i, j: (mod_of_batch(b), 0, 0)),
                  pl.BlockSpec((1, 1, d), lambda b, i, j: (mod_of_batch(b), 0, 0)),
                  pl.BlockSpec((d, tn), lambda b, i, j: (0, j)),
                  pl.BlockSpec((d, LANES), lambda b, i, j: (0, 0))] + extra_specs,
        out_specs=[pl.BlockSpec(oblk, omap), pl.BlockSpec(gblk, gmap)],
        out_shape=[jax.ShapeDtypeStruct((bsz, n_slots, rblocks, ML_RB, nml), BF16),
                   jax.ShapeDtypeStruct((bsz, n_slots, rblocks, ML_RB, LANES), F32)],
        scratch_shapes=[pltpu.VMEM((tm, d), BF16),
                        pltpu.VMEM((d // LANES, ML_RB * ROW_PITCH, LANES) if ctx_into is None
                                   else (1, 8, LANES), F32)],
        input_output_aliases=aliases,
        compiler_params=_cparams(("parallel", "parallel", "arbitrary")),
        name="in_projection_ml" if ctx_into is None else "in_projection_ml_ctx",
    )(x, g0, sc, sh, w_ml, w_gate, *extra_in)


def _dot_nt6(a, b):
    a1, a2, a3 = _split3(a)
    b1, b2, b3 = _split3(b)
    return (_dot_nt(a1, b1) + (_dot_nt(a1, b2) + _dot_nt(a2, b1))
            + (_dot_nt(a1, b3) + _dot_nt(a3, b1) + _dot_nt(a2, b2)))


S5_LANES = 4 * S5_STATE


def _s5ops_kernel(prm_ref, bt_ref, c_ref, rexp_ref, rc_ref, place_ref, m_ref, bp_ref, cpt_ref, av_ref):
    t = S5_T
    prm = prm_ref[0]
    lr, li = jnp.minimum(prm[0:1], -1e-4), prm[1:2]
    dt = jnp.exp(prm[2:3])
    zr, zi = lr * dt, li * dt
    tau = lax.broadcasted_iota(I32, (32, S5_LANES), 0).astype(F32)
    mag, ang = jnp.exp(tau * zr), tau * zi
    pwr, pwi = mag * jnp.cos(ang), mag * jnp.sin(ang)
    xr, xi = pwr[1:2] - 1.0, pwi[1:2]
    den = lr * lr + li * li
    qr, qi = (xr * lr + xi * li) / den, (xi * lr - xr * li) / den
    btr, bti = bt_ref[0, 0], bt_ref[0, 1]
    bbr, bbi = qr * btr - qi * bti, qr * bti + qi * btr
    cr, ci = c_ref[0, 0], c_ref[0, 1]

    lane = lax.broadcasted_iota(I32, (t * S5_GROUP, S5_LANES), 1)
    fwd = ((lane // S5_STATE) % 2) == 0

    def expand(r, x):
        return _dot3_right(r, x)

    e_t = [expand(rexp_ref[0], z) for z in (pwr, pwi)]
    e_r = [expand(rexp_ref[1], z) for z in (pwr, pwi)]
    e_p = [expand(rexp_ref[2], z) for z in (pwr, pwi)]
    e_q = [expand(rexp_ref[3], z) for z in (pwr, pwi)]
    pk = [jnp.where(fwd, e_t[i], e_r[i]) for i in range(2)]
    pb = [jnp.where(fwd, e_r[i], e_t[i]) for i in range(2)]
    pc = [jnp.where(fwd, e_p[i], e_q[i]) for i in range(2)]
    cxr, cxi = expand(rc_ref[...], cr), expand(rc_ref[...], ci)
    bxr, bxi = expand(rc_ref[...], bbr), expand(rc_ref[...], bbi)

    wkr, wki = pk[0] * cxr - pk[1] * cxi, pk[0] * cxi + pk[1] * cxr
    row_blk = lax.broadcasted_iota(I32, (4 * S5_GROUP, S5_LANES), 0) // S5_GROUP
    lane_blk = lax.broadcasted_iota(I32, (4 * S5_GROUP, S5_LANES), 1) // S5_STATE
    own = row_blk == lane_blk
    bmr = jnp.where(own, jnp.concatenate([bbr] * 4, axis=0), 0.0)
    bmi = jnp.where(own, jnp.concatenate([bbi] * 4, axis=0), 0.0)
    kt = _dot_nt6(bmr, wkr) - _dot_nt6(bmi, wki)

    ln = lax.broadcasted_iota(I32, (S5_GROUP, t * S5_GROUP), 1)
    for gi in range(2):
        ktf = kt[(2 * gi) * S5_GROUP:(2 * gi + 1) * S5_GROUP]
        ktb = kt[(2 * gi + 1) * S5_GROUP:(2 * gi + 2) * S5_GROUP]
        blocks = []
        for s in range(t):
            f = ktf if s == 0 else pltpu.roll(ktf, S5_GROUP * s, axis=1)
            sh = (t * S5_GROUP - S5_GROUP * (t - 1 - s)) % (t * S5_GROUP)
            bk = ktb if sh == 0 else pltpu.roll(ktb, sh, axis=1)
            blocks.append(jnp.where(ln >= S5_GROUP * s, f, 0.0)
                          + jnp.where(ln < S5_GROUP * (s + 1), bk, 0.0))
        m_ref[0, gi] = jnp.concatenate(blocks, axis=0).astype(BF16)

    inj = jnp.concatenate([pb[0] * bxr - pb[1] * bxi, pb[0] * bxi + pb[1] * bxr], axis=-1).astype(BF16)
    out = jnp.concatenate([pc[0] * cxr - pc[1] * cxi, -(pc[0] * cxi + pc[1] * cxr)], axis=-1).astype(BF16)
    for gi in range(2):
        bp_ref[0, gi * t * S5_GROUP:(gi + 1) * t * S5_GROUP, :] = _dot(inj, place_ref[gi]).astype(BF16)
        cpt_ref[0, gi * t * S5_GROUP:(gi + 1) * t * S5_GROUP, :] = _dot(out, place_ref[gi]).astype(BF16)
    a_t = jnp.concatenate([pwr[t:t + 8], pwi[t:t + 8]], axis=-1)
    av_ref[0] = _dot3_left(a_t, place_ref[0]) + _dot3_left(a_t, place_ref[1])


def _s5_operators(a_re, a_im, log_dt, b_re, b_im, c_re, c_im):
    t = S5_T
    g, p = a_re.shape[1], a_re.shape[2]
    npair = g // 2

    def lanes(z):
        return z.reshape(2, npair, 2, p).transpose(1, 2, 0, 3).reshape(npair, 4 * p)

    prm = jnp.stack([lanes(a_re.astype(F32)), lanes(a_im.astype(F32)),
                     lanes(jnp.broadcast_to(log_dt.astype(F32)[..., None], a_re.shape))], axis=1)
    prm = jnp.concatenate([prm, jnp.zeros((npair, 5, 4 * p), F32)], axis=1)

    def tile_d(z):
        z = z.reshape(npair, 2, S5_GROUP, p).transpose(0, 2, 1, 3)
        return jnp.broadcast_to(z[:, :, :, None, :], (npair, S5_GROUP, 2, 2, p)).reshape(npair, S5_GROUP, 4 * p)

    bt = jnp.stack([tile_d(b_re.astype(F32).swapaxes(1, 2)), tile_d(b_im.astype(F32).swapaxes(1, 2))], axis=1)
    ct = jnp.stack([tile_d(c_re.astype(F32)), tile_d(c_im.astype(F32))], axis=1)

    row = jnp.arange(t * S5_GROUP)
    tau, ch = row // S5_GROUP, row % S5_GROUP
    e = jnp.arange(32)
    rexp = jnp.stack([(expo[:, None] == e[None, :]) for expo in (tau, t - 1 - tau, tau + 1, t - tau)]
                     ).astype(BF16)
    rc = (ch[:, None] == jnp.arange(S5_GROUP)[None, :]).astype(BF16)
    src = jnp.arange(2 * 4 * p)
    reim, gi_s, d_s, p_s = src // (4 * p), (src // (2 * p)) % 2, (src // p) % 2, src % p
    dst = d_s * (4 * p) + reim * (2 * p) + gi_s * p + p_s
    col = jnp.arange(8 * p)
    place = jnp.stack([((dst[:, None] == col[None, :]) & (gi_s[:, None] == gi)) for gi in range(2)]
                      ).astype(BF16)
    blk3 = lambda shape: pl.BlockSpec((1,) + shape, lambda i: (i,) + (0,) * len(shape))
    full = lambda arr: pl.BlockSpec(arr.shape, lambda i: (0,) * arr.ndim)
    m, bp, cpt, av = pl.pallas_call(
        _s5ops_kernel,
        grid=(npair,),
        in_specs=[blk3((8, 4 * p)), blk3((2, S5_GROUP, 4 * p)), blk3((2, S5_GROUP, 4 * p)),
                  full(rexp), full(rc), full(place)],
        out_specs=[blk3((2, t * S5_GROUP, t * S5_GROUP)), blk3((2 * t * S5_GROUP, 8 * p)),
                   blk3((2 * t * S5_GROUP, 8 * p)), blk3((8, 8 * p))],
        out_shape=[jax.ShapeDtypeStruct((npair, 2, t * S5_GROUP, t * S5_GROUP), BF16),
                   jax.ShapeDtypeStruct((npair, 2 * t * S5_GROUP, 8 * p), BF16),
                   jax.ShapeDtypeStruct((npair, 2 * t * S5_GROUP, 8 * p), BF16),
                   jax.ShapeDtypeStruct((npair, 8, 8 * p), F32)],
        compiler_params=_cparams(("parallel",)),
        name="s5_operators",
    )(prm, bt, ct, rexp, rc, place)
    avec = av[:, 0].reshape(npair, 4, 2 * p)
    return m, bp, cpt, jnp.concatenate([avec, jnp.zeros_like(avec)], axis=1)


def _s5_permutation():
    i = jnp.arange(8 * 8 * S5_GROUP)
    t8, j, c = i // (8 * S5_GROUP), (i // S5_GROUP) % 8, i % S5_GROUP
    dst = j * (8 * S5_GROUP) + t8 * S5_GROUP + c
    e1 = (dst[:, None] == i[None, :]).astype(BF16)
    return e1, e1.T


S5_PAIRS = 4


def _s5_kernel(ux_ref, uc_ref, e1_ref, e1t_ref, m_ref, bp_ref, cp_ref, av_ref, y_ref, s_scr, hp_scr,
               tab_scr, *, cc, xc):
    nc = cc + xc

    def chunk_rows(ref, nchunks):
        cols = [ref[0, pl.ds(t, nchunks, stride=S5_T), :].astype(BF16) for t in range(S5_T)]
        return [_dot(jnp.concatenate(cols[8 * h:8 * h + 8], axis=-1), e1_ref[...]).astype(BF16)
                for h in range(2)]

    def pair_lanes(halves, p):
        return jnp.concatenate([halves[h][:, j * LANES:(j + 1) * LANES]
                                for j in (2 * p, 2 * p + 1) for h in range(2)], axis=-1)

    pc = chunk_rows(uc_ref, cc)
    px = chunk_rows(ux_ref, xc)
    for p in range(S5_PAIRS):
        s_scr[0:cc, p * 512:(p + 1) * 512] = _dot(pair_lanes(pc, p), bp_ref[p])
        s_scr[cc:nc, p * 512:(p + 1) * 512] = _dot(pair_lanes(px, p), bp_ref[p])

    av = av_ref[...]
    sub = lax.broadcasted_iota(I32, (8, 128), 0)

    def cmul(ar, ai, br, bi):
        return ar * br - ai * bi, ar * bi + ai * br

    for p in range(S5_PAIRS):
        for d in range(2):
            a1 = (av[p, 2 * d:2 * d + 1], av[p, 2 * d + 1:2 * d + 2])
            a2 = cmul(*a1, *a1)
            a4 = cmul(*a2, *a2)
            pw_re, pw_im = jnp.zeros((8, 128), F32), jnp.zeros((8, 128), F32)
            cur = a1
            for k in range(8):
                r = k if d == 0 else 7 - k
                pw_re = jnp.where(sub == r, cur[0], pw_re)
                pw_im = jnp.where(sub == r, cur[1], pw_im)
                cur = cmul(*cur, *a1)
            for q, z in enumerate((a1, a2, a4)):
                tab_scr[p, d, 2 * q] = jnp.broadcast_to(z[0], (8, 128))
                tab_scr[p, d, 2 * q + 1] = jnp.broadcast_to(z[1], (8, 128))
            tab_scr[p, d, 6] = pw_re
            tab_scr[p, d, 7] = pw_im

    def scan8(s_re, s_im, h_re, h_im, p, d):
        back = d == 1
        x_re, x_im = s_re, s_im
        for q, k in enumerate((1, 2, 4)):
            keep = (sub < 8 - k) if back else (sub >= k)
            sh = (8 - k) if back else k
            y_re = jnp.where(keep, pltpu.roll(x_re, sh, axis=0), 0.0)
            y_im = jnp.where(keep, pltpu.roll(x_im, sh, axis=0), 0.0)
            m_re, m_im = cmul(tab_scr[p, d, 2 * q], tab_scr[p, d, 2 * q + 1], y_re, y_im)
            x_re, x_im = x_re + m_re, x_im + m_im
        c_re, c_im = cmul(tab_scr[p, d, 6], tab_scr[p, d, 7], h_re, h_im)
        x_re, x_im = x_re + c_re, x_im + c_im
        edge = 7 if back else 0
        sh = 7 if back else 1
        e_re = jnp.where(sub == edge, h_re, pltpu.roll(x_re, sh, axis=0))
        e_im = jnp.where(sub == edge, h_im, pltpu.roll(x_im, sh, axis=0))
        last = 0 if back else 7
        return e_re, e_im, x_re[last:last + 1], x_im[last:last + 1]

    def step(row_f, row_b, carry):
        rf = pl.multiple_of(row_f, 8)
        rb = pl.multiple_of(row_b, 8)
        new = []
        for p in range(S5_PAIRS):
            hr, hi, gr, gi = carry[4 * p:4 * p + 4]
            c0 = p * 512
            p_re, p_im, hr, hi = scan8(s_scr[pl.ds(rf, 8), c0:c0 + 128], s_scr[pl.ds(rf, 8), c0 + 128:c0 + 256],
                                       hr, hi, p, 0)
            hp_scr[pl.ds(rf, 8), c0:c0 + 128] = p_re
            hp_scr[pl.ds(rf, 8), c0 + 128:c0 + 256] = p_im
            p_re, p_im, gr, gi = scan8(s_scr[pl.ds(rb, 8), c0 + 256:c0 + 384], s_scr[pl.ds(rb, 8), c0 + 384:c0 + 512],
                                       gr, gi, p, 1)
            hp_scr[pl.ds(rb, 8), c0 + 256:c0 + 384] = p_re
            hp_scr[pl.ds(rb, 8), c0 + 384:c0 + 512] = p_im
            new += [hr, hi, gr, gi]
        return tuple(new)

    z = jnp.zeros((1, 128), F32)
    carry = lax.fori_loop(0, cc // 8, lambda i, c: step(8 * i, cc - 8 - 8 * i, c), (z,) * (4 * S5_PAIRS))
    lax.fori_loop(0, xc // 8, lambda i, c: step(cc + 8 * i, nc - 8 - 8 * i, c), carry)

    ys = []
    for p in range(S5_PAIRS):
        ux = pair_lanes(px, p)
        y = _dot_nt(hp_scr[cc:nc, p * 512:(p + 1) * 512].astype(BF16), cp_ref[p])
        ys.append(y + jnp.concatenate([_dot(ux[:, 0:256], m_ref[p, 0]), _dot(ux[:, 256:512], m_ref[p, 1])],
                                      axis=-1))
    for h in range(2):
        yh = jnp.concatenate([ys[j // 2][:, (j % 2) * 256 + h * LANES:(j % 2) * 256 + (h + 1) * LANES]
                              for j in range(8)], axis=-1).astype(BF16)
        r = _dot(yh, e1t_ref[...])
        for t8 in range(8):
            y_ref[0, pl.ds(8 * h + t8, xc, stride=S5_T), :] = r[:, t8 * LANES:(t8 + 1) * LANES]


def _s5_scan(ux, uc, m, bp, cp, avec):
    bsz, seq, ds5 = ux.shape
    lctx = uc.shape[1]
    cc, xc = lctx // S5_T, seq // S5_T
    assert cc % 8 == 0 and xc % 8 == 0
    e1, e1t = _s5_permutation()
    kern = functools.partial(_s5_kernel, cc=cc, xc=xc)
    nblk = ds5 // LANES
    const = lambda j, b: (0, 0)
    return pl.pallas_call(
        kern,
        grid=(nblk, bsz),
        in_specs=[pl.BlockSpec((1, seq, LANES), lambda j, b: (b, 0, j)),
                  pl.BlockSpec((1, lctx, LANES), lambda j, b: (b, 0, j)),
                  pl.BlockSpec(e1.shape, const),
                  pl.BlockSpec(e1.shape, const),
                  pl.BlockSpec((S5_PAIRS, 2, 256, 256), lambda j, b: (j, 0, 0, 0)),
                  pl.BlockSpec((S5_PAIRS, 512, 512), lambda j, b: (j, 0, 0)),
                  pl.BlockSpec((S5_PAIRS, 512, 512), lambda j, b: (j, 0, 0)),
                  pl.BlockSpec((S5_PAIRS, 8, 128), lambda j, b: (j, 0, 0))],
        out_specs=pl.BlockSpec((1, seq, LANES), lambda j, b: (b, 0, j)),
        out_shape=jax.ShapeDtypeStruct((bsz, seq, ds5), F32),
        scratch_shapes=[pltpu.VMEM((cc + xc, S5_PAIRS * 512), F32),
                        pltpu.VMEM((cc + xc, S5_PAIRS * 512), F32),
                        pltpu.VMEM((S5_PAIRS, 2, 8, 8, 128), F32)],
        compiler_params=_cparams(("parallel", "parallel")),
        name="s5_scan",
    )(ux, uc, e1, e1t, m, bp, cp, avec)


def _conv_kernel(x_ref, p_ref, n_ref, w_ref, b_ref, sc_ref, o_ref, scr, *, rows, seq_tiles):
    i = pl.program_id(1)
    first = functools.reduce(jnp.logical_or, [i == s for s in seq_tiles[0]])
    last = functools.reduce(jnp.logical_or, [i == s for s in seq_tiles[1]])
    pm = jnp.where(first, 0.0, 1.0)
    nm = jnp.where(last, 0.0, 1.0)
    scr[0:8, :] = p_ref[0, 8:16, :].astype(F32) * pm
    scr[8:8 + rows, :] = x_ref[0].astype(F32)
    scr[8 + rows:16 + rows, :] = n_ref[0, 0:8, :].astype(F32) * nm
    w = w_ref[...]
    acc = scr[6:6 + rows, :] * w[0:1]
    for j in range(1, CONV_K):
        acc = acc + scr[6 + j:6 + j + rows, :] * w[j:j + 1]
    z = acc + b_ref[...]
    o_ref[0] = (_silu(z) * sc_ref[...]).astype(o_ref.dtype)


def _conv_silu(ml, conv_w8, conv_b, scale, seq_starts, seq_ends):
    bsz, ltot, _ = ml.shape
    rows = 256
    ct = 512
    nch = conv_b.shape[-1]
    nt = ltot // rows
    hb = rows // 16
    nhb = ltot // 16
    seq_tiles = (tuple(s // rows for s in seq_starts), tuple(e // rows - 1 for e in seq_ends))
    kern = functools.partial(_conv_kernel, rows=rows, seq_tiles=seq_tiles)
    return pl.pallas_call(
        kern,
        grid=(bsz, nt, nch // ct),
        in_specs=[pl.BlockSpec((1, rows, ct), lambda b, i, c: (b, i, c)),
                  pl.BlockSpec((1, 16, ct), lambda b, i, c: (b, jnp.maximum(i * hb - 1, 0), c)),
                  pl.BlockSpec((1, 16, ct), lambda b, i, c: (b, jnp.minimum((i + 1) * hb, nhb - 1), c)),
                  pl.BlockSpec((8, ct), lambda b, i, c: (0, c)),
                  pl.BlockSpec((1, ct), lambda b, i, c: (0, c)),
                  pl.BlockSpec((1, ct), lambda b, i, c: (0, c))],
        out_specs=pl.BlockSpec((1, rows, ct), lambda b, i, c: (b, i, c)),
        out_shape=jax.ShapeDtypeStruct((bsz, ltot, nch), BF16),
        scratch_shapes=[pltpu.VMEM((rows + 16, ct), F32)],
        compiler_params=_cparams(("parallel", "parallel", "parallel")),
        name="conv_silu",
    )(ml, ml, ml, conv_w8, conv_b, scale)


def _mlstm_chain_step(q, k, v, i_col, f_col, i_row, f_row, b_col, b_row, c_ref, n_ref, m_ref,
                      idx, backward):
    t = q.shape[0]
    rid = lax.broadcasted_iota(I32, (t, t), 0)
    cid = lax.broadcasted_iota(I32, (t, t), 1)
    mask = (cid >= rid) if backward else (rid >= cid)
    m_st = m_ref[idx]
    d = jnp.where(mask, b_col - b_row + i_row, -jnp.inf)
    inter = b_col + m_st
    m_t = jnp.maximum(inter, jnp.max(d, axis=-1, keepdims=True))
    w = jnp.exp(d - m_t)
    s_inter = jnp.exp(inter - m_t)
    s = _dot_nt(q, k) * w
    c_st = c_ref[idx]
    n_st = n_ref[idx]
    num = _dot(s.astype(BF16), v) + s_inter * _dot_nt(q, c_st.astype(BF16))
    qf = q.astype(F32)
    den = jnp.sum(s, axis=-1, keepdims=True) + s_inter * jnp.sum(qf * n_st, axis=-1, keepdims=True)
    h = num / jnp.maximum(jnp.abs(den), jnp.exp(-m_t))
    total = jnp.sum(f_col, axis=0, keepdims=True)
    d_end = total - b_col + i_col
    m_new = jnp.maximum(total + m_st, jnp.max(d_end, axis=0, keepdims=True))
    w_end = jnp.exp(d_end - m_new)
    dec = jnp.exp(total + m_st - m_new)
    vw = (v.astype(F32) * w_end).astype(BF16)
    c_ref[idx] = dec * c_st + _dot_tn(vw, k)
    n_ref[idx] = dec * n_st + jnp.sum(k.astype(F32) * w_end, axis=0, keepdims=True)
    m_ref[idx] = m_new
    return h


def _mlstm_kernel(qf_ref, kf_ref, vf_ref, gf_ref, gtf_ref, qb_ref, kb_ref, vb_ref, gb_ref, gtb_ref,
                  gbias_ref, gbias_t_ref, hf_ref, hb_ref, c_scr, n_scr, m_scr, *, dh):
    @pl.when(pl.program_id(1) == 0)
    def _():
        c_scr[...] = jnp.zeros_like(c_scr)
        n_scr[...] = jnp.zeros_like(n_scr)
        m_scr[...] = jnp.zeros_like(m_scr)

    t = qf_ref.shape[1]
    rid = lax.broadcasted_iota(I32, (t, t), 0)
    cid = lax.broadcasted_iota(I32, (t, t), 1)
    lower = jnp.where(rid >= cid, 1.0, 0.0).astype(BF16)
    upper = jnp.where(cid >= rid, 1.0, 0.0).astype(BF16)

    for d, (q_ref, k_ref, v_ref, g_ref, gt_ref, h_ref) in enumerate(
            [(qf_ref, kf_ref, vf_ref, gf_ref, gtf_ref, hf_ref),
             (qb_ref, kb_ref, vb_ref, gb_ref, gtb_ref, hb_ref)]):
        backward = d == 1
        g = g_ref[0, :, 0:16] + gbias_ref[...]
        gt = gt_ref[0] + gbias_t_ref[...]
        lf = _log_sigmoid(g)
        lft = _log_sigmoid(gt)
        cum_col = _dot3_right(upper if backward else lower, lf)
        cum_row = _dot3_left(lft, lower if backward else upper)
        for hd in range(ML_HEADS):
            ci = 2 * ML_HEADS * d + hd
            cf = ci + ML_HEADS
            sl = slice(hd * dh, (hd + 1) * dh)
            h = _mlstm_chain_step(
                q_ref[0, :, sl], k_ref[0, :, sl], v_ref[0, :, sl],
                g[:, ci:ci + 1], lf[:, cf:cf + 1], gt[ci:ci + 1, :], lft[cf:cf + 1, :],
                cum_col[:, cf:cf + 1], cum_row[cf:cf + 1, :],
                c_scr, n_scr, m_scr, d * ML_HEADS + hd, backward)
            h_ref[0, :, sl] = h.astype(h_ref.dtype)


def _mlstm_scan(qk, ml, gates, gates_t, gbias, gbias_t, n_ctx_chunks):
    bsz, ltot, dml2 = qk.shape
    dml = dml2 // 2
    dh = dml // ML_HEADS
    t = ML_T
    nchunk = ltot // t
    cc = n_ctx_chunks
    nx = nchunk - cc

    def fwd(s):
        return jnp.where(s < cc, nx + s, s - cc)

    def bwd(s):
        return jnp.where(s < cc, nx + cc - 1 - s, nx - 1 - (s - cc))

    def specs(cmap):
        return [pl.BlockSpec((1, t, dml), lambda b, s: (b, cmap(s), 0)),
                pl.BlockSpec((1, t, dml), lambda b, s: (b, cmap(s), 1)),
                pl.BlockSpec((1, t, dml), lambda b, s: (b, cmap(s), 2)),
                pl.BlockSpec((1, t, LANES), lambda b, s: (b, cmap(s), 0)),
                pl.BlockSpec((1, 16, t), lambda b, s: (b, 0, cmap(s)))]

    kern = functools.partial(_mlstm_kernel, dh=dh)
    nchain = 2 * ML_HEADS
    return pl.pallas_call(
        kern,
        grid=(bsz, nchunk),
        in_specs=specs(fwd) + specs(bwd) + [pl.BlockSpec((1, 16), lambda b, s: (0, 0)),
                                            pl.BlockSpec((16, 1), lambda b, s: (0, 0))],
        out_specs=[pl.BlockSpec((1, t, dml), lambda b, s: (b, fwd(s), 0)),
                   pl.BlockSpec((1, t, dml), lambda b, s: (b, bwd(s), 0))],
        out_shape=[jax.ShapeDtypeStruct((bsz, ltot, dml), BF16)] * 2,
        scratch_shapes=[pltpu.VMEM((nchain, dh, dh), F32),
                        pltpu.VMEM((nchain, 1, dh), F32),
                        pltpu.VMEM((nchain, 1, 1), F32)],
        compiler_params=_cparams(("parallel", "arbitrary")),
        name="mlstm_scan",
    )(qk, qk, ml, gates, gates_t, qk, qk, ml, gates, gates_t, gbias, gbias_t)


COL_PITCH = ML_RB + 8


def _mlpost_kernel(hf_ref, hb_ref, o_ref, mg_ref, out_ref, scr, *, dh):
    nw, _, rb, dml = hf_ref.shape[1:]
    h = (hf_ref[0].astype(F32) + hb_ref[0].astype(F32)).reshape(nw * rb, dml)
    parts = []
    for hd in range(ML_HEADS):
        hh = h[:, hd * dh:(hd + 1) * dh]
        parts.append(hh * lax.rsqrt(jnp.mean(hh * hh, axis=-1, keepdims=True) + EPS))
    hn = jnp.concatenate(parts, axis=-1) * mg_ref[...]
    res = hn * _sigmoid(o_ref[0].astype(F32).reshape(nw * rb, dml))
    nlb = dml // LANES
    for c in range(nlb):
        for w in range(nw):
            scr[c, w * COL_PITCH:w * COL_PITCH + rb, :] = res[w * rb:(w + 1) * rb, c * LANES:(c + 1) * LANES]
    for r in range(rb):
        row = jnp.concatenate([scr[c, pl.ds(r, nw, stride=COL_PITCH), :] for c in range(nlb)], axis=-1)
        out_ref[0, r * nw:(r + 1) * nw, :] = row.astype(out_ref.dtype)


def _ml_post(hf5, hb5, ml5, ml_g, seq):
    bsz, _, rblocks, rb, dml = hf5.shape
    kern = functools.partial(_mlpost_kernel, dh=dml // ML_HEADS)
    blk = (1, GRID_W, 1, rb, dml)
    return pl.pallas_call(
        kern,
        grid=(bsz, rblocks),
        in_specs=[pl.BlockSpec(blk, lambda b, i: (b, 0, i, 0, 0)),
                  pl.BlockSpec(blk, lambda b, i: (b, 0, i, 0, 0)),
                  pl.BlockSpec(blk, lambda b, i: (b, 0, i, 0, 3)),
                  pl.BlockSpec((1, dml), lambda b, i: (0, 0))],
        out_specs=pl.BlockSpec((1, rb * GRID_W, dml), lambda b, i: (b, i, 0)),
        out_shape=jax.ShapeDtypeStruct((bsz, seq, dml), BF16),
        scratch_shapes=[pltpu.VMEM((dml // LANES, GRID_W * COL_PITCH, LANES), F32)],
        compiler_params=_cparams(("parallel", "parallel")),
        name="mlstm_post",
    )(hf5, hb5, ml5, ml_g)


def _gelu_tanh(y):
    return 0.5 * y * (1.0 + jnp.tanh(math.sqrt(2.0 / math.pi) * (y + 0.044715 * (y * y * y))))


def _mixout_kernel(ys_ref, u_ref, ml_ref, x_ref, dsk_ref, gw_ref, gb_ref, wo_ref, ng_ref, mod_ref,
                   rwh_ref, rwl_ref, x1_ref, hx_ref, aff_ref, *, n_exp):
    y = ys_ref[...] + dsk_ref[...] * u_ref[...]
    ge = _gelu_tanh(y)
    z = _dot(ge.astype(BF16), gw_ref[...]) + gb_ref[...]
    s5 = (ge * _sigmoid(z)).astype(BF16)
    yx = _dot(jnp.concatenate([s5, ml_ref[...]], axis=-1), wo_ref[...])
    ng = ng_ref[...]
    mod = mod_ref[0]
    r1 = yx * lax.rsqrt(jnp.mean(yx * yx, axis=-1, keepdims=True) + EPS) * ng[1:2]
    x1 = x_ref[...] + mod[0:1] * r1
    x1_ref[...] = x1
    hx = x1 * lax.rsqrt(jnp.mean(x1 * x1, axis=-1, keepdims=True) + EPS) * ng[2:3]
    hx = hx * (1.0 + mod[1:2]) + mod[2:3]
    hx_ref[...] = hx.astype(hx_ref.dtype)
    h1, h2, h3 = _split3(hx)
    rwh = rwh_ref[...]
    logits = _dot(h1, rwh) + _dot(h2, rwh) + _dot(h3, rwh) + _dot(h1, rwl_ref[...])
    lane = lax.broadcasted_iota(I32, logits.shape, 1)
    logits = jnp.where(lane < n_exp, logits, -jnp.inf)
    e = jnp.exp(logits - jnp.max(logits, axis=-1, keepdims=True))
    aff_ref[...] = e / jnp.sum(e, axis=-1, keepdims=True)


def _mixer_out(ys, u, mlo, x2d, rows_per_batch, dsk, glu_w, glu_b, w_out, norm_g, mod3, rw_hi, rw_lo):
    n, d = x2d.shape
    ds5 = ys.shape[1]
    dml = mlo.shape[1]
    tm = 256
    tiles_per_batch = rows_per_batch // tm
    kern = functools.partial(_mixout_kernel, n_exp=N_EXPERTS)
    row = lambda i: (i, 0)
    const = lambda i: (0, 0)
    return pl.pallas_call(
        kern,
        grid=(n // tm,),
        in_specs=[pl.BlockSpec((tm, ds5), row),
                  pl.BlockSpec((tm, ds5), row),
                  pl.BlockSpec((tm, dml), row),
                  pl.BlockSpec((tm, d), row),
                  pl.BlockSpec((1, ds5), const),
                  pl.BlockSpec((ds5, ds5), const),
                  pl.BlockSpec((1, ds5), const),
                  pl.BlockSpec((d, d), const),
                  pl.BlockSpec((4, d), const),
                  pl.BlockSpec((1, 8, d), lambda i: (i // tiles_per_batch, 0, 0)),
                  pl.BlockSpec((d, LANES), const),
                  pl.BlockSpec((d, LANES), const)],
        out_specs=[pl.BlockSpec((tm, d), row),
                   pl.BlockSpec((tm, d), row),
                   pl.BlockSpec((tm, LANES), row)],
        out_shape=[jax.ShapeDtypeStruct((n, d), F32),
                   jax.ShapeDtypeStruct((n, d), BF16),
                   jax.ShapeDtypeStruct((n, LANES), F32)],
        compiler_params=_cparams(("parallel",)),
        name="mixer_out",
    )(ys, u, mlo, x2d, dsk, glu_w, glu_b, w_out, norm_g, mod3, rw_hi, rw_lo)


def _count(mask):
    c = jnp.sum(jnp.where(mask, 1.0, 0.0), axis=2, keepdims=True)
    return jnp.sum(c, axis=1, keepdims=True)


SELECT_BISECTIONS = 48


def _pow2_neg(n):
    p = jnp.ones(n.shape, F32)
    for i in range(7):
        p = p * jnp.where(((n >> i) & 1) == 1, 2.0 ** -(2 ** i), 1.0)
    return p


def _select_kernel(aff_ref, pos_ref, off_ref, *, cap, n_tok):
    a = aff_ref[0]
    e, nb, _ = a.shape

    def enough(thr):
        return _count(a >= thr) >= cap

    ex_lo = jnp.full((e, 1, 1), -127, I32)
    ex_hi = jnp.full((e, 1, 1), 1, I32)
    for _ in range(7):
        ex_mid = (ex_lo + ex_hi) >> 1
        ok = enough(_pow2_neg(-ex_mid))
        ex_lo = jnp.where(ok, ex_mid, ex_lo)
        ex_hi = jnp.where(ok, ex_hi, ex_mid)
    lo = jnp.where(ex_lo == -127, 0.0, _pow2_neg(jnp.minimum(-ex_lo, 126)))
    hi = jnp.where(ex_hi == 1, 2.0, _pow2_neg(jnp.maximum(-ex_hi, 0)))
    for _ in range(SELECT_BISECTIONS):
        mid = lo + 0.5 * (hi - lo)
        ok = enough(mid)
        lo = jnp.where(ok, mid, lo)
        hi = jnp.where(ok, hi, mid)
    gt = a >= hi
    tie = (a >= lo) & (a < hi)
    need = cap - _count(gt)
    tok = (lax.broadcasted_iota(I32, a.shape, 1) * LANES + lax.broadcasted_iota(I32, a.shape, 2))
    cut = jnp.zeros((e, 1, 1), I32)
    for bit in range(n_tok.bit_length() - 1, -1, -1):
        cand = cut | (1 << bit)
        cut = jnp.where(_count(tie & (tok < cand)) < need, cand, cut)
    sel = gt | (tie & (tok <= cut))
    self32 = jnp.where(sel, 1.0, 0.0).reshape(e * nb, LANES)
    r = lax.broadcasted_iota(I32, (LANES, LANES), 0)
    c = lax.broadcasted_iota(I32, (LANES, LANES), 1)
    incl = _dot(self32.astype(BF16), jnp.where(r <= c, 1.0, 0.0).astype(BF16))
    tot = jnp.broadcast_to(incl[:, LANES - 1:LANES], (e * nb, LANES)).astype(BF16)
    rr = lax.broadcasted_iota(I32, (e * nb, e * nb), 0)
    cc = lax.broadcasted_iota(I32, (e * nb, e * nb), 1)
    same = (rr // nb) == (cc // nb)
    before = jnp.where(same & (cc < rr), 1.0, 0.0).astype(BF16)
    off = _dot(before, tot)
    pos = (off + incl - 1.0).astype(I32)
    pos_ref[0] = jnp.where(self32 > 0.5, pos, -1).reshape(e, nb, LANES)
    off_ref[0] = off.astype(I32).reshape(e, nb, LANES)


def _select(aff_t, cap):
    bsz, e, nb, _ = aff_t.shape
    kern = functools.partial(_select_kernel, cap=cap, n_tok=nb * LANES)
    blk = pl.BlockSpec((1, e, nb, LANES), lambda b: (b, 0, 0, 0))
    return pl.pallas_call(
        kern,
        grid=(bsz,),
        in_specs=[blk],
        out_specs=[blk, blk],
        out_shape=[jax.ShapeDtypeStruct(aff_t.shape, I32)] * 2,
        compiler_params=_cparams(("parallel",)),
        name="ec_select",
    )(aff_t)


def _dispatch_kernel(off_ref, hx_ref, pos_ref, o_ref, acc, *, cap, nb, n_exp):
    b = pl.program_id(0)
    ex = pl.program_id(2)
    acc[...] = jnp.zeros_like(acc)
    kiota = lax.broadcasted_iota(I32, (DISP_W, DISP_TOK), 0)
    bpt = DISP_TOK // TOK_BLK

    def body(i, _):
        j = i * bpt
        start = pl.multiple_of((off_ref[(b * n_exp + ex) * nb + j] // 8) * 8, 8)
        prow = jnp.concatenate([pos_ref[0, 0, pl.ds(j + q, 1), :] for q in range(bpt)], axis=-1)
        onehot = jnp.where(prow - start == kiota, 1.0, 0.0).astype(BF16)
        xj = hx_ref[0, pl.ds(pl.multiple_of(i * DISP_TOK, DISP_TOK), DISP_TOK), :]
        acc[pl.ds(start, DISP_W), :] += _dot(onehot, xj)
        return 0

    lax.fori_loop(0, nb // bpt, body, 0)
    o_ref[0, 0] = acc[0:cap, :].astype(o_ref.dtype)


def _dispatch(off_flat, hx, pos, cap):
    bsz, n_tok, d = hx.shape
    n_exp, nb = pos.shape[1], pos.shape[2]
    dblk = 1024
    kern = functools.partial(_dispatch_kernel, cap=cap, nb=nb, n_exp=n_exp)
    return pl.pallas_call(
        kern,
        grid_spec=pltpu.PrefetchScalarGridSpec(
            num_scalar_prefetch=1,
            grid=(bsz, d // dblk, n_exp),
            in_specs=[pl.BlockSpec((1, n_tok, dblk), lambda b, c, e, off: (b, 0, c)),
                      pl.BlockSpec((1, 1, nb, LANES), lambda b, c, e, off: (b, e, 0, 0))],
            out_specs=pl.BlockSpec((1, 1, cap, dblk), lambda b, c, e, off: (b, e, 0, c)),
            scratch_shapes=[pltpu.VMEM((cap + DISP_W, dblk), F32)]),
        out_shape=jax.ShapeDtypeStruct((bsz, n_exp, cap, d), BF16),
        compiler_params=_cparams(("parallel", "parallel", "arbitrary")),
        name="ec_dispatch",
    )(off_flat, hx, pos)


def _ffn_kernel(x_ref, wg_ref, wu_ref, wd_ref, o_ref, acc):
    f = pl.program_id(2)

    @pl.when(f == 0)
    def _():
        acc[...] = jnp.zeros_like(acc)

    x = x_ref[0, 0]
    a = _dot(x, wg_ref[0].astype(BF16))
    u = _dot(x, wu_ref[0].astype(BF16))
    hmid = (_silu(a) * u).astype(BF16)
    acc[...] += _dot(hmid, wd_ref[0].astype(BF16))

    @pl.when(f == pl.num_programs(2) - 1)
    def _():
        o_ref[0, 0] = acc[...].astype(o_ref.dtype)


def _expert_ffn(xs, w_gate, w_up, w_down):
    bsz, n_exp, cap, d = xs.shape
    ff = w_gate.shape[2]
    tf = 256
    return pl.pallas_call(
        _ffn_kernel,
        grid=(n_exp, bsz, ff // tf),
        in_specs=[pl.BlockSpec((1, 1, cap, d), lambda e, b, f: (b, e, 0, 0)),
                  pl.BlockSpec((1, d, tf), lambda e, b, f: (e, 0, f)),
                  pl.BlockSpec((1, d, tf), lambda e, b, f: (e, 0, f)),
                  pl.BlockSpec((1, tf, d), lambda e, b, f: (e, f, 0))],
        out_specs=pl.BlockSpec((1, 1, cap, d), lambda e, b, f: (b, e, 0, 0)),
        out_shape=jax.ShapeDtypeStruct((bsz, n_exp, cap, d), BF16),
        scratch_shapes=[pltpu.VMEM((cap, d), F32)],
        compiler_params=_cparams(("parallel", "parallel", "arbitrary")),
        name="expert_ffn",
    )(xs, w_gate, w_up, w_down)


def _combine_kernel(off_ref, y_ref, pos_ref, aff_ref, o_ref, *, cap, nb, n_exp, tok, win):
    b = pl.program_id(0)
    j = pl.program_id(2)
    kiota = lax.broadcasted_iota(I32, (tok, win), 1)
    pos = pos_ref[0]
    aff = aff_ref[0]
    for ex in range(n_exp):
        start = (off_ref[(b * n_exp + ex) * nb + j * (tok // TOK_BLK)] // 16) * 16
        start = pl.multiple_of(jnp.minimum(start, cap - win), 16)
        onehot = jnp.where(pos[:, ex:ex + 1] - start == kiota, 1.0, 0.0).astype(BF16)
        contrib = aff[:, ex:ex + 1] * _dot(onehot, y_ref[0, ex, pl.ds(start, win), :])
        if ex == 0:
            o_ref[0] = contrib
        else:
            o_ref[0] += contrib


def _combine(off_flat, y, pos_t, aff, cap):
    bsz, n_exp, _, d = y.shape
    n_tok = pos_t.shape[1]
    nb = n_tok // TOK_BLK
    tok = 512 if cap >= 512 + 16 else TOK_BLK
    win = tok + 16
    dblk = 512
    kern = functools.partial(_combine_kernel, cap=cap, nb=nb, n_exp=n_exp, tok=tok, win=win)
    return pl.pallas_call(
        kern,
        grid_spec=pltpu.PrefetchScalarGridSpec(
            num_scalar_prefetch=1,
            grid=(bsz, d // dblk, n_tok // tok),
            in_specs=[pl.BlockSpec((1, n_exp, cap, dblk), lambda b, c, j, off: (b, 0, 0, c)),
                      pl.BlockSpec((1, tok, n_exp), lambda b, c, j, off: (b, j, 0)),
                      pl.BlockSpec((1, tok, n_exp), lambda b, c, j, off: (b, j, 0))],
            out_specs=pl.BlockSpec((1, tok, dblk), lambda b, c, j, off: (b, j, c))),
        out_shape=jax.ShapeDtypeStruct((bsz, n_tok, d), F32),
        compiler_params=_cparams(("parallel", "parallel", "arbitrary")),
        name="ec_combine",
    )(off_flat, y, pos_t, aff)


def _final_kernel(x1_ref, moe_ref, g_ref, mod_ref, o_ref):
    m = moe_ref[...]
    r = m * lax.rsqrt(jnp.mean(m * m, axis=-1, keepdims=True) + EPS) * g_ref[...]
    o_ref[...] = x1_ref[...] + mod_ref[0] * r


def _final(x1, moe, g3, g2, rows_per_batch):
    n, d = x1.shape
    tm = 512
    tiles_per_batch = rows_per_batch // tm
    return pl.pallas_call(
        _final_kernel,
        grid=(n // tm,),
        in_specs=[pl.BlockSpec((tm, d), lambda i: (i, 0)),
                  pl.BlockSpec((tm, d), lambda i: (i, 0)),
                  pl.BlockSpec((1, d), lambda i: (0, 0)),
                  pl.BlockSpec((1, 1, d), lambda i: (i // tiles_per_batch, 0, 0))],
        out_specs=pl.BlockSpec((tm, d), lambda i: (i, 0)),
        out_shape=jax.ShapeDtypeStruct((n, d), F32),
        compiler_params=_cparams(("parallel",)),
        name="final_residual",
    )(x1, moe, g3, g2)


def _layer(x, c, ctx, c_ctx, ada_w, ada_b, norm_g, w_in, s5_a_re, s5_a_im, s5_log_dt, s5_b_re,
           s5_b_im, s5_c_re, s5_c_im, s5_d, s5_glu_w, s5_glu_b, ml_conv_w, ml_conv_b, ml_gate_b,
           ml_norm_g, w_out, router_w, exp_w_gate, exp_w_up, exp_w_down):
    bsz, seq, d = x.shape
    lctx = ctx.shape[1]
    ds5 = s5_d.shape[0]
    dml = ml_norm_g.shape[0]
    ngroups = ds5 // S5_GROUP
    rows = seq // GRID_W
    n = bsz * seq
    cap = EC_FACTOR * seq // N_EXPERTS
    assert cap >= TOK_BLK + 16 and seq % DISP_TOK == 0 and lctx % ML_T == 0 and seq % ML_T == 0

    cond = jnp.concatenate([c, c_ctx[None], jnp.zeros((8 - bsz - 1, d), F32)], axis=0)
    mod = _modulation(cond, ada_w, ada_b).reshape(8, N_MOD, d)
    sh1, sc1, g1, sh2, sc2, g2 = [mod[:, i] for i in range(N_MOD)]

    nmain = ds5 + 4 * dml
    w_s5 = w_in[:, :ds5].astype(BF16)
    w_ml = w_in[:, ds5:nmain].astype(BF16)
    w_gate = jnp.pad(w_in[:, nmain:], ((0, 0), (0, LANES - 4 * ML_HEADS))).astype(BF16)
    g0 = norm_g[0:1]
    sc, sh = sc1[:, None], sh1[:, None]
    of_sample = lambda b: b
    of_ctx = lambda b: bsz
    ux = _in_projection_s5(x, g0, sc, sh, of_sample, w_s5)
    uc = _in_projection_s5(ctx, g0, sc, sh, of_ctx, w_s5)
    ltot = seq + lctx
    n_slots = ltot // rows
    ml5, gates5 = _in_projection_ml(x, g0, sc, sh, of_sample, w_ml, w_gate, n_slots)
    ml5, gates5 = _in_projection_ml(ctx, g0, sc, sh, of_ctx, w_ml, w_gate, n_slots,
                                    ctx_into=(ml5, gates5, GRID_W))

    m, bp, cpt, avec = _s5_operators(s5_a_re, s5_a_im, s5_log_dt, s5_b_re, s5_b_im, s5_c_re, s5_c_im)
    ys = _s5_scan(ux, uc, m, bp, cpt, avec).reshape(n, ds5)

    ml = ml5.reshape(bsz, ltot, 4 * dml)
    gates = gates5.reshape(bsz, ltot, LANES)
    conv_w8 = jnp.pad(ml_conv_w, ((0, 8 - CONV_K), (0, 0)))
    qscale = jnp.concatenate([jnp.full((1, dml), (dml // ML_HEADS) ** -0.5, F32),
                              jnp.ones((1, dml), F32)], axis=1)
    qk = _conv_silu(ml, conv_w8, ml_conv_b[None], qscale, (0, seq), (seq, ltot))
    gbias = ml_gate_b.reshape(1, 4 * ML_HEADS)
    gates_t = gates[..., :4 * ML_HEADS].swapaxes(1, 2)
    hf, hb = _mlstm_scan(qk, ml, gates, gates_t, gbias, gbias.reshape(4 * ML_HEADS, 1), lctx // ML_T)
    mlo = _ml_post(hf.reshape(ml5.shape[:4] + (dml,)), hb.reshape(ml5.shape[:4] + (dml,)), ml5,
                   ml_norm_g[None], seq).reshape(n, dml)

    mod3 = jnp.stack([g1[:bsz], sc2[:bsz], sh2[:bsz]], axis=1)
    mod3 = jnp.concatenate([mod3, jnp.zeros((bsz, 5, d), F32)], axis=1)
    rw = jnp.pad(router_w, ((0, 0), (0, LANES - N_EXPERTS)))
    rw_hi = rw.astype(BF16)
    rw_lo = (rw - rw_hi.astype(F32)).astype(BF16)
    x1, hx2, aff = _mixer_out(ys, ux.reshape(n, ds5), mlo, x.reshape(n, d), seq, s5_d[None],
                              s5_glu_w.astype(BF16), s5_glu_b[None], w_out.astype(BF16), norm_g, mod3,
                              rw_hi, rw_lo)

    nb = seq // TOK_BLK
    aff16 = aff[:, :N_EXPERTS].reshape(bsz, seq, N_EXPERTS)
    aff_t = aff16.swapaxes(1, 2).reshape(bsz, N_EXPERTS, nb, LANES)
    pos, off = _select(aff_t, cap)
    off_flat = off[..., 0].reshape(-1)
    pos_t = pos.reshape(bsz, N_EXPERTS, seq).swapaxes(1, 2)
    xs = _dispatch(off_flat, hx2.reshape(bsz, seq, d), pos, cap)
    ye = _expert_ffn(xs, exp_w_gate, exp_w_up, exp_w_down)
    moe = _combine(off_flat, ye, pos_t, aff16, cap)
    out = _final(x1, moe.reshape(n, d), norm_g[3:4], g2[:bsz, None], seq)
    return out.reshape(bsz, seq, d)


def kernel(x, c, ctx, c_ctx, ada_w, ada_b, norm_g, w_in, s5_a_re, s5_a_im, s5_log_dt, s5_b_re, s5_b_im, s5_c_re, s5_c_im, s5_d, s5_glu_w, s5_glu_b, ml_conv_w, ml_conv_b, ml_gate_b, ml_norm_g, w_out, router_w, exp_w_gate, exp_w_up, exp_w_down):
    depth = ada_w.shape[0]
    assert depth == 1, "context outputs are only produced when another layer follows"
    li = 0
    return _layer(x, c, ctx, c_ctx, ada_w[li], ada_b[li], norm_g[li], w_in[li], s5_a_re[li],
                  s5_a_im[li], s5_log_dt[li], s5_b_re[li], s5_b_im[li], s5_c_re[li], s5_c_im[li],
                  s5_d[li], s5_glu_w[li], s5_glu_b[li], ml_conv_w[li], ml_conv_b[li], ml_gate_b[li],
                  ml_norm_g[li], w_out[li], router_w[li], exp_w_gate[li], exp_w_up[li],
                  exp_w_down[li])
```

```python
import functools
import math

import jax
import jax.numpy as jnp
from jax import lax
from jax.experimental import pallas as pl
from jax.experimental.pallas import tpu as pltpu

F32 = jnp.float32
BF16 = jnp.bfloat16
I32 = jnp.int32

EPS = 1e-6
GRID_W = 64
S5_GROUP = 16
S5_STATE = 64
S5_T = 16
ML_HEADS = 4
ML_T = 256
CONV_K = 5
N_EXPERTS = 16
EC_FACTOR = 2
N_MOD = 6
LANES = 128
TOK_BLK = 128
DISP_TOK = 512
DISP_W = DISP_TOK + 8
VMEM_LIMIT = 56 * 1024 * 1024


def _cparams(sem):
    return pltpu.CompilerParams(dimension_semantics=sem, vmem_limit_bytes=VMEM_LIMIT)


def _sigmoid(x):
    return 1.0 / (1.0 + jnp.exp(-x))


def _silu(x):
    return x * _sigmoid(x)


def _log_sigmoid(x):
    return jnp.minimum(x, 0.0) - jnp.log(1.0 + jnp.exp(-jnp.abs(x)))


def _split3(x):
    x1 = x.astype(BF16)
    r = x - x1.astype(F32)
    x2 = r.astype(BF16)
    x3 = (r - x2.astype(F32)).astype(BF16)
    return x1, x2, x3


def _dot(a, b):
    return jnp.dot(a, b, preferred_element_type=F32)


def _dot_nt(a, b):
    return lax.dot_general(a, b, (((1,), (1,)), ((), ())), preferred_element_type=F32)


def _dot_tn(a, b):
    return lax.dot_general(a, b, (((0,), (0,)), ((), ())), preferred_element_type=F32)


def _dot3_left(a_f32, b_bf16):
    a1, a2, a3 = _split3(a_f32)
    return _dot(a1, b_bf16) + _dot(a2, b_bf16) + _dot(a3, b_bf16)


def _dot3_right(a_bf16, b_f32):
    b1, b2, b3 = _split3(b_f32)
    return _dot(a_bf16, b1) + _dot(a_bf16, b2) + _dot(a_bf16, b3)


def _mod_kernel(c_ref, w_ref, b_ref, o_ref):
    c = c_ref[...]
    s = _silu(c).astype(BF16)
    o_ref[...] = _dot(s, w_ref[...].astype(BF16)) + b_ref[...]


def _modulation(cond8, ada_w, ada_b):
    d, n = ada_w.shape
    tn = 1024
    return pl.pallas_call(
        _mod_kernel,
        grid=(n // tn,),
        in_specs=[pl.BlockSpec((8, d), lambda j: (0, 0)),
                  pl.BlockSpec((d, tn), lambda j: (0, j)),
                  pl.BlockSpec((1, tn), lambda j: (0, j))],
        out_specs=pl.BlockSpec((8, tn), lambda j: (0, j)),
        out_shape=jax.ShapeDtypeStruct((8, n), F32),
        compiler_params=_cparams(("parallel",)),
        name="modulation",
    )(cond8, ada_w, ada_b.reshape(1, n))


def _norm_mod(x, g, sc, sh):
    ms = jnp.mean(x * x, axis=-1, keepdims=True)
    h = x * lax.rsqrt(ms + EPS) * g
    return (h * (1.0 + sc) + sh).astype(BF16)


def _inproj_s5_kernel(x_ref, g_ref, sc_ref, sh_ref, w_ref, o_ref, h_scr):
    @pl.when(pl.program_id(2) == 0)
    def _():
        h_scr[...] = _norm_mod(x_ref[0], g_ref[...], sc_ref[0], sh_ref[0])

    o_ref[0] = _dot(h_scr[...], w_ref[...])


def _in_projection_s5(x, g0, sc, sh, mod_of_batch, w_s5):
    bsz, length, d = x.shape
    ds5 = w_s5.shape[1]
    tm = min(1024, length)
    tn = 512
    return pl.pallas_call(
        _inproj_s5_kernel,
        grid=(bsz, length // tm, ds5 // tn),
        in_specs=[pl.BlockSpec((1, tm, d), lambda b, i, j: (b, i, 0)),
                  pl.BlockSpec((1, d), lambda b, i, j: (0, 0)),
                  pl.BlockSpec((1, 1, d), lambda b, i, j: (mod_of_batch(b), 0, 0)),
                  pl.BlockSpec((1, 1, d), lambda b, i, j: (mod_of_batch(b), 0, 0)),
                  pl.BlockSpec((d, tn), lambda b, i, j: (0, j))],
        out_specs=pl.BlockSpec((1, tm, tn), lambda b, i, j: (b, i, j)),
        out_shape=jax.ShapeDtypeStruct((bsz, length, ds5), F32),
        scratch_shapes=[pltpu.VMEM((tm, d), BF16)],
        compiler_params=_cparams(("parallel", "parallel", "arbitrary")),
        name="in_projection_s5",
    )(x, g0, sc, sh, w_s5)


def _inproj_ml_kernel(x_ref, c_ref, g_ref, sc_ref, sh_ref, csc_ref, csh_ref, w_ref, wg_ref,
                      o_ref, og_ref, h_scr, x_scr):
    rb = o_ref.shape[3]
    nlb = x_ref.shape[2] // LANES

    @pl.when(pl.program_id(2) == 0)
    def _():
        for c in range(nlb):
            for r in range(rb):
                x_scr[c, r * ROW_PITCH:r * ROW_PITCH + GRID_W, :] = (
                    x_ref[0, r * GRID_W:(r + 1) * GRID_W, c * LANES:(c + 1) * LANES])
        for w in range(GRID_W):
            xw = jnp.concatenate([x_scr[c, pl.ds(w, rb, stride=ROW_PITCH), :] for c in range(nlb)],
                                 axis=-1)
            h_scr[w * rb:(w + 1) * rb, :] = _norm_mod(xw, g_ref[...], sc_ref[0], sh_ref[0])
        for s in range(c_ref.shape[1]):
            h_scr[(GRID_W + s) * rb:(GRID_W + s + 1) * rb, :] = _norm_mod(
                c_ref[0, s, 0], g_ref[...], csc_ref[0], csh_ref[0])
        og_ref[0] = _dot(h_scr[...], wg_ref[...]).reshape(og_ref.shape[1:])

    o_ref[0] = _dot(h_scr[...], w_ref[...]).reshape(o_ref.shape[1:]).astype(o_ref.dtype)


ML_RB = 16
ROW_PITCH = GRID_W + 8


def _in_projection_ml(x, ctx, g0, sc, sh, ctx_mod, w_ml, w_gate):
    bsz, length, d = x.shape
    lctx = ctx.shape[1]
    nml = w_ml.shape[1]
    tn = 512
    rows = length // GRID_W
    rblocks = rows // ML_RB
    nslot_c = lctx // rows
    assert nslot_c * rows == lctx and rblocks * ML_RB == rows
    n_slots = GRID_W + nslot_c
    tm = ML_RB * GRID_W
    ctx5 = ctx.reshape(bsz, nslot_c, rblocks, ML_RB, d)
    return pl.pallas_call(
        _inproj_ml_kernel,
        grid=(bsz, rblocks, nml // tn),
        in_specs=[pl.BlockSpec((1, tm, d), lambda b, i, j: (b, i, 0)),
                  pl.BlockSpec((1, nslot_c, 1, ML_RB, d), lambda b, i, j: (b, 0, i, 0, 0)),
                  pl.BlockSpec((1, d), lambda b, i, j: (0, 0)),
                  pl.BlockSpec((1, 1, d), lambda b, i, j: (b, 0, 0)),
                  pl.BlockSpec((1, 1, d), lambda b, i, j: (b, 0, 0)),
                  pl.BlockSpec((1, 1, d), lambda b, i, j: (ctx_mod, 0, 0)),
                  pl.BlockSpec((1, 1, d), lambda b, i, j: (ctx_mod, 0, 0)),
                  pl.BlockSpec((d, tn), lambda b, i, j: (0, j)),
                  pl.BlockSpec((d, LANES), lambda b, i, j: (0, 0))],
        out_specs=[pl.BlockSpec((1, n_slots, 1, ML_RB, tn), lambda b, i, j: (b, 0, i, 0, j)),
                   pl.BlockSpec((1, n_slots, 1, ML_RB, LANES), lambda b, i, j: (b, 0, i, 0, 0))],
        out_shape=[jax.ShapeDtypeStruct((bsz, n_slots, rblocks, ML_RB, nml), BF16),
                   jax.ShapeDtypeStruct((bsz, n_slots, rblocks, ML_RB, LANES), F32)],
        scratch_shapes=[pltpu.VMEM((n_slots * ML_RB, d), BF16),
                        pltpu.VMEM((d // LANES, ML_RB * ROW_PITCH, LANES), F32)],
        compiler_params=_cparams(("parallel", "parallel", "arbitrary")),
        name="in_projection_ml",
    )(x, ctx5, g0, sc, sh, sc, sh, w_ml, w_gate)


def _dot_nt6(a, b):
    a1, a2, a3 = _split3(a)
    b1, b2, b3 = _split3(b)
    return (_dot_nt(a1, b1) + (_dot_nt(a1, b2) + _dot_nt(a2, b1))
            + (_dot_nt(a1, b3) + _dot_nt(a3, b1) + _dot_nt(a2, b2)))


S5_LANES = 4 * S5_STATE


def _s5ops_kernel(prm_ref, bt_ref, c_ref, rexp_ref, rc_ref, place_ref, m_ref, bp_ref, cpt_ref, av_ref):
    t = S5_T
    prm = prm_ref[0]
    lr, li = jnp.minimum(prm[0:1], -1e-4), prm[1:2]
    dt = jnp.exp(prm[2:3])
    zr, zi = lr * dt, li * dt
    tau = lax.broadcasted_iota(I32, (32, S5_LANES), 0).astype(F32)
    mag, ang = jnp.exp(tau * zr), tau * zi
    pwr, pwi = mag * jnp.cos(ang), mag * jnp.sin(ang)
    xr, xi = pwr[1:2] - 1.0, pwi[1:2]
    den = lr * lr + li * li
    qr, qi = (xr * lr + xi * li) / den, (xi * lr - xr * li) / den
    btr, bti = bt_ref[0, 0], bt_ref[0, 1]
    bbr, bbi = qr * btr - qi * bti, qr * bti + qi * btr
    cr, ci = c_ref[0, 0], c_ref[0, 1]

    lane = lax.broadcasted_iota(I32, (t * S5_GROUP, S5_LANES), 1)
    fwd = ((lane // S5_STATE) % 2) == 0

    def expand(r, x):
        return _dot3_right(r, x)

    e_t = [expand(rexp_ref[0], z) for z in (pwr, pwi)]
    e_r = [expand(rexp_ref[1], z) for z in (pwr, pwi)]
    e_p = [expand(rexp_ref[2], z) for z in (pwr, pwi)]
    e_q = [expand(rexp_ref[3], z) for z in (pwr, pwi)]
    pk = [jnp.where(fwd, e_t[i], e_r[i]) for i in range(2)]
    pb = [jnp.where(fwd, e_r[i], e_t[i]) for i in range(2)]
    pc = [jnp.where(fwd, e_p[i], e_q[i]) for i in range(2)]
    cxr, cxi = expand(rc_ref[...], cr), expand(rc_ref[...], ci)
    bxr, bxi = expand(rc_ref[...], bbr), expand(rc_ref[...], bbi)

    wkr, wki = pk[0] * cxr - pk[1] * cxi, pk[0] * cxi + pk[1] * cxr
    row_blk = lax.broadcasted_iota(I32, (4 * S5_GROUP, S5_LANES), 0) // S5_GROUP
    lane_blk = lax.broadcasted_iota(I32, (4 * S5_GROUP, S5_LANES), 1) // S5_STATE
    own = row_blk == lane_blk
    bmr = jnp.where(own, jnp.concatenate([bbr] * 4, axis=0), 0.0)
    bmi = jnp.where(own, jnp.concatenate([bbi] * 4, axis=0), 0.0)
    kt = _dot_nt6(bmr, wkr) - _dot_nt6(bmi, wki)

    ln = lax.broadcasted_iota(I32, (S5_GROUP, t * S5_GROUP), 1)
    for gi in range(2):
        ktf = kt[(2 * gi) * S5_GROUP:(2 * gi + 1) * S5_GROUP]
        ktb = kt[(2 * gi + 1) * S5_GROUP:(2 * gi + 2) * S5_GROUP]
        blocks = []
        for s in range(t):
            f = ktf if s == 0 else pltpu.roll(ktf, S5_GROUP * s, axis=1)
            sh = (t * S5_GROUP - S5_GROUP * (t - 1 - s)) % (t * S5_GROUP)
            bk = ktb if sh == 0 else pltpu.roll(ktb, sh, axis=1)
            blocks.append(jnp.where(ln >= S5_GROUP * s, f, 0.0)
                          + jnp.where(ln < S5_GROUP * (s + 1), bk, 0.0))
        m_ref[0, gi] = jnp.concatenate(blocks, axis=0).astype(BF16)

    inj = jnp.concatenate([pb[0] * bxr - pb[1] * bxi, pb[0] * bxi + pb[1] * bxr], axis=-1).astype(BF16)
    out = jnp.concatenate([pc[0] * cxr - pc[1] * cxi, -(pc[0] * cxi + pc[1] * cxr)], axis=-1).astype(BF16)
    for gi in range(2):
        bp_ref[0, gi * t * S5_GROUP:(gi + 1) * t * S5_GROUP, :] = _dot(inj, place_ref[gi]).astype(BF16)
        cpt_ref[0, gi * t * S5_GROUP:(gi + 1) * t * S5_GROUP, :] = _dot(out, place_ref[gi]).astype(BF16)
    a_t = jnp.concatenate([pwr[t:t + 8], pwi[t:t + 8]], axis=-1)
    av_ref[0] = _dot3_left(a_t, place_ref[0]) + _dot3_left(a_t, place_ref[1])


def _s5_operators(a_re, a_im, log_dt, b_re, b_im, c_re, c_im):
    t = S5_T
    g, p = a_re.shape[1], a_re.shape[2]
    npair = g // 2

    def lanes(z):
        return z.reshape(2, npair, 2, p).transpose(1, 2, 0, 3).reshape(npair, 4 * p)

    prm = jnp.stack([lanes(a_re.astype(F32)), lanes(a_im.astype(F32)),
                     lanes(jnp.broadcast_to(log_dt.astype(F32)[..., None], a_re.shape))], axis=1)
    prm = jnp.concatenate([prm, jnp.zeros((npair, 5, 4 * p), F32)], axis=1)

    def tile_d(z):
        z = z.reshape(npair, 2, S5_GROUP, p).transpose(0, 2, 1, 3)
        return jnp.broadcast_to(z[:, :, :, None, :], (npair, S5_GROUP, 2, 2, p)).reshape(npair, S5_GROUP, 4 * p)

    bt = jnp.stack([tile_d(b_re.astype(F32).swapaxes(1, 2)), tile_d(b_im.astype(F32).swapaxes(1, 2))], axis=1)
    ct = jnp.stack([tile_d(c_re.astype(F32)), tile_d(c_im.astype(F32))], axis=1)

    row = jnp.arange(t * S5_GROUP)
    tau, ch = row // S5_GROUP, row % S5_GROUP
    e = jnp.arange(32)
    rexp = jnp.stack([(expo[:, None] == e[None, :]) for expo in (tau, t - 1 - tau, tau + 1, t - tau)]
                     ).astype(BF16)
    rc = (ch[:, None] == jnp.arange(S5_GROUP)[None, :]).astype(BF16)
    src = jnp.arange(2 * 4 * p)
    reim, gi_s, d_s, p_s = src // (4 * p), (src // (2 * p)) % 2, (src // p) % 2, src % p
    dst = d_s * (4 * p) + reim * (2 * p) + gi_s * p + p_s
    col = jnp.arange(8 * p)
    place = jnp.stack([((dst[:, None] == col[None, :]) & (gi_s[:, None] == gi)) for gi in range(2)]
                      ).astype(BF16)
    blk3 = lambda shape: pl.BlockSpec((1,) + shape, lambda i: (i,) + (0,) * len(shape))
    full = lambda arr: pl.BlockSpec(arr.shape, lambda i: (0,) * arr.ndim)
    m, bp, cpt, av = pl.pallas_call(
        _s5ops_kernel,
        grid=(npair,),
        in_specs=[blk3((8, 4 * p)), blk3((2, S5_GROUP, 4 * p)), blk3((2, S5_GROUP, 4 * p)),
                  full(rexp), full(rc), full(place)],
        out_specs=[blk3((2, t * S5_GROUP, t * S5_GROUP)), blk3((2 * t * S5_GROUP, 8 * p)),
                   blk3((2 * t * S5_GROUP, 8 * p)), blk3((8, 8 * p))],
        out_shape=[jax.ShapeDtypeStruct((npair, 2, t * S5_GROUP, t * S5_GROUP), BF16),
                   jax.ShapeDtypeStruct((npair, 2 * t * S5_GROUP, 8 * p), BF16),
                   jax.ShapeDtypeStruct((npair, 2 * t * S5_GROUP, 8 * p), BF16),
                   jax.ShapeDtypeStruct((npair, 8, 8 * p), F32)],
        compiler_params=_cparams(("parallel",)),
        name="s5_operators",
    )(prm, bt, ct, rexp, rc, place)
    avec = av[:, 0].reshape(npair, 4, 2 * p)
    return m, bp, cpt, jnp.concatenate([avec, jnp.zeros_like(avec)], axis=1)


def _s5_permutation():
    i = jnp.arange(8 * 8 * S5_GROUP)
    t8, j, c = i // (8 * S5_GROUP), (i // S5_GROUP) % 8, i % S5_GROUP
    dst = j * (8 * S5_GROUP) + t8 * S5_GROUP + c
    e1 = (dst[:, None] == i[None, :]).astype(BF16)
    return e1, e1.T


S5_PAIRS = 4


def _s5_kernel(ux_ref, uc_ref, e1_ref, e1t_ref, m_ref, bp_ref, cp_ref, av_ref, y_ref, s_scr, hp_scr,
               tab_scr, *, cc, xc):
    nc = cc + xc

    def chunk_rows(ref, nchunks):
        cols = [ref[0, pl.ds(t, nchunks, stride=S5_T), :].astype(BF16) for t in range(S5_T)]
        return [_dot(jnp.concatenate(cols[8 * h:8 * h + 8], axis=-1), e1_ref[...]).astype(BF16)
                for h in range(2)]

    def pair_lanes(halves, p):
        return jnp.concatenate([halves[h][:, j * LANES:(j + 1) * LANES]
                                for j in (2 * p, 2 * p + 1) for h in range(2)], axis=-1)

    pc = chunk_rows(uc_ref, cc)
    px = chunk_rows(ux_ref, xc)
    for p in range(S5_PAIRS):
        s_scr[0:cc, p * 512:(p + 1) * 512] = _dot(pair_lanes(pc, p), bp_ref[p])
        s_scr[cc:nc, p * 512:(p + 1) * 512] = _dot(pair_lanes(px, p), bp_ref[p])

    av = av_ref[...]
    sub = lax.broadcasted_iota(I32, (8, 128), 0)

    def cmul(ar, ai, br, bi):
        return ar * br - ai * bi, ar * bi + ai * br

    for p in range(S5_PAIRS):
        for d in range(2):
            a1 = (av[p, 2 * d:2 * d + 1], av[p, 2 * d + 1:2 * d + 2])
            a2 = cmul(*a1, *a1)
            a4 = cmul(*a2, *a2)
            pw_re, pw_im = jnp.zeros((8, 128), F32), jnp.zeros((8, 128), F32)
            cur = a1
            for k in range(8):
                r = k if d == 0 else 7 - k
                pw_re = jnp.where(sub == r, cur[0], pw_re)
                pw_im = jnp.where(sub == r, cur[1], pw_im)
                cur = cmul(*cur, *a1)
            for q, z in enumerate((a1, a2, a4)):
                tab_scr[p, d, 2 * q] = jnp.broadcast_to(z[0], (8, 128))
                tab_scr[p, d, 2 * q + 1] = jnp.broadcast_to(z[1], (8, 128))
            tab_scr[p, d, 6] = pw_re
            tab_scr[p, d, 7] = pw_im

    def scan8(s_re, s_im, h_re, h_im, p, d):
        back = d == 1
        x_re, x_im = s_re, s_im
        for q, k in enumerate((1, 2, 4)):
            keep = (sub < 8 - k) if back else (sub >= k)
            sh = (8 - k) if back else k
            y_re = jnp.where(keep, pltpu.roll(x_re, sh, axis=0), 0.0)
            y_im = jnp.where(keep, pltpu.roll(x_im, sh, axis=0), 0.0)
            m_re, m_im = cmul(tab_scr[p, d, 2 * q], tab_scr[p, d, 2 * q + 1], y_re, y_im)
            x_re, x_im = x_re + m_re, x_im + m_im
        c_re, c_im = cmul(tab_scr[p, d, 6], tab_scr[p, d, 7], h_re, h_im)
        x_re, x_im = x_re + c_re, x_im + c_im
        edge = 7 if back else 0
        sh = 7 if back else 1
        e_re = jnp.where(sub == edge, h_re, pltpu.roll(x_re, sh, axis=0))
        e_im = jnp.where(sub == edge, h_im, pltpu.roll(x_im, sh, axis=0))
        last = 0 if back else 7
        return e_re, e_im, x_re[last:last + 1], x_im[last:last + 1]

    def step(row_f, row_b, carry):
        rf = pl.multiple_of(row_f, 8)
        rb = pl.multiple_of(row_b, 8)
        new = []
        for p in range(S5_PAIRS):
            hr, hi, gr, gi = carry[4 * p:4 * p + 4]
            c0 = p * 512
            p_re, p_im, hr, hi = scan8(s_scr[pl.ds(rf, 8), c0:c0 + 128], s_scr[pl.ds(rf, 8), c0 + 128:c0 + 256],
                                       hr, hi, p, 0)
            hp_scr[pl.ds(rf, 8), c0:c0 + 128] = p_re
            hp_scr[pl.ds(rf, 8), c0 + 128:c0 + 256] = p_im
            p_re, p_im, gr, gi = scan8(s_scr[pl.ds(rb, 8), c0 + 256:c0 + 384], s_scr[pl.ds(rb, 8), c0 + 384:c0 + 512],
                                       gr, gi, p, 1)
            hp_scr[pl.ds(rb, 8), c0 + 256:c0 + 384] = p_re
            hp_scr[pl.ds(rb, 8), c0 + 384:c0 + 512] = p_im
            new += [hr, hi, gr, gi]
        return tuple(new)

    z = jnp.zeros((1, 128), F32)
    carry = lax.fori_loop(0, cc // 8, lambda i, c: step(8 * i, cc - 8 - 8 * i, c), (z,) * (4 * S5_PAIRS))
    lax.fori_loop(0, xc // 8, lambda i, c: step(cc + 8 * i, nc - 8 - 8 * i, c), carry)

    ys = []
    for p in range(S5_PAIRS):
        ux = pair_lanes(px, p)
        y = _dot_nt(hp_scr[cc:nc, p * 512:(p + 1) * 512].astype(BF16), cp_ref[p])
        ys.append(y + jnp.concatenate([_dot(ux[:, 0:256], m_ref[p, 0]), _dot(ux[:, 256:512], m_ref[p, 1])],
                                      axis=-1))
    for h in range(2):
        yh = jnp.concatenate([ys[j // 2][:, (j % 2) * 256 + h * LANES:(j % 2) * 256 + (h + 1) * LANES]
                              for j in range(8)], axis=-1).astype(BF16)
        r = _dot(yh, e1t_ref[...])
        for t8 in range(8):
            y_ref[0, pl.ds(8 * h + t8, xc, stride=S5_T), :] = r[:, t8 * LANES:(t8 + 1) * LANES]


def _s5_scan(ux, uc, m, bp, cp, avec):
    bsz, seq, ds5 = ux.shape
    lctx = uc.shape[1]
    cc, xc = lctx // S5_T, seq // S5_T
    assert cc % 8 == 0 and xc % 8 == 0
    e1, e1t = _s5_permutation()
    kern = functools.partial(_s5_kernel, cc=cc, xc=xc)
    nblk = ds5 // LANES
    const = lambda j, b: (0, 0)
    return pl.pallas_call(
        kern,
        grid=(nblk, bsz),
        in_specs=[pl.BlockSpec((1, seq, LANES), lambda j, b: (b, 0, j)),
                  pl.BlockSpec((1, lctx, LANES), lambda j, b: (b, 0, j)),
                  pl.BlockSpec(e1.shape, const),
                  pl.BlockSpec(e1.shape, const),
                  pl.BlockSpec((S5_PAIRS, 2, 256, 256), lambda j, b: (j, 0, 0, 0)),
                  pl.BlockSpec((S5_PAIRS, 512, 512), lambda j, b: (j, 0, 0)),
                  pl.BlockSpec((S5_PAIRS, 512, 512), lambda j, b: (j, 0, 0)),
                  pl.BlockSpec((S5_PAIRS, 8, 128), lambda j, b: (j, 0, 0))],
        out_specs=pl.BlockSpec((1, seq, LANES), lambda j, b: (b, 0, j)),
        out_shape=jax.ShapeDtypeStruct((bsz, seq, ds5), F32),
        scratch_shapes=[pltpu.VMEM((cc + xc, S5_PAIRS * 512), F32),
                        pltpu.VMEM((cc + xc, S5_PAIRS * 512), F32),
                        pltpu.VMEM((S5_PAIRS, 2, 8, 8, 128), F32)],
        compiler_params=_cparams(("parallel", "parallel")),
        name="s5_scan",
    )(ux, uc, e1, e1t, m, bp, cp, avec)


def _conv_kernel(x_ref, p_ref, n_ref, w_ref, b_ref, sc_ref, o_ref, scr, *, rows, seq_tiles):
    i = pl.program_id(1)
    first = functools.reduce(jnp.logical_or, [i == s for s in seq_tiles[0]])
    last = functools.reduce(jnp.logical_or, [i == s for s in seq_tiles[1]])
    pm = jnp.where(first, 0.0, 1.0)
    nm = jnp.where(last, 0.0, 1.0)
    scr[0:8, :] = p_ref[0, 8:16, :].astype(F32) * pm
    scr[8:8 + rows, :] = x_ref[0].astype(F32)
    scr[8 + rows:16 + rows, :] = n_ref[0, 0:8, :].astype(F32) * nm
    w = w_ref[...]
    acc = scr[6:6 + rows, :] * w[0:1]
    for j in range(1, CONV_K):
        acc = acc + scr[6 + j:6 + j + rows, :] * w[j:j + 1]
    z = acc + b_ref[...]
    o_ref[0] = (_silu(z) * sc_ref[...]).astype(o_ref.dtype)


def _conv_silu(ml, conv_w8, conv_b, scale, seq_starts, seq_ends):
    bsz, ltot, _ = ml.shape
    rows = 256
    ct = 512
    nch = conv_b.shape[-1]
    nt = ltot // rows
    hb = rows // 16
    nhb = ltot // 16
    seq_tiles = (tuple(s // rows for s in seq_starts), tuple(e // rows - 1 for e in seq_ends))
    kern = functools.partial(_conv_kernel, rows=rows, seq_tiles=seq_tiles)
    return pl.pallas_call(
        kern,
        grid=(bsz, nt, nch // ct),
        in_specs=[pl.BlockSpec((1, rows, ct), lambda b, i, c: (b, i, c)),
                  pl.BlockSpec((1, 16, ct), lambda b, i, c: (b, jnp.maximum(i * hb - 1, 0), c)),
                  pl.BlockSpec((1, 16, ct), lambda b, i, c: (b, jnp.minimum((i + 1) * hb, nhb - 1), c)),
                  pl.BlockSpec((8, ct), lambda b, i, c: (0, c)),
                  pl.BlockSpec((1, ct), lambda b, i, c: (0, c)),
                  pl.BlockSpec((1, ct), lambda b, i, c: (0, c))],
        out_specs=pl.BlockSpec((1, rows, ct), lambda b, i, c: (b, i, c)),
        out_shape=jax.ShapeDtypeStruct((bsz, ltot, nch), BF16),
        scratch_shapes=[pltpu.VMEM((rows + 16, ct), F32)],
        compiler_params=_cparams(("parallel", "parallel", "parallel")),
        name="conv_silu",
    )(ml, ml, ml, conv_w8, conv_b, scale)


def _mlstm_chain_step(q, k, v, i_col, f_col, i_row, f_row, b_col, b_row, c_ref, n_ref, m_ref,
                      idx, backward):
    t = q.shape[0]
    rid = lax.broadcasted_iota(I32, (t, t), 0)
    cid = lax.broadcasted_iota(I32, (t, t), 1)
    mask = (cid >= rid) if backward else (rid >= cid)
    m_st = m_ref[idx]
    d = jnp.where(mask, b_col - b_row + i_row, -jnp.inf)
    inter = b_col + m_st
    m_t = jnp.maximum(inter, jnp.max(d, axis=-1, keepdims=True))
    w = jnp.exp(d - m_t)
    s_inter = jnp.exp(inter - m_t)
    s = _dot_nt(q, k) * w
    c_st = c_ref[idx]
    n_st = n_ref[idx]
    num = _dot(s.astype(BF16), v) + s_inter * _dot_nt(q, c_st.astype(BF16))
    qf = q.astype(F32)
    den = jnp.sum(s, axis=-1, keepdims=True) + s_inter * jnp.sum(qf * n_st, axis=-1, keepdims=True)
    h = num / jnp.maximum(jnp.abs(den), jnp.exp(-m_t))
    total = jnp.sum(f_col, axis=0, keepdims=True)
    d_end = total - b_col + i_col
    m_new = jnp.maximum(total + m_st, jnp.max(d_end, axis=0, keepdims=True))
    w_end = jnp.exp(d_end - m_new)
    dec = jnp.exp(total + m_st - m_new)
    vw = (v.astype(F32) * w_end).astype(BF16)
    c_ref[idx] = dec * c_st + _dot_tn(vw, k)
    n_ref[idx] = dec * n_st + jnp.sum(k.astype(F32) * w_end, axis=0, keepdims=True)
    m_ref[idx] = m_new
    return h


def _mlstm_kernel(qf_ref, kf_ref, vf_ref, gf_ref, gtf_ref, qb_ref, kb_ref, vb_ref, gb_ref, gtb_ref,
                  gbias_ref, gbias_t_ref, hf_ref, hb_ref, c_scr, n_scr, m_scr, *, dh):
    @pl.when(pl.program_id(1) == 0)
    def _():
        c_scr[...] = jnp.zeros_like(c_scr)
        n_scr[...] = jnp.zeros_like(n_scr)
        m_scr[...] = jnp.zeros_like(m_scr)

    t = qf_ref.shape[1]
    rid = lax.broadcasted_iota(I32, (t, t), 0)
    cid = lax.broadcasted_iota(I32, (t, t), 1)
    lower = jnp.where(rid >= cid, 1.0, 0.0).astype(BF16)
    upper = jnp.where(cid >= rid, 1.0, 0.0).astype(BF16)

    for d, (q_ref, k_ref, v_ref, g_ref, gt_ref, h_ref) in enumerate(
            [(qf_ref, kf_ref, vf_ref, gf_ref, gtf_ref, hf_ref),
             (qb_ref, kb_ref, vb_ref, gb_ref, gtb_ref, hb_ref)]):
        backward = d == 1
        g = g_ref[0, :, 0:16] + gbias_ref[...]
        gt = gt_ref[0] + gbias_t_ref[...]
        lf = _log_sigmoid(g)
        lft = _log_sigmoid(gt)
        cum_col = _dot3_right(upper if backward else lower, lf)
        cum_row = _dot3_left(lft, lower if backward else upper)
        for hd in range(ML_HEADS):
            ci = 2 * ML_HEADS * d + hd
            cf = ci + ML_HEADS
            sl = slice(hd * dh, (hd + 1) * dh)
            h = _mlstm_chain_step(
                q_ref[0, :, sl], k_ref[0, :, sl], v_ref[0, :, sl],
                g[:, ci:ci + 1], lf[:, cf:cf + 1], gt[ci:ci + 1, :], lft[cf:cf + 1, :],
                cum_col[:, cf:cf + 1], cum_row[cf:cf + 1, :],
                c_scr, n_scr, m_scr, d * ML_HEADS + hd, backward)
            h_ref[0, :, sl] = h.astype(h_ref.dtype)


def _mlstm_scan(qk, ml, gates, gates_t, gbias, gbias_t, n_ctx_chunks):
    bsz, ltot, dml2 = qk.shape
    dml = dml2 // 2
    dh = dml // ML_HEADS
    t = ML_T
    nchunk = ltot // t
    cc = n_ctx_chunks
    nx = nchunk - cc

    def fwd(s):
        return jnp.where(s < cc, nx + s, s - cc)

    def bwd(s):
        return jnp.where(s < cc, nx + cc - 1 - s, nx - 1 - (s - cc))

    def specs(cmap):
        return [pl.BlockSpec((1, t, dml), lambda b, s: (b, cmap(s), 0)),
                pl.BlockSpec((1, t, dml), lambda b, s: (b, cmap(s), 1)),
                pl.BlockSpec((1, t, dml), lambda b, s: (b, cmap(s), 2)),
                pl.BlockSpec((1, t, LANES), lambda b, s: (b, cmap(s), 0)),
                pl.BlockSpec((1, 16, t), lambda b, s: (b, 0, cmap(s)))]

    kern = functools.partial(_mlstm_kernel, dh=dh)
    nchain = 2 * ML_HEADS
    return pl.pallas_call(
        kern,
        grid=(bsz, nchunk),
        in_specs=specs(fwd) + specs(bwd) + [pl.BlockSpec((1, 16), lambda b, s: (0, 0)),
                                            pl.BlockSpec((16, 1), lambda b, s: (0, 0))],
        out_specs=[pl.BlockSpec((1, t, dml), lambda b, s: (b, fwd(s), 0)),
                   pl.BlockSpec((1, t, dml), lambda b, s: (b, bwd(s), 0))],
        out_shape=[jax.ShapeDtypeStruct((bsz, ltot, dml), BF16)] * 2,
        scratch_shapes=[pltpu.VMEM((nchain, dh, dh), F32),
                        pltpu.VMEM((nchain, 1, dh), F32),
                        pltpu.VMEM((nchain, 1, 1), F32)],
        compiler_params=_cparams(("parallel", "arbitrary")),
        name="mlstm_scan",
    )(qk, qk, ml, gates, gates_t, qk, qk, ml, gates, gates_t, gbias, gbias_t)


COL_PITCH = ML_RB + 8


def _mlpost_kernel(hf_ref, hb_ref, o_ref, mg_ref, out_ref, scr, *, dh):
    nw, _, rb, dml = hf_ref.shape[1:]
    h = (hf_ref[0].astype(F32) + hb_ref[0].astype(F32)).reshape(nw * rb, dml)
    parts = []
    for hd in range(ML_HEADS):
        hh = h[:, hd * dh:(hd + 1) * dh]
        parts.append(hh * lax.rsqrt(jnp.mean(hh * hh, axis=-1, keepdims=True) + EPS))
    hn = jnp.concatenate(parts, axis=-1) * mg_ref[...]
    res = hn * _sigmoid(o_ref[0].astype(F32).reshape(nw * rb, dml))
    nlb = dml // LANES
    for c in range(nlb):
        for w in range(nw):
            scr[c, w * COL_PITCH:w * COL_PITCH + rb, :] = res[w * rb:(w + 1) * rb, c * LANES:(c + 1) * LANES]
    for r in range(rb):
        row = jnp.concatenate([scr[c, pl.ds(r, nw, stride=COL_PITCH), :] for c in range(nlb)], axis=-1)
        out_ref[0, r * nw:(r + 1) * nw, :] = row.astype(out_ref.dtype)


def _ml_post(hf5, hb5, ml5, ml_g, seq):
    bsz, _, rblocks, rb, dml = hf5.shape
    kern = functools.partial(_mlpost_kernel, dh=dml // ML_HEADS)
    blk = (1, GRID_W, 1, rb, dml)
    return pl.pallas_call(
        kern,
        grid=(bsz, rblocks),
        in_specs=[pl.BlockSpec(blk, lambda b, i: (b, 0, i, 0, 0)),
                  pl.BlockSpec(blk, lambda b, i: (b, 0, i, 0, 0)),
                  pl.BlockSpec(blk, lambda b, i: (b, 0, i, 0, 3)),
                  pl.BlockSpec((1, dml), lambda b, i: (0, 0))],
        out_specs=pl.BlockSpec((1, rb * GRID_W, dml), lambda b, i: (b, i, 0)),
        out_shape=jax.ShapeDtypeStruct((bsz, seq, dml), BF16),
        scratch_shapes=[pltpu.VMEM((dml // LANES, GRID_W * COL_PITCH, LANES), F32)],
        compiler_params=_cparams(("parallel", "parallel")),
        name="mlstm_post",
    )(hf5, hb5, ml5, ml_g)


def _gelu_tanh(y):
    return 0.5 * y * (1.0 + jnp.tanh(math.sqrt(2.0 / math.pi) * (y + 0.044715 * (y * y * y))))


def _mixout_kernel(ys_ref, u_ref, ml_ref, x_ref, dsk_ref, gw_ref, gb_ref, wo_ref, ng_ref, mod_ref,
                   rwh_ref, rwl_ref, x1_ref, hx_ref, aff_ref, *, n_exp):
    y = ys_ref[...] + dsk_ref[...] * u_ref[...]
    ge = _gelu_tanh(y)
    z = _dot(ge.astype(BF16), gw_ref[...]) + gb_ref[...]
    s5 = (ge * _sigmoid(z)).astype(BF16)
    yx = _dot(jnp.concatenate([s5, ml_ref[...]], axis=-1), wo_ref[...])
    ng = ng_ref[...]
    mod = mod_ref[0]
    r1 = yx * lax.rsqrt(jnp.mean(yx * yx, axis=-1, keepdims=True) + EPS) * ng[1:2]
    x1 = x_ref[...] + mod[0:1] * r1
    x1_ref[...] = x1
    hx = x1 * lax.rsqrt(jnp.mean(x1 * x1, axis=-1, keepdims=True) + EPS) * ng[2:3]
    hx = hx * (1.0 + mod[1:2]) + mod[2:3]
    for cb in range(hx_ref.shape[1]):
        hx_ref[:, cb, :] = hx[:, cb * LANES:(cb + 1) * LANES]
    h1, h2, h3 = _split3(hx)
    rwh = rwh_ref[...]
    logits = _dot(h1, rwh) + _dot(h2, rwh) + _dot(h3, rwh) + _dot(h1, rwl_ref[...])
    lane = lax.broadcasted_iota(I32, logits.shape, 1)
    logits = jnp.where(lane < n_exp, logits, -jnp.inf)
    e = jnp.exp(logits - jnp.max(logits, axis=-1, keepdims=True))
    aff_ref[...] = e / jnp.sum(e, axis=-1, keepdims=True)


def _mixer_out(ys, u, mlo, x2d, rows_per_batch, dsk, glu_w, glu_b, w_out, norm_g, mod3, rw_hi, rw_lo):
    n, d = x2d.shape
    ds5 = ys.shape[1]
    dml = mlo.shape[1]
    tm = 256
    tiles_per_batch = rows_per_batch // tm
    kern = functools.partial(_mixout_kernel, n_exp=N_EXPERTS)
    row = lambda i: (i, 0)
    const = lambda i: (0, 0)
    return pl.pallas_call(
        kern,
        grid=(n // tm,),
        in_specs=[pl.BlockSpec((tm, ds5), row),
                  pl.BlockSpec((tm, ds5), row),
                  pl.BlockSpec((tm, dml), row),
                  pl.BlockSpec((tm, d), row),
                  pl.BlockSpec((1, ds5), const),
                  pl.BlockSpec((ds5, ds5), const),
                  pl.BlockSpec((1, ds5), const),
                  pl.BlockSpec((d, d), const),
                  pl.BlockSpec((4, d), const),
                  pl.BlockSpec((1, 8, d), lambda i: (i // tiles_per_batch, 0, 0)),
                  pl.BlockSpec((d, LANES), const),
                  pl.BlockSpec((d, LANES), const)],
        out_specs=[pl.BlockSpec((tm, d), row),
                   pl.BlockSpec((tm, d // LANES, LANES), lambda i: (i, 0, 0)),
                   pl.BlockSpec((tm, LANES), row)],
        out_shape=[jax.ShapeDtypeStruct((n, d), F32),
                   jax.ShapeDtypeStruct((n, d // LANES, LANES), F32),
                   jax.ShapeDtypeStruct((n, LANES), F32)],
        compiler_params=_cparams(("parallel",)),
        name="mixer_out",
    )(ys, u, mlo, x2d, dsk, glu_w, glu_b, w_out, norm_g, mod3, rw_hi, rw_lo)


def _count(mask):
    c = jnp.sum(jnp.where(mask, 1.0, 0.0), axis=2, keepdims=True)
    return jnp.sum(c, axis=1, keepdims=True)


SELECT_BISECTIONS = 48


def _pow2_neg(n):
    p = jnp.ones(n.shape, F32)
    for i in range(7):
        p = p * jnp.where(((n >> i) & 1) == 1, 2.0 ** -(2 ** i), 1.0)
    return p


def _select_kernel(aff_ref, off_ref, idx_ref, gate_ref, *, cap, n_tok):
    a = aff_ref[0]
    e, nb, _ = a.shape

    def enough(thr):
        return _count(a >= thr) >= cap

    ex_lo = jnp.full((e, 1, 1), -127, I32)
    ex_hi = jnp.full((e, 1, 1), 1, I32)
    for _ in range(7):
        ex_mid = (ex_lo + ex_hi) >> 1
        ok = enough(_pow2_neg(-ex_mid))
        ex_lo = jnp.where(ok, ex_mid, ex_lo)
        ex_hi = jnp.where(ok, ex_hi, ex_mid)
    lo = jnp.where(ex_lo == -127, 0.0, _pow2_neg(jnp.minimum(-ex_lo, 126)))
    hi = jnp.where(ex_hi == 1, 2.0, _pow2_neg(jnp.maximum(-ex_hi, 0)))
    for _ in range(SELECT_BISECTIONS):
        mid = lo + 0.5 * (hi - lo)
        ok = enough(mid)
        lo = jnp.where(ok, mid, lo)
        hi = jnp.where(ok, hi, mid)
    gt = a >= hi
    tie = (a >= lo) & (a < hi)
    need = cap - _count(gt)
    tok = (lax.broadcasted_iota(I32, a.shape, 1) * LANES + lax.broadcasted_iota(I32, a.shape, 2))
    cut = jnp.zeros((e, 1, 1), I32)
    for bit in range(n_tok.bit_length() - 1, -1, -1):
        cand = cut | (1 << bit)
        cut = jnp.where(_count(tie & (tok < cand)) < need, cand, cut)
    sel = gt | (tie & (tok <= cut))
    self32 = jnp.where(sel, 1.0, 0.0).reshape(e * nb, LANES)
    r = lax.broadcasted_iota(I32, (LANES, LANES), 0)
    c = lax.broadcasted_iota(I32, (LANES, LANES), 1)
    incl = _dot(self32.astype(BF16), jnp.where(r <= c, 1.0, 0.0).astype(BF16))
    tot = jnp.broadcast_to(incl[:, LANES - 1:LANES], (e * nb, LANES)).astype(BF16)
    rr = lax.broadcasted_iota(I32, (e * nb, e * nb), 0)
    cc = lax.broadcasted_iota(I32, (e * nb, e * nb), 1)
    same = (rr // nb) == (cc // nb)
    before = jnp.where(same & (cc < rr), 1.0, 0.0).astype(BF16)
    off = _dot(before, tot)
    off_ref[0] = off.astype(I32).reshape(e, nb, LANES)

    kcol = lax.broadcasted_iota(I32, (cap, nb), 0).astype(F32)
    jrow = lax.broadcasted_iota(I32, (cap, nb), 1).astype(F32)
    lane = lax.broadcasted_iota(I32, (cap, LANES), 1)
    lane_f = lane.astype(F32)
    ones8 = jnp.ones((8, LANES), BF16)
    jr = lax.broadcasted_iota(I32, (nb, nb), 0)
    jc = lax.broadcasted_iota(I32, (nb, nb), 1)
    upper = jnp.where(jr <= jc, 1.0, 0.0).astype(BF16)
    sel3 = self32.reshape(e, nb, LANES)
    incl3 = incl.reshape(e, nb, LANES)

    def to_row(col):
        return _dot_nt(ones8, jnp.where(lane == 0, col, 0.0).astype(BF16))

    for ex in range(e):
        tot_row = _dot_nt(ones8, sel3[ex].astype(BF16))[0:1]
        end_row = _dot(jnp.broadcast_to(tot_row, (8, nb)).astype(BF16), upper)[0:1]
        start_row = end_row - tot_row
        jk = jnp.sum(jnp.where(end_row <= kcol, 1.0, 0.0), axis=1, keepdims=True)
        onehot_j = jnp.where(jrow == jk, 1.0, 0.0)
        kk = kcol[:, 0:1] - jnp.sum(onehot_j * start_row, axis=1, keepdims=True)
        d = _dot(onehot_j.astype(BF16), incl3[ex].astype(BF16))
        tl = jnp.sum(jnp.where(d <= kk, 1.0, 0.0), axis=1, keepdims=True)
        dg = _dot3_right(onehot_j.astype(BF16), a[ex])
        gate = jnp.sum(jnp.where(lane_f == tl, dg, 0.0), axis=1, keepdims=True)
        idx_ref[0, ex] = (to_row(jk) * float(LANES) + to_row(tl)).astype(I32)
        g1, g2, g3 = _split3(gate)
        gate_ref[0, ex] = (to_row(g1.astype(F32)) + to_row(g2.astype(F32))) + to_row(g3.astype(F32))


def _select(aff_t, cap):
    bsz, e, nb, _ = aff_t.shape
    kern = functools.partial(_select_kernel, cap=cap, n_tok=nb * LANES)
    blk = pl.BlockSpec((1, e, nb, LANES), lambda b: (b, 0, 0, 0))
    slot = pl.BlockSpec((1, e, 8, cap), lambda b: (b, 0, 0, 0))
    return pl.pallas_call(
        kern,
        grid=(bsz,),
        in_specs=[blk],
        out_specs=[blk, slot, slot],
        out_shape=[jax.ShapeDtypeStruct(aff_t.shape, I32),
                   jax.ShapeDtypeStruct((bsz, e, 8, cap), I32),
                   jax.ShapeDtypeStruct((bsz, e, 8, cap), F32)],
        compiler_params=_cparams(("parallel",)),
        name="ec_select",
    )(aff_t)


def _row_copy(hx_hbm, xbuf, sem, src_row, dst_row):
    return pltpu.make_async_copy(hx_hbm.at[src_row], xbuf.at[dst_row], sem)


def _dispatch_kernel(idx_ref, hx_hbm, o_ref, xbuf, sem, *, cap, n_tok, n_exp):
    b = pl.program_id(0)
    ex = pl.program_id(1)
    base = (b * n_exp + ex) * cap

    def issue(k, _):
        _row_copy(hx_hbm, xbuf, sem, b * n_tok + idx_ref[base + k], k).start()
        return 0

    lax.fori_loop(0, cap, issue, 0)

    def drain(k, _):
        _row_copy(hx_hbm, xbuf, sem, 0, k).wait()
        return 0

    lax.fori_loop(0, cap, drain, 0)
    for c in range(xbuf.shape[1]):
        o_ref[0, 0, :, c * LANES:(c + 1) * LANES] = xbuf[:, c, :].astype(o_ref.dtype)


def _dispatch(idx_flat, hx_slabs, bsz, n_exp, cap):
    n_rows, nslab, _ = hx_slabs.shape
    n_tok = n_rows // bsz
    d = nslab * LANES
    kern = functools.partial(_dispatch_kernel, cap=cap, n_tok=n_tok, n_exp=n_exp)
    return pl.pallas_call(
        kern,
        grid_spec=pltpu.PrefetchScalarGridSpec(
            num_scalar_prefetch=1,
            grid=(bsz, n_exp),
            in_specs=[pl.BlockSpec(memory_space=pl.ANY)],
            out_specs=pl.BlockSpec((1, 1, cap, d), lambda b, e, idx: (b, e, 0, 0)),
            scratch_shapes=[pltpu.VMEM((cap, nslab, LANES), F32), pltpu.SemaphoreType.DMA(())]),
        out_shape=jax.ShapeDtypeStruct((bsz, n_exp, cap, d), BF16),
        compiler_params=_cparams(("arbitrary", "arbitrary")),
        name="ec_dispatch",
    )(idx_flat, hx_slabs)


def _ffn_kernel(x_ref, g_ref, wg_ref, wu_ref, wd_ref, o_ref, acc):
    f = pl.program_id(2)

    @pl.when(f == 0)
    def _():
        acc[...] = jnp.zeros_like(acc)

    x = x_ref[0, 0]
    a = _dot(x, wg_ref[0].astype(BF16))
    u = _dot(x, wu_ref[0].astype(BF16))
    hmid = (_silu(a) * u).astype(BF16)
    acc[...] += _dot(hmid, wd_ref[0].astype(BF16))

    @pl.when(f == pl.num_programs(2) - 1)
    def _():
        g = g_ref[0, 0]
        for c in range(o_ref.shape[3]):
            o_ref[0, 0, :, c, :] = acc[:, c * LANES:(c + 1) * LANES] * g


def _expert_ffn(xs, gates, w_gate, w_up, w_down):
    bsz, n_exp, cap, d = xs.shape
    ff = w_gate.shape[2]
    tf = 256
    return pl.pallas_call(
        _ffn_kernel,
        grid=(n_exp, bsz, ff // tf),
        in_specs=[pl.BlockSpec((1, 1, cap, d), lambda e, b, f: (b, e, 0, 0)),
                  pl.BlockSpec((1, 1, cap, 1), lambda e, b, f: (b, e, 0, 0)),
                  pl.BlockSpec((1, d, tf), lambda e, b, f: (e, 0, f)),
                  pl.BlockSpec((1, d, tf), lambda e, b, f: (e, 0, f)),
                  pl.BlockSpec((1, tf, d), lambda e, b, f: (e, f, 0))],
        out_specs=pl.BlockSpec((1, 1, cap, d // LANES, LANES), lambda e, b, f: (b, e, 0, 0, 0)),
        out_shape=jax.ShapeDtypeStruct((bsz, n_exp, cap, d // LANES, LANES), F32),
        scratch_shapes=[pltpu.VMEM((cap, d), F32)],
        compiler_params=_cparams(("parallel", "parallel", "arbitrary")),
        name="expert_ffn",
    )(xs, gates, w_gate, w_up, w_down)


COMB_UNROLL = 4


def _combine_kernel(idx_ref, off_ref, y_ref, o_ref, *, cap, nb, n_exp, tok_part, n_part):
    b = pl.program_id(0)
    part = pl.program_id(1)
    ex = pl.program_id(3)

    @pl.when(ex == 0)
    def _():
        o_ref[...] = jnp.zeros_like(o_ref)

    be = b * n_exp + ex
    blocks_per_part = nb // n_part
    k_lo = off_ref[be * nb + part * blocks_per_part]
    k_hi = jnp.where(part == n_part - 1, cap,
                     off_ref[be * nb + jnp.minimum((part + 1) * blocks_per_part, nb - 1)])
    t0 = part * tok_part

    def add_rows(k, n):
        toks = [idx_ref[be * cap + k + i] - t0 for i in range(n)]
        rows = [o_ref[0, toks[i]] + y_ref[0, 0, k + i] for i in range(n)]
        for i in range(n):
            o_ref[0, toks[i]] = rows[i]

    n_full = (k_hi - k_lo) // COMB_UNROLL

    def body(i, _):
        add_rows(k_lo + i * COMB_UNROLL, COMB_UNROLL)
        return 0

    lax.fori_loop(0, n_full, body, 0)

    def tail(k, _):
        add_rows(k, 1)
        return 0

    lax.fori_loop(k_lo + n_full * COMB_UNROLL, k_hi, tail, 0)


def _combine(idx_flat, off_flat, y, n_tok):
    bsz, n_exp, cap, nslab, _ = y.shape
    nb = n_tok // TOK_BLK
    n_part = 2
    sblk = 8
    kern = functools.partial(_combine_kernel, cap=cap, nb=nb, n_exp=n_exp, tok_part=n_tok // n_part,
                             n_part=n_part)
    return pl.pallas_call(
        kern,
        grid_spec=pltpu.PrefetchScalarGridSpec(
            num_scalar_prefetch=2,
            grid=(bsz, n_part, nslab // sblk, n_exp),
            in_specs=[pl.BlockSpec((1, 1, cap, sblk, LANES), lambda b, p, c, e, idx, off: (b, e, 0, c, 0))],
            out_specs=pl.BlockSpec((1, n_tok // n_part, sblk, LANES),
                                   lambda b, p, c, e, idx, off: (b, p, c, 0))),
        out_shape=jax.ShapeDtypeStruct((bsz, n_tok, nslab, LANES), F32),
        compiler_params=_cparams(("parallel", "parallel", "parallel", "arbitrary")),
        name="ec_combine",
    )(idx_flat, off_flat, y)


def _final_kernel(x1_ref, moe_ref, g_ref, mod_ref, o_ref):
    m = jnp.concatenate([moe_ref[:, c, :] for c in range(moe_ref.shape[1])], axis=-1)
    r = m * lax.rsqrt(jnp.mean(m * m, axis=-1, keepdims=True) + EPS) * g_ref[...]
    o_ref[...] = x1_ref[...] + mod_ref[0] * r


def _final(x1, moe, g3, g2, rows_per_batch):
    n, d = x1.shape
    tm = 512
    tiles_per_batch = rows_per_batch // tm
    return pl.pallas_call(
        _final_kernel,
        grid=(n // tm,),
        in_specs=[pl.BlockSpec((tm, d), lambda i: (i, 0)),
                  pl.BlockSpec((tm, d // LANES, LANES), lambda i: (i, 0, 0)),
                  pl.BlockSpec((1, d), lambda i: (0, 0)),
                  pl.BlockSpec((1, 1, d), lambda i: (i // tiles_per_batch, 0, 0))],
        out_specs=pl.BlockSpec((tm, d), lambda i: (i, 0)),
        out_shape=jax.ShapeDtypeStruct((n, d), F32),
        compiler_params=_cparams(("parallel",)),
        name="final_residual",
    )(x1, moe, g3, g2)


def _layer(x, c, ctx, c_ctx, ada_w, ada_b, norm_g, w_in, s5_a_re, s5_a_im, s5_log_dt, s5_b_re,
           s5_b_im, s5_c_re, s5_c_im, s5_d, s5_glu_w, s5_glu_b, ml_conv_w, ml_conv_b, ml_gate_b,
           ml_norm_g, w_out, router_w, exp_w_gate, exp_w_up, exp_w_down):
    bsz, seq, d = x.shape
    lctx = ctx.shape[1]
    ds5 = s5_d.shape[0]
    dml = ml_norm_g.shape[0]
    ngroups = ds5 // S5_GROUP
    rows = seq // GRID_W
    n = bsz * seq
    cap = EC_FACTOR * seq // N_EXPERTS
    assert cap >= TOK_BLK + 16 and seq % DISP_TOK == 0 and lctx % ML_T == 0 and seq % ML_T == 0

    cond = jnp.concatenate([c, c_ctx[None], jnp.zeros((8 - bsz - 1, d), F32)], axis=0)
    mod = _modulation(cond, ada_w, ada_b).reshape(8, N_MOD, d)
    sh1, sc1, g1, sh2, sc2, g2 = [mod[:, i] for i in range(N_MOD)]

    nmain = ds5 + 4 * dml
    w_s5 = w_in[:, :ds5].astype(BF16)
    w_ml = w_in[:, ds5:nmain].astype(BF16)
    w_gate = jnp.pad(w_in[:, nmain:], ((0, 0), (0, LANES - 4 * ML_HEADS))).astype(BF16)
    g0 = norm_g[0:1]
    sc, sh = sc1[:, None], sh1[:, None]
    of_sample = lambda b: b
    of_ctx = lambda b: bsz
    ux = _in_projection_s5(x, g0, sc, sh, of_sample, w_s5)
    uc = _in_projection_s5(ctx, g0, sc, sh, of_ctx, w_s5)
    ltot = seq + lctx
    ml5, gates5 = _in_projection_ml(x, ctx, g0, sc, sh, bsz, w_ml, w_gate)

    m, bp, cpt, avec = _s5_operators(s5_a_re, s5_a_im, s5_log_dt, s5_b_re, s5_b_im, s5_c_re, s5_c_im)
    ys = _s5_scan(ux, uc, m, bp, cpt, avec).reshape(n, ds5)

    ml = ml5.reshape(bsz, ltot, 4 * dml)
    gates = gates5.reshape(bsz, ltot, LANES)
    conv_w8 = jnp.pad(ml_conv_w, ((0, 8 - CONV_K), (0, 0)))
    qscale = jnp.concatenate([jnp.full((1, dml), (dml // ML_HEADS) ** -0.5, F32),
                              jnp.ones((1, dml), F32)], axis=1)
    qk = _conv_silu(ml, conv_w8, ml_conv_b[None], qscale, (0, seq), (seq, ltot))
    gbias = ml_gate_b.reshape(1, 4 * ML_HEADS)
    gates_t = gates[..., :4 * ML_HEADS].swapaxes(1, 2)
    hf, hb = _mlstm_scan(qk, ml, gates, gates_t, gbias, gbias.reshape(4 * ML_HEADS, 1), lctx // ML_T)
    mlo = _ml_post(hf.reshape(ml5.shape[:4] + (dml,)), hb.reshape(ml5.shape[:4] + (dml,)), ml5,
                   ml_norm_g[None], seq).reshape(n, dml)

    mod3 = jnp.stack([g1[:bsz], sc2[:bsz], sh2[:bsz]], axis=1)
    mod3 = jnp.concatenate([mod3, jnp.zeros((bsz, 5, d), F32)], axis=1)
    rw = jnp.pad(router_w, ((0, 0), (0, LANES - N_EXPERTS)))
    rw_hi = rw.astype(BF16)
    rw_lo = (rw - rw_hi.astype(F32)).astype(BF16)
    x1, hx2, aff = _mixer_out(ys, ux.reshape(n, ds5), mlo, x.reshape(n, d), seq, s5_d[None],
                              s5_glu_w.astype(BF16), s5_glu_b[None], w_out.astype(BF16), norm_g, mod3,
                              rw_hi, rw_lo)

    nb = seq // TOK_BLK
    aff16 = aff[:, :N_EXPERTS].reshape(bsz, seq, N_EXPERTS)
    aff_t = aff16.swapaxes(1, 2).reshape(bsz, N_EXPERTS, nb, LANES)
    off, idx, gates = _select(aff_t, cap)
    off_flat = off[..., 0].reshape(-1)
    idx_flat = idx[:, :, 0].reshape(-1)
    xs = _dispatch(idx_flat, hx2, bsz, N_EXPERTS, cap)
    ye = _expert_ffn(xs, gates[:, :, 0, :, None], exp_w_gate, exp_w_up, exp_w_down)
    moe = _combine(idx_flat, off_flat, ye, seq)
    out = _final(x1, moe.reshape(n, d // LANES, LANES), norm_g[3:4], g2[:bsz, None], seq)
    return out.reshape(bsz, seq, d)


def kernel(x, c, ctx, c_ctx, ada_w, ada_b, norm_g, w_in, s5_a_re, s5_a_im, s5_log_dt, s5_b_re, s5_b_im, s5_c_re, s5_c_im, s5_d, s5_glu_w, s5_glu_b, ml_conv_w, ml_conv_b, ml_gate_b, ml_norm_g, w_out, router_w, exp_w_gate, exp_w_up, exp_w_down):
    depth = ada_w.shape[0]
    assert depth == 1, "context outputs are only produced when another layer follows"
    li = 0
    return _layer(x, c, ctx, c_ctx, ada_w[li], ada_b[li], norm_g[li], w_in[li], s5_a_re[li],
                  s5_a_im[li], s5_log_dt[li], s5_b_re[li], s5_b_im[li], s5_c_re[li], s5_c_im[li],
                  s5_d[li], s5_glu_w[li], s5_glu_b[li], ml_conv_w[li], ml_conv_b[li], ml_gate_b[li],
                  ml_norm_g[li], w_out[li], router_w[li], exp_w_gate[li], exp_w_up[li],
                  exp_w_down[li])
```

```python
import functools
import math

import jax
import jax.numpy as jnp
from jax import lax
from jax.experimental import pallas as pl
from jax.experimental.pallas import tpu as pltpu

F32 = jnp.float32
BF16 = jnp.bfloat16
I32 = jnp.int32

EPS = 1e-6
GRID_W = 64
S5_GROUP = 16
S5_STATE = 64
S5_T = 16
ML_HEADS = 4
ML_T = 256
CONV_K = 5
N_EXPERTS = 16
EC_FACTOR = 2
N_MOD = 6
LANES = 128
TOK_BLK = 128
DISP_TOK = 512
DISP_W = DISP_TOK + 8
VMEM_LIMIT = 56 * 1024 * 1024


def _cparams(sem):
    return pltpu.CompilerParams(dimension_semantics=sem, vmem_limit_bytes=VMEM_LIMIT)


def _sigmoid(x):
    return 1.0 / (1.0 + jnp.exp(-x))


def _silu(x):
    return x * _sigmoid(x)


def _log_sigmoid(x):
    return jnp.minimum(x, 0.0) - jnp.log(1.0 + jnp.exp(-jnp.abs(x)))


def _split3(x):
    x1 = x.astype(BF16)
    r = x - x1.astype(F32)
    x2 = r.astype(BF16)
    x3 = (r - x2.astype(F32)).astype(BF16)
    return x1, x2, x3


def _dot(a, b):
    return jnp.dot(a, b, preferred_element_type=F32)


def _dot_nt(a, b):
    return lax.dot_general(a, b, (((1,), (1,)), ((), ())), preferred_element_type=F32)


def _dot_tn(a, b):
    return lax.dot_general(a, b, (((0,), (0,)), ((), ())), preferred_element_type=F32)


def _dot3_left(a_f32, b_bf16):
    a1, a2, a3 = _split3(a_f32)
    return _dot(a1, b_bf16) + _dot(a2, b_bf16) + _dot(a3, b_bf16)


def _dot3_right(a_bf16, b_f32):
    b1, b2, b3 = _split3(b_f32)
    return _dot(a_bf16, b1) + _dot(a_bf16, b2) + _dot(a_bf16, b3)


def _mod_kernel(c_ref, w_ref, b_ref, o_ref):
    c = c_ref[...]
    s = _silu(c).astype(BF16)
    o_ref[...] = _dot(s, w_ref[...].astype(BF16)) + b_ref[...]


def _modulation(cond8, ada_w, ada_b):
    d, n = ada_w.shape
    tn = 1024
    return pl.pallas_call(
        _mod_kernel,
        grid=(n // tn,),
        in_specs=[pl.BlockSpec((8, d), lambda j: (0, 0)),
                  pl.BlockSpec((d, tn), lambda j: (0, j)),
                  pl.BlockSpec((1, tn), lambda j: (0, j))],
        out_specs=pl.BlockSpec((8, tn), lambda j: (0, j)),
        out_shape=jax.ShapeDtypeStruct((8, n), F32),
        compiler_params=_cparams(("parallel",)),
        name="modulation",
    )(cond8, ada_w, ada_b.reshape(1, n))


def _norm_mod(x, g, sc, sh):
    ms = jnp.mean(x * x, axis=-1, keepdims=True)
    h = x * lax.rsqrt(ms + EPS) * g
    return (h * (1.0 + sc) + sh).astype(BF16)


def _inproj_s5_kernel(x_ref, g_ref, sc_ref, sh_ref, w_ref, o_ref, h_scr):
    @pl.when(pl.program_id(2) == 0)
    def _():
        h_scr[...] = _norm_mod(x_ref[0], g_ref[...], sc_ref[0], sh_ref[0])

    o_ref[0] = _dot(h_scr[...], w_ref[...])


def _in_projection_s5(x, g0, sc, sh, mod_of_batch, w_s5):
    bsz, length, d = x.shape
    ds5 = w_s5.shape[1]
    tm = min(1024, length)
    tn = 512
    return pl.pallas_call(
        _inproj_s5_kernel,
        grid=(bsz, length // tm, ds5 // tn),
        in_specs=[pl.BlockSpec((1, tm, d), lambda b, i, j: (b, i, 0)),
                  pl.BlockSpec((1, d), lambda b, i, j: (0, 0)),
                  pl.BlockSpec((1, 1, d), lambda b, i, j: (mod_of_batch(b), 0, 0)),
                  pl.BlockSpec((1, 1, d), lambda b, i, j: (mod_of_batch(b), 0, 0)),
                  pl.BlockSpec((d, tn), lambda b, i, j: (0, j))],
        out_specs=pl.BlockSpec((1, tm, tn), lambda b, i, j: (b, i, j)),
        out_shape=jax.ShapeDtypeStruct((bsz, length, ds5), F32),
        scratch_shapes=[pltpu.VMEM((tm, d), BF16)],
        compiler_params=_cparams(("parallel", "parallel", "arbitrary")),
        name="in_projection_s5",
    )(x, g0, sc, sh, w_s5)


def _inproj_ml_kernel(x_ref, c_ref, g_ref, sc_ref, sh_ref, csc_ref, csh_ref, w_ref, wg_ref,
                      o_ref, og_ref, h_scr, x_scr):
    rb = o_ref.shape[3]
    nlb = x_ref.shape[2] // LANES

    @pl.when(pl.program_id(2) == 0)
    def _():
        for c in range(nlb):
            for r in range(rb):
                x_scr[c, r * ROW_PITCH:r * ROW_PITCH + GRID_W, :] = (
                    x_ref[0, r * GRID_W:(r + 1) * GRID_W, c * LANES:(c + 1) * LANES])
        for w in range(GRID_W):
            xw = jnp.concatenate([x_scr[c, pl.ds(w, rb, stride=ROW_PITCH), :] for c in range(nlb)],
                                 axis=-1)
            h_scr[w * rb:(w + 1) * rb, :] = _norm_mod(xw, g_ref[...], sc_ref[0], sh_ref[0])
        for s in range(c_ref.shape[1]):
            h_scr[(GRID_W + s) * rb:(GRID_W + s + 1) * rb, :] = _norm_mod(
                c_ref[0, s, 0], g_ref[...], csc_ref[0], csh_ref[0])
        og_ref[0] = _dot(h_scr[...], wg_ref[...]).reshape(og_ref.shape[1:])

    o_ref[0] = _dot(h_scr[...], w_ref[...]).reshape(o_ref.shape[1:]).astype(o_ref.dtype)


ML_RB = 16
ROW_PITCH = GRID_W + 8


def _in_projection_ml(x, ctx, g0, sc, sh, ctx_mod, w_ml, w_gate):
    bsz, length, d = x.shape
    lctx = ctx.shape[1]
    nml = w_ml.shape[1]
    tn = 512
    rows = length // GRID_W
    rblocks = rows // ML_RB
    nslot_c = lctx // rows
    assert nslot_c * rows == lctx and rblocks * ML_RB == rows
    n_slots = GRID_W + nslot_c
    tm = ML_RB * GRID_W
    ctx5 = ctx.reshape(bsz, nslot_c, rblocks, ML_RB, d)
    return pl.pallas_call(
        _inproj_ml_kernel,
        grid=(bsz, rblocks, nml // tn),
        in_specs=[pl.BlockSpec((1, tm, d), lambda b, i, j: (b, i, 0)),
                  pl.BlockSpec((1, nslot_c, 1, ML_RB, d), lambda b, i, j: (b, 0, i, 0, 0)),
                  pl.BlockSpec((1, d), lambda b, i, j: (0, 0)),
                  pl.BlockSpec((1, 1, d), lambda b, i, j: (b, 0, 0)),
                  pl.BlockSpec((1, 1, d), lambda b, i, j: (b, 0, 0)),
                  pl.BlockSpec((1, 1, d), lambda b, i, j: (ctx_mod, 0, 0)),
                  pl.BlockSpec((1, 1, d), lambda b, i, j: (ctx_mod, 0, 0)),
                  pl.BlockSpec((d, tn), lambda b, i, j: (0, j)),
                  pl.BlockSpec((d, LANES), lambda b, i, j: (0, 0))],
        out_specs=[pl.BlockSpec((1, n_slots, 1, ML_RB, tn), lambda b, i, j: (b, 0, i, 0, j)),
                   pl.BlockSpec((1, n_slots, 1, ML_RB, LANES), lambda b, i, j: (b, 0, i, 0, 0))],
        out_shape=[jax.ShapeDtypeStruct((bsz, n_slots, rblocks, ML_RB, nml), BF16),
                   jax.ShapeDtypeStruct((bsz, n_slots, rblocks, ML_RB, LANES), F32)],
        scratch_shapes=[pltpu.VMEM((n_slots * ML_RB, d), BF16),
                        pltpu.VMEM((d // LANES, ML_RB * ROW_PITCH, LANES), F32)],
        compiler_params=_cparams(("parallel", "parallel", "arbitrary")),
        name="in_projection_ml",
    )(x, ctx5, g0, sc, sh, sc, sh, w_ml, w_gate)


def _dot_nt6(a, b):
    a1, a2, a3 = _split3(a)
    b1, b2, b3 = _split3(b)
    return (_dot_nt(a1, b1) + (_dot_nt(a1, b2) + _dot_nt(a2, b1))
            + (_dot_nt(a1, b3) + _dot_nt(a3, b1) + _dot_nt(a2, b2)))


S5_LANES = 4 * S5_STATE


def _s5ops_kernel(prm_ref, bt_ref, c_ref, rexp_ref, rc_ref, place_ref, m_ref, bp_ref, cpt_ref, av_ref):
    t = S5_T
    prm = prm_ref[0]
    lr, li = jnp.minimum(prm[0:1], -1e-4), prm[1:2]
    dt = jnp.exp(prm[2:3])
    zr, zi = lr * dt, li * dt
    tau = lax.broadcasted_iota(I32, (32, S5_LANES), 0).astype(F32)
    mag, ang = jnp.exp(tau * zr), tau * zi
    pwr, pwi = mag * jnp.cos(ang), mag * jnp.sin(ang)
    xr, xi = pwr[1:2] - 1.0, pwi[1:2]
    den = lr * lr + li * li
    qr, qi = (xr * lr + xi * li) / den, (xi * lr - xr * li) / den
    btr, bti = bt_ref[0, 0], bt_ref[0, 1]
    bbr, bbi = qr * btr - qi * bti, qr * bti + qi * btr
    cr, ci = c_ref[0, 0], c_ref[0, 1]

    lane = lax.broadcasted_iota(I32, (t * S5_GROUP, S5_LANES), 1)
    fwd = ((lane // S5_STATE) % 2) == 0

    def expand(r, x):
        return _dot3_right(r, x)

    e_t = [expand(rexp_ref[0], z) for z in (pwr, pwi)]
    e_r = [expand(rexp_ref[1], z) for z in (pwr, pwi)]
    e_p = [expand(rexp_ref[2], z) for z in (pwr, pwi)]
    e_q = [expand(rexp_ref[3], z) for z in (pwr, pwi)]
    pk = [jnp.where(fwd, e_t[i], e_r[i]) for i in range(2)]
    pb = [jnp.where(fwd, e_r[i], e_t[i]) for i in range(2)]
    pc = [jnp.where(fwd, e_p[i], e_q[i]) for i in range(2)]
    cxr, cxi = expand(rc_ref[...], cr), expand(rc_ref[...], ci)
    bxr, bxi = expand(rc_ref[...], bbr), expand(rc_ref[...], bbi)

    wkr, wki = pk[0] * cxr - pk[1] * cxi, pk[0] * cxi + pk[1] * cxr
    row_blk = lax.broadcasted_iota(I32, (4 * S5_GROUP, S5_LANES), 0) // S5_GROUP
    lane_blk = lax.broadcasted_iota(I32, (4 * S5_GROUP, S5_LANES), 1) // S5_STATE
    own = row_blk == lane_blk
    bmr = jnp.where(own, jnp.concatenate([bbr] * 4, axis=0), 0.0)
    bmi = jnp.where(own, jnp.concatenate([bbi] * 4, axis=0), 0.0)
    kt = _dot_nt6(bmr, wkr) - _dot_nt6(bmi, wki)

    ln = lax.broadcasted_iota(I32, (S5_GROUP, t * S5_GROUP), 1)
    for gi in range(2):
        ktf = kt[(2 * gi) * S5_GROUP:(2 * gi + 1) * S5_GROUP]
        ktb = kt[(2 * gi + 1) * S5_GROUP:(2 * gi + 2) * S5_GROUP]
        blocks = []
        for s in range(t):
            f = ktf if s == 0 else pltpu.roll(ktf, S5_GROUP * s, axis=1)
            sh = (t * S5_GROUP - S5_GROUP * (t - 1 - s)) % (t * S5_GROUP)
            bk = ktb if sh == 0 else pltpu.roll(ktb, sh, axis=1)
            blocks.append(jnp.where(ln >= S5_GROUP * s, f, 0.0)
                          + jnp.where(ln < S5_GROUP * (s + 1), bk, 0.0))
        m_ref[0, gi] = jnp.concatenate(blocks, axis=0).astype(BF16)

    inj = jnp.concatenate([pb[0] * bxr - pb[1] * bxi, pb[0] * bxi + pb[1] * bxr], axis=-1).astype(BF16)
    out = jnp.concatenate([pc[0] * cxr - pc[1] * cxi, -(pc[0] * cxi + pc[1] * cxr)], axis=-1).astype(BF16)
    for gi in range(2):
        bp_ref[0, gi * t * S5_GROUP:(gi + 1) * t * S5_GROUP, :] = _dot(inj, place_ref[gi]).astype(BF16)
        cpt_ref[0, gi * t * S5_GROUP:(gi + 1) * t * S5_GROUP, :] = _dot(out, place_ref[gi]).astype(BF16)
    a_t = jnp.concatenate([pwr[t:t + 8], pwi[t:t + 8]], axis=-1)
    av_ref[0] = _dot3_left(a_t, place_ref[0]) + _dot3_left(a_t, place_ref[1])


def _s5_operators(a_re, a_im, log_dt, b_re, b_im, c_re, c_im):
    t = S5_T
    g, p = a_re.shape[1], a_re.shape[2]
    npair = g // 2

    def lanes(z):
        return z.reshape(2, npair, 2, p).transpose(1, 2, 0, 3).reshape(npair, 4 * p)

    prm = jnp.stack([lanes(a_re.astype(F32)), lanes(a_im.astype(F32)),
                     lanes(jnp.broadcast_to(log_dt.astype(F32)[..., None], a_re.shape))], axis=1)
    prm = jnp.concatenate([prm, jnp.zeros((npair, 5, 4 * p), F32)], axis=1)

    def tile_d(z):
        z = z.reshape(npair, 2, S5_GROUP, p).transpose(0, 2, 1, 3)
        return jnp.broadcast_to(z[:, :, :, None, :], (npair, S5_GROUP, 2, 2, p)).reshape(npair, S5_GROUP, 4 * p)

    bt = jnp.stack([tile_d(b_re.astype(F32).swapaxes(1, 2)), tile_d(b_im.astype(F32).swapaxes(1, 2))], axis=1)
    ct = jnp.stack([tile_d(c_re.astype(F32)), tile_d(c_im.astype(F32))], axis=1)

    row = jnp.arange(t * S5_GROUP)
    tau, ch = row // S5_GROUP, row % S5_GROUP
    e = jnp.arange(32)
    rexp = jnp.stack([(expo[:, None] == e[None, :]) for expo in (tau, t - 1 - tau, tau + 1, t - tau)]
                     ).astype(BF16)
    rc = (ch[:, None] == jnp.arange(S5_GROUP)[None, :]).astype(BF16)
    src = jnp.arange(2 * 4 * p)
    reim, gi_s, d_s, p_s = src // (4 * p), (src // (2 * p)) % 2, (src // p) % 2, src % p
    dst = d_s * (4 * p) + reim * (2 * p) + gi_s * p + p_s
    col = jnp.arange(8 * p)
    place = jnp.stack([((dst[:, None] == col[None, :]) & (gi_s[:, None] == gi)) for gi in range(2)]
                      ).astype(BF16)
    blk3 = lambda shape: pl.BlockSpec((1,) + shape, lambda i: (i,) + (0,) * len(shape))
    full = lambda arr: pl.BlockSpec(arr.shape, lambda i: (0,) * arr.ndim)
    m, bp, cpt, av = pl.pallas_call(
        _s5ops_kernel,
        grid=(npair,),
        in_specs=[blk3((8, 4 * p)), blk3((2, S5_GROUP, 4 * p)), blk3((2, S5_GROUP, 4 * p)),
                  full(rexp), full(rc), full(place)],
        out_specs=[blk3((2, t * S5_GROUP, t * S5_GROUP)), blk3((2 * t * S5_GROUP, 8 * p)),
                   blk3((2 * t * S5_GROUP, 8 * p)), blk3((8, 8 * p))],
        out_shape=[jax.ShapeDtypeStruct((npair, 2, t * S5_GROUP, t * S5_GROUP), BF16),
                   jax.ShapeDtypeStruct((npair, 2 * t * S5_GROUP, 8 * p), BF16),
                   jax.ShapeDtypeStruct((npair, 2 * t * S5_GROUP, 8 * p), BF16),
                   jax.ShapeDtypeStruct((npair, 8, 8 * p), F32)],
        compiler_params=_cparams(("parallel",)),
        name="s5_operators",
    )(prm, bt, ct, rexp, rc, place)
    avec = av[:, 0].reshape(npair, 4, 2 * p)
    return m, bp, cpt, jnp.concatenate([avec, jnp.zeros_like(avec)], axis=1)


def _s5_permutation():
    i = jnp.arange(8 * 8 * S5_GROUP)
    t8, j, c = i // (8 * S5_GROUP), (i // S5_GROUP) % 8, i % S5_GROUP
    dst = j * (8 * S5_GROUP) + t8 * S5_GROUP + c
    e1 = (dst[:, None] == i[None, :]).astype(BF16)
    return e1, e1.T


S5_PAIRS = 4


def _s5_kernel(ux_ref, uc_ref, e1_ref, e1t_ref, m_ref, bp_ref, cp_ref, av_ref, y_ref, s_scr, hp_scr,
               tab_scr, *, cc, xc):
    nc = cc + xc

    def chunk_rows(ref, nchunks):
        cols = [ref[0, pl.ds(t, nchunks, stride=S5_T), :].astype(BF16) for t in range(S5_T)]
        return [_dot(jnp.concatenate(cols[8 * h:8 * h + 8], axis=-1), e1_ref[...]).astype(BF16)
                for h in range(2)]

    def pair_lanes(halves, p):
        return jnp.concatenate([halves[h][:, j * LANES:(j + 1) * LANES]
                                for j in (2 * p, 2 * p + 1) for h in range(2)], axis=-1)

    pc = chunk_rows(uc_ref, cc)
    px = chunk_rows(ux_ref, xc)
    for p in range(S5_PAIRS):
        s_scr[0:cc, p * 512:(p + 1) * 512] = _dot(pair_lanes(pc, p), bp_ref[p])
        s_scr[cc:nc, p * 512:(p + 1) * 512] = _dot(pair_lanes(px, p), bp_ref[p])

    av = av_ref[...]
    sub = lax.broadcasted_iota(I32, (8, 128), 0)

    def cmul(ar, ai, br, bi):
        return ar * br - ai * bi, ar * bi + ai * br

    for p in range(S5_PAIRS):
        for d in range(2):
            a1 = (av[p, 2 * d:2 * d + 1], av[p, 2 * d + 1:2 * d + 2])
            a2 = cmul(*a1, *a1)
            a4 = cmul(*a2, *a2)
            pw_re, pw_im = jnp.zeros((8, 128), F32), jnp.zeros((8, 128), F32)
            cur = a1
            for k in range(8):
                r = k if d == 0 else 7 - k
                pw_re = jnp.where(sub == r, cur[0], pw_re)
                pw_im = jnp.where(sub == r, cur[1], pw_im)
                cur = cmul(*cur, *a1)
            for q, z in enumerate((a1, a2, a4)):
                tab_scr[p, d, 2 * q] = jnp.broadcast_to(z[0], (8, 128))
                tab_scr[p, d, 2 * q + 1] = jnp.broadcast_to(z[1], (8, 128))
            tab_scr[p, d, 6] = pw_re
            tab_scr[p, d, 7] = pw_im

    def scan8(s_re, s_im, h_re, h_im, p, d):
        back = d == 1
        x_re, x_im = s_re, s_im
        for q, k in enumerate((1, 2, 4)):
            keep = (sub < 8 - k) if back else (sub >= k)
            sh = (8 - k) if back else k
            y_re = jnp.where(keep, pltpu.roll(x_re, sh, axis=0), 0.0)
            y_im = jnp.where(keep, pltpu.roll(x_im, sh, axis=0), 0.0)
            m_re, m_im = cmul(tab_scr[p, d, 2 * q], tab_scr[p, d, 2 * q + 1], y_re, y_im)
            x_re, x_im = x_re + m_re, x_im + m_im
        c_re, c_im = cmul(tab_scr[p, d, 6], tab_scr[p, d, 7], h_re, h_im)
        x_re, x_im = x_re + c_re, x_im + c_im
        edge = 7 if back else 0
        sh = 7 if back else 1
        e_re = jnp.where(sub == edge, h_re, pltpu.roll(x_re, sh, axis=0))
        e_im = jnp.where(sub == edge, h_im, pltpu.roll(x_im, sh, axis=0))
        last = 0 if back else 7
        return e_re, e_im, x_re[last:last + 1], x_im[last:last + 1]

    def step(row_f, row_b, carry):
        rf = pl.multiple_of(row_f, 8)
        rb = pl.multiple_of(row_b, 8)
        new = []
        for p in range(S5_PAIRS):
            hr, hi, gr, gi = carry[4 * p:4 * p + 4]
            c0 = p * 512
            p_re, p_im, hr, hi = scan8(s_scr[pl.ds(rf, 8), c0:c0 + 128], s_scr[pl.ds(rf, 8), c0 + 128:c0 + 256],
                                       hr, hi, p, 0)
            hp_scr[pl.ds(rf, 8), c0:c0 + 128] = p_re
            hp_scr[pl.ds(rf, 8), c0 + 128:c0 + 256] = p_im
            p_re, p_im, gr, gi = scan8(s_scr[pl.ds(rb, 8), c0 + 256:c0 + 384], s_scr[pl.ds(rb, 8), c0 + 384:c0 + 512],
                                       gr, gi, p, 1)
            hp_scr[pl.ds(rb, 8), c0 + 256:c0 + 384] = p_re
            hp_scr[pl.ds(rb, 8), c0 + 384:c0 + 512] = p_im
            new += [hr, hi, gr, gi]
        return tuple(new)

    z = jnp.zeros((1, 128), F32)
    carry = lax.fori_loop(0, cc // 8, lambda i, c: step(8 * i, cc - 8 - 8 * i, c), (z,) * (4 * S5_PAIRS))
    lax.fori_loop(0, xc // 8, lambda i, c: step(cc + 8 * i, nc - 8 - 8 * i, c), carry)

    ys = []
    for p in range(S5_PAIRS):
        ux = pair_lanes(px, p)
        y = _dot_nt(hp_scr[cc:nc, p * 512:(p + 1) * 512].astype(BF16), cp_ref[p])
        ys.append(y + jnp.concatenate([_dot(ux[:, 0:256], m_ref[p, 0]), _dot(ux[:, 256:512], m_ref[p, 1])],
                                      axis=-1))
    for h in range(2):
        yh = jnp.concatenate([ys[j // 2][:, (j % 2) * 256 + h * LANES:(j % 2) * 256 + (h + 1) * LANES]
                              for j in range(8)], axis=-1).astype(BF16)
        r = _dot(yh, e1t_ref[...])
        for t8 in range(8):
            y_ref[0, pl.ds(8 * h + t8, xc, stride=S5_T), :] = r[:, t8 * LANES:(t8 + 1) * LANES]


def _s5_scan(ux, uc, m, bp, cp, avec):
    bsz, seq, ds5 = ux.shape
    lctx = uc.shape[1]
    cc, xc = lctx // S5_T, seq // S5_T
    assert cc % 8 == 0 and xc % 8 == 0
    e1, e1t = _s5_permutation()
    kern = functools.partial(_s5_kernel, cc=cc, xc=xc)
    nblk = ds5 // LANES
    const = lambda j, b: (0, 0)
    return pl.pallas_call(
        kern,
        grid=(nblk, bsz),
        in_specs=[pl.BlockSpec((1, seq, LANES), lambda j, b: (b, 0, j)),
                  pl.BlockSpec((1, lctx, LANES), lambda j, b: (b, 0, j)),
                  pl.BlockSpec(e1.shape, const),
                  pl.BlockSpec(e1.shape, const),
                  pl.BlockSpec((S5_PAIRS, 2, 256, 256), lambda j, b: (j, 0, 0, 0)),
                  pl.BlockSpec((S5_PAIRS, 512, 512), lambda j, b: (j, 0, 0)),
                  pl.BlockSpec((S5_PAIRS, 512, 512), lambda j, b: (j, 0, 0)),
                  pl.BlockSpec((S5_PAIRS, 8, 128), lambda j, b: (j, 0, 0))],
        out_specs=pl.BlockSpec((1, seq, LANES), lambda j, b: (b, 0, j)),
        out_shape=jax.ShapeDtypeStruct((bsz, seq, ds5), F32),
        scratch_shapes=[pltpu.VMEM((cc + xc, S5_PAIRS * 512), F32),
                        pltpu.VMEM((cc + xc, S5_PAIRS * 512), F32),
                        pltpu.VMEM((S5_PAIRS, 2, 8, 8, 128), F32)],
        compiler_params=_cparams(("parallel", "parallel")),
        name="s5_scan",
    )(ux, uc, e1, e1t, m, bp, cp, avec)


def _conv_kernel(x_ref, p_ref, n_ref, w_ref, b_ref, sc_ref, o_ref, scr, *, rows, seq_tiles):
    i = pl.program_id(1)
    first = functools.reduce(jnp.logical_or, [i == s for s in seq_tiles[0]])
    last = functools.reduce(jnp.logical_or, [i == s for s in seq_tiles[1]])
    pm = jnp.where(first, 0.0, 1.0)
    nm = jnp.where(last, 0.0, 1.0)
    scr[0:8, :] = p_ref[0, 8:16, :].astype(F32) * pm
    scr[8:8 + rows, :] = x_ref[0].astype(F32)
    scr[8 + rows:16 + rows, :] = n_ref[0, 0:8, :].astype(F32) * nm
    w = w_ref[...]
    acc = scr[6:6 + rows, :] * w[0:1]
    for j in range(1, CONV_K):
        acc = acc + scr[6 + j:6 + j + rows, :] * w[j:j + 1]
    z = acc + b_ref[...]
    o_ref[0] = (_silu(z) * sc_ref[...]).astype(o_ref.dtype)


def _conv_silu(ml, conv_w8, conv_b, scale, seq_starts, seq_ends):
    bsz, ltot, _ = ml.shape
    rows = 256
    ct = 512
    nch = conv_b.shape[-1]
    nt = ltot // rows
    hb = rows // 16
    nhb = ltot // 16
    seq_tiles = (tuple(s // rows for s in seq_starts), tuple(e // rows - 1 for e in seq_ends))
    kern = functools.partial(_conv_kernel, rows=rows, seq_tiles=seq_tiles)
    return pl.pallas_call(
        kern,
        grid=(bsz, nt, nch // ct),
        in_specs=[pl.BlockSpec((1, rows, ct), lambda b, i, c: (b, i, c)),
                  pl.BlockSpec((1, 16, ct), lambda b, i, c: (b, jnp.maximum(i * hb - 1, 0), c)),
                  pl.BlockSpec((1, 16, ct), lambda b, i, c: (b, jnp.minimum((i + 1) * hb, nhb - 1), c)),
                  pl.BlockSpec((8, ct), lambda b, i, c: (0, c)),
                  pl.BlockSpec((1, ct), lambda b, i, c: (0, c)),
                  pl.BlockSpec((1, ct), lambda b, i, c: (0, c))],
        out_specs=pl.BlockSpec((1, rows, ct), lambda b, i, c: (b, i, c)),
        out_shape=jax.ShapeDtypeStruct((bsz, ltot, nch), BF16),
        scratch_shapes=[pltpu.VMEM((rows + 16, ct), F32)],
        compiler_params=_cparams(("parallel", "parallel", "parallel")),
        name="conv_silu",
    )(ml, ml, ml, conv_w8, conv_b, scale)


def _mlstm_chain_step(q, k, v, i_col, f_col, i_row, f_row, b_col, b_row, c_ref, n_ref, m_ref,
                      idx, backward):
    t = q.shape[0]
    rid = lax.broadcasted_iota(I32, (t, t), 0)
    cid = lax.broadcasted_iota(I32, (t, t), 1)
    mask = (cid >= rid) if backward else (rid >= cid)
    m_st = m_ref[idx]
    d = jnp.where(mask, b_col - b_row + i_row, -jnp.inf)
    inter = b_col + m_st
    m_t = jnp.maximum(inter, jnp.max(d, axis=-1, keepdims=True))
    w = jnp.exp(d - m_t)
    s_inter = jnp.exp(inter - m_t)
    s = _dot_nt(q, k) * w
    c_st = c_ref[idx]
    n_st = n_ref[idx]
    num = _dot(s.astype(BF16), v) + s_inter * _dot_nt(q, c_st.astype(BF16))
    qf = q.astype(F32)
    den = jnp.sum(s, axis=-1, keepdims=True) + s_inter * jnp.sum(qf * n_st, axis=-1, keepdims=True)
    h = num / jnp.maximum(jnp.abs(den), jnp.exp(-m_t))
    total = jnp.sum(f_col, axis=0, keepdims=True)
    d_end = total - b_col + i_col
    m_new = jnp.maximum(total + m_st, jnp.max(d_end, axis=0, keepdims=True))
    w_end = jnp.exp(d_end - m_new)
    dec = jnp.exp(total + m_st - m_new)
    vw = (v.astype(F32) * w_end).astype(BF16)
    c_ref[idx] = dec * c_st + _dot_tn(vw, k)
    n_ref[idx] = dec * n_st + jnp.sum(k.astype(F32) * w_end, axis=0, keepdims=True)
    m_ref[idx] = m_new
    return h


def _mlstm_kernel(qf_ref, kf_ref, vf_ref, gf_ref, gtf_ref, qb_ref, kb_ref, vb_ref, gb_ref, gtb_ref,
                  gbias_ref, gbias_t_ref, hf_ref, hb_ref, c_scr, n_scr, m_scr, *, dh):
    @pl.when(pl.program_id(1) == 0)
    def _():
        c_scr[...] = jnp.zeros_like(c_scr)
        n_scr[...] = jnp.zeros_like(n_scr)
        m_scr[...] = jnp.zeros_like(m_scr)

    t = qf_ref.shape[1]
    rid = lax.broadcasted_iota(I32, (t, t), 0)
    cid = lax.broadcasted_iota(I32, (t, t), 1)
    lower = jnp.where(rid >= cid, 1.0, 0.0).astype(BF16)
    upper = jnp.where(cid >= rid, 1.0, 0.0).astype(BF16)

    for d, (q_ref, k_ref, v_ref, g_ref, gt_ref, h_ref) in enumerate(
            [(qf_ref, kf_ref, vf_ref, gf_ref, gtf_ref, hf_ref),
             (qb_ref, kb_ref, vb_ref, gb_ref, gtb_ref, hb_ref)]):
        backward = d == 1
        g = g_ref[0, :, 0:16] + gbias_ref[...]
        gt = gt_ref[0] + gbias_t_ref[...]
        lf = _log_sigmoid(g)
        lft = _log_sigmoid(gt)
        cum_col = _dot3_right(upper if backward else lower, lf)
        cum_row = _dot3_left(lft, lower if backward else upper)
        for hd in range(ML_HEADS):
            ci = 2 * ML_HEADS * d + hd
            cf = ci + ML_HEADS
            sl = slice(hd * dh, (hd + 1) * dh)
            h = _mlstm_chain_step(
                q_ref[0, :, sl], k_ref[0, :, sl], v_ref[0, :, sl],
                g[:, ci:ci + 1], lf[:, cf:cf + 1], gt[ci:ci + 1, :], lft[cf:cf + 1, :],
                cum_col[:, cf:cf + 1], cum_row[cf:cf + 1, :],
                c_scr, n_scr, m_scr, d * ML_HEADS + hd, backward)
            h_ref[0, :, sl] = h.astype(h_ref.dtype)


def _mlstm_scan(qk, ml, gates, gates_t, gbias, gbias_t, n_ctx_chunks):
    bsz, ltot, dml2 = qk.shape
    dml = dml2 // 2
    dh = dml // ML_HEADS
    t = ML_T
    nchunk = ltot // t
    cc = n_ctx_chunks
    nx = nchunk - cc

    def fwd(s):
        return jnp.where(s < cc, nx + s, s - cc)

    def bwd(s):
        return jnp.where(s < cc, nx + cc - 1 - s, nx - 1 - (s - cc))

    def specs(cmap):
        return [pl.BlockSpec((1, t, dml), lambda b, s: (b, cmap(s), 0)),
                pl.BlockSpec((1, t, dml), lambda b, s: (b, cmap(s), 1)),
                pl.BlockSpec((1, t, dml), lambda b, s: (b, cmap(s), 2)),
                pl.BlockSpec((1, t, LANES), lambda b, s: (b, cmap(s), 0)),
                pl.BlockSpec((1, 16, t), lambda b, s: (b, 0, cmap(s)))]

    kern = functools.partial(_mlstm_kernel, dh=dh)
    nchain = 2 * ML_HEADS
    return pl.pallas_call(
        kern,
        grid=(bsz, nchunk),
        in_specs=specs(fwd) + specs(bwd) + [pl.BlockSpec((1, 16), lambda b, s: (0, 0)),
                                            pl.BlockSpec((16, 1), lambda b, s: (0, 0))],
        out_specs=[pl.BlockSpec((1, t, dml), lambda b, s: (b, fwd(s), 0)),
                   pl.BlockSpec((1, t, dml), lambda b, s: (b, bwd(s), 0))],
        out_shape=[jax.ShapeDtypeStruct((bsz, ltot, dml), BF16)] * 2,
        scratch_shapes=[pltpu.VMEM((nchain, dh, dh), F32),
                        pltpu.VMEM((nchain, 1, dh), F32),
                        pltpu.VMEM((nchain, 1, 1), F32)],
        compiler_params=_cparams(("parallel", "arbitrary")),
        name="mlstm_scan",
    )(qk, qk, ml, gates, gates_t, qk, qk, ml, gates, gates_t, gbias, gbias_t)


COL_PITCH = ML_RB + 8


def _mlpost_kernel(hf_ref, hb_ref, o_ref, mg_ref, out_ref, scr, *, dh):
    nw, _, rb, dml = hf_ref.shape[1:]
    h = (hf_ref[0].astype(F32) + hb_ref[0].astype(F32)).reshape(nw * rb, dml)
    parts = []
    for hd in range(ML_HEADS):
        hh = h[:, hd * dh:(hd + 1) * dh]
        parts.append(hh * lax.rsqrt(jnp.mean(hh * hh, axis=-1, keepdims=True) + EPS))
    hn = jnp.concatenate(parts, axis=-1) * mg_ref[...]
    res = hn * _sigmoid(o_ref[0].astype(F32).reshape(nw * rb, dml))
    nlb = dml // LANES
    for c in range(nlb):
        for w in range(nw):
            scr[c, w * COL_PITCH:w * COL_PITCH + rb, :] = res[w * rb:(w + 1) * rb, c * LANES:(c + 1) * LANES]
    for r in range(rb):
        row = jnp.concatenate([scr[c, pl.ds(r, nw, stride=COL_PITCH), :] for c in range(nlb)], axis=-1)
        out_ref[0, r * nw:(r + 1) * nw, :] = row.astype(out_ref.dtype)


def _ml_post(hf5, hb5, ml5, ml_g, seq):
    bsz, _, rblocks, rb, dml = hf5.shape
    kern = functools.partial(_mlpost_kernel, dh=dml // ML_HEADS)
    blk = (1, GRID_W, 1, rb, dml)
    return pl.pallas_call(
        kern,
        grid=(bsz, rblocks),
        in_specs=[pl.BlockSpec(blk, lambda b, i: (b, 0, i, 0, 0)),
                  pl.BlockSpec(blk, lambda b, i: (b, 0, i, 0, 0)),
                  pl.BlockSpec(blk, lambda b, i: (b, 0, i, 0, 3)),
                  pl.BlockSpec((1, dml), lambda b, i: (0, 0))],
        out_specs=pl.BlockSpec((1, rb * GRID_W, dml), lambda b, i: (b, i, 0)),
        out_shape=jax.ShapeDtypeStruct((bsz, seq, dml), BF16),
        scratch_shapes=[pltpu.VMEM((dml // LANES, GRID_W * COL_PITCH, LANES), F32)],
        compiler_params=_cparams(("parallel", "parallel")),
        name="mlstm_post",
    )(hf5, hb5, ml5, ml_g)


def _gelu_tanh(y):
    return 0.5 * y * (1.0 + jnp.tanh(math.sqrt(2.0 / math.pi) * (y + 0.044715 * (y * y * y))))


def _mixout_kernel(ys_ref, u_ref, ml_ref, x_ref, dsk_ref, gw_ref, gb_ref, wo_ref, ng_ref, mod_ref,
                   rwh_ref, rwl_ref, x1_ref, hx_ref, aff_ref, *, n_exp):
    y = ys_ref[...] + dsk_ref[...] * u_ref[...]
    ge = _gelu_tanh(y)
    z = _dot(ge.astype(BF16), gw_ref[...]) + gb_ref[...]
    s5 = (ge * _sigmoid(z)).astype(BF16)
    yx = _dot(jnp.concatenate([s5, ml_ref[...]], axis=-1), wo_ref[...])
    ng = ng_ref[...]
    mod = mod_ref[0]
    r1 = yx * lax.rsqrt(jnp.mean(yx * yx, axis=-1, keepdims=True) + EPS) * ng[1:2]
    x1 = x_ref[...] + mod[0:1] * r1
    x1_ref[...] = x1
    hx = x1 * lax.rsqrt(jnp.mean(x1 * x1, axis=-1, keepdims=True) + EPS) * ng[2:3]
    hx = hx * (1.0 + mod[1:2]) + mod[2:3]
    nslab = hx.shape[1] // LANES
    for cb in range(nslab):
        hx_ref[pl.ds(cb, hx.shape[0], stride=nslab), :] = hx[:, cb * LANES:(cb + 1) * LANES]
    h1, h2, h3 = _split3(hx)
    rwh = rwh_ref[...]
    logits = _dot(h1, rwh) + _dot(h2, rwh) + _dot(h3, rwh) + _dot(h1, rwl_ref[...])
    lane = lax.broadcasted_iota(I32, logits.shape, 1)
    logits = jnp.where(lane < n_exp, logits, -jnp.inf)
    e = jnp.exp(logits - jnp.max(logits, axis=-1, keepdims=True))
    aff_ref[...] = e / jnp.sum(e, axis=-1, keepdims=True)


def _mixer_out(ys, u, mlo, x2d, rows_per_batch, dsk, glu_w, glu_b, w_out, norm_g, mod3, rw_hi, rw_lo):
    n, d = x2d.shape
    ds5 = ys.shape[1]
    dml = mlo.shape[1]
    tm = 256
    tiles_per_batch = rows_per_batch // tm
    kern = functools.partial(_mixout_kernel, n_exp=N_EXPERTS)
    row = lambda i: (i, 0)
    const = lambda i: (0, 0)
    return pl.pallas_call(
        kern,
        grid=(n // tm,),
        in_specs=[pl.BlockSpec((tm, ds5), row),
                  pl.BlockSpec((tm, ds5), row),
                  pl.BlockSpec((tm, dml), row),
                  pl.BlockSpec((tm, d), row),
                  pl.BlockSpec((1, ds5), const),
                  pl.BlockSpec((ds5, ds5), const),
                  pl.BlockSpec((1, ds5), const),
                  pl.BlockSpec((d, d), const),
                  pl.BlockSpec((4, d), const),
                  pl.BlockSpec((1, 8, d), lambda i: (i // tiles_per_batch, 0, 0)),
                  pl.BlockSpec((d, LANES), const),
                  pl.BlockSpec((d, LANES), const)],
        out_specs=[pl.BlockSpec((tm, d), row),
                   pl.BlockSpec((tm * (d // LANES), LANES), row),
                   pl.BlockSpec((tm, LANES), row)],
        out_shape=[jax.ShapeDtypeStruct((n, d), F32),
                   jax.ShapeDtypeStruct((n * (d // LANES), LANES), F32),
                   jax.ShapeDtypeStruct((n, LANES), F32)],
        compiler_params=_cparams(("parallel",)),
        name="mixer_out",
    )(ys, u, mlo, x2d, dsk, glu_w, glu_b, w_out, norm_g, mod3, rw_hi, rw_lo)


def _count(mask):
    c = jnp.sum(jnp.where(mask, 1.0, 0.0), axis=2, keepdims=True)
    return jnp.sum(c, axis=1, keepdims=True)


SELECT_BISECTIONS = 48


def _pow2_neg(n):
    p = jnp.ones(n.shape, F32)
    for i in range(7):
        p = p * jnp.where(((n >> i) & 1) == 1, 2.0 ** -(2 ** i), 1.0)
    return p


def _select_kernel(aff_ref, off_ref, idx_ref, gate_ref, *, cap, n_tok):
    a = aff_ref[0]
    e, nb, _ = a.shape

    def enough(thr):
        return _count(a >= thr) >= cap

    ex_lo = jnp.full((e, 1, 1), -127, I32)
    ex_hi = jnp.full((e, 1, 1), 1, I32)
    for _ in range(7):
        ex_mid = (ex_lo + ex_hi) >> 1
        ok = enough(_pow2_neg(-ex_mid))
        ex_lo = jnp.where(ok, ex_mid, ex_lo)
        ex_hi = jnp.where(ok, ex_hi, ex_mid)
    lo = jnp.where(ex_lo == -127, 0.0, _pow2_neg(jnp.minimum(-ex_lo, 126)))
    hi = jnp.where(ex_hi == 1, 2.0, _pow2_neg(jnp.maximum(-ex_hi, 0)))
    for _ in range(SELECT_BISECTIONS):
        mid = lo + 0.5 * (hi - lo)
        ok = enough(mid)
        lo = jnp.where(ok, mid, lo)
        hi = jnp.where(ok, hi, mid)
    gt = a >= hi
    tie = (a >= lo) & (a < hi)
    need = cap - _count(gt)
    tok = (lax.broadcasted_iota(I32, a.shape, 1) * LANES + lax.broadcasted_iota(I32, a.shape, 2))
    cut = jnp.zeros((e, 1, 1), I32)
    for bit in range(n_tok.bit_length() - 1, -1, -1):
        cand = cut | (1 << bit)
        cut = jnp.where(_count(tie & (tok < cand)) < need, cand, cut)
    sel = gt | (tie & (tok <= cut))
    self32 = jnp.where(sel, 1.0, 0.0).reshape(e * nb, LANES)
    r = lax.broadcasted_iota(I32, (LANES, LANES), 0)
    c = lax.broadcasted_iota(I32, (LANES, LANES), 1)
    incl = _dot(self32.astype(BF16), jnp.where(r <= c, 1.0, 0.0).astype(BF16))
    tot = jnp.broadcast_to(incl[:, LANES - 1:LANES], (e * nb, LANES)).astype(BF16)
    rr = lax.broadcasted_iota(I32, (e * nb, e * nb), 0)
    cc = lax.broadcasted_iota(I32, (e * nb, e * nb), 1)
    same = (rr // nb) == (cc // nb)
    before = jnp.where(same & (cc < rr), 1.0, 0.0).astype(BF16)
    off = _dot(before, tot)
    off_ref[0] = off.astype(I32).reshape(e, nb, LANES)

    kcol = lax.broadcasted_iota(I32, (cap, nb), 0).astype(F32)
    jrow = lax.broadcasted_iota(I32, (cap, nb), 1).astype(F32)
    lane = lax.broadcasted_iota(I32, (cap, LANES), 1)
    lane_f = lane.astype(F32)
    ones8 = jnp.ones((8, LANES), BF16)
    jr = lax.broadcasted_iota(I32, (nb, nb), 0)
    jc = lax.broadcasted_iota(I32, (nb, nb), 1)
    upper = jnp.where(jr <= jc, 1.0, 0.0).astype(BF16)
    sel3 = self32.reshape(e, nb, LANES)
    incl3 = incl.reshape(e, nb, LANES)

    def to_row(col):
        return _dot_nt(ones8, jnp.where(lane == 0, col, 0.0).astype(BF16))

    for ex in range(e):
        tot_row = _dot_nt(ones8, sel3[ex].astype(BF16))[0:1]
        end_row = _dot(jnp.broadcast_to(tot_row, (8, nb)).astype(BF16), upper)[0:1]
        start_row = end_row - tot_row
        jk = jnp.sum(jnp.where(end_row <= kcol, 1.0, 0.0), axis=1, keepdims=True)
        onehot_j = jnp.where(jrow == jk, 1.0, 0.0)
        kk = kcol[:, 0:1] - jnp.sum(onehot_j * start_row, axis=1, keepdims=True)
        d = _dot(onehot_j.astype(BF16), incl3[ex].astype(BF16))
        tl = jnp.sum(jnp.where(d <= kk, 1.0, 0.0), axis=1, keepdims=True)
        dg = _dot3_right(onehot_j.astype(BF16), a[ex])
        gate = jnp.sum(jnp.where(lane_f == tl, dg, 0.0), axis=1, keepdims=True)
        idx_ref[0, ex] = (to_row(jk) * float(LANES) + to_row(tl)).astype(I32)
        g1, g2, g3 = _split3(gate)
        gate_ref[0, ex] = (to_row(g1.astype(F32)) + to_row(g2.astype(F32))) + to_row(g3.astype(F32))


def _select(aff_t, cap):
    bsz, e, nb, _ = aff_t.shape
    kern = functools.partial(_select_kernel, cap=cap, n_tok=nb * LANES)
    blk = pl.BlockSpec((1, e, nb, LANES), lambda b: (b, 0, 0, 0))
    slot = pl.BlockSpec((1, e, 8, cap), lambda b: (b, 0, 0, 0))
    return pl.pallas_call(
        kern,
        grid=(bsz,),
        in_specs=[blk],
        out_specs=[blk, slot, slot],
        out_shape=[jax.ShapeDtypeStruct(aff_t.shape, I32),
                   jax.ShapeDtypeStruct((bsz, e, 8, cap), I32),
                   jax.ShapeDtypeStruct((bsz, e, 8, cap), F32)],
        compiler_params=_cparams(("parallel",)),
        name="ec_select",
    )(aff_t)


SLAB_PITCH = 24


def _row_copy(hx_hbm, xbuf, sem, src_row, dst_row, nslab):
    return pltpu.make_async_copy(hx_hbm.at[pl.ds(pl.multiple_of(src_row * nslab, nslab), nslab)],
                                 xbuf.at[pl.ds(pl.multiple_of(dst_row * SLAB_PITCH, 8), nslab)], sem)


def _dispatch_kernel(idx_ref, hx_hbm, o_ref, xbuf, sem, *, cap, n_tok, n_exp, nslab):
    b = pl.program_id(0)
    ex = pl.program_id(1)
    base = (b * n_exp + ex) * cap

    def issue(k, _):
        _row_copy(hx_hbm, xbuf, sem, b * n_tok + idx_ref[base + k], k, nslab).start()
        return 0

    lax.fori_loop(0, cap, issue, 0, unroll=8)

    def drain(k, _):
        _row_copy(hx_hbm, xbuf, sem, 0, k, nslab).wait()
        return 0

    lax.fori_loop(0, cap, drain, 0, unroll=8)
    for c in range(nslab):
        o_ref[0, 0, :, c * LANES:(c + 1) * LANES] = xbuf[pl.ds(c, cap, stride=SLAB_PITCH), :].astype(o_ref.dtype)


def _dispatch(idx_flat, hx_slabs, bsz, n_exp, cap, nslab):
    n_tok = hx_slabs.shape[0] // (bsz * nslab)
    d = nslab * LANES
    kern = functools.partial(_dispatch_kernel, cap=cap, n_tok=n_tok, n_exp=n_exp, nslab=nslab)
    return pl.pallas_call(
        kern,
        grid_spec=pltpu.PrefetchScalarGridSpec(
            num_scalar_prefetch=1,
            grid=(bsz, n_exp),
            in_specs=[pl.BlockSpec(memory_space=pl.ANY)],
            out_specs=pl.BlockSpec((1, 1, cap, d), lambda b, e, idx: (b, e, 0, 0)),
            scratch_shapes=[pltpu.VMEM((cap * SLAB_PITCH, LANES), F32), pltpu.SemaphoreType.DMA(())]),
        out_shape=jax.ShapeDtypeStruct((bsz, n_exp, cap, d), BF16),
        compiler_params=_cparams(("arbitrary", "arbitrary")),
        name="ec_dispatch",
    )(idx_flat, hx_slabs)


def _ffn_kernel(x_ref, g_ref, wg_ref, wu_ref, wd_ref, o_ref, acc):
    f = pl.program_id(2)

    @pl.when(f == 0)
    def _():
        acc[...] = jnp.zeros_like(acc)

    x = x_ref[0, 0]
    a = _dot(x, wg_ref[0].astype(BF16))
    u = _dot(x, wu_ref[0].astype(BF16))
    hmid = (_silu(a) * u).astype(BF16)
    acc[...] += _dot(hmid, wd_ref[0].astype(BF16))

    @pl.when(f == pl.num_programs(2) - 1)
    def _():
        g = g_ref[0, 0]
        cap = g.shape[0]
        for c in range(o_ref.shape[2]):
            o_ref[0, 0, c, 0:cap, :] = acc[:, c * LANES:(c + 1) * LANES] * g
            o_ref[0, 0, c, cap:, :] = jnp.zeros((o_ref.shape[3] - cap, LANES), F32)


PLANE_PAD = 8


def _expert_ffn(xs, gates, w_gate, w_up, w_down):
    bsz, n_exp, cap, d = xs.shape
    ff = w_gate.shape[2]
    tf = 256
    rows = cap + PLANE_PAD
    return pl.pallas_call(
        _ffn_kernel,
        grid=(n_exp, bsz, ff // tf),
        in_specs=[pl.BlockSpec((1, 1, cap, d), lambda e, b, f: (b, e, 0, 0)),
                  pl.BlockSpec((1, 1, cap, 1), lambda e, b, f: (b, e, 0, 0)),
                  pl.BlockSpec((1, d, tf), lambda e, b, f: (e, 0, f)),
                  pl.BlockSpec((1, d, tf), lambda e, b, f: (e, 0, f)),
                  pl.BlockSpec((1, tf, d), lambda e, b, f: (e, f, 0))],
        out_specs=pl.BlockSpec((1, 1, d // LANES, rows, LANES), lambda e, b, f: (b, e, 0, 0, 0)),
        out_shape=jax.ShapeDtypeStruct((bsz, n_exp, d // LANES, rows, LANES), F32),
        scratch_shapes=[pltpu.VMEM((cap, d), F32)],
        compiler_params=_cparams(("parallel", "parallel", "arbitrary")),
        name="expert_ffn",
    )(xs, gates, w_gate, w_up, w_down)


COMB_UNROLL = 4


def _combine_kernel(idx_ref, off_ref, y_ref, o_ref, *, cap, nb, n_exp, tok_part, n_part):
    b = pl.program_id(0)
    part = pl.program_id(1)
    ex = pl.program_id(3)

    @pl.when(ex == 0)
    def _():
        o_ref[...] = jnp.zeros_like(o_ref)

    be = b * n_exp + ex
    blocks_per_part = nb // n_part
    k_lo = off_ref[be * nb + part * blocks_per_part]
    k_hi = jnp.where(part == n_part - 1, cap,
                     off_ref[be * nb + jnp.minimum((part + 1) * blocks_per_part, nb - 1)])
    t0 = part * tok_part

    def add_rows(k, n):
        toks = [idx_ref[be * cap + k + i] - t0 for i in range(n)]
        rows = [o_ref[0, toks[i]] + y_ref[0, 0, :, k + i, :] for i in range(n)]
        for i in range(n):
            o_ref[0, toks[i]] = rows[i]

    n_full = (k_hi - k_lo) // COMB_UNROLL

    def body(i, _):
        add_rows(k_lo + i * COMB_UNROLL, COMB_UNROLL)
        return 0

    lax.fori_loop(0, n_full, body, 0)

    def tail(k, _):
        add_rows(k, 1)
        return 0

    lax.fori_loop(k_lo + n_full * COMB_UNROLL, k_hi, tail, 0)


def _combine(idx_flat, off_flat, y, n_tok, cap):
    bsz, n_exp, nslab, rows, _ = y.shape
    nb = n_tok // TOK_BLK
    n_part = 2
    sblk = 8
    kern = functools.partial(_combine_kernel, cap=cap, nb=nb, n_exp=n_exp, tok_part=n_tok // n_part,
                             n_part=n_part)
    return pl.pallas_call(
        kern,
        grid_spec=pltpu.PrefetchScalarGridSpec(
            num_scalar_prefetch=2,
            grid=(bsz, n_part, nslab // sblk, n_exp),
            in_specs=[pl.BlockSpec((1, 1, sblk, rows, LANES), lambda b, p, c, e, idx, off: (b, e, c, 0, 0))],
            out_specs=pl.BlockSpec((1, n_tok // n_part, sblk, LANES),
                                   lambda b, p, c, e, idx, off: (b, p, c, 0))),
        out_shape=jax.ShapeDtypeStruct((bsz, n_tok, nslab, LANES), F32),
        compiler_params=_cparams(("parallel", "parallel", "parallel", "arbitrary")),
        name="ec_combine",
    )(idx_flat, off_flat, y)


def _final_kernel(x1_ref, moe_ref, g_ref, mod_ref, o_ref):
    tm = x1_ref.shape[0]
    nslab = moe_ref.shape[0] // tm
    m = jnp.concatenate([moe_ref[pl.ds(c, tm, stride=nslab), :] for c in range(nslab)], axis=-1)
    r = m * lax.rsqrt(jnp.mean(m * m, axis=-1, keepdims=True) + EPS) * g_ref[...]
    o_ref[...] = x1_ref[...] + mod_ref[0] * r


def _final(x1, moe, g3, g2, rows_per_batch):
    n, d = x1.shape
    tm = 512
    tiles_per_batch = rows_per_batch // tm
    return pl.pallas_call(
        _final_kernel,
        grid=(n // tm,),
        in_specs=[pl.BlockSpec((tm, d), lambda i: (i, 0)),
                  pl.BlockSpec((tm * (d // LANES), LANES), lambda i: (i, 0)),
                  pl.BlockSpec((1, d), lambda i: (0, 0)),
                  pl.BlockSpec((1, 1, d), lambda i: (i // tiles_per_batch, 0, 0))],
        out_specs=pl.BlockSpec((tm, d), lambda i: (i, 0)),
        out_shape=jax.ShapeDtypeStruct((n, d), F32),
        compiler_params=_cparams(("parallel",)),
        name="final_residual",
    )(x1, moe, g3, g2)


def _layer(x, c, ctx, c_ctx, ada_w, ada_b, norm_g, w_in, s5_a_re, s5_a_im, s5_log_dt, s5_b_re,
           s5_b_im, s5_c_re, s5_c_im, s5_d, s5_glu_w, s5_glu_b, ml_conv_w, ml_conv_b, ml_gate_b,
           ml_norm_g, w_out, router_w, exp_w_gate, exp_w_up, exp_w_down):
    bsz, seq, d = x.shape
    lctx = ctx.shape[1]
    ds5 = s5_d.shape[0]
    dml = ml_norm_g.shape[0]
    ngroups = ds5 // S5_GROUP
    rows = seq // GRID_W
    n = bsz * seq
    cap = EC_FACTOR * seq // N_EXPERTS
    assert cap >= TOK_BLK + 16 and seq % DISP_TOK == 0 and lctx % ML_T == 0 and seq % ML_T == 0

    cond = jnp.concatenate([c, c_ctx[None], jnp.zeros((8 - bsz - 1, d), F32)], axis=0)
    mod = _modulation(cond, ada_w, ada_b).reshape(8, N_MOD, d)
    sh1, sc1, g1, sh2, sc2, g2 = [mod[:, i] for i in range(N_MOD)]

    nmain = ds5 + 4 * dml
    w_s5 = w_in[:, :ds5].astype(BF16)
    w_ml = w_in[:, ds5:nmain].astype(BF16)
    w_gate = jnp.pad(w_in[:, nmain:], ((0, 0), (0, LANES - 4 * ML_HEADS))).astype(BF16)
    g0 = norm_g[0:1]
    sc, sh = sc1[:, None], sh1[:, None]
    of_sample = lambda b: b
    of_ctx = lambda b: bsz
    ux = _in_projection_s5(x, g0, sc, sh, of_sample, w_s5)
    uc = _in_projection_s5(ctx, g0, sc, sh, of_ctx, w_s5)
    ltot = seq + lctx
    ml5, gates5 = _in_projection_ml(x, ctx, g0, sc, sh, bsz, w_ml, w_gate)

    m, bp, cpt, avec = _s5_operators(s5_a_re, s5_a_im, s5_log_dt, s5_b_re, s5_b_im, s5_c_re, s5_c_im)
    ys = _s5_scan(ux, uc, m, bp, cpt, avec).reshape(n, ds5)

    ml = ml5.reshape(bsz, ltot, 4 * dml)
    gates = gates5.reshape(bsz, ltot, LANES)
    conv_w8 = jnp.pad(ml_conv_w, ((0, 8 - CONV_K), (0, 0)))
    qscale = jnp.concatenate([jnp.full((1, dml), (dml // ML_HEADS) ** -0.5, F32),
                              jnp.ones((1, dml), F32)], axis=1)
    qk = _conv_silu(ml, conv_w8, ml_conv_b[None], qscale, (0, seq), (seq, ltot))
    gbias = ml_gate_b.reshape(1, 4 * ML_HEADS)
    gates_t = gates[..., :4 * ML_HEADS].swapaxes(1, 2)
    hf, hb = _mlstm_scan(qk, ml, gates, gates_t, gbias, gbias.reshape(4 * ML_HEADS, 1), lctx // ML_T)
    mlo = _ml_post(hf.reshape(ml5.shape[:4] + (dml,)), hb.reshape(ml5.shape[:4] + (dml,)), ml5,
                   ml_norm_g[None], seq).reshape(n, dml)

    mod3 = jnp.stack([g1[:bsz], sc2[:bsz], sh2[:bsz]], axis=1)
    mod3 = jnp.concatenate([mod3, jnp.zeros((bsz, 5, d), F32)], axis=1)
    rw = jnp.pad(router_w, ((0, 0), (0, LANES - N_EXPERTS)))
    rw_hi = rw.astype(BF16)
    rw_lo = (rw - rw_hi.astype(F32)).astype(BF16)
    x1, hx2, aff = _mixer_out(ys, ux.reshape(n, ds5), mlo, x.reshape(n, d), seq, s5_d[None],
                              s5_glu_w.astype(BF16), s5_glu_b[None], w_out.astype(BF16), norm_g, mod3,
                              rw_hi, rw_lo)

    nb = seq // TOK_BLK
    aff16 = aff[:, :N_EXPERTS].reshape(bsz, seq, N_EXPERTS)
    aff_t = aff16.swapaxes(1, 2).reshape(bsz, N_EXPERTS, nb, LANES)
    off, idx, gates = _select(aff_t, cap)
    off_flat = off[..., 0].reshape(-1)
    idx_flat = idx[:, :, 0].reshape(-1)
    xs = _dispatch(idx_flat, hx2, bsz, N_EXPERTS, cap, d // LANES)
    ye = _expert_ffn(xs, gates[:, :, 0, :, None], exp_w_gate, exp_w_up, exp_w_down)
    moe = _combine(idx_flat, off_flat, ye, seq, cap)
    out = _final(x1, moe.reshape(n * (d // LANES), LANES), norm_g[3:4], g2[:bsz, None], seq)
    return out.reshape(bsz, seq, d)


def kernel(x, c, ctx, c_ctx, ada_w, ada_b, norm_g, w_in, s5_a_re, s5_a_im, s5_log_dt, s5_b_re, s5_b_im, s5_c_re, s5_c_im, s5_d, s5_glu_w, s5_glu_b, ml_conv_w, ml_conv_b, ml_gate_b, ml_norm_g, w_out, router_w, exp_w_gate, exp_w_up, exp_w_down):
    depth = ada_w.shape[0]
    assert depth == 1, "context outputs are only produced when another layer follows"
    li = 0
    return _layer(x, c, ctx, c_ctx, ada_w[li], ada_b[li], norm_g[li], w_in[li], s5_a_re[li],
                  s5_a_im[li], s5_log_dt[li], s5_b_re[li], s5_b_im[li], s5_c_re[li], s5_c_im[li],
                  s5_d[li], s5_glu_w[li], s5_glu_b[li], ml_conv_w[li], ml_conv_b[li], ml_gate_b[li],
                  ml_norm_g[li], w_out[li], router_w[li], exp_w_gate[li], exp_w_up[li],
                  exp_w_down[li])
```

```python
import functools
import math

import jax
import jax.numpy as jnp
from jax import lax
from jax.experimental import pallas as pl
from jax.experimental.pallas import tpu as pltpu

F32 = jnp.float32
BF16 = jnp.bfloat16
I32 = jnp.int32

EPS = 1e-6
GRID_W = 64
S5_GROUP = 16
S5_STATE = 64
S5_T = 16
ML_HEADS = 4
ML_T = 256
CONV_K = 5
N_EXPERTS = 16
EC_FACTOR = 2
N_MOD = 6
LANES = 128
TOK_BLK = 128
DISP_TOK = 512
DISP_W = DISP_TOK + 8
VMEM_LIMIT = 56 * 1024 * 1024


def _cparams(sem):
    return pltpu.CompilerParams(dimension_semantics=sem, vmem_limit_bytes=VMEM_LIMIT)


def _sigmoid(x):
    return 1.0 / (1.0 + jnp.exp(-x))


def _silu(x):
    return x * _sigmoid(x)


def _log_sigmoid(x):
    return jnp.minimum(x, 0.0) - jnp.log(1.0 + jnp.exp(-jnp.abs(x)))


def _split3(x):
    x1 = x.astype(BF16)
    r = x - x1.astype(F32)
    x2 = r.astype(BF16)
    x3 = (r - x2.astype(F32)).astype(BF16)
    return x1, x2, x3


def _dot(a, b):
    return jnp.dot(a, b, preferred_element_type=F32)


def _dot_nt(a, b):
    return lax.dot_general(a, b, (((1,), (1,)), ((), ())), preferred_element_type=F32)


def _dot_tn(a, b):
    return lax.dot_general(a, b, (((0,), (0,)), ((), ())), preferred_element_type=F32)


def _dot3_left(a_f32, b_bf16):
    a1, a2, a3 = _split3(a_f32)
    return _dot(a1, b_bf16) + _dot(a2, b_bf16) + _dot(a3, b_bf16)


def _dot3_right(a_bf16, b_f32):
    b1, b2, b3 = _split3(b_f32)
    return _dot(a_bf16, b1) + _dot(a_bf16, b2) + _dot(a_bf16, b3)


def _mod_kernel(c_ref, w_ref, b_ref, o_ref):
    c = c_ref[...]
    s = _silu(c).astype(BF16)
    o_ref[...] = _dot(s, w_ref[...].astype(BF16)) + b_ref[...]


def _modulation(cond8, ada_w, ada_b):
    d, n = ada_w.shape
    tn = 1024
    return pl.pallas_call(
        _mod_kernel,
        grid=(n // tn,),
        in_specs=[pl.BlockSpec((8, d), lambda j: (0, 0)),
                  pl.BlockSpec((d, tn), lambda j: (0, j)),
                  pl.BlockSpec((1, tn), lambda j: (0, j))],
        out_specs=pl.BlockSpec((8, tn), lambda j: (0, j)),
        out_shape=jax.ShapeDtypeStruct((8, n), F32),
        compiler_params=_cparams(("parallel",)),
        name="modulation",
    )(cond8, ada_w, ada_b.reshape(1, n))


def _norm_mod(x, g, sc, sh):
    ms = jnp.mean(x * x, axis=-1, keepdims=True)
    h = x * lax.rsqrt(ms + EPS) * g
    return (h * (1.0 + sc) + sh).astype(BF16)


def _inproj_s5_kernel(x_ref, g_ref, sc_ref, sh_ref, w_ref, o_ref, h_scr):
    @pl.when(pl.program_id(2) == 0)
    def _():
        h_scr[...] = _norm_mod(x_ref[0], g_ref[...], sc_ref[0], sh_ref[0])

    o_ref[0] = _dot(h_scr[...], w_ref[...])


def _in_projection_s5(x, g0, sc, sh, mod_of_batch, w_s5):
    bsz, length, d = x.shape
    ds5 = w_s5.shape[1]
    tm = min(1024, length)
    tn = 512
    return pl.pallas_call(
        _inproj_s5_kernel,
        grid=(bsz, length // tm, ds5 // tn),
        in_specs=[pl.BlockSpec((1, tm, d), lambda b, i, j: (b, i, 0)),
                  pl.BlockSpec((1, d), lambda b, i, j: (0, 0)),
                  pl.BlockSpec((1, 1, d), lambda b, i, j: (mod_of_batch(b), 0, 0)),
                  pl.BlockSpec((1, 1, d), lambda b, i, j: (mod_of_batch(b), 0, 0)),
                  pl.BlockSpec((d, tn), lambda b, i, j: (0, j))],
        out_specs=pl.BlockSpec((1, tm, tn), lambda b, i, j: (b, i, j)),
        out_shape=jax.ShapeDtypeStruct((bsz, length, ds5), F32),
        scratch_shapes=[pltpu.VMEM((tm, d), BF16)],
        compiler_params=_cparams(("parallel", "parallel", "arbitrary")),
        name="in_projection_s5",
    )(x, g0, sc, sh, w_s5)


def _inproj_ml_kernel(x_ref, c_ref, g_ref, sc_ref, sh_ref, csc_ref, csh_ref, w_ref, wg_ref, ws_ref,
                      o_ref, og_ref, u_ref, h_scr, x_scr, hr_scr, *, n_ml_blocks):
    rb = o_ref.shape[3]
    nlb = x_ref.shape[2] // LANES
    j = pl.program_id(2)

    @pl.when(j == 0)
    def _():
        hr_scr[...] = _norm_mod(x_ref[0], g_ref[...], sc_ref[0], sh_ref[0])
        for c in range(nlb):
            for r in range(rb):
                x_scr[c, r * ROW_PITCH:r * ROW_PITCH + GRID_W, :] = (
                    x_ref[0, r * GRID_W:(r + 1) * GRID_W, c * LANES:(c + 1) * LANES])
        for w in range(GRID_W):
            xw = jnp.concatenate([x_scr[c, pl.ds(w, rb, stride=ROW_PITCH), :] for c in range(nlb)],
                                 axis=-1)
            h_scr[w * rb:(w + 1) * rb, :] = _norm_mod(xw, g_ref[...], sc_ref[0], sh_ref[0])
        for s in range(c_ref.shape[1]):
            h_scr[(GRID_W + s) * rb:(GRID_W + s + 1) * rb, :] = _norm_mod(
                c_ref[0, s, 0], g_ref[...], csc_ref[0], csh_ref[0])
        og_ref[0] = _dot(h_scr[...], wg_ref[...]).reshape(og_ref.shape[1:])

    @pl.when(j < n_ml_blocks)
    def _():
        o_ref[0] = _dot(h_scr[...], w_ref[...]).reshape(o_ref.shape[1:]).astype(o_ref.dtype)

    @pl.when(j >= n_ml_blocks)
    def _():
        u_ref[0] = _dot(hr_scr[...], ws_ref[...])


ML_RB = 16
ROW_PITCH = GRID_W + 8


def _in_projection_ml(x, ctx, g0, sc, sh, ctx_mod, w_ml, w_gate, w_s5):
    bsz, length, d = x.shape
    lctx = ctx.shape[1]
    nml = w_ml.shape[1]
    ds5 = w_s5.shape[1]
    tn = 512
    tns = 256
    nmb, nsb = nml // tn, ds5 // tns
    rows = length // GRID_W
    rblocks = rows // ML_RB
    nslot_c = lctx // rows
    assert nslot_c * rows == lctx and rblocks * ML_RB == rows
    n_slots = GRID_W + nslot_c
    tm = ML_RB * GRID_W
    ctx5 = ctx.reshape(bsz, nslot_c, rblocks, ML_RB, d)
    kern = functools.partial(_inproj_ml_kernel, n_ml_blocks=nmb)
    return pl.pallas_call(
        kern,
        grid=(bsz, rblocks, nmb + nsb),
        in_specs=[pl.BlockSpec((1, tm, d), lambda b, i, j: (b, i, 0)),
                  pl.BlockSpec((1, nslot_c, 1, ML_RB, d), lambda b, i, j: (b, 0, i, 0, 0)),
                  pl.BlockSpec((1, d), lambda b, i, j: (0, 0)),
                  pl.BlockSpec((1, 1, d), lambda b, i, j: (b, 0, 0)),
                  pl.BlockSpec((1, 1, d), lambda b, i, j: (b, 0, 0)),
                  pl.BlockSpec((1, 1, d), lambda b, i, j: (ctx_mod, 0, 0)),
                  pl.BlockSpec((1, 1, d), lambda b, i, j: (ctx_mod, 0, 0)),
                  pl.BlockSpec((d, tn), lambda b, i, j: (0, jnp.minimum(j, nmb - 1))),
                  pl.BlockSpec((d, LANES), lambda b, i, j: (0, 0)),
                  pl.BlockSpec((d, tns), lambda b, i, j: (0, jnp.maximum(j - nmb, 0)))],
        out_specs=[pl.BlockSpec((1, n_slots, 1, ML_RB, tn),
                                lambda b, i, j: (b, 0, i, 0, jnp.minimum(j, nmb - 1))),
                   pl.BlockSpec((1, n_slots, 1, ML_RB, LANES), lambda b, i, j: (b, 0, i, 0, 0)),
                   pl.BlockSpec((1, tm, tns), lambda b, i, j: (b, i, jnp.maximum(j - nmb, 0)))],
        out_shape=[jax.ShapeDtypeStruct((bsz, n_slots, rblocks, ML_RB, nml), BF16),
                   jax.ShapeDtypeStruct((bsz, n_slots, rblocks, ML_RB, LANES), F32),
                   jax.ShapeDtypeStruct((bsz, length, ds5), F32)],
        scratch_shapes=[pltpu.VMEM((n_slots * ML_RB, d), BF16),
                        pltpu.VMEM((d // LANES, ML_RB * ROW_PITCH, LANES), F32),
                        pltpu.VMEM((tm, d), BF16)],
        compiler_params=_cparams(("parallel", "parallel", "arbitrary")),
        name="in_projection_ml",
    )(x, ctx5, g0, sc, sh, sc, sh, w_ml, w_gate, w_s5)


def _dot_nt6(a, b):
    a1, a2, a3 = _split3(a)
    b1, b2, b3 = _split3(b)
    return (_dot_nt(a1, b1) + (_dot_nt(a1, b2) + _dot_nt(a2, b1))
            + (_dot_nt(a1, b3) + _dot_nt(a3, b1) + _dot_nt(a2, b2)))


S5_LANES = 4 * S5_STATE


def _s5ops_kernel(prm_ref, bt_ref, c_ref, rexp_ref, rc_ref, place_ref, m_ref, bp_ref, cpt_ref, av_ref):
    t = S5_T
    prm = prm_ref[0]
    lr, li = jnp.minimum(prm[0:1], -1e-4), prm[1:2]
    dt = jnp.exp(prm[2:3])
    zr, zi = lr * dt, li * dt
    tau = lax.broadcasted_iota(I32, (32, S5_LANES), 0).astype(F32)
    mag, ang = jnp.exp(tau * zr), tau * zi
    pwr, pwi = mag * jnp.cos(ang), mag * jnp.sin(ang)
    xr, xi = pwr[1:2] - 1.0, pwi[1:2]
    den = lr * lr + li * li
    qr, qi = (xr * lr + xi * li) / den, (xi * lr - xr * li) / den
    btr, bti = bt_ref[0, 0], bt_ref[0, 1]
    bbr, bbi = qr * btr - qi * bti, qr * bti + qi * btr
    cr, ci = c_ref[0, 0], c_ref[0, 1]

    lane = lax.broadcasted_iota(I32, (t * S5_GROUP, S5_LANES), 1)
    fwd = ((lane // S5_STATE) % 2) == 0

    def expand(r, x):
        return _dot3_right(r, x)

    e_t = [expand(rexp_ref[0], z) for z in (pwr, pwi)]
    e_r = [expand(rexp_ref[1], z) for z in (pwr, pwi)]
    e_p = [expand(rexp_ref[2], z) for z in (pwr, pwi)]
    e_q = [expand(rexp_ref[3], z) for z in (pwr, pwi)]
    pk = [jnp.where(fwd, e_t[i], e_r[i]) for i in range(2)]
    pb = [jnp.where(fwd, e_r[i], e_t[i]) for i in range(2)]
    pc = [jnp.where(fwd, e_p[i], e_q[i]) for i in range(2)]
    cxr, cxi = expand(rc_ref[...], cr), expand(rc_ref[...], ci)
    bxr, bxi = expand(rc_ref[...], bbr), expand(rc_ref[...], bbi)

    wkr, wki = pk[0] * cxr - pk[1] * cxi, pk[0] * cxi + pk[1] * cxr
    row_blk = lax.broadcasted_iota(I32, (4 * S5_GROUP, S5_LANES), 0) // S5_GROUP
    lane_blk = lax.broadcasted_iota(I32, (4 * S5_GROUP, S5_LANES), 1) // S5_STATE
    own = row_blk == lane_blk
    bmr = jnp.where(own, jnp.concatenate([bbr] * 4, axis=0), 0.0)
    bmi = jnp.where(own, jnp.concatenate([bbi] * 4, axis=0), 0.0)
    kt = _dot_nt6(bmr, wkr) - _dot_nt6(bmi, wki)

    ln = lax.broadcasted_iota(I32, (S5_GROUP, t * S5_GROUP), 1)
    for gi in range(2):
        ktf = kt[(2 * gi) * S5_GROUP:(2 * gi + 1) * S5_GROUP]
        ktb = kt[(2 * gi + 1) * S5_GROUP:(2 * gi + 2) * S5_GROUP]
        blocks = []
        for s in range(t):
            f = ktf if s == 0 else pltpu.roll(ktf, S5_GROUP * s, axis=1)
            sh = (t * S5_GROUP - S5_GROUP * (t - 1 - s)) % (t * S5_GROUP)
            bk = ktb if sh == 0 else pltpu.roll(ktb, sh, axis=1)
            blocks.append(jnp.where(ln >= S5_GROUP * s, f, 0.0)
                          + jnp.where(ln < S5_GROUP * (s + 1), bk, 0.0))
        m_ref[0, gi] = jnp.concatenate(blocks, axis=0).astype(BF16)

    inj = jnp.concatenate([pb[0] * bxr - pb[1] * bxi, pb[0] * bxi + pb[1] * bxr], axis=-1).astype(BF16)
    out = jnp.concatenate([pc[0] * cxr - pc[1] * cxi, -(pc[0] * cxi + pc[1] * cxr)], axis=-1).astype(BF16)
    for gi in range(2):
        bp_ref[0, gi * t * S5_GROUP:(gi + 1) * t * S5_GROUP, :] = _dot(inj, place_ref[gi]).astype(BF16)
        cpt_ref[0, gi * t * S5_GROUP:(gi + 1) * t * S5_GROUP, :] = _dot(out, place_ref[gi]).astype(BF16)
    a_t = jnp.concatenate([pwr[t:t + 8], pwi[t:t + 8]], axis=-1)
    av_ref[0] = _dot3_left(a_t, place_ref[0]) + _dot3_left(a_t, place_ref[1])


def _s5_operators(a_re, a_im, log_dt, b_re, b_im, c_re, c_im):
    t = S5_T
    g, p = a_re.shape[1], a_re.shape[2]
    npair = g // 2

    def lanes(z):
        return z.reshape(2, npair, 2, p).transpose(1, 2, 0, 3).reshape(npair, 4 * p)

    prm = jnp.stack([lanes(a_re.astype(F32)), lanes(a_im.astype(F32)),
                     lanes(jnp.broadcast_to(log_dt.astype(F32)[..., None], a_re.shape))], axis=1)
    prm = jnp.concatenate([prm, jnp.zeros((npair, 5, 4 * p), F32)], axis=1)

    def tile_d(z):
        z = z.reshape(npair, 2, S5_GROUP, p).transpose(0, 2, 1, 3)
        return jnp.broadcast_to(z[:, :, :, None, :], (npair, S5_GROUP, 2, 2, p)).reshape(npair, S5_GROUP, 4 * p)

    bt = jnp.stack([tile_d(b_re.astype(F32).swapaxes(1, 2)), tile_d(b_im.astype(F32).swapaxes(1, 2))], axis=1)
    ct = jnp.stack([tile_d(c_re.astype(F32)), tile_d(c_im.astype(F32))], axis=1)

    row = jnp.arange(t * S5_GROUP)
    tau, ch = row // S5_GROUP, row % S5_GROUP
    e = jnp.arange(32)
    rexp = jnp.stack([(expo[:, None] == e[None, :]) for expo in (tau, t - 1 - tau, tau + 1, t - tau)]
                     ).astype(BF16)
    rc = (ch[:, None] == jnp.arange(S5_GROUP)[None, :]).astype(BF16)
    src = jnp.arange(2 * 4 * p)
    reim, gi_s, d_s, p_s = src // (4 * p), (src // (2 * p)) % 2, (src // p) % 2, src % p
    dst = d_s * (4 * p) + reim * (2 * p) + gi_s * p + p_s
    col = jnp.arange(8 * p)
    place = jnp.stack([((dst[:, None] == col[None, :]) & (gi_s[:, None] == gi)) for gi in range(2)]
                      ).astype(BF16)
    blk3 = lambda shape: pl.BlockSpec((1,) + shape, lambda i: (i,) + (0,) * len(shape))
    full = lambda arr: pl.BlockSpec(arr.shape, lambda i: (0,) * arr.ndim)
    m, bp, cpt, av = pl.pallas_call(
        _s5ops_kernel,
        grid=(npair,),
        in_specs=[blk3((8, 4 * p)), blk3((2, S5_GROUP, 4 * p)), blk3((2, S5_GROUP, 4 * p)),
                  full(rexp), full(rc), full(place)],
        out_specs=[blk3((2, t * S5_GROUP, t * S5_GROUP)), blk3((2 * t * S5_GROUP, 8 * p)),
                   blk3((2 * t * S5_GROUP, 8 * p)), blk3((8, 8 * p))],
        out_shape=[jax.ShapeDtypeStruct((npair, 2, t * S5_GROUP, t * S5_GROUP), BF16),
                   jax.ShapeDtypeStruct((npair, 2 * t * S5_GROUP, 8 * p), BF16),
                   jax.ShapeDtypeStruct((npair, 2 * t * S5_GROUP, 8 * p), BF16),
                   jax.ShapeDtypeStruct((npair, 8, 8 * p), F32)],
        compiler_params=_cparams(("parallel",)),
        name="s5_operators",
    )(prm, bt, ct, rexp, rc, place)
    avec = av[:, 0].reshape(npair, 4, 2 * p)
    return m, bp, cpt, jnp.concatenate([avec, jnp.zeros_like(avec)], axis=1)


def _s5_permutation():
    i = jnp.arange(8 * 8 * S5_GROUP)
    t8, j, c = i // (8 * S5_GROUP), (i // S5_GROUP) % 8, i % S5_GROUP
    dst = j * (8 * S5_GROUP) + t8 * S5_GROUP + c
    e1 = (dst[:, None] == i[None, :]).astype(BF16)
    return e1, e1.T


S5_PAIRS = 4


def _s5_kernel(ux_ref, uc_ref, e1_ref, e1t_ref, m_ref, bp_ref, cp_ref, av_ref, y_ref, s_scr, hp_scr,
               tab_scr, *, cc, xc):
    nc = cc + xc

    def chunk_rows(ref, nchunks):
        cols = [ref[0, pl.ds(t, nchunks, stride=S5_T), :].astype(BF16) for t in range(S5_T)]
        return [_dot(jnp.concatenate(cols[8 * h:8 * h + 8], axis=-1), e1_ref[...]).astype(BF16)
                for h in range(2)]

    def pair_lanes(halves, p):
        return jnp.concatenate([halves[h][:, j * LANES:(j + 1) * LANES]
                                for j in (2 * p, 2 * p + 1) for h in range(2)], axis=-1)

    pc = chunk_rows(uc_ref, cc)
    px = chunk_rows(ux_ref, xc)
    for p in range(S5_PAIRS):
        s_scr[0:cc, p * 512:(p + 1) * 512] = _dot(pair_lanes(pc, p), bp_ref[p])
        s_scr[cc:nc, p * 512:(p + 1) * 512] = _dot(pair_lanes(px, p), bp_ref[p])

    av = av_ref[...]
    sub = lax.broadcasted_iota(I32, (8, 128), 0)

    def cmul(ar, ai, br, bi):
        return ar * br - ai * bi, ar * bi + ai * br

    for p in range(S5_PAIRS):
        for d in range(2):
            a1 = (av[p, 2 * d:2 * d + 1], av[p, 2 * d + 1:2 * d + 2])
            a2 = cmul(*a1, *a1)
            a4 = cmul(*a2, *a2)
            pw_re, pw_im = jnp.zeros((8, 128), F32), jnp.zeros((8, 128), F32)
            cur = a1
            for k in range(8):
                r = k if d == 0 else 7 - k
                pw_re = jnp.where(sub == r, cur[0], pw_re)
                pw_im = jnp.where(sub == r, cur[1], pw_im)
                cur = cmul(*cur, *a1)
            for q, z in enumerate((a1, a2, a4)):
                tab_scr[p, d, 2 * q] = jnp.broadcast_to(z[0], (8, 128))
                tab_scr[p, d, 2 * q + 1] = jnp.broadcast_to(z[1], (8, 128))
            tab_scr[p, d, 6] = pw_re
            tab_scr[p, d, 7] = pw_im

    def scan8(s_re, s_im, h_re, h_im, p, d):
        back = d == 1
        x_re, x_im = s_re, s_im
        for q, k in enumerate((1, 2, 4)):
            keep = (sub < 8 - k) if back else (sub >= k)
            sh = (8 - k) if back else k
            y_re = jnp.where(keep, pltpu.roll(x_re, sh, axis=0), 0.0)
            y_im = jnp.where(keep, pltpu.roll(x_im, sh, axis=0), 0.0)
            m_re, m_im = cmul(tab_scr[p, d, 2 * q], tab_scr[p, d, 2 * q + 1], y_re, y_im)
            x_re, x_im = x_re + m_re, x_im + m_im
        c_re, c_im = cmul(tab_scr[p, d, 6], tab_scr[p, d, 7], h_re, h_im)
        x_re, x_im = x_re + c_re, x_im + c_im
        edge = 7 if back else 0
        sh = 7 if back else 1
        e_re = jnp.where(sub == edge, h_re, pltpu.roll(x_re, sh, axis=0))
        e_im = jnp.where(sub == edge, h_im, pltpu.roll(x_im, sh, axis=0))
        last = 0 if back else 7
        return e_re, e_im, x_re[last:last + 1], x_im[last:last + 1]

    def step(row_f, row_b, carry):
        rf = pl.multiple_of(row_f, 8)
        rb = pl.multiple_of(row_b, 8)
        new = []
        for p in range(S5_PAIRS):
            hr, hi, gr, gi = carry[4 * p:4 * p + 4]
            c0 = p * 512
            p_re, p_im, hr, hi = scan8(s_scr[pl.ds(rf, 8), c0:c0 + 128], s_scr[pl.ds(rf, 8), c0 + 128:c0 + 256],
                                       hr, hi, p, 0)
            hp_scr[pl.ds(rf, 8), c0:c0 + 128] = p_re
            hp_scr[pl.ds(rf, 8), c0 + 128:c0 + 256] = p_im
            p_re, p_im, gr, gi = scan8(s_scr[pl.ds(rb, 8), c0 + 256:c0 + 384], s_scr[pl.ds(rb, 8), c0 + 384:c0 + 512],
                                       gr, gi, p, 1)
            hp_scr[pl.ds(rb, 8), c0 + 256:c0 + 384] = p_re
            hp_scr[pl.ds(rb, 8), c0 + 384:c0 + 512] = p_im
            new += [hr, hi, gr, gi]
        return tuple(new)

    z = jnp.zeros((1, 128), F32)
    carry = lax.fori_loop(0, cc // 8, lambda i, c: step(8 * i, cc - 8 - 8 * i, c), (z,) * (4 * S5_PAIRS))
    lax.fori_loop(0, xc // 8, lambda i, c: step(cc + 8 * i, nc - 8 - 8 * i, c), carry)

    ys = []
    for p in range(S5_PAIRS):
        ux = pair_lanes(px, p)
        y = _dot_nt(hp_scr[cc:nc, p * 512:(p + 1) * 512].astype(BF16), cp_ref[p])
        ys.append(y + jnp.concatenate([_dot(ux[:, 0:256], m_ref[p, 0]), _dot(ux[:, 256:512], m_ref[p, 1])],
                                      axis=-1))
    for h in range(2):
        yh = jnp.concatenate([ys[j // 2][:, (j % 2) * 256 + h * LANES:(j % 2) * 256 + (h + 1) * LANES]
                              for j in range(8)], axis=-1).astype(BF16)
        r = _dot(yh, e1t_ref[...])
        for t8 in range(8):
            y_ref[0, pl.ds(8 * h + t8, xc, stride=S5_T), :] = r[:, t8 * LANES:(t8 + 1) * LANES]


def _s5_scan(ux, uc, m, bp, cp, avec):
    bsz, seq, ds5 = ux.shape
    lctx = uc.shape[1]
    cc, xc = lctx // S5_T, seq // S5_T
    assert cc % 8 == 0 and xc % 8 == 0
    e1, e1t = _s5_permutation()
    kern = functools.partial(_s5_kernel, cc=cc, xc=xc)
    nblk = ds5 // LANES
    const = lambda j, b: (0, 0)
    return pl.pallas_call(
        kern,
        grid=(nblk, bsz),
        in_specs=[pl.BlockSpec((1, seq, LANES), lambda j, b: (b, 0, j)),
                  pl.BlockSpec((1, lctx, LANES), lambda j, b: (b, 0, j)),
                  pl.BlockSpec(e1.shape, const),
                  pl.BlockSpec(e1.shape, const),
                  pl.BlockSpec((S5_PAIRS, 2, 256, 256), lambda j, b: (j, 0, 0, 0)),
                  pl.BlockSpec((S5_PAIRS, 512, 512), lambda j, b: (j, 0, 0)),
                  pl.BlockSpec((S5_PAIRS, 512, 512), lambda j, b: (j, 0, 0)),
                  pl.BlockSpec((S5_PAIRS, 8, 128), lambda j, b: (j, 0, 0))],
        out_specs=pl.BlockSpec((1, seq, LANES), lambda j, b: (b, 0, j)),
        out_shape=jax.ShapeDtypeStruct((bsz, seq, ds5), F32),
        scratch_shapes=[pltpu.VMEM((cc + xc, S5_PAIRS * 512), F32),
                        pltpu.VMEM((cc + xc, S5_PAIRS * 512), F32),
                        pltpu.VMEM((S5_PAIRS, 2, 8, 8, 128), F32)],
        compiler_params=_cparams(("parallel", "parallel")),
        name="s5_scan",
    )(ux, uc, e1, e1t, m, bp, cp, avec)


def _conv_kernel(x_ref, p_ref, n_ref, w_ref, b_ref, sc_ref, o_ref, scr, *, rows, seq_tiles):
    i = pl.program_id(1)
    first = functools.reduce(jnp.logical_or, [i == s for s in seq_tiles[0]])
    last = functools.reduce(jnp.logical_or, [i == s for s in seq_tiles[1]])
    pm = jnp.where(first, 0.0, 1.0)
    nm = jnp.where(last, 0.0, 1.0)
    scr[0:8, :] = p_ref[0, 8:16, :].astype(F32) * pm
    scr[8:8 + rows, :] = x_ref[0].astype(F32)
    scr[8 + rows:16 + rows, :] = n_ref[0, 0:8, :].astype(F32) * nm
    w = w_ref[...]
    acc = scr[6:6 + rows, :] * w[0:1]
    for j in range(1, CONV_K):
        acc = acc + scr[6 + j:6 + j + rows, :] * w[j:j + 1]
    z = acc + b_ref[...]
    o_ref[0] = (_silu(z) * sc_ref[...]).astype(o_ref.dtype)


def _conv_silu(ml, conv_w8, conv_b, scale, seq_starts, seq_ends):
    bsz, ltot, _ = ml.shape
    rows = 256
    ct = 512
    nch = conv_b.shape[-1]
    nt = ltot // rows
    hb = rows // 16
    nhb = ltot // 16
    seq_tiles = (tuple(s // rows for s in seq_starts), tuple(e // rows - 1 for e in seq_ends))
    kern = functools.partial(_conv_kernel, rows=rows, seq_tiles=seq_tiles)
    return pl.pallas_call(
        kern,
        grid=(bsz, nt, nch // ct),
        in_specs=[pl.BlockSpec((1, rows, ct), lambda b, i, c: (b, i, c)),
                  pl.BlockSpec((1, 16, ct), lambda b, i, c: (b, jnp.maximum(i * hb - 1, 0), c)),
                  pl.BlockSpec((1, 16, ct), lambda b, i, c: (b, jnp.minimum((i + 1) * hb, nhb - 1), c)),
                  pl.BlockSpec((8, ct), lambda b, i, c: (0, c)),
                  pl.BlockSpec((1, ct), lambda b, i, c: (0, c)),
                  pl.BlockSpec((1, ct), lambda b, i, c: (0, c))],
        out_specs=pl.BlockSpec((1, rows, ct), lambda b, i, c: (b, i, c)),
        out_shape=jax.ShapeDtypeStruct((bsz, ltot, nch), BF16),
        scratch_shapes=[pltpu.VMEM((rows + 16, ct), F32)],
        compiler_params=_cparams(("parallel", "parallel", "parallel")),
        name="conv_silu",
    )(ml, ml, ml, conv_w8, conv_b, scale)


def _mlstm_chain_step(q, k, v, i_col, f_col, i_row, b_col, b_row, mask, c_ref, n_ref, m_ref, idx):
    m_st = m_ref[idx]
    d = jnp.where(mask, b_col - b_row + i_row, -jnp.inf)
    inter = b_col + m_st
    m_t = jnp.maximum(inter, jnp.max(d, axis=-1, keepdims=True))
    w = jnp.exp(d - m_t)
    s_inter = jnp.exp(inter - m_t)
    s = _dot_nt(q, k) * w
    c_st = c_ref[idx]
    n_st = n_ref[idx]
    num = _dot(s.astype(BF16), v) + s_inter * _dot_nt(q, c_st.astype(BF16))
    qf = q.astype(F32)
    den = jnp.sum(s, axis=-1, keepdims=True) + s_inter * jnp.sum(qf * n_st, axis=-1, keepdims=True)
    h = num / jnp.maximum(jnp.abs(den), jnp.exp(-m_t))
    total = jnp.sum(f_col, axis=0, keepdims=True)
    d_end = total - b_col + i_col
    m_new = jnp.maximum(total + m_st, jnp.max(d_end, axis=0, keepdims=True))
    w_end = jnp.exp(d_end - m_new)
    dec = jnp.exp(total + m_st - m_new)
    vw = (v.astype(F32) * w_end).astype(BF16)
    c_ref[idx] = dec * c_st + _dot_tn(vw, k)
    n_ref[idx] = dec * n_st + jnp.sum(k.astype(F32) * w_end, axis=0, keepdims=True)
    m_ref[idx] = m_new
    return h


def _mlstm_kernel(qf_ref, kf_ref, vf_ref, gf_ref, gtf_ref, qb_ref, kb_ref, vb_ref, gb_ref, gtb_ref,
                  gbias_ref, gbias_t_ref, hf_ref, hb_ref, c_scr, n_scr, m_scr, *, dh):
    @pl.when(pl.program_id(1) == 0)
    def _():
        c_scr[...] = jnp.zeros_like(c_scr)
        n_scr[...] = jnp.zeros_like(n_scr)
        m_scr[...] = jnp.zeros_like(m_scr)

    bsz, t = qf_ref.shape[0], qf_ref.shape[1]
    rid = lax.broadcasted_iota(I32, (t, t), 0)
    cid = lax.broadcasted_iota(I32, (t, t), 1)
    lower = jnp.where(rid >= cid, 1.0, 0.0).astype(BF16)
    upper = jnp.where(cid >= rid, 1.0, 0.0).astype(BF16)

    for bb in range(bsz):
        for d, (q_ref, k_ref, v_ref, g_ref, gt_ref, h_ref) in enumerate(
                [(qf_ref, kf_ref, vf_ref, gf_ref, gtf_ref, hf_ref),
                 (qb_ref, kb_ref, vb_ref, gb_ref, gtb_ref, hb_ref)]):
            backward = d == 1
            g = g_ref[bb, :, 0:16] + gbias_ref[...]
            gt = gt_ref[bb] + gbias_t_ref[...]
            lf = _log_sigmoid(g)
            lft = _log_sigmoid(gt)
            cum_col = _dot3_right(upper if backward else lower, lf)
            cum_row = _dot3_left(lft, lower if backward else upper)
            for hd in range(ML_HEADS):
                ci = 2 * ML_HEADS * d + hd
                cf = ci + ML_HEADS
                sl = slice(hd * dh, (hd + 1) * dh)
                h = _mlstm_chain_step(
                    q_ref[bb, :, sl], k_ref[bb, :, sl], v_ref[bb, :, sl],
                    g[:, ci:ci + 1], lf[:, cf:cf + 1], gt[ci:ci + 1, :],
                    cum_col[:, cf:cf + 1], cum_row[cf:cf + 1, :],
                    (cid >= rid) if backward else (rid >= cid),
                    c_scr, n_scr, m_scr, (bb * 2 + d) * ML_HEADS + hd)
                h_ref[bb, :, sl] = h.astype(h_ref.dtype)


def _mlstm_scan(qk, ml, gates, gates_t, gbias, gbias_t, n_ctx_chunks):
    bsz, ltot, dml2 = qk.shape
    dml = dml2 // 2
    dh = dml // ML_HEADS
    t = ML_T
    nchunk = ltot // t
    cc = n_ctx_chunks
    nx = nchunk - cc

    def fwd(s):
        return jnp.where(s < cc, nx + s, s - cc)

    def bwd(s):
        return jnp.where(s < cc, nx + cc - 1 - s, nx - 1 - (s - cc))

    bps = 1

    def specs(cmap):
        return [pl.BlockSpec((bps, t, dml), lambda b, s: (b, cmap(s), 0)),
                pl.BlockSpec((bps, t, dml), lambda b, s: (b, cmap(s), 1)),
                pl.BlockSpec((bps, t, dml), lambda b, s: (b, cmap(s), 2)),
                pl.BlockSpec((bps, t, LANES), lambda b, s: (b, cmap(s), 0)),
                pl.BlockSpec((bps, 16, t), lambda b, s: (b, 0, cmap(s)))]

    kern = functools.partial(_mlstm_kernel, dh=dh)
    nchain = bps * 2 * ML_HEADS
    return pl.pallas_call(
        kern,
        grid=(bsz // bps, nchunk),
        in_specs=specs(fwd) + specs(bwd) + [pl.BlockSpec((1, 16), lambda b, s: (0, 0)),
                                            pl.BlockSpec((16, 1), lambda b, s: (0, 0))],
        out_specs=[pl.BlockSpec((bps, t, dml), lambda b, s: (b, fwd(s), 0)),
                   pl.BlockSpec((bps, t, dml), lambda b, s: (b, bwd(s), 0))],
        out_shape=[jax.ShapeDtypeStruct((bsz, ltot, dml), BF16)] * 2,
        scratch_shapes=[pltpu.VMEM((nchain, dh, dh), F32),
                        pltpu.VMEM((nchain, 1, dh), F32),
                        pltpu.VMEM((nchain, 1, 1), F32)],
        compiler_params=_cparams(("parallel", "arbitrary")),
        name="mlstm_scan",
    )(qk, qk, ml, gates, gates_t, qk, qk, ml, gates, gates_t, gbias, gbias_t)


COL_PITCH = ML_RB + 8


def _mlpost_kernel(hf_ref, hb_ref, o_ref, mg_ref, out_ref, scr, *, dh):
    nw, _, rb, dml = hf_ref.shape[1:]
    h = (hf_ref[0].astype(F32) + hb_ref[0].astype(F32)).reshape(nw * rb, dml)
    parts = []
    for hd in range(ML_HEADS):
        hh = h[:, hd * dh:(hd + 1) * dh]
        parts.append(hh * lax.rsqrt(jnp.mean(hh * hh, axis=-1, keepdims=True) + EPS))
    hn = jnp.concatenate(parts, axis=-1) * mg_ref[...]
    res = hn * _sigmoid(o_ref[0].astype(F32).reshape(nw * rb, dml))
    nlb = dml // LANES
    for c in range(nlb):
        for w in range(nw):
            scr[c, w * COL_PITCH:w * COL_PITCH + rb, :] = res[w * rb:(w + 1) * rb, c * LANES:(c + 1) * LANES]
    for r in range(rb):
        row = jnp.concatenate([scr[c, pl.ds(r, nw, stride=COL_PITCH), :] for c in range(nlb)], axis=-1)
        out_ref[0, r * nw:(r + 1) * nw, :] = row.astype(out_ref.dtype)


def _ml_post(hf5, hb5, ml5, ml_g, seq):
    bsz, _, rblocks, rb, dml = hf5.shape
    kern = functools.partial(_mlpost_kernel, dh=dml // ML_HEADS)
    blk = (1, GRID_W, 1, rb, dml)
    return pl.pallas_call(
        kern,
        grid=(bsz, rblocks),
        in_specs=[pl.BlockSpec(blk, lambda b, i: (b, 0, i, 0, 0)),
                  pl.BlockSpec(blk, lambda b, i: (b, 0, i, 0, 0)),
                  pl.BlockSpec(blk, lambda b, i: (b, 0, i, 0, 3)),
                  pl.BlockSpec((1, dml), lambda b, i: (0, 0))],
        out_specs=pl.BlockSpec((1, rb * GRID_W, dml), lambda b, i: (b, i, 0)),
        out_shape=jax.ShapeDtypeStruct((bsz, seq, dml), BF16),
        scratch_shapes=[pltpu.VMEM((dml // LANES, GRID_W * COL_PITCH, LANES), F32)],
        compiler_params=_cparams(("parallel", "parallel")),
        name="mlstm_post",
    )(hf5, hb5, ml5, ml_g)


def _gelu_tanh(y):
    return 0.5 * y * (1.0 + jnp.tanh(math.sqrt(2.0 / math.pi) * (y + 0.044715 * (y * y * y))))


def _mixout_kernel(ys_ref, u_ref, ml_ref, x_ref, dsk_ref, gw_ref, gb_ref, wo_ref, ng_ref, mod_ref,
                   rwh_ref, rwl_ref, x1_ref, hx_ref, aff_ref, *, n_exp):
    y = ys_ref[...] + dsk_ref[...] * u_ref[...]
    ge = _gelu_tanh(y)
    z = _dot(ge.astype(BF16), gw_ref[...]) + gb_ref[...]
    s5 = (ge * _sigmoid(z)).astype(BF16)
    yx = _dot(jnp.concatenate([s5, ml_ref[...]], axis=-1), wo_ref[...])
    ng = ng_ref[...]
    mod = mod_ref[0]
    r1 = yx * lax.rsqrt(jnp.mean(yx * yx, axis=-1, keepdims=True) + EPS) * ng[1:2]
    x1 = x_ref[...] + mod[0:1] * r1
    x1_ref[...] = x1
    hx = x1 * lax.rsqrt(jnp.mean(x1 * x1, axis=-1, keepdims=True) + EPS) * ng[2:3]
    hx = hx * (1.0 + mod[1:2]) + mod[2:3]
    nslab = hx.shape[1] // LANES
    for cb in range(nslab):
        hx_ref[pl.ds(cb, hx.shape[0], stride=nslab), :] = hx[:, cb * LANES:(cb + 1) * LANES]
    h1, h2, h3 = _split3(hx)
    rwh = rwh_ref[...]
    logits = _dot(h1, rwh) + _dot(h2, rwh) + _dot(h3, rwh) + _dot(h1, rwl_ref[...])
    lane = lax.broadcasted_iota(I32, logits.shape, 1)
    logits = jnp.where(lane < n_exp, logits, -jnp.inf)
    e = jnp.exp(logits - jnp.max(logits, axis=-1, keepdims=True))
    aff_ref[...] = e / jnp.sum(e, axis=-1, keepdims=True)


def _mixer_out(ys, u, mlo, x2d, rows_per_batch, dsk, glu_w, glu_b, w_out, norm_g, mod3, rw_hi, rw_lo):
    n, d = x2d.shape
    ds5 = ys.shape[1]
    dml = mlo.shape[1]
    tm = 256
    tiles_per_batch = rows_per_batch // tm
    kern = functools.partial(_mixout_kernel, n_exp=N_EXPERTS)
    row = lambda i: (i, 0)
    const = lambda i: (0, 0)
    return pl.pallas_call(
        kern,
        grid=(n // tm,),
        in_specs=[pl.BlockSpec((tm, ds5), row),
                  pl.BlockSpec((tm, ds5), row),
                  pl.BlockSpec((tm, dml), row),
                  pl.BlockSpec((tm, d), row),
                  pl.BlockSpec((1, ds5), const),
                  pl.BlockSpec((ds5, ds5), const),
                  pl.BlockSpec((1, ds5), const),
                  pl.BlockSpec((d, d), const),
                  pl.BlockSpec((4, d), const),
                  pl.BlockSpec((1, 8, d), lambda i: (i // tiles_per_batch, 0, 0)),
                  pl.BlockSpec((d, LANES), const),
                  pl.BlockSpec((d, LANES), const)],
        out_specs=[pl.BlockSpec((tm, d), row),
                   pl.BlockSpec((tm * (d // LANES), LANES), row),
                   pl.BlockSpec((tm, LANES), row)],
        out_shape=[jax.ShapeDtypeStruct((n, d), F32),
                   jax.ShapeDtypeStruct((n * (d // LANES), LANES), F32),
                   jax.ShapeDtypeStruct((n, LANES), F32)],
        compiler_params=_cparams(("parallel",)),
        name="mixer_out",
    )(ys, u, mlo, x2d, dsk, glu_w, glu_b, w_out, norm_g, mod3, rw_hi, rw_lo)


def _count(mask):
    c = jnp.sum(jnp.where(mask, 1.0, 0.0), axis=2, keepdims=True)
    return jnp.sum(c, axis=1, keepdims=True)


SELECT_BISECTIONS = 48


def _pow2_neg(n):
    p = jnp.ones(n.shape, F32)
    for i in range(7):
        p = p * jnp.where(((n >> i) & 1) == 1, 2.0 ** -(2 ** i), 1.0)
    return p


def _select_kernel(aff_ref, off_ref, idx_ref, gate_ref, *, cap, n_tok):
    a = aff_ref[0]
    e, nb, _ = a.shape

    def enough(thr):
        return _count(a >= thr) >= cap

    ex_lo = jnp.full((e, 1, 1), -127, I32)
    ex_hi = jnp.full((e, 1, 1), 1, I32)
    for _ in range(7):
        ex_mid = (ex_lo + ex_hi) >> 1
        ok = enough(_pow2_neg(-ex_mid))
        ex_lo = jnp.where(ok, ex_mid, ex_lo)
        ex_hi = jnp.where(ok, ex_hi, ex_mid)
    lo = jnp.where(ex_lo == -127, 0.0, _pow2_neg(jnp.minimum(-ex_lo, 126)))
    hi = jnp.where(ex_hi == 1, 2.0, _pow2_neg(jnp.maximum(-ex_hi, 0)))
    for _ in range(SELECT_BISECTIONS):
        mid = lo + 0.5 * (hi - lo)
        ok = enough(mid)
        lo = jnp.where(ok, mid, lo)
        hi = jnp.where(ok, hi, mid)
    gt = a >= hi
    tie = (a >= lo) & (a < hi)
    need = cap - _count(gt)
    tok = (lax.broadcasted_iota(I32, a.shape, 1) * LANES + lax.broadcasted_iota(I32, a.shape, 2))
    cut = jnp.zeros((e, 1, 1), I32)
    for bit in range(n_tok.bit_length() - 1, -1, -1):
        cand = cut | (1 << bit)
        cut = jnp.where(_count(tie & (tok < cand)) < need, cand, cut)
    sel = gt | (tie & (tok <= cut))
    self32 = jnp.where(sel, 1.0, 0.0).reshape(e * nb, LANES)
    r = lax.broadcasted_iota(I32, (LANES, LANES), 0)
    c = lax.broadcasted_iota(I32, (LANES, LANES), 1)
    incl = _dot(self32.astype(BF16), jnp.where(r <= c, 1.0, 0.0).astype(BF16))
    tot = jnp.broadcast_to(incl[:, LANES - 1:LANES], (e * nb, LANES)).astype(BF16)
    rr = lax.broadcasted_iota(I32, (e * nb, e * nb), 0)
    cc = lax.broadcasted_iota(I32, (e * nb, e * nb), 1)
    same = (rr // nb) == (cc // nb)
    before = jnp.where(same & (cc < rr), 1.0, 0.0).astype(BF16)
    off = _dot(before, tot)
    off_ref[0] = off.astype(I32).reshape(e, nb, LANES)

    kcol = lax.broadcasted_iota(I32, (cap, nb), 0).astype(F32)
    jrow = lax.broadcasted_iota(I32, (cap, nb), 1).astype(F32)
    lane = lax.broadcasted_iota(I32, (cap, LANES), 1)
    lane_f = lane.astype(F32)
    ones8 = jnp.ones((8, LANES), BF16)
    jr = lax.broadcasted_iota(I32, (nb, nb), 0)
    jc = lax.broadcasted_iota(I32, (nb, nb), 1)
    upper = jnp.where(jr <= jc, 1.0, 0.0).astype(BF16)
    sel3 = self32.reshape(e, nb, LANES)
    incl3 = incl.reshape(e, nb, LANES)

    def to_row(col):
        return _dot_nt(ones8, jnp.where(lane == 0, col, 0.0).astype(BF16))

    for ex in range(e):
        tot_row = _dot_nt(ones8, sel3[ex].astype(BF16))[0:1]
        end_row = _dot(jnp.broadcast_to(tot_row, (8, nb)).astype(BF16), upper)[0:1]
        start_row = end_row - tot_row
        jk = jnp.sum(jnp.where(end_row <= kcol, 1.0, 0.0), axis=1, keepdims=True)
        onehot_j = jnp.where(jrow == jk, 1.0, 0.0)
        kk = kcol[:, 0:1] - jnp.sum(onehot_j * start_row, axis=1, keepdims=True)
        d = _dot(onehot_j.astype(BF16), incl3[ex].astype(BF16))
        tl = jnp.sum(jnp.where(d <= kk, 1.0, 0.0), axis=1, keepdims=True)
        dg = _dot3_right(onehot_j.astype(BF16), a[ex])
        gate = jnp.sum(jnp.where(lane_f == tl, dg, 0.0), axis=1, keepdims=True)
        idx_ref[0, ex] = (to_row(jk) * float(LANES) + to_row(tl)).astype(I32)
        g1, g2, g3 = _split3(gate)
        gate_ref[0, ex] = (to_row(g1.astype(F32)) + to_row(g2.astype(F32))) + to_row(g3.astype(F32))


def _select(aff_t, cap):
    bsz, e, nb, _ = aff_t.shape
    kern = functools.partial(_select_kernel, cap=cap, n_tok=nb * LANES)
    blk = pl.BlockSpec((1, e, nb, LANES), lambda b: (b, 0, 0, 0))
    slot = pl.BlockSpec((1, e, 8, cap), lambda b: (b, 0, 0, 0))
    return pl.pallas_call(
        kern,
        grid=(bsz,),
        in_specs=[blk],
        out_specs=[blk, slot, slot],
        out_shape=[jax.ShapeDtypeStruct(aff_t.shape, I32),
                   jax.ShapeDtypeStruct((bsz, e, 8, cap), I32),
                   jax.ShapeDtypeStruct((bsz, e, 8, cap), F32)],
        compiler_params=_cparams(("parallel",)),
        name="ec_select",
    )(aff_t)


SLAB_PITCH = 24


def _row_copy(hx_hbm, xbuf, sem, src_row, dst_row, nslab):
    return pltpu.make_async_copy(hx_hbm.at[pl.ds(pl.multiple_of(src_row * nslab, nslab), nslab)],
                                 xbuf.at[pl.ds(pl.multiple_of(dst_row * SLAB_PITCH, 8), nslab)], sem)


def _dispatch_kernel(idx_ref, hx_hbm, o_ref, xbuf, sems, *, cap, n_tok, n_exp, nslab, n_steps):
    step = pl.program_id(0) * n_exp + pl.program_id(1)

    def issue_all(s):
        half = s % 2
        row0 = (s // n_exp) * n_tok

        def issue(k, _):
            _row_copy(hx_hbm, xbuf, sems.at[half], row0 + idx_ref[s * cap + k], half * cap + k, nslab).start()
            return 0

        lax.fori_loop(0, cap, issue, 0, unroll=8)

    @pl.when(step == 0)
    def _():
        issue_all(step)

    @pl.when(step + 1 < n_steps)
    def _():
        issue_all(step + 1)

    half = step % 2

    def drain(k, _):
        _row_copy(hx_hbm, xbuf, sems.at[half], 0, half * cap + k, nslab).wait()
        return 0

    lax.fori_loop(0, cap, drain, 0, unroll=8)
    first = pl.multiple_of(half * cap * SLAB_PITCH, 8)
    for c in range(nslab):
        o_ref[0, 0, :, c * LANES:(c + 1) * LANES] = (
            xbuf[pl.ds(first + c, cap, stride=SLAB_PITCH), :].astype(o_ref.dtype))


def _dispatch(idx_flat, hx_slabs, bsz, n_exp, cap, nslab):
    n_tok = hx_slabs.shape[0] // (bsz * nslab)
    d = nslab * LANES
    kern = functools.partial(_dispatch_kernel, cap=cap, n_tok=n_tok, n_exp=n_exp, nslab=nslab,
                             n_steps=bsz * n_exp)
    return pl.pallas_call(
        kern,
        grid_spec=pltpu.PrefetchScalarGridSpec(
            num_scalar_prefetch=1,
            grid=(bsz, n_exp),
            in_specs=[pl.BlockSpec(memory_space=pl.ANY)],
            out_specs=pl.BlockSpec((1, 1, cap, d), lambda b, e, idx: (b, e, 0, 0)),
            scratch_shapes=[pltpu.VMEM((2 * cap * SLAB_PITCH, LANES), F32),
                            pltpu.SemaphoreType.DMA((2,))]),
        out_shape=jax.ShapeDtypeStruct((bsz, n_exp, cap, d), BF16),
        compiler_params=_cparams(("arbitrary", "arbitrary")),
        name="ec_dispatch",
    )(idx_flat, hx_slabs)


def _ffn_kernel(x_ref, g_ref, wg_ref, wu_ref, wd_ref, o_ref):
    f = pl.program_id(2)
    x = x_ref[0, 0]
    cap = x.shape[0]
    a = _dot(x, wg_ref[0].astype(BF16))
    u = _dot(x, wu_ref[0].astype(BF16))
    hmid = (_silu(a) * u * g_ref[0, 0]).astype(BF16)
    y = _dot(hmid, wd_ref[0].astype(BF16))

    @pl.when(f == 0)
    def _():
        for c in range(o_ref.shape[2]):
            o_ref[0, 0, c, 0:cap, :] = y[:, c * LANES:(c + 1) * LANES]
            o_ref[0, 0, c, cap:, :] = jnp.zeros((o_ref.shape[3] - cap, LANES), F32)

    @pl.when(f > 0)
    def _():
        for c in range(o_ref.shape[2]):
            o_ref[0, 0, c, 0:cap, :] += y[:, c * LANES:(c + 1) * LANES]


PLANE_PAD = 8


def _expert_ffn(xs, gates, w_gate, w_up, w_down):
    bsz, n_exp, cap, d = xs.shape
    ff = w_gate.shape[2]
    tf = 256
    rows = cap + PLANE_PAD
    return pl.pallas_call(
        _ffn_kernel,
        grid=(n_exp, bsz, ff // tf),
        in_specs=[pl.BlockSpec((1, 1, cap, d), lambda e, b, f: (b, e, 0, 0)),
                  pl.BlockSpec((1, 1, cap, 1), lambda e, b, f: (b, e, 0, 0)),
                  pl.BlockSpec((1, d, tf), lambda e, b, f: (e, 0, f)),
                  pl.BlockSpec((1, d, tf), lambda e, b, f: (e, 0, f)),
                  pl.BlockSpec((1, tf, d), lambda e, b, f: (e, f, 0))],
        out_specs=pl.BlockSpec((1, 1, d // LANES, rows, LANES), lambda e, b, f: (b, e, 0, 0, 0)),
        out_shape=jax.ShapeDtypeStruct((bsz, n_exp, d // LANES, rows, LANES), F32),
        compiler_params=_cparams(("parallel", "parallel", "arbitrary")),
        name="expert_ffn",
    )(xs, gates, w_gate, w_up, w_down)


COMB_UNROLL = 4


def _combine_kernel(idx_ref, off_ref, y_ref, o_ref, *, cap, nb, n_exp, tok_part, n_part):
    b = pl.program_id(0)
    part = pl.program_id(1)
    ex = pl.program_id(3)

    @pl.when(ex == 0)
    def _():
        o_ref[...] = jnp.zeros_like(o_ref)

    be = b * n_exp + ex
    blocks_per_part = nb // n_part
    k_lo = off_ref[be * nb + part * blocks_per_part]
    k_hi = jnp.where(part == n_part - 1, cap,
                     off_ref[be * nb + jnp.minimum((part + 1) * blocks_per_part, nb - 1)])
    t0 = part * tok_part

    def add_rows(k, n):
        toks = [idx_ref[be * cap + k + i] - t0 for i in range(n)]
        rows = [o_ref[0, toks[i]] + y_ref[0, 0, :, k + i, :] for i in range(n)]
        for i in range(n):
            o_ref[0, toks[i]] = rows[i]

    n_full = (k_hi - k_lo) // COMB_UNROLL

    def body(i, _):
        add_rows(k_lo + i * COMB_UNROLL, COMB_UNROLL)
        return 0

    lax.fori_loop(0, n_full, body, 0)

    def tail(k, _):
        add_rows(k, 1)
        return 0

    lax.fori_loop(k_lo + n_full * COMB_UNROLL, k_hi, tail, 0)


def _combine(idx_flat, off_flat, y, n_tok, cap):
    bsz, n_exp, nslab, rows, _ = y.shape
    nb = n_tok // TOK_BLK
    n_part = 4
    sblk = nslab
    kern = functools.partial(_combine_kernel, cap=cap, nb=nb, n_exp=n_exp, tok_part=n_tok // n_part,
                             n_part=n_part)
    return pl.pallas_call(
        kern,
        grid_spec=pltpu.PrefetchScalarGridSpec(
            num_scalar_prefetch=2,
            grid=(bsz, n_part, nslab // sblk, n_exp),
            in_specs=[pl.BlockSpec((1, 1, sblk, rows, LANES), lambda b, p, c, e, idx, off: (b, e, c, 0, 0))],
            out_specs=pl.BlockSpec((1, n_tok // n_part, sblk, LANES),
                                   lambda b, p, c, e, idx, off: (b, p, c, 0))),
        out_shape=jax.ShapeDtypeStruct((bsz, n_tok, nslab, LANES), F32),
        compiler_params=_cparams(("parallel", "parallel", "parallel", "arbitrary")),
        name="ec_combine",
    )(idx_flat, off_flat, y)


def _final_kernel(x1_ref, moe_ref, g_ref, mod_ref, o_ref):
    tm = x1_ref.shape[0]
    nslab = moe_ref.shape[0] // tm
    m = jnp.concatenate([moe_ref[pl.ds(c, tm, stride=nslab), :] for c in range(nslab)], axis=-1)
    r = m * lax.rsqrt(jnp.mean(m * m, axis=-1, keepdims=True) + EPS) * g_ref[...]
    o_ref[...] = x1_ref[...] + mod_ref[0] * r


def _final(x1, moe, g3, g2, rows_per_batch):
    n, d = x1.shape
    tm = 512
    tiles_per_batch = rows_per_batch // tm
    return pl.pallas_call(
        _final_kernel,
        grid=(n // tm,),
        in_specs=[pl.BlockSpec((tm, d), lambda i: (i, 0)),
                  pl.BlockSpec((tm * (d // LANES), LANES), lambda i: (i, 0)),
                  pl.BlockSpec((1, d), lambda i: (0, 0)),
                  pl.BlockSpec((1, 1, d), lambda i: (i // tiles_per_batch, 0, 0))],
        out_specs=pl.BlockSpec((tm, d), lambda i: (i, 0)),
        out_shape=jax.ShapeDtypeStruct((n, d), F32),
        compiler_params=_cparams(("parallel",)),
        name="final_residual",
    )(x1, moe, g3, g2)


def _layer(x, c, ctx, c_ctx, ada_w, ada_b, norm_g, w_in, s5_a_re, s5_a_im, s5_log_dt, s5_b_re,
           s5_b_im, s5_c_re, s5_c_im, s5_d, s5_glu_w, s5_glu_b, ml_conv_w, ml_conv_b, ml_gate_b,
           ml_norm_g, w_out, router_w, exp_w_gate, exp_w_up, exp_w_down):
    bsz, seq, d = x.shape
    lctx = ctx.shape[1]
    ds5 = s5_d.shape[0]
    dml = ml_norm_g.shape[0]
    ngroups = ds5 // S5_GROUP
    rows = seq // GRID_W
    n = bsz * seq
    cap = EC_FACTOR * seq // N_EXPERTS
    assert cap >= TOK_BLK + 16 and seq % DISP_TOK == 0 and lctx % ML_T == 0 and seq % ML_T == 0

    cond = jnp.concatenate([c, c_ctx[None], jnp.zeros((8 - bsz - 1, d), F32)], axis=0)
    mod = _modulation(cond, ada_w, ada_b).reshape(8, N_MOD, d)
    sh1, sc1, g1, sh2, sc2, g2 = [mod[:, i] for i in range(N_MOD)]

    nmain = ds5 + 4 * dml
    w_s5 = w_in[:, :ds5].astype(BF16)
    w_ml = w_in[:, ds5:nmain].astype(BF16)
    w_gate = jnp.pad(w_in[:, nmain:], ((0, 0), (0, LANES - 4 * ML_HEADS))).astype(BF16)
    g0 = norm_g[0:1]
    sc, sh = sc1[:, None], sh1[:, None]
    of_sample = lambda b: b
    of_ctx = lambda b: bsz
    uc = _in_projection_s5(ctx, g0, sc, sh, of_ctx, w_s5)
    ltot = seq + lctx
    ml5, gates5, ux = _in_projection_ml(x, ctx, g0, sc, sh, bsz, w_ml, w_gate, w_s5)

    m, bp, cpt, avec = _s5_operators(s5_a_re, s5_a_im, s5_log_dt, s5_b_re, s5_b_im, s5_c_re, s5_c_im)
    ys = _s5_scan(ux, uc, m, bp, cpt, avec).reshape(n, ds5)

    ml = ml5.reshape(bsz, ltot, 4 * dml)
    gates = gates5.reshape(bsz, ltot, LANES)
    conv_w8 = jnp.pad(ml_conv_w, ((0, 8 - CONV_K), (0, 0)))
    qscale = jnp.concatenate([jnp.full((1, dml), (dml // ML_HEADS) ** -0.5, F32),
                              jnp.ones((1, dml), F32)], axis=1)
    qk = _conv_silu(ml, conv_w8, ml_conv_b[None], qscale, (0, seq), (seq, ltot))
    gbias = ml_gate_b.reshape(1, 4 * ML_HEADS)
    gates_t = gates[..., :4 * ML_HEADS].swapaxes(1, 2)
    hf, hb = _mlstm_scan(qk, ml, gates, gates_t, gbias, gbias.reshape(4 * ML_HEADS, 1), lctx // ML_T)
    mlo = _ml_post(hf.reshape(ml5.shape[:4] + (dml,)), hb.reshape(ml5.shape[:4] + (dml,)), ml5,
                   ml_norm_g[None], seq).reshape(n, dml)

    mod3 = jnp.stack([g1[:bsz], sc2[:bsz], sh2[:bsz]], axis=1)
    mod3 = jnp.concatenate([mod3, jnp.zeros((bsz, 5, d), F32)], axis=1)
    rw = jnp.pad(router_w, ((0, 0), (0, LANES - N_EXPERTS)))
    rw_hi = rw.astype(BF16)
    rw_lo = (rw - rw_hi.astype(F32)).astype(BF16)
    x1, hx2, aff = _mixer_out(ys, ux.reshape(n, ds5), mlo, x.reshape(n, d), seq, s5_d[None],
                              s5_glu_w.astype(BF16), s5_glu_b[None], w_out.astype(BF16), norm_g, mod3,
                              rw_hi, rw_lo)

    nb = seq // TOK_BLK
    aff16 = aff[:, :N_EXPERTS].reshape(bsz, seq, N_EXPERTS)
    aff_t = aff16.swapaxes(1, 2).reshape(bsz, N_EXPERTS, nb, LANES)
    off, idx, gates = _select(aff_t, cap)
    off_flat = off[..., 0].reshape(-1)
    idx_flat = idx[:, :, 0].reshape(-1)
    xs = _dispatch(idx_flat, hx2, bsz, N_EXPERTS, cap, d // LANES)
    ye = _expert_ffn(xs, gates[:, :, 0, :, None], exp_w_gate, exp_w_up, exp_w_down)
    moe = _combine(idx_flat, off_flat, ye, seq, cap)
    out = _final(x1, moe.reshape(n * (d // LANES), LANES), norm_g[3:4], g2[:bsz, None], seq)
    return out.reshape(bsz, seq, d)


def kernel(x, c, ctx, c_ctx, ada_w, ada_b, norm_g, w_in, s5_a_re, s5_a_im, s5_log_dt, s5_b_re, s5_b_im, s5_c_re, s5_c_im, s5_d, s5_glu_w, s5_glu_b, ml_conv_w, ml_conv_b, ml_gate_b, ml_norm_g, w_out, router_w, exp_w_gate, exp_w_up, exp_w_down):
    depth = ada_w.shape[0]
    assert depth == 1, "context outputs are only produced when another layer follows"
    li = 0
    return _layer(x, c, ctx, c_ctx, ada_w[li], ada_b[li], norm_g[li], w_in[li], s5_a_re[li],
                  s5_a_im[li], s5_log_dt[li], s5_b_re[li], s5_b_im[li], s5_c_re[li], s5_c_im[li],
                  s5_d[li], s5_glu_w[li], s5_glu_b[li], ml_conv_w[li], ml_conv_b[li], ml_gate_b[li],
                  ml_norm_g[li], w_out[li], router_w[li], exp_w_gate[li], exp_w_up[li],
                  exp_w_down[li])
```

```python
import functools
import math

import jax
import jax.numpy as jnp
from jax import lax
from jax.experimental import pallas as pl
from jax.experimental.pallas import tpu as pltpu

F32 = jnp.float32
BF16 = jnp.bfloat16
I32 = jnp.int32

EPS = 1e-6
GRID_W = 64
S5_GROUP = 16
S5_STATE = 64
S5_T = 16
ML_HEADS = 4
ML_T = 256
CONV_K = 5
N_EXPERTS = 16
EC_FACTOR = 2
N_MOD = 6
LANES = 128
TOK_BLK = 128
DISP_TOK = 512
DISP_W = DISP_TOK + 8
VMEM_LIMIT = 56 * 1024 * 1024


def _cparams(sem):
    return pltpu.CompilerParams(dimension_semantics=sem, vmem_limit_bytes=VMEM_LIMIT)


def _sigmoid(x):
    return 1.0 / (1.0 + jnp.exp(-x))


def _silu(x):
    return x * _sigmoid(x)


def _log_sigmoid(x):
    return jnp.minimum(x, 0.0) - jnp.log(1.0 + jnp.exp(-jnp.abs(x)))


def _split3(x):
    x1 = x.astype(BF16)
    r = x - x1.astype(F32)
    x2 = r.astype(BF16)
    x3 = (r - x2.astype(F32)).astype(BF16)
    return x1, x2, x3


def _dot(a, b):
    return jnp.dot(a, b, preferred_element_type=F32)


def _dot_nt(a, b):
    return lax.dot_general(a, b, (((1,), (1,)), ((), ())), preferred_element_type=F32)


def _dot_tn(a, b):
    return lax.dot_general(a, b, (((0,), (0,)), ((), ())), preferred_element_type=F32)


def _dot3_left(a_f32, b_bf16):
    a1, a2, a3 = _split3(a_f32)
    return _dot(a1, b_bf16) + _dot(a2, b_bf16) + _dot(a3, b_bf16)


def _dot3_right(a_bf16, b_f32):
    b1, b2, b3 = _split3(b_f32)
    return _dot(a_bf16, b1) + _dot(a_bf16, b2) + _dot(a_bf16, b3)


def _mod_kernel(c_ref, w_ref, b_ref, o_ref):
    c = c_ref[...]
    s = _silu(c).astype(BF16)
    o_ref[...] = _dot(s, w_ref[...].astype(BF16)) + b_ref[...]


def _modulation(cond8, ada_w, ada_b):
    d, n = ada_w.shape
    tn = 1024
    return pl.pallas_call(
        _mod_kernel,
        grid=(n // tn,),
        in_specs=[pl.BlockSpec((8, d), lambda j: (0, 0)),
                  pl.BlockSpec((d, tn), lambda j: (0, j)),
                  pl.BlockSpec((1, tn), lambda j: (0, j))],
        out_specs=pl.BlockSpec((8, tn), lambda j: (0, j)),
        out_shape=jax.ShapeDtypeStruct((8, n), F32),
        compiler_params=_cparams(("parallel",)),
        name="modulation",
    )(cond8, ada_w, ada_b.reshape(1, n))


def _norm_mod(x, g, sc, sh):
    ms = jnp.mean(x * x, axis=-1, keepdims=True)
    h = x * lax.rsqrt(ms + EPS) * g
    return (h * (1.0 + sc) + sh).astype(BF16)


def _inproj_s5_kernel(x_ref, g_ref, sc_ref, sh_ref, w_ref, o_ref, h_scr):
    @pl.when(pl.program_id(2) == 0)
    def _():
        h_scr[...] = _norm_mod(x_ref[0], g_ref[...], sc_ref[0], sh_ref[0])

    o_ref[0] = _dot(h_scr[...], w_ref[...])


def _in_projection_s5(x, g0, sc, sh, mod_of_batch, w_s5):
    bsz, length, d = x.shape
    ds5 = w_s5.shape[1]
    tm = min(1024, length)
    tn = 512
    return pl.pallas_call(
        _inproj_s5_kernel,
        grid=(bsz, length // tm, ds5 // tn),
        in_specs=[pl.BlockSpec((1, tm, d), lambda b, i, j: (b, i, 0)),
                  pl.BlockSpec((1, d), lambda b, i, j: (0, 0)),
                  pl.BlockSpec((1, 1, d), lambda b, i, j: (mod_of_batch(b), 0, 0)),
                  pl.BlockSpec((1, 1, d), lambda b, i, j: (mod_of_batch(b), 0, 0)),
                  pl.BlockSpec((d, tn), lambda b, i, j: (0, j))],
        out_specs=pl.BlockSpec((1, tm, tn), lambda b, i, j: (b, i, j)),
        out_shape=jax.ShapeDtypeStruct((bsz, length, ds5), F32),
        scratch_shapes=[pltpu.VMEM((tm, d), BF16)],
        compiler_params=_cparams(("parallel", "parallel", "arbitrary")),
        name="in_projection_s5",
    )(x, g0, sc, sh, w_s5)


def _inproj_ml_kernel(x_ref, c_ref, g_ref, sc_ref, sh_ref, csc_ref, csh_ref, w_ref, wg_ref, ws_ref,
                      o_ref, og_ref, u_ref, h_scr, x_scr, hr_scr, *, n_ml_blocks):
    rb = o_ref.shape[3]
    nlb = x_ref.shape[2] // LANES
    j = pl.program_id(2)

    @pl.when(j == 0)
    def _():
        hr_scr[...] = _norm_mod(x_ref[0], g_ref[...], sc_ref[0], sh_ref[0])
        for c in range(nlb):
            for r in range(rb):
                x_scr[c, r * ROW_PITCH:r * ROW_PITCH + GRID_W, :] = (
                    x_ref[0, r * GRID_W:(r + 1) * GRID_W, c * LANES:(c + 1) * LANES])
        for w in range(GRID_W):
            xw = jnp.concatenate([x_scr[c, pl.ds(w, rb, stride=ROW_PITCH), :] for c in range(nlb)],
                                 axis=-1)
            h_scr[w * rb:(w + 1) * rb, :] = _norm_mod(xw, g_ref[...], sc_ref[0], sh_ref[0])
        for s in range(c_ref.shape[1]):
            h_scr[(GRID_W + s) * rb:(GRID_W + s + 1) * rb, :] = _norm_mod(
                c_ref[0, s, 0], g_ref[...], csc_ref[0], csh_ref[0])
        og_ref[0] = _dot(h_scr[...], wg_ref[...]).reshape(og_ref.shape[1:])

    @pl.when(j < n_ml_blocks)
    def _():
        o_ref[0] = _dot(h_scr[...], w_ref[...]).reshape(o_ref.shape[1:]).astype(o_ref.dtype)

    @pl.when(j >= n_ml_blocks)
    def _():
        u_ref[0] = _dot(hr_scr[...], ws_ref[...])


ML_RB = 16
ROW_PITCH = GRID_W + 8


def _in_projection_ml(x, ctx, g0, sc, sh, ctx_mod, w_ml, w_gate, w_s5):
    bsz, length, d = x.shape
    lctx = ctx.shape[1]
    nml = w_ml.shape[1]
    ds5 = w_s5.shape[1]
    tn = 512
    tns = 256
    nmb, nsb = nml // tn, ds5 // tns
    rows = length // GRID_W
    rblocks = rows // ML_RB
    nslot_c = lctx // rows
    assert nslot_c * rows == lctx and rblocks * ML_RB == rows
    n_slots = GRID_W + nslot_c
    tm = ML_RB * GRID_W
    ctx5 = ctx.reshape(bsz, nslot_c, rblocks, ML_RB, d)
    kern = functools.partial(_inproj_ml_kernel, n_ml_blocks=nmb)
    return pl.pallas_call(
        kern,
        grid=(bsz, rblocks, nmb + nsb),
        in_specs=[pl.BlockSpec((1, tm, d), lambda b, i, j: (b, i, 0)),
                  pl.BlockSpec((1, nslot_c, 1, ML_RB, d), lambda b, i, j: (b, 0, i, 0, 0)),
                  pl.BlockSpec((1, d), lambda b, i, j: (0, 0)),
                  pl.BlockSpec((1, 1, d), lambda b, i, j: (b, 0, 0)),
                  pl.BlockSpec((1, 1, d), lambda b, i, j: (b, 0, 0)),
                  pl.BlockSpec((1, 1, d), lambda b, i, j: (ctx_mod, 0, 0)),
                  pl.BlockSpec((1, 1, d), lambda b, i, j: (ctx_mod, 0, 0)),
                  pl.BlockSpec((d, tn), lambda b, i, j: (0, jnp.minimum(j, nmb - 1))),
                  pl.BlockSpec((d, LANES), lambda b, i, j: (0, 0)),
                  pl.BlockSpec((d, tns), lambda b, i, j: (0, jnp.maximum(j - nmb, 0)))],
        out_specs=[pl.BlockSpec((1, n_slots, 1, ML_RB, tn),
                                lambda b, i, j: (b, 0, i, 0, jnp.minimum(j, nmb - 1))),
                   pl.BlockSpec((1, n_slots, 1, ML_RB, LANES), lambda b, i, j: (b, 0, i, 0, 0)),
                   pl.BlockSpec((1, tm, tns), lambda b, i, j: (b, i, jnp.maximum(j - nmb, 0)))],
        out_shape=[jax.ShapeDtypeStruct((bsz, n_slots, rblocks, ML_RB, nml), BF16),
                   jax.ShapeDtypeStruct((bsz, n_slots, rblocks, ML_RB, LANES), F32),
                   jax.ShapeDtypeStruct((bsz, length, ds5), F32)],
        scratch_shapes=[pltpu.VMEM((n_slots * ML_RB, d), BF16),
                        pltpu.VMEM((d // LANES, ML_RB * ROW_PITCH, LANES), F32),
                        pltpu.VMEM((tm, d), BF16)],
        compiler_params=_cparams(("parallel", "parallel", "arbitrary")),
        name="in_projection_ml",
    )(x, ctx5, g0, sc, sh, sc, sh, w_ml, w_gate, w_s5)


def _dot_nt6(a, b):
    a1, a2, a3 = _split3(a)
    b1, b2, b3 = _split3(b)
    return (_dot_nt(a1, b1) + (_dot_nt(a1, b2) + _dot_nt(a2, b1))
            + (_dot_nt(a1, b3) + _dot_nt(a3, b1) + _dot_nt(a2, b2)))


S5_LANES = 4 * S5_STATE


def _s5ops_kernel(prm_ref, bt_ref, c_ref, rexp_ref, rc_ref, place_ref, m_ref, bp_ref, cpt_ref, av_ref):
    t = S5_T
    prm = prm_ref[0]
    lr, li = jnp.minimum(prm[0:1], -1e-4), prm[1:2]
    dt = jnp.exp(prm[2:3])
    zr, zi = lr * dt, li * dt
    tau = lax.broadcasted_iota(I32, (32, S5_LANES), 0).astype(F32)
    mag, ang = jnp.exp(tau * zr), tau * zi
    pwr, pwi = mag * jnp.cos(ang), mag * jnp.sin(ang)
    xr, xi = pwr[1:2] - 1.0, pwi[1:2]
    den = lr * lr + li * li
    qr, qi = (xr * lr + xi * li) / den, (xi * lr - xr * li) / den
    btr, bti = bt_ref[0, 0], bt_ref[0, 1]
    bbr, bbi = qr * btr - qi * bti, qr * bti + qi * btr
    cr, ci = c_ref[0, 0], c_ref[0, 1]

    lane = lax.broadcasted_iota(I32, (t * S5_GROUP, S5_LANES), 1)
    fwd = ((lane // S5_STATE) % 2) == 0

    def expand(r, x):
        return _dot3_right(r, x)

    e_t = [expand(rexp_ref[0], z) for z in (pwr, pwi)]
    e_r = [expand(rexp_ref[1], z) for z in (pwr, pwi)]
    e_p = [expand(rexp_ref[2], z) for z in (pwr, pwi)]
    e_q = [expand(rexp_ref[3], z) for z in (pwr, pwi)]
    pk = [jnp.where(fwd, e_t[i], e_r[i]) for i in range(2)]
    pb = [jnp.where(fwd, e_r[i], e_t[i]) for i in range(2)]
    pc = [jnp.where(fwd, e_p[i], e_q[i]) for i in range(2)]
    cxr, cxi = expand(rc_ref[...], cr), expand(rc_ref[...], ci)
    bxr, bxi = expand(rc_ref[...], bbr), expand(rc_ref[...], bbi)

    wkr, wki = pk[0] * cxr - pk[1] * cxi, pk[0] * cxi + pk[1] * cxr
    row_blk = lax.broadcasted_iota(I32, (4 * S5_GROUP, S5_LANES), 0) // S5_GROUP
    lane_blk = lax.broadcasted_iota(I32, (4 * S5_GROUP, S5_LANES), 1) // S5_STATE
    own = row_blk == lane_blk
    bmr = jnp.where(own, jnp.concatenate([bbr] * 4, axis=0), 0.0)
    bmi = jnp.where(own, jnp.concatenate([bbi] * 4, axis=0), 0.0)
    kt = _dot_nt6(bmr, wkr) - _dot_nt6(bmi, wki)

    ln = lax.broadcasted_iota(I32, (S5_GROUP, t * S5_GROUP), 1)
    for gi in range(2):
        ktf = kt[(2 * gi) * S5_GROUP:(2 * gi + 1) * S5_GROUP]
        ktb = kt[(2 * gi + 1) * S5_GROUP:(2 * gi + 2) * S5_GROUP]
        blocks = []
        for s in range(t):
            f = ktf if s == 0 else pltpu.roll(ktf, S5_GROUP * s, axis=1)
            sh = (t * S5_GROUP - S5_GROUP * (t - 1 - s)) % (t * S5_GROUP)
            bk = ktb if sh == 0 else pltpu.roll(ktb, sh, axis=1)
            blocks.append(jnp.where(ln >= S5_GROUP * s, f, 0.0)
                          + jnp.where(ln < S5_GROUP * (s + 1), bk, 0.0))
        m_ref[0, gi] = jnp.concatenate(blocks, axis=0).astype(BF16)

    inj = jnp.concatenate([pb[0] * bxr - pb[1] * bxi, pb[0] * bxi + pb[1] * bxr], axis=-1).astype(BF16)
    out = jnp.concatenate([pc[0] * cxr - pc[1] * cxi, -(pc[0] * cxi + pc[1] * cxr)], axis=-1).astype(BF16)
    for gi in range(2):
        bp_ref[0, gi * t * S5_GROUP:(gi + 1) * t * S5_GROUP, :] = _dot(inj, place_ref[gi]).astype(BF16)
        cpt_ref[0, gi * t * S5_GROUP:(gi + 1) * t * S5_GROUP, :] = _dot(out, place_ref[gi]).astype(BF16)
    a_t = jnp.concatenate([pwr[t:t + 8], pwi[t:t + 8]], axis=-1)
    av_ref[0] = _dot3_left(a_t, place_ref[0]) + _dot3_left(a_t, place_ref[1])


def _s5_operators(a_re, a_im, log_dt, b_re, b_im, c_re, c_im):
    t = S5_T
    g, p = a_re.shape[1], a_re.shape[2]
    npair = g // 2

    def lanes(z):
        return z.reshape(2, npair, 2, p).transpose(1, 2, 0, 3).reshape(npair, 4 * p)

    prm = jnp.stack([lanes(a_re.astype(F32)), lanes(a_im.astype(F32)),
                     lanes(jnp.broadcast_to(log_dt.astype(F32)[..., None], a_re.shape))], axis=1)
    prm = jnp.concatenate([prm, jnp.zeros((npair, 5, 4 * p), F32)], axis=1)

    def tile_d(z):
        z = z.reshape(npair, 2, S5_GROUP, p).transpose(0, 2, 1, 3)
        return jnp.broadcast_to(z[:, :, :, None, :], (npair, S5_GROUP, 2, 2, p)).reshape(npair, S5_GROUP, 4 * p)

    bt = jnp.stack([tile_d(b_re.astype(F32).swapaxes(1, 2)), tile_d(b_im.astype(F32).swapaxes(1, 2))], axis=1)
    ct = jnp.stack([tile_d(c_re.astype(F32)), tile_d(c_im.astype(F32))], axis=1)

    row = jnp.arange(t * S5_GROUP)
    tau, ch = row // S5_GROUP, row % S5_GROUP
    e = jnp.arange(32)
    rexp = jnp.stack([(expo[:, None] == e[None, :]) for expo in (tau, t - 1 - tau, tau + 1, t - tau)]
                     ).astype(BF16)
    rc = (ch[:, None] == jnp.arange(S5_GROUP)[None, :]).astype(BF16)
    src = jnp.arange(2 * 4 * p)
    reim, gi_s, d_s, p_s = src // (4 * p), (src // (2 * p)) % 2, (src // p) % 2, src % p
    dst = d_s * (4 * p) + reim * (2 * p) + gi_s * p + p_s
    col = jnp.arange(8 * p)
    place = jnp.stack([((dst[:, None] == col[None, :]) & (gi_s[:, None] == gi)) for gi in range(2)]
                      ).astype(BF16)
    blk3 = lambda shape: pl.BlockSpec((1,) + shape, lambda i: (i,) + (0,) * len(shape))
    full = lambda arr: pl.BlockSpec(arr.shape, lambda i: (0,) * arr.ndim)
    m, bp, cpt, av = pl.pallas_call(
        _s5ops_kernel,
        grid=(npair,),
        in_specs=[blk3((8, 4 * p)), blk3((2, S5_GROUP, 4 * p)), blk3((2, S5_GROUP, 4 * p)),
                  full(rexp), full(rc), full(place)],
        out_specs=[blk3((2, t * S5_GROUP, t * S5_GROUP)), blk3((2 * t * S5_GROUP, 8 * p)),
                   blk3((2 * t * S5_GROUP, 8 * p)), blk3((8, 8 * p))],
        out_shape=[jax.ShapeDtypeStruct((npair, 2, t * S5_GROUP, t * S5_GROUP), BF16),
                   jax.ShapeDtypeStruct((npair, 2 * t * S5_GROUP, 8 * p), BF16),
                   jax.ShapeDtypeStruct((npair, 2 * t * S5_GROUP, 8 * p), BF16),
                   jax.ShapeDtypeStruct((npair, 8, 8 * p), F32)],
        compiler_params=_cparams(("parallel",)),
        name="s5_operators",
    )(prm, bt, ct, rexp, rc, place)
    avec = av[:, 0].reshape(npair, 4, 2 * p)
    return m, bp, cpt, jnp.concatenate([avec, jnp.zeros_like(avec)], axis=1)


def _s5_permutation():
    i = jnp.arange(8 * 8 * S5_GROUP)
    t8, j, c = i // (8 * S5_GROUP), (i // S5_GROUP) % 8, i % S5_GROUP
    dst = j * (8 * S5_GROUP) + t8 * S5_GROUP + c
    e1 = (dst[:, None] == i[None, :]).astype(BF16)
    return e1, e1.T


S5_PAIRS = 4


def _s5_kernel(ux_ref, uc_ref, e1_ref, e1t_ref, m_ref, bp_ref, cp_ref, av_ref, y_ref, s_scr, hp_scr,
               tab_scr, *, cc, xc):
    nc = cc + xc

    def chunk_rows(ref, nchunks):
        cols = [ref[0, pl.ds(t, nchunks, stride=S5_T), :].astype(BF16) for t in range(S5_T)]
        return [_dot(jnp.concatenate(cols[8 * h:8 * h + 8], axis=-1), e1_ref[...]).astype(BF16)
                for h in range(2)]

    def pair_lanes(halves, p):
        return jnp.concatenate([halves[h][:, j * LANES:(j + 1) * LANES]
                                for j in (2 * p, 2 * p + 1) for h in range(2)], axis=-1)

    pc = chunk_rows(uc_ref, cc)
    px = chunk_rows(ux_ref, xc)
    for p in range(S5_PAIRS):
        s_scr[0:cc, p * 512:(p + 1) * 512] = _dot(pair_lanes(pc, p), bp_ref[p])
        s_scr[cc:nc, p * 512:(p + 1) * 512] = _dot(pair_lanes(px, p), bp_ref[p])

    av = av_ref[...]
    sub = lax.broadcasted_iota(I32, (8, 128), 0)

    def cmul(ar, ai, br, bi):
        return ar * br - ai * bi, ar * bi + ai * br

    for p in range(S5_PAIRS):
        for d in range(2):
            a1 = (av[p, 2 * d:2 * d + 1], av[p, 2 * d + 1:2 * d + 2])
            a2 = cmul(*a1, *a1)
            a4 = cmul(*a2, *a2)
            pw_re, pw_im = jnp.zeros((8, 128), F32), jnp.zeros((8, 128), F32)
            cur = a1
            for k in range(8):
                r = k if d == 0 else 7 - k
                pw_re = jnp.where(sub == r, cur[0], pw_re)
                pw_im = jnp.where(sub == r, cur[1], pw_im)
                cur = cmul(*cur, *a1)
            for q, z in enumerate((a1, a2, a4)):
                tab_scr[p, d, 2 * q] = jnp.broadcast_to(z[0], (8, 128))
                tab_scr[p, d, 2 * q + 1] = jnp.broadcast_to(z[1], (8, 128))
            tab_scr[p, d, 6] = pw_re
            tab_scr[p, d, 7] = pw_im

    def scan8(s_re, s_im, h_re, h_im, p, d):
        back = d == 1
        x_re, x_im = s_re, s_im
        for q, k in enumerate((1, 2, 4)):
            keep = (sub < 8 - k) if back else (sub >= k)
            sh = (8 - k) if back else k
            y_re = jnp.where(keep, pltpu.roll(x_re, sh, axis=0), 0.0)
            y_im = jnp.where(keep, pltpu.roll(x_im, sh, axis=0), 0.0)
            m_re, m_im = cmul(tab_scr[p, d, 2 * q], tab_scr[p, d, 2 * q + 1], y_re, y_im)
            x_re, x_im = x_re + m_re, x_im + m_im
        c_re, c_im = cmul(tab_scr[p, d, 6], tab_scr[p, d, 7], h_re, h_im)
        x_re, x_im = x_re + c_re, x_im + c_im
        edge = 7 if back else 0
        sh = 7 if back else 1
        e_re = jnp.where(sub == edge, h_re, pltpu.roll(x_re, sh, axis=0))
        e_im = jnp.where(sub == edge, h_im, pltpu.roll(x_im, sh, axis=0))
        last = 0 if back else 7
        return e_re, e_im, x_re[last:last + 1], x_im[last:last + 1]

    def step(row_f, row_b, carry):
        rf = pl.multiple_of(row_f, 8)
        rb = pl.multiple_of(row_b, 8)
        new = []
        for p in range(S5_PAIRS):
            hr, hi, gr, gi = carry[4 * p:4 * p + 4]
            c0 = p * 512
            p_re, p_im, hr, hi = scan8(s_scr[pl.ds(rf, 8), c0:c0 + 128], s_scr[pl.ds(rf, 8), c0 + 128:c0 + 256],
                                       hr, hi, p, 0)
            hp_scr[pl.ds(rf, 8), c0:c0 + 128] = p_re
            hp_scr[pl.ds(rf, 8), c0 + 128:c0 + 256] = p_im
            p_re, p_im, gr, gi = scan8(s_scr[pl.ds(rb, 8), c0 + 256:c0 + 384], s_scr[pl.ds(rb, 8), c0 + 384:c0 + 512],
                                       gr, gi, p, 1)
            hp_scr[pl.ds(rb, 8), c0 + 256:c0 + 384] = p_re
            hp_scr[pl.ds(rb, 8), c0 + 384:c0 + 512] = p_im
            new += [hr, hi, gr, gi]
        return tuple(new)

    z = jnp.zeros((1, 128), F32)
    carry = lax.fori_loop(0, cc // 8, lambda i, c: step(8 * i, cc - 8 - 8 * i, c), (z,) * (4 * S5_PAIRS))
    lax.fori_loop(0, xc // 8, lambda i, c: step(cc + 8 * i, nc - 8 - 8 * i, c), carry)

    ys = []
    for p in range(S5_PAIRS):
        ux = pair_lanes(px, p)
        y = _dot_nt(hp_scr[cc:nc, p * 512:(p + 1) * 512].astype(BF16), cp_ref[p])
        ys.append(y + jnp.concatenate([_dot(ux[:, 0:256], m_ref[p, 0]), _dot(ux[:, 256:512], m_ref[p, 1])],
                                      axis=-1))
    for h in range(2):
        yh = jnp.concatenate([ys[j // 2][:, (j % 2) * 256 + h * LANES:(j % 2) * 256 + (h + 1) * LANES]
                              for j in range(8)], axis=-1).astype(BF16)
        r = _dot(yh, e1t_ref[...])
        for t8 in range(8):
            y_ref[0, pl.ds(8 * h + t8, xc, stride=S5_T), :] = r[:, t8 * LANES:(t8 + 1) * LANES]


def _s5_scan(ux, uc, m, bp, cp, avec):
    bsz, seq, ds5 = ux.shape
    lctx = uc.shape[1]
    cc, xc = lctx // S5_T, seq // S5_T
    assert cc % 8 == 0 and xc % 8 == 0
    e1, e1t = _s5_permutation()
    kern = functools.partial(_s5_kernel, cc=cc, xc=xc)
    nblk = ds5 // LANES
    const = lambda j, b: (0, 0)
    return pl.pallas_call(
        kern,
        grid=(nblk, bsz),
        in_specs=[pl.BlockSpec((1, seq, LANES), lambda j, b: (b, 0, j)),
                  pl.BlockSpec((1, lctx, LANES), lambda j, b: (b, 0, j)),
                  pl.BlockSpec(e1.shape, const),
                  pl.BlockSpec(e1.shape, const),
                  pl.BlockSpec((S5_PAIRS, 2, 256, 256), lambda j, b: (j, 0, 0, 0)),
                  pl.BlockSpec((S5_PAIRS, 512, 512), lambda j, b: (j, 0, 0)),
                  pl.BlockSpec((S5_PAIRS, 512, 512), lambda j, b: (j, 0, 0)),
                  pl.BlockSpec((S5_PAIRS, 8, 128), lambda j, b: (j, 0, 0))],
        out_specs=pl.BlockSpec((1, seq, LANES), lambda j, b: (b, 0, j)),
        out_shape=jax.ShapeDtypeStruct((bsz, seq, ds5), F32),
        scratch_shapes=[pltpu.VMEM((cc + xc, S5_PAIRS * 512), F32),
                        pltpu.VMEM((cc + xc, S5_PAIRS * 512), F32),
                        pltpu.VMEM((S5_PAIRS, 2, 8, 8, 128), F32)],
        compiler_params=_cparams(("parallel", "parallel")),
        name="s5_scan",
    )(ux, uc, e1, e1t, m, bp, cp, avec)


def _conv_kernel(x_ref, p_ref, n_ref, w_ref, b_ref, sc_ref, o_ref, scr, *, rows, seq_tiles):
    i = pl.program_id(1)
    first = functools.reduce(jnp.logical_or, [i == s for s in seq_tiles[0]])
    last = functools.reduce(jnp.logical_or, [i == s for s in seq_tiles[1]])
    pm = jnp.where(first, 0.0, 1.0)
    nm = jnp.where(last, 0.0, 1.0)
    scr[0:8, :] = p_ref[0, 8:16, :].astype(F32) * pm
    scr[8:8 + rows, :] = x_ref[0].astype(F32)
    scr[8 + rows:16 + rows, :] = n_ref[0, 0:8, :].astype(F32) * nm
    w = w_ref[...]
    acc = scr[6:6 + rows, :] * w[0:1]
    for j in range(1, CONV_K):
        acc = acc + scr[6 + j:6 + j + rows, :] * w[j:j + 1]
    z = acc + b_ref[...]
    o_ref[0] = (_silu(z) * sc_ref[...]).astype(o_ref.dtype)


def _conv_silu(ml, conv_w8, conv_b, scale, seq_starts, seq_ends):
    bsz, ltot, _ = ml.shape
    rows = 256
    ct = 512
    nch = conv_b.shape[-1]
    nt = ltot // rows
    hb = rows // 16
    nhb = ltot // 16
    seq_tiles = (tuple(s // rows for s in seq_starts), tuple(e // rows - 1 for e in seq_ends))
    kern = functools.partial(_conv_kernel, rows=rows, seq_tiles=seq_tiles)
    return pl.pallas_call(
        kern,
        grid=(bsz, nt, nch // ct),
        in_specs=[pl.BlockSpec((1, rows, ct), lambda b, i, c: (b, i, c)),
                  pl.BlockSpec((1, 16, ct), lambda b, i, c: (b, jnp.maximum(i * hb - 1, 0), c)),
                  pl.BlockSpec((1, 16, ct), lambda b, i, c: (b, jnp.minimum((i + 1) * hb, nhb - 1), c)),
                  pl.BlockSpec((8, ct), lambda b, i, c: (0, c)),
                  pl.BlockSpec((1, ct), lambda b, i, c: (0, c)),
                  pl.BlockSpec((1, ct), lambda b, i, c: (0, c))],
        out_specs=pl.BlockSpec((1, rows, ct), lambda b, i, c: (b, i, c)),
        out_shape=jax.ShapeDtypeStruct((bsz, ltot, nch), BF16),
        scratch_shapes=[pltpu.VMEM((rows + 16, ct), F32)],
        compiler_params=_cparams(("parallel", "parallel", "parallel")),
        name="conv_silu",
    )(ml, ml, ml, conv_w8, conv_b, scale)


def _mlstm_chain_step(q, k, v, i_col, f_col, i_row, b_col, b_row, mask, c_ref, n_ref, m_ref, idx):
    m_st = m_ref[idx]
    d = jnp.where(mask, b_col - b_row + i_row, -jnp.inf)
    inter = b_col + m_st
    m_t = jnp.maximum(inter, jnp.max(d, axis=-1, keepdims=True))
    w = jnp.exp(d - m_t)
    s_inter = jnp.exp(inter - m_t)
    s = _dot_nt(q, k) * w
    c_st = c_ref[idx]
    n_st = n_ref[idx]
    num = _dot(s.astype(BF16), v) + s_inter * _dot_nt(q, c_st.astype(BF16))
    qf = q.astype(F32)
    den = jnp.sum(s, axis=-1, keepdims=True) + s_inter * jnp.sum(qf * n_st, axis=-1, keepdims=True)
    h = num / jnp.maximum(jnp.abs(den), jnp.exp(-m_t))
    total = jnp.sum(f_col, axis=0, keepdims=True)
    d_end = total - b_col + i_col
    m_new = jnp.maximum(total + m_st, jnp.max(d_end, axis=0, keepdims=True))
    w_end = jnp.exp(d_end - m_new)
    dec = jnp.exp(total + m_st - m_new)
    vw = (v.astype(F32) * w_end).astype(BF16)
    c_ref[idx] = dec * c_st + _dot_tn(vw, k)
    n_ref[idx] = dec * n_st + jnp.sum(k.astype(F32) * w_end, axis=0, keepdims=True)
    m_ref[idx] = m_new
    return h


def _mlstm_kernel(qf_ref, kf_ref, vf_ref, gf_ref, gtf_ref, qb_ref, kb_ref, vb_ref, gb_ref, gtb_ref,
                  gbias_ref, gbias_t_ref, hf_ref, hb_ref, c_scr, n_scr, m_scr, *, dh):
    @pl.when(pl.program_id(1) == 0)
    def _():
        c_scr[...] = jnp.zeros_like(c_scr)
        n_scr[...] = jnp.zeros_like(n_scr)
        m_scr[...] = jnp.zeros_like(m_scr)

    bsz, t = qf_ref.shape[0], qf_ref.shape[1]
    rid = lax.broadcasted_iota(I32, (t, t), 0)
    cid = lax.broadcasted_iota(I32, (t, t), 1)
    lower = jnp.where(rid >= cid, 1.0, 0.0).astype(BF16)
    upper = jnp.where(cid >= rid, 1.0, 0.0).astype(BF16)

    for bb in range(bsz):
        for d, (q_ref, k_ref, v_ref, g_ref, gt_ref, h_ref) in enumerate(
                [(qf_ref, kf_ref, vf_ref, gf_ref, gtf_ref, hf_ref),
                 (qb_ref, kb_ref, vb_ref, gb_ref, gtb_ref, hb_ref)]):
            backward = d == 1
            g = g_ref[bb, :, 0:16] + gbias_ref[...]
            gt = gt_ref[bb] + gbias_t_ref[...]
            lf = _log_sigmoid(g)
            lft = _log_sigmoid(gt)
            cum_col = _dot3_right(upper if backward else lower, lf)
            cum_row = _dot3_left(lft, lower if backward else upper)
            for hd in range(ML_HEADS):
                ci = 2 * ML_HEADS * d + hd
                cf = ci + ML_HEADS
                sl = slice(hd * dh, (hd + 1) * dh)
                h = _mlstm_chain_step(
                    q_ref[bb, :, sl], k_ref[bb, :, sl], v_ref[bb, :, sl],
                    g[:, ci:ci + 1], lf[:, cf:cf + 1], gt[ci:ci + 1, :],
                    cum_col[:, cf:cf + 1], cum_row[cf:cf + 1, :],
                    (cid >= rid) if backward else (rid >= cid),
                    c_scr, n_scr, m_scr, (bb * 2 + d) * ML_HEADS + hd)
                h_ref[bb, :, sl] = h.astype(h_ref.dtype)


def _mlstm_scan(qk, ml, gates, gates_t, gbias, gbias_t, n_ctx_chunks):
    bsz, ltot, dml2 = qk.shape
    dml = dml2 // 2
    dh = dml // ML_HEADS
    t = ML_T
    nchunk = ltot // t
    cc = n_ctx_chunks
    nx = nchunk - cc

    def fwd(s):
        return jnp.where(s < cc, nx + s, s - cc)

    def bwd(s):
        return jnp.where(s < cc, nx + cc - 1 - s, nx - 1 - (s - cc))

    bps = 1

    def specs(cmap):
        return [pl.BlockSpec((bps, t, dml), lambda b, s: (b, cmap(s), 0)),
                pl.BlockSpec((bps, t, dml), lambda b, s: (b, cmap(s), 1)),
                pl.BlockSpec((bps, t, dml), lambda b, s: (b, cmap(s), 2)),
                pl.BlockSpec((bps, t, LANES), lambda b, s: (b, cmap(s), 0)),
                pl.BlockSpec((bps, 16, t), lambda b, s: (b, 0, cmap(s)))]

    kern = functools.partial(_mlstm_kernel, dh=dh)
    nchain = bps * 2 * ML_HEADS
    return pl.pallas_call(
        kern,
        grid=(bsz // bps, nchunk),
        in_specs=specs(fwd) + specs(bwd) + [pl.BlockSpec((1, 16), lambda b, s: (0, 0)),
                                            pl.BlockSpec((16, 1), lambda b, s: (0, 0))],
        out_specs=[pl.BlockSpec((bps, t, dml), lambda b, s: (b, fwd(s), 0)),
                   pl.BlockSpec((bps, t, dml), lambda b, s: (b, bwd(s), 0))],
        out_shape=[jax.ShapeDtypeStruct((bsz, ltot, dml), BF16)] * 2,
        scratch_shapes=[pltpu.VMEM((nchain, dh, dh), F32),
                        pltpu.VMEM((nchain, 1, dh), F32),
                        pltpu.VMEM((nchain, 1, 1), F32)],
        compiler_params=_cparams(("parallel", "arbitrary")),
        name="mlstm_scan",
    )(qk, qk, ml, gates, gates_t, qk, qk, ml, gates, gates_t, gbias, gbias_t)


COL_PITCH = ML_RB + 8


def _mlpost_kernel(hf_ref, hb_ref, o_ref, mg_ref, out_ref, scr, *, dh):
    nw, _, rb, dml = hf_ref.shape[1:]
    h = (hf_ref[0].astype(F32) + hb_ref[0].astype(F32)).reshape(nw * rb, dml)
    parts = []
    for hd in range(ML_HEADS):
        hh = h[:, hd * dh:(hd + 1) * dh]
        parts.append(hh * lax.rsqrt(jnp.mean(hh * hh, axis=-1, keepdims=True) + EPS))
    hn = jnp.concatenate(parts, axis=-1) * mg_ref[...]
    res = hn * _sigmoid(o_ref[0].astype(F32).reshape(nw * rb, dml))
    nlb = dml // LANES
    for c in range(nlb):
        for w in range(nw):
            scr[c, w * COL_PITCH:w * COL_PITCH + rb, :] = res[w * rb:(w + 1) * rb, c * LANES:(c + 1) * LANES]
    for r in range(rb):
        row = jnp.concatenate([scr[c, pl.ds(r, nw, stride=COL_PITCH), :] for c in range(nlb)], axis=-1)
        out_ref[0, r * nw:(r + 1) * nw, :] = row.astype(out_ref.dtype)


def _ml_post(hf5, hb5, ml5, ml_g, seq):
    bsz, _, rblocks, rb, dml = hf5.shape
    kern = functools.partial(_mlpost_kernel, dh=dml // ML_HEADS)
    blk = (1, GRID_W, 1, rb, dml)
    return pl.pallas_call(
        kern,
        grid=(bsz, rblocks),
        in_specs=[pl.BlockSpec(blk, lambda b, i: (b, 0, i, 0, 0)),
                  pl.BlockSpec(blk, lambda b, i: (b, 0, i, 0, 0)),
                  pl.BlockSpec(blk, lambda b, i: (b, 0, i, 0, 3)),
                  pl.BlockSpec((1, dml), lambda b, i: (0, 0))],
        out_specs=pl.BlockSpec((1, rb * GRID_W, dml), lambda b, i: (b, i, 0)),
        out_shape=jax.ShapeDtypeStruct((bsz, seq, dml), BF16),
        scratch_shapes=[pltpu.VMEM((dml // LANES, GRID_W * COL_PITCH, LANES), F32)],
        compiler_params=_cparams(("parallel", "parallel")),
        name="mlstm_post",
    )(hf5, hb5, ml5, ml_g)


def _gelu_tanh(y):
    return 0.5 * y * (1.0 + jnp.tanh(math.sqrt(2.0 / math.pi) * (y + 0.044715 * (y * y * y))))


def _mixout_kernel(ys_ref, u_ref, ml_ref, x_ref, dsk_ref, gw_ref, gb_ref, wo_ref, ng_ref, mod_ref,
                   rw_ref, x1_ref, hx_ref, aff_ref, *, n_exp):
    y = ys_ref[...] + dsk_ref[...] * u_ref[...]
    ge = _gelu_tanh(y)
    z = _dot(ge.astype(BF16), gw_ref[...]) + gb_ref[...]
    s5 = (ge * _sigmoid(z)).astype(BF16)
    yx = _dot(jnp.concatenate([s5, ml_ref[...]], axis=-1), wo_ref[...])
    ng = ng_ref[...]
    mod = mod_ref[0]
    r1 = yx * lax.rsqrt(jnp.mean(yx * yx, axis=-1, keepdims=True) + EPS) * ng[1:2]
    x1 = x_ref[...] + mod[0:1] * r1
    x1_ref[...] = x1
    hx = x1 * lax.rsqrt(jnp.mean(x1 * x1, axis=-1, keepdims=True) + EPS) * ng[2:3]
    hx = hx * (1.0 + mod[1:2]) + mod[2:3]
    nslab = hx.shape[1] // LANES
    for cb in range(nslab):
        hx_ref[pl.ds(cb, hx.shape[0], stride=nslab), :] = hx[:, cb * LANES:(cb + 1) * LANES]
    h1, h2, _ = _split3(hx)
    rw2 = rw_ref[...]
    l1 = _dot(h1, rw2)
    logits = (l1[:, 0:LANES] + l1[:, LANES:2 * LANES]) + _dot(h2, rw2[:, 0:LANES])
    lane = lax.broadcasted_iota(I32, logits.shape, 1)
    logits = jnp.where(lane < n_exp, logits, -jnp.inf)
    e = jnp.exp(logits - jnp.max(logits, axis=-1, keepdims=True))
    aff_ref[...] = e / jnp.sum(e, axis=-1, keepdims=True)


def _mixer_out(ys, u, mlo, x2d, rows_per_batch, dsk, glu_w, glu_b, w_out, norm_g, mod3, rw2):
    n, d = x2d.shape
    ds5 = ys.shape[1]
    dml = mlo.shape[1]
    tm = 256
    tiles_per_batch = rows_per_batch // tm
    kern = functools.partial(_mixout_kernel, n_exp=N_EXPERTS)
    row = lambda i: (i, 0)
    const = lambda i: (0, 0)
    return pl.pallas_call(
        kern,
        grid=(n // tm,),
        in_specs=[pl.BlockSpec((tm, ds5), row),
                  pl.BlockSpec((tm, ds5), row),
                  pl.BlockSpec((tm, dml), row),
                  pl.BlockSpec((tm, d), row),
                  pl.BlockSpec((1, ds5), const),
                  pl.BlockSpec((ds5, ds5), const),
                  pl.BlockSpec((1, ds5), const),
                  pl.BlockSpec((d, d), const),
                  pl.BlockSpec((4, d), const),
                  pl.BlockSpec((1, 8, d), lambda i: (i // tiles_per_batch, 0, 0)),
                  pl.BlockSpec((d, 2 * LANES), const)],
        out_specs=[pl.BlockSpec((tm, d), row),
                   pl.BlockSpec((tm * (d // LANES), LANES), row),
                   pl.BlockSpec((tm, LANES), row)],
        out_shape=[jax.ShapeDtypeStruct((n, d), F32),
                   jax.ShapeDtypeStruct((n * (d // LANES), LANES), F32),
                   jax.ShapeDtypeStruct((n, LANES), F32)],
        compiler_params=_cparams(("parallel",)),
        name="mixer_out",
    )(ys, u, mlo, x2d, dsk, glu_w, glu_b, w_out, norm_g, mod3, rw2)


def _count(mask):
    c = jnp.sum(jnp.where(mask, 1.0, 0.0), axis=2, keepdims=True)
    return jnp.sum(c, axis=1, keepdims=True)


SELECT_BISECTIONS = 48


def _pow2_neg(n):
    p = jnp.ones(n.shape, F32)
    for i in range(7):
        p = p * jnp.where(((n >> i) & 1) == 1, 2.0 ** -(2 ** i), 1.0)
    return p


def _select_kernel(aff_ref, off_ref, idx_ref, gate_ref, *, cap, n_tok):
    a = aff_ref[0]
    e, nb, _ = a.shape

    def enough(thr):
        return _count(a >= thr) >= cap

    ex_lo = jnp.full((e, 1, 1), -127, I32)
    ex_hi = jnp.full((e, 1, 1), 1, I32)
    for _ in range(7):
        ex_mid = (ex_lo + ex_hi) >> 1
        ok = enough(_pow2_neg(-ex_mid))
        ex_lo = jnp.where(ok, ex_mid, ex_lo)
        ex_hi = jnp.where(ok, ex_hi, ex_mid)
    lo = jnp.where(ex_lo == -127, 0.0, _pow2_neg(jnp.minimum(-ex_lo, 126)))
    hi = jnp.where(ex_hi == 1, 2.0, _pow2_neg(jnp.maximum(-ex_hi, 0)))
    for _ in range(SELECT_BISECTIONS):
        mid = lo + 0.5 * (hi - lo)
        ok = enough(mid)
        lo = jnp.where(ok, mid, lo)
        hi = jnp.where(ok, hi, mid)
    gt = a >= hi
    tie = (a >= lo) & (a < hi)
    need = cap - _count(gt)
    tok = (lax.broadcasted_iota(I32, a.shape, 1) * LANES + lax.broadcasted_iota(I32, a.shape, 2))
    cut = jnp.zeros((e, 1, 1), I32)
    for bit in range(n_tok.bit_length() - 1, -1, -1):
        cand = cut | (1 << bit)
        cut = jnp.where(_count(tie & (tok < cand)) < need, cand, cut)
    sel = gt | (tie & (tok <= cut))
    self32 = jnp.where(sel, 1.0, 0.0).reshape(e * nb, LANES)
    r = lax.broadcasted_iota(I32, (LANES, LANES), 0)
    c = lax.broadcasted_iota(I32, (LANES, LANES), 1)
    incl = _dot(self32.astype(BF16), jnp.where(r <= c, 1.0, 0.0).astype(BF16))
    tot = jnp.broadcast_to(incl[:, LANES - 1:LANES], (e * nb, LANES)).astype(BF16)
    rr = lax.broadcasted_iota(I32, (e * nb, e * nb), 0)
    cc = lax.broadcasted_iota(I32, (e * nb, e * nb), 1)
    same = (rr // nb) == (cc // nb)
    before = jnp.where(same & (cc < rr), 1.0, 0.0).astype(BF16)
    off = _dot(before, tot)
    off_ref[0] = off.astype(I32).reshape(e, nb, LANES)

    kcol = lax.broadcasted_iota(I32, (cap, nb), 0).astype(F32)
    jrow = lax.broadcasted_iota(I32, (cap, nb), 1).astype(F32)
    lane = lax.broadcasted_iota(I32, (cap, LANES), 1)
    lane_f = lane.astype(F32)
    ones8 = jnp.ones((8, LANES), BF16)
    jr = lax.broadcasted_iota(I32, (nb, nb), 0)
    jc = lax.broadcasted_iota(I32, (nb, nb), 1)
    upper = jnp.where(jr <= jc, 1.0, 0.0).astype(BF16)
    sel3 = self32.reshape(e, nb, LANES)
    incl3 = incl.reshape(e, nb, LANES)

    def to_row(col):
        return _dot_nt(ones8, jnp.where(lane == 0, col, 0.0).astype(BF16))

    for ex in range(e):
        tot_row = _dot_nt(ones8, sel3[ex].astype(BF16))[0:1]
        end_row = _dot(jnp.broadcast_to(tot_row, (8, nb)).astype(BF16), upper)[0:1]
        start_row = end_row - tot_row
        jk = jnp.sum(jnp.where(end_row <= kcol, 1.0, 0.0), axis=1, keepdims=True)
        onehot_j = jnp.where(jrow == jk, 1.0, 0.0)
        kk = kcol[:, 0:1] - jnp.sum(onehot_j * start_row, axis=1, keepdims=True)
        d = _dot(onehot_j.astype(BF16), incl3[ex].astype(BF16))
        tl = jnp.sum(jnp.where(d <= kk, 1.0, 0.0), axis=1, keepdims=True)
        dg = _dot3_right(onehot_j.astype(BF16), a[ex])
        gate = jnp.sum(jnp.where(lane_f == tl, dg, 0.0), axis=1, keepdims=True)
        idx_ref[0, ex] = (to_row(jk) * float(LANES) + to_row(tl)).astype(I32)
        g1, g2, g3 = _split3(gate)
        gate_ref[0, ex] = (to_row(g1.astype(F32)) + to_row(g2.astype(F32))) + to_row(g3.astype(F32))


def _select(aff_t, cap):
    bsz, e, nb, _ = aff_t.shape
    kern = functools.partial(_select_kernel, cap=cap, n_tok=nb * LANES)
    blk = pl.BlockSpec((1, e, nb, LANES), lambda b: (b, 0, 0, 0))
    slot = pl.BlockSpec((1, e, 8, cap), lambda b: (b, 0, 0, 0))
    return pl.pallas_call(
        kern,
        grid=(bsz,),
        in_specs=[blk],
        out_specs=[blk, slot, slot],
        out_shape=[jax.ShapeDtypeStruct(aff_t.shape, I32),
                   jax.ShapeDtypeStruct((bsz, e, 8, cap), I32),
                   jax.ShapeDtypeStruct((bsz, e, 8, cap), F32)],
        compiler_params=_cparams(("parallel",)),
        name="ec_select",
    )(aff_t)


SLAB_PITCH = 24


def _row_copy(hx_hbm, xbuf, sem, src_row, dst_row, nslab):
    return pltpu.make_async_copy(hx_hbm.at[pl.ds(pl.multiple_of(src_row * nslab, nslab), nslab)],
                                 xbuf.at[pl.ds(pl.multiple_of(dst_row * SLAB_PITCH, 8), nslab)], sem)


def _dispatch_kernel(idx_ref, hx_hbm, o_ref, xbuf, sems, *, cap, n_tok, n_exp, nslab, n_steps):
    step = pl.program_id(0) * n_exp + pl.program_id(1)

    def issue_all(s):
        half = s % 2
        row0 = (s // n_exp) * n_tok

        def issue(k, _):
            _row_copy(hx_hbm, xbuf, sems.at[half], row0 + idx_ref[s * cap + k], half * cap + k, nslab).start()
            return 0

        lax.fori_loop(0, cap, issue, 0, unroll=8)

    @pl.when(step == 0)
    def _():
        issue_all(step)

    @pl.when(step + 1 < n_steps)
    def _():
        issue_all(step + 1)

    half = step % 2

    def drain(k, _):
        _row_copy(hx_hbm, xbuf, sems.at[half], 0, half * cap + k, nslab).wait()
        return 0

    lax.fori_loop(0, cap, drain, 0, unroll=8)
    first = pl.multiple_of(half * cap * SLAB_PITCH, 8)
    for c in range(nslab):
        o_ref[0, 0, :, c * LANES:(c + 1) * LANES] = (
            xbuf[pl.ds(first + c, cap, stride=SLAB_PITCH), :].astype(o_ref.dtype))


def _dispatch(idx_flat, hx_slabs, bsz, n_exp, cap, nslab):
    n_tok = hx_slabs.shape[0] // (bsz * nslab)
    d = nslab * LANES
    kern = functools.partial(_dispatch_kernel, cap=cap, n_tok=n_tok, n_exp=n_exp, nslab=nslab,
                             n_steps=bsz * n_exp)
    return pl.pallas_call(
        kern,
        grid_spec=pltpu.PrefetchScalarGridSpec(
            num_scalar_prefetch=1,
            grid=(bsz, n_exp),
            in_specs=[pl.BlockSpec(memory_space=pl.ANY)],
            out_specs=pl.BlockSpec((1, 1, cap, d), lambda b, e, idx: (b, e, 0, 0)),
            scratch_shapes=[pltpu.VMEM((2 * cap * SLAB_PITCH, LANES), F32),
                            pltpu.SemaphoreType.DMA((2,))]),
        out_shape=jax.ShapeDtypeStruct((bsz, n_exp, cap, d), BF16),
        compiler_params=_cparams(("arbitrary", "arbitrary")),
        name="ec_dispatch",
    )(idx_flat, hx_slabs)


FFN_COL_BLOCKS = 4


def _ffn_kernel(x_ref, g_ref, wg_ref, wu_ref, wd_ref, o_ref, acc):
    f = pl.program_id(2)

    @pl.when(f == 0)
    def _():
        acc[...] = jnp.zeros_like(acc)

    x = x_ref[0, 0]
    a = _dot(x, wg_ref[0].astype(BF16))
    u = _dot(x, wu_ref[0].astype(BF16))
    hmid = (_silu(a) * u).astype(BF16)
    wd = wd_ref[0].astype(BF16)
    cw = acc.shape[1] // FFN_COL_BLOCKS
    for cb in range(FFN_COL_BLOCKS):
        acc[:, cb * cw:(cb + 1) * cw] += _dot(hmid, wd[:, cb * cw:(cb + 1) * cw])

    @pl.when(f == pl.num_programs(2) - 1)
    def _():
        g = g_ref[0, 0]
        cap = g.shape[0]
        for c in range(o_ref.shape[2]):
            o_ref[0, 0, c, 0:cap, :] = acc[:, c * LANES:(c + 1) * LANES] * g
            o_ref[0, 0, c, cap:, :] = jnp.zeros((o_ref.shape[3] - cap, LANES), F32)


PLANE_PAD = 8


def _expert_ffn(xs, gates, w_gate, w_up, w_down):
    bsz, n_exp, cap, d = xs.shape
    ff = w_gate.shape[2]
    tf = 256
    rows = cap + PLANE_PAD
    return pl.pallas_call(
        _ffn_kernel,
        grid=(n_exp, bsz, ff // tf),
        in_specs=[pl.BlockSpec((1, 1, cap, d), lambda e, b, f: (b, e, 0, 0)),
                  pl.BlockSpec((1, 1, cap, 1), lambda e, b, f: (b, e, 0, 0)),
                  pl.BlockSpec((1, d, tf), lambda e, b, f: (e, 0, f)),
                  pl.BlockSpec((1, d, tf), lambda e, b, f: (e, 0, f)),
                  pl.BlockSpec((1, tf, d), lambda e, b, f: (e, f, 0))],
        out_specs=pl.BlockSpec((1, 1, d // LANES, rows, LANES), lambda e, b, f: (b, e, 0, 0, 0)),
        out_shape=jax.ShapeDtypeStruct((bsz, n_exp, d // LANES, rows, LANES), F32),
        scratch_shapes=[pltpu.VMEM((cap, d), F32)],
        compiler_params=_cparams(("parallel", "parallel", "arbitrary")),
        name="expert_ffn",
    )(xs, gates, w_gate, w_up, w_down)


COMB_UNROLL = 4


def _combine_kernel(idx_ref, off_ref, y_ref, o_ref, *, cap, nb, n_exp, tok_part, n_part):
    b = pl.program_id(0)
    part = pl.program_id(1)
    ex = pl.program_id(3)

    @pl.when(ex == 0)
    def _():
        o_ref[...] = jnp.zeros_like(o_ref)

    be = b * n_exp + ex
    blocks_per_part = nb // n_part
    k_lo = off_ref[be * nb + part * blocks_per_part]
    k_hi = jnp.where(part == n_part - 1, cap,
                     off_ref[be * nb + jnp.minimum((part + 1) * blocks_per_part, nb - 1)])
    t0 = part * tok_part

    def add_rows(k, n):
        toks = [idx_ref[be * cap + k + i] - t0 for i in range(n)]
        rows = [o_ref[0, toks[i]] + y_ref[0, 0, :, k + i, :] for i in range(n)]
        for i in range(n):
            o_ref[0, toks[i]] = rows[i]

    n_full = (k_hi - k_lo) // COMB_UNROLL

    def body(i, _):
        add_rows(k_lo + i * COMB_UNROLL, COMB_UNROLL)
        return 0

    lax.fori_loop(0, n_full, body, 0)

    def tail(k, _):
        add_rows(k, 1)
        return 0

    lax.fori_loop(k_lo + n_full * COMB_UNROLL, k_hi, tail, 0)


def _combine(idx_flat, off_flat, y, n_tok, cap):
    bsz, n_exp, nslab, rows, _ = y.shape
    nb = n_tok // TOK_BLK
    n_part = 2
    sblk = 8
    kern = functools.partial(_combine_kernel, cap=cap, nb=nb, n_exp=n_exp, tok_part=n_tok // n_part,
                             n_part=n_part)
    return pl.pallas_call(
        kern,
        grid_spec=pltpu.PrefetchScalarGridSpec(
            num_scalar_prefetch=2,
            grid=(bsz, n_part, nslab // sblk, n_exp),
            in_specs=[pl.BlockSpec((1, 1, sblk, rows, LANES), lambda b, p, c, e, idx, off: (b, e, c, 0, 0))],
            out_specs=pl.BlockSpec((1, n_tok // n_part, sblk, LANES),
                                   lambda b, p, c, e, idx, off: (b, p, c, 0))),
        out_shape=jax.ShapeDtypeStruct((bsz, n_tok, nslab, LANES), F32),
        compiler_params=_cparams(("parallel", "parallel", "parallel", "arbitrary")),
        name="ec_combine",
    )(idx_flat, off_flat, y)


def _final_kernel(x1_ref, moe_ref, g_ref, mod_ref, o_ref):
    tm = x1_ref.shape[0]
    nslab = moe_ref.shape[0] // tm
    m = jnp.concatenate([moe_ref[pl.ds(c, tm, stride=nslab), :] for c in range(nslab)], axis=-1)
    r = m * lax.rsqrt(jnp.mean(m * m, axis=-1, keepdims=True) + EPS) * g_ref[...]
    o_ref[...] = x1_ref[...] + mod_ref[0] * r


def _final(x1, moe, g3, g2, rows_per_batch):
    n, d = x1.shape
    tm = 512
    tiles_per_batch = rows_per_batch // tm
    return pl.pallas_call(
        _final_kernel,
        grid=(n // tm,),
        in_specs=[pl.BlockSpec((tm, d), lambda i: (i, 0)),
                  pl.BlockSpec((tm * (d // LANES), LANES), lambda i: (i, 0)),
                  pl.BlockSpec((1, d), lambda i: (0, 0)),
                  pl.BlockSpec((1, 1, d), lambda i: (i // tiles_per_batch, 0, 0))],
        out_specs=pl.BlockSpec((tm, d), lambda i: (i, 0)),
        out_shape=jax.ShapeDtypeStruct((n, d), F32),
        compiler_params=_cparams(("parallel",)),
        name="final_residual",
    )(x1, moe, g3, g2)


def _layer(x, c, ctx, c_ctx, ada_w, ada_b, norm_g, w_in, s5_a_re, s5_a_im, s5_log_dt, s5_b_re,
           s5_b_im, s5_c_re, s5_c_im, s5_d, s5_glu_w, s5_glu_b, ml_conv_w, ml_conv_b, ml_gate_b,
           ml_norm_g, w_out, router_w, exp_w_gate, exp_w_up, exp_w_down):
    bsz, seq, d = x.shape
    lctx = ctx.shape[1]
    ds5 = s5_d.shape[0]
    dml = ml_norm_g.shape[0]
    ngroups = ds5 // S5_GROUP
    rows = seq // GRID_W
    n = bsz * seq
    cap = EC_FACTOR * seq // N_EXPERTS
    assert cap >= TOK_BLK + 16 and seq % DISP_TOK == 0 and lctx % ML_T == 0 and seq % ML_T == 0

    cond = jnp.concatenate([c, c_ctx[None], jnp.zeros((8 - bsz - 1, d), F32)], axis=0)
    mod = _modulation(cond, ada_w, ada_b).reshape(8, N_MOD, d)
    sh1, sc1, g1, sh2, sc2, g2 = [mod[:, i] for i in range(N_MOD)]

    nmain = ds5 + 4 * dml
    w_s5 = w_in[:, :ds5].astype(BF16)
    w_ml = w_in[:, ds5:nmain].astype(BF16)
    w_gate = jnp.pad(w_in[:, nmain:], ((0, 0), (0, LANES - 4 * ML_HEADS))).astype(BF16)
    g0 = norm_g[0:1]
    sc, sh = sc1[:, None], sh1[:, None]
    of_sample = lambda b: b
    of_ctx = lambda b: bsz
    uc = _in_projection_s5(ctx, g0, sc, sh, of_ctx, w_s5)
    ltot = seq + lctx
    ml5, gates5, ux = _in_projection_ml(x, ctx, g0, sc, sh, bsz, w_ml, w_gate, w_s5)

    m, bp, cpt, avec = _s5_operators(s5_a_re, s5_a_im, s5_log_dt, s5_b_re, s5_b_im, s5_c_re, s5_c_im)
    ys = _s5_scan(ux, uc, m, bp, cpt, avec).reshape(n, ds5)

    ml = ml5.reshape(bsz, ltot, 4 * dml)
    gates = gates5.reshape(bsz, ltot, LANES)
    conv_w8 = jnp.pad(ml_conv_w, ((0, 8 - CONV_K), (0, 0)))
    qscale = jnp.concatenate([jnp.full((1, dml), (dml // ML_HEADS) ** -0.5, F32),
                              jnp.ones((1, dml), F32)], axis=1)
    qk = _conv_silu(ml, conv_w8, ml_conv_b[None], qscale, (0, seq), (seq, ltot))
    gbias = ml_gate_b.reshape(1, 4 * ML_HEADS)
    gates_t = gates[..., :4 * ML_HEADS].swapaxes(1, 2)
    hf, hb = _mlstm_scan(qk, ml, gates, gates_t, gbias, gbias.reshape(4 * ML_HEADS, 1), lctx // ML_T)
    mlo = _ml_post(hf.reshape(ml5.shape[:4] + (dml,)), hb.reshape(ml5.shape[:4] + (dml,)), ml5,
                   ml_norm_g[None], seq).reshape(n, dml)

    mod3 = jnp.stack([g1[:bsz], sc2[:bsz], sh2[:bsz]], axis=1)
    mod3 = jnp.concatenate([mod3, jnp.zeros((bsz, 5, d), F32)], axis=1)
    rw = jnp.pad(router_w, ((0, 0), (0, LANES - N_EXPERTS)))
    rw_hi = rw.astype(BF16)
    rw_lo = (rw - rw_hi.astype(F32)).astype(BF16)
    x1, hx2, aff = _mixer_out(ys, ux.reshape(n, ds5), mlo, x.reshape(n, d), seq, s5_d[None],
                              s5_glu_w.astype(BF16), s5_glu_b[None], w_out.astype(BF16), norm_g, mod3,
                              jnp.concatenate([rw_hi, rw_lo], axis=1))

    nb = seq // TOK_BLK
    aff16 = aff[:, :N_EXPERTS].reshape(bsz, seq, N_EXPERTS)
    aff_t = aff16.swapaxes(1, 2).reshape(bsz, N_EXPERTS, nb, LANES)
    off, idx, gates = _select(aff_t, cap)
    off_flat = off[..., 0].reshape(-1)
    idx_flat = idx[:, :, 0].reshape(-1)
    xs = _dispatch(idx_flat, hx2, bsz, N_EXPERTS, cap, d // LANES)
    ye = _expert_ffn(xs, gates[:, :, 0, :, None], exp_w_gate, exp_w_up, exp_w_down)
    moe = _combine(idx_flat, off_flat, ye, seq, cap)
    out = _final(x1, moe.reshape(n * (d // LANES), LANES), norm_g[3:4], g2[:bsz, None], seq)
    return out.reshape(bsz, seq, d)


def kernel(x, c, ctx, c_ctx, ada_w, ada_b, norm_g, w_in, s5_a_re, s5_a_im, s5_log_dt, s5_b_re, s5_b_im, s5_c_re, s5_c_im, s5_d, s5_glu_w, s5_glu_b, ml_conv_w, ml_conv_b, ml_gate_b, ml_norm_g, w_out, router_w, exp_w_gate, exp_w_up, exp_w_down):
    depth = ada_w.shape[0]
    assert depth == 1, "context outputs are only produced when another layer follows"
    li = 0
    return _layer(x, c, ctx, c_ctx, ada_w[li], ada_b[li], norm_g[li], w_in[li], s5_a_re[li],
                  s5_a_im[li], s5_log_dt[li], s5_b_re[li], s5_b_im[li], s5_c_re[li], s5_c_im[li],
                  s5_d[li], s5_glu_w[li], s5_glu_b[li], ml_conv_w[li], ml_conv_b[li], ml_gate_b[li],
                  ml_norm_g[li], w_out[li], router_w[li], exp_w_gate[li], exp_w_up[li],
                  exp_w_down[li])
```

```python
import functools
import math

import jax
import jax.numpy as jnp
from jax import lax
from jax.experimental import pallas as pl
from jax.experimental.pallas import tpu as pltpu

F32 = jnp.float32
BF16 = jnp.bfloat16
I32 = jnp.int32

EPS = 1e-6
GRID_W = 64
S5_GROUP = 16
S5_STATE = 64
S5_T = 16
ML_HEADS = 4
ML_T = 256
CONV_K = 5
N_EXPERTS = 16
EC_FACTOR = 2
N_MOD = 6
LANES = 128
TOK_BLK = 128
DISP_TOK = 512
DISP_W = DISP_TOK + 8
VMEM_LIMIT = 56 * 1024 * 1024


def _cparams(sem):
    return pltpu.CompilerParams(dimension_semantics=sem, vmem_limit_bytes=VMEM_LIMIT)


def _sigmoid(x):
    return 1.0 / (1.0 + jnp.exp(-x))


def _silu(x):
    return x * _sigmoid(x)


def _log_sigmoid(x):
    return jnp.minimum(x, 0.0) - jnp.log(1.0 + jnp.exp(-jnp.abs(x)))


def _split3(x):
    x1 = x.astype(BF16)
    r = x - x1.astype(F32)
    x2 = r.astype(BF16)
    x3 = (r - x2.astype(F32)).astype(BF16)
    return x1, x2, x3


def _dot(a, b):
    return jnp.dot(a, b, preferred_element_type=F32)


def _dot_nt(a, b):
    return lax.dot_general(a, b, (((1,), (1,)), ((), ())), preferred_element_type=F32)


def _dot_tn(a, b):
    return lax.dot_general(a, b, (((0,), (0,)), ((), ())), preferred_element_type=F32)


def _dot3_left(a_f32, b_bf16):
    a1, a2, a3 = _split3(a_f32)
    return _dot(a1, b_bf16) + _dot(a2, b_bf16) + _dot(a3, b_bf16)


def _dot3_right(a_bf16, b_f32):
    b1, b2, b3 = _split3(b_f32)
    return _dot(a_bf16, b1) + _dot(a_bf16, b2) + _dot(a_bf16, b3)


def _mod_kernel(c_ref, w_ref, b_ref, o_ref):
    c = c_ref[...]
    s = _silu(c).astype(BF16)
    o_ref[...] = _dot(s, w_ref[...].astype(BF16)) + b_ref[...]


def _modulation(cond8, ada_w, ada_b):
    d, n = ada_w.shape
    tn = 1024
    return pl.pallas_call(
        _mod_kernel,
        grid=(n // tn,),
        in_specs=[pl.BlockSpec((8, d), lambda j: (0, 0)),
                  pl.BlockSpec((d, tn), lambda j: (0, j)),
                  pl.BlockSpec((1, tn), lambda j: (0, j))],
        out_specs=pl.BlockSpec((8, tn), lambda j: (0, j)),
        out_shape=jax.ShapeDtypeStruct((8, n), F32),
        compiler_params=_cparams(("parallel",)),
        name="modulation",
    )(cond8, ada_w, ada_b.reshape(1, n))


def _norm_mod(x, g, sc, sh):
    ms = jnp.mean(x * x, axis=-1, keepdims=True)
    h = x * lax.rsqrt(ms + EPS) * g
    return (h * (1.0 + sc) + sh).astype(BF16)


def _inproj_s5_kernel(x_ref, g_ref, sc_ref, sh_ref, w_ref, o_ref, h_scr):
    @pl.when(pl.program_id(2) == 0)
    def _():
        h_scr[...] = _norm_mod(x_ref[0], g_ref[...], sc_ref[0], sh_ref[0])

    o_ref[0] = _dot(h_scr[...], w_ref[...])


def _in_projection_s5(x, g0, sc, sh, mod_of_batch, w_s5):
    bsz, length, d = x.shape
    ds5 = w_s5.shape[1]
    tm = min(1024, length)
    tn = 512
    return pl.pallas_call(
        _inproj_s5_kernel,
        grid=(bsz, length // tm, ds5 // tn),
        in_specs=[pl.BlockSpec((1, tm, d), lambda b, i, j: (b, i, 0)),
                  pl.BlockSpec((1, d), lambda b, i, j: (0, 0)),
                  pl.BlockSpec((1, 1, d), lambda b, i, j: (mod_of_batch(b), 0, 0)),
                  pl.BlockSpec((1, 1, d), lambda b, i, j: (mod_of_batch(b), 0, 0)),
                  pl.BlockSpec((d, tn), lambda b, i, j: (0, j))],
        out_specs=pl.BlockSpec((1, tm, tn), lambda b, i, j: (b, i, j)),
        out_shape=jax.ShapeDtypeStruct((bsz, length, ds5), F32),
        scratch_shapes=[pltpu.VMEM((tm, d), BF16)],
        compiler_params=_cparams(("parallel", "parallel", "arbitrary")),
        name="in_projection_s5",
    )(x, g0, sc, sh, w_s5)


def _inproj_ml_kernel(x_ref, c_ref, g_ref, sc_ref, sh_ref, csc_ref, csh_ref, w_ref, wg_ref, ws_ref,
                      o_ref, og_ref, u_ref, h_scr, x_scr, hr_scr, *, n_ml_blocks):
    rb = o_ref.shape[3]
    nlb = x_ref.shape[2] // LANES
    j = pl.program_id(2)

    @pl.when(j == 0)
    def _():
        hr_scr[...] = _norm_mod(x_ref[0], g_ref[...], sc_ref[0], sh_ref[0])
        for c in range(nlb):
            for r in range(rb):
                x_scr[c, r * ROW_PITCH:r * ROW_PITCH + GRID_W, :] = (
                    x_ref[0, r * GRID_W:(r + 1) * GRID_W, c * LANES:(c + 1) * LANES])
        for w in range(GRID_W):
            xw = jnp.concatenate([x_scr[c, pl.ds(w, rb, stride=ROW_PITCH), :] for c in range(nlb)],
                                 axis=-1)
            h_scr[w * rb:(w + 1) * rb, :] = _norm_mod(xw, g_ref[...], sc_ref[0], sh_ref[0])
        for s in range(c_ref.shape[1]):
            h_scr[(GRID_W + s) * rb:(GRID_W + s + 1) * rb, :] = _norm_mod(
                c_ref[0, s, 0], g_ref[...], csc_ref[0], csh_ref[0])
        og_ref[0] = _dot(h_scr[...], wg_ref[...]).reshape(og_ref.shape[1:])

    @pl.when(j < n_ml_blocks)
    def _():
        o_ref[0] = _dot(h_scr[...], w_ref[...]).reshape(o_ref.shape[1:]).astype(o_ref.dtype)

    @pl.when(j >= n_ml_blocks)
    def _():
        u_ref[0] = _dot(hr_scr[...], ws_ref[...])


ML_RB = 16
ROW_PITCH = GRID_W + 8


def _in_projection_ml(x, ctx, g0, sc, sh, ctx_mod, w_ml, w_gate, w_s5):
    bsz, length, d = x.shape
    lctx = ctx.shape[1]
    nml = w_ml.shape[1]
    ds5 = w_s5.shape[1]
    tn = 512
    tns = 256
    nmb, nsb = nml // tn, ds5 // tns
    rows = length // GRID_W
    rblocks = rows // ML_RB
    nslot_c = lctx // rows
    assert nslot_c * rows == lctx and rblocks * ML_RB == rows
    n_slots = GRID_W + nslot_c
    tm = ML_RB * GRID_W
    ctx5 = ctx.reshape(bsz, nslot_c, rblocks, ML_RB, d)
    kern = functools.partial(_inproj_ml_kernel, n_ml_blocks=nmb)
    return pl.pallas_call(
        kern,
        grid=(bsz, rblocks, nmb + nsb),
        in_specs=[pl.BlockSpec((1, tm, d), lambda b, i, j: (b, i, 0)),
                  pl.BlockSpec((1, nslot_c, 1, ML_RB, d), lambda b, i, j: (b, 0, i, 0, 0)),
                  pl.BlockSpec((1, d), lambda b, i, j: (0, 0)),
                  pl.BlockSpec((1, 1, d), lambda b, i, j: (b, 0, 0)),
                  pl.BlockSpec((1, 1, d), lambda b, i, j: (b, 0, 0)),
                  pl.BlockSpec((1, 1, d), lambda b, i, j: (ctx_mod, 0, 0)),
                  pl.BlockSpec((1, 1, d), lambda b, i, j: (ctx_mod, 0, 0)),
                  pl.BlockSpec((d, tn), lambda b, i, j: (0, jnp.minimum(j, nmb - 1))),
                  pl.BlockSpec((d, LANES), lambda b, i, j: (0, 0)),
                  pl.BlockSpec((d, tns), lambda b, i, j: (0, jnp.maximum(j - nmb, 0)))],
        out_specs=[pl.BlockSpec((1, n_slots, 1, ML_RB, tn),
                                lambda b, i, j: (b, 0, i, 0, jnp.minimum(j, nmb - 1))),
                   pl.BlockSpec((1, n_slots, 1, ML_RB, LANES), lambda b, i, j: (b, 0, i, 0, 0)),
                   pl.BlockSpec((1, tm, tns), lambda b, i, j: (b, i, jnp.maximum(j - nmb, 0)))],
        out_shape=[jax.ShapeDtypeStruct((bsz, n_slots, rblocks, ML_RB, nml), BF16),
                   jax.ShapeDtypeStruct((bsz, n_slots, rblocks, ML_RB, LANES), F32),
                   jax.ShapeDtypeStruct((bsz, length, ds5), F32)],
        scratch_shapes=[pltpu.VMEM((n_slots * ML_RB, d), BF16),
                        pltpu.VMEM((d // LANES, ML_RB * ROW_PITCH, LANES), F32),
                        pltpu.VMEM((tm, d), BF16)],
        compiler_params=_cparams(("parallel", "parallel", "arbitrary")),
        name="in_projection_ml",
    )(x, ctx5, g0, sc, sh, sc, sh, w_ml, w_gate, w_s5)


def _dot_nt6(a, b):
    a1, a2, a3 = _split3(a)
    b1, b2, b3 = _split3(b)
    return (_dot_nt(a1, b1) + (_dot_nt(a1, b2) + _dot_nt(a2, b1))
            + (_dot_nt(a1, b3) + _dot_nt(a3, b1) + _dot_nt(a2, b2)))


S5_LANES = 4 * S5_STATE


def _s5ops_kernel(prm_ref, bt_ref, c_ref, rexp_ref, rc_ref, place_ref, m_ref, bp_ref, cpt_ref, av_ref):
    t = S5_T
    prm = prm_ref[0]
    lr, li = jnp.minimum(prm[0:1], -1e-4), prm[1:2]
    dt = jnp.exp(prm[2:3])
    zr, zi = lr * dt, li * dt
    tau = lax.broadcasted_iota(I32, (32, S5_LANES), 0).astype(F32)
    mag, ang = jnp.exp(tau * zr), tau * zi
    pwr, pwi = mag * jnp.cos(ang), mag * jnp.sin(ang)
    xr, xi = pwr[1:2] - 1.0, pwi[1:2]
    den = lr * lr + li * li
    qr, qi = (xr * lr + xi * li) / den, (xi * lr - xr * li) / den
    btr, bti = bt_ref[0, 0], bt_ref[0, 1]
    bbr, bbi = qr * btr - qi * bti, qr * bti + qi * btr
    cr, ci = c_ref[0, 0], c_ref[0, 1]

    lane = lax.broadcasted_iota(I32, (t * S5_GROUP, S5_LANES), 1)
    fwd = ((lane // S5_STATE) % 2) == 0

    def expand(r, x):
        return _dot3_right(r, x)

    e_t = [expand(rexp_ref[0], z) for z in (pwr, pwi)]
    e_r = [expand(rexp_ref[1], z) for z in (pwr, pwi)]
    e_p = [expand(rexp_ref[2], z) for z in (pwr, pwi)]
    e_q = [expand(rexp_ref[3], z) for z in (pwr, pwi)]
    pk = [jnp.where(fwd, e_t[i], e_r[i]) for i in range(2)]
    pb = [jnp.where(fwd, e_r[i], e_t[i]) for i in range(2)]
    pc = [jnp.where(fwd, e_p[i], e_q[i]) for i in range(2)]
    cxr, cxi = expand(rc_ref[...], cr), expand(rc_ref[...], ci)
    bxr, bxi = expand(rc_ref[...], bbr), expand(rc_ref[...], bbi)

    wkr, wki = pk[0] * cxr - pk[1] * cxi, pk[0] * cxi + pk[1] * cxr
    row_blk = lax.broadcasted_iota(I32, (4 * S5_GROUP, S5_LANES), 0) // S5_GROUP
    lane_blk = lax.broadcasted_iota(I32, (4 * S5_GROUP, S5_LANES), 1) // S5_STATE
    own = row_blk == lane_blk
    bmr = jnp.where(own, jnp.concatenate([bbr] * 4, axis=0), 0.0)
    bmi = jnp.where(own, jnp.concatenate([bbi] * 4, axis=0), 0.0)
    kt = _dot_nt6(bmr, wkr) - _dot_nt6(bmi, wki)

    ln = lax.broadcasted_iota(I32, (S5_GROUP, t * S5_GROUP), 1)
    for gi in range(2):
        ktf = kt[(2 * gi) * S5_GROUP:(2 * gi + 1) * S5_GROUP]
        ktb = kt[(2 * gi + 1) * S5_GROUP:(2 * gi + 2) * S5_GROUP]
        blocks = []
        for s in range(t):
            f = ktf if s == 0 else pltpu.roll(ktf, S5_GROUP * s, axis=1)
            sh = (t * S5_GROUP - S5_GROUP * (t - 1 - s)) % (t * S5_GROUP)
            bk = ktb if sh == 0 else pltpu.roll(ktb, sh, axis=1)
            blocks.append(jnp.where(ln >= S5_GROUP * s, f, 0.0)
                          + jnp.where(ln < S5_GROUP * (s + 1), bk, 0.0))
        m_ref[0, gi] = jnp.concatenate(blocks, axis=0).astype(BF16)

    inj = jnp.concatenate([pb[0] * bxr - pb[1] * bxi, pb[0] * bxi + pb[1] * bxr], axis=-1).astype(BF16)
    out = jnp.concatenate([pc[0] * cxr - pc[1] * cxi, -(pc[0] * cxi + pc[1] * cxr)], axis=-1).astype(BF16)
    for gi in range(2):
        bp_ref[0, gi * t * S5_GROUP:(gi + 1) * t * S5_GROUP, :] = _dot(inj, place_ref[gi]).astype(BF16)
        cpt_ref[0, gi * t * S5_GROUP:(gi + 1) * t * S5_GROUP, :] = _dot(out, place_ref[gi]).astype(BF16)
    a_t = jnp.concatenate([pwr[t:t + 8], pwi[t:t + 8]], axis=-1)
    av_ref[0] = _dot3_left(a_t, place_ref[0]) + _dot3_left(a_t, place_ref[1])


def _s5_operators(a_re, a_im, log_dt, b_re, b_im, c_re, c_im):
    t = S5_T
    g, p = a_re.shape[1], a_re.shape[2]
    npair = g // 2

    def lanes(z):
        return z.reshape(2, npair, 2, p).transpose(1, 2, 0, 3).reshape(npair, 4 * p)

    prm = jnp.stack([lanes(a_re.astype(F32)), lanes(a_im.astype(F32)),
                     lanes(jnp.broadcast_to(log_dt.astype(F32)[..., None], a_re.shape))], axis=1)
    prm = jnp.concatenate([prm, jnp.zeros((npair, 5, 4 * p), F32)], axis=1)

    def tile_d(z):
        z = z.reshape(npair, 2, S5_GROUP, p).transpose(0, 2, 1, 3)
        return jnp.broadcast_to(z[:, :, :, None, :], (npair, S5_GROUP, 2, 2, p)).reshape(npair, S5_GROUP, 4 * p)

    bt = jnp.stack([tile_d(b_re.astype(F32).swapaxes(1, 2)), tile_d(b_im.astype(F32).swapaxes(1, 2))], axis=1)
    ct = jnp.stack([tile_d(c_re.astype(F32)), tile_d(c_im.astype(F32))], axis=1)

    row = jnp.arange(t * S5_GROUP)
    tau, ch = row // S5_GROUP, row % S5_GROUP
    e = jnp.arange(32)
    rexp = jnp.stack([(expo[:, None] == e[None, :]) for expo in (tau, t - 1 - tau, tau + 1, t - tau)]
                     ).astype(BF16)
    rc = (ch[:, None] == jnp.arange(S5_GROUP)[None, :]).astype(BF16)
    src = jnp.arange(2 * 4 * p)
    reim, gi_s, d_s, p_s = src // (4 * p), (src // (2 * p)) % 2, (src // p) % 2, src % p
    dst = d_s * (4 * p) + reim * (2 * p) + gi_s * p + p_s
    col = jnp.arange(8 * p)
    place = jnp.stack([((dst[:, None] == col[None, :]) & (gi_s[:, None] == gi)) for gi in range(2)]
                      ).astype(BF16)
    blk3 = lambda shape: pl.BlockSpec((1,) + shape, lambda i: (i,) + (0,) * len(shape))
    full = lambda arr: pl.BlockSpec(arr.shape, lambda i: (0,) * arr.ndim)
    m, bp, cpt, av = pl.pallas_call(
        _s5ops_kernel,
        grid=(npair,),
        in_specs=[blk3((8, 4 * p)), blk3((2, S5_GROUP, 4 * p)), blk3((2, S5_GROUP, 4 * p)),
                  full(rexp), full(rc), full(place)],
        out_specs=[blk3((2, t * S5_GROUP, t * S5_GROUP)), blk3((2 * t * S5_GROUP, 8 * p)),
                   blk3((2 * t * S5_GROUP, 8 * p)), blk3((8, 8 * p))],
        out_shape=[jax.ShapeDtypeStruct((npair, 2, t * S5_GROUP, t * S5_GROUP), BF16),
                   jax.ShapeDtypeStruct((npair, 2 * t * S5_GROUP, 8 * p), BF16),
                   jax.ShapeDtypeStruct((npair, 2 * t * S5_GROUP, 8 * p), BF16),
                   jax.ShapeDtypeStruct((npair, 8, 8 * p), F32)],
        compiler_params=_cparams(("parallel",)),
        name="s5_operators",
    )(prm, bt, ct, rexp, rc, place)
    avec = av[:, 0].reshape(npair, 4, 2 * p)
    return m, bp, cpt, jnp.concatenate([avec, jnp.zeros_like(avec)], axis=1)


def _s5_permutation():
    i = jnp.arange(8 * 8 * S5_GROUP)
    t8, j, c = i // (8 * S5_GROUP), (i // S5_GROUP) % 8, i % S5_GROUP
    dst = j * (8 * S5_GROUP) + t8 * S5_GROUP + c
    e1 = (dst[:, None] == i[None, :]).astype(BF16)
    return e1, e1.T


S5_PAIRS = 4


def _s5_kernel(ux_ref, uc_ref, e1_ref, e1t_ref, m_ref, bp_ref, cp_ref, av_ref, y_ref, s_scr, hp_scr,
               tab_scr, *, cc, xc):
    nc = cc + xc

    def chunk_rows(ref, nchunks):
        cols = [ref[0, pl.ds(t, nchunks, stride=S5_T), :].astype(BF16) for t in range(S5_T)]
        return [_dot(jnp.concatenate(cols[8 * h:8 * h + 8], axis=-1), e1_ref[...]).astype(BF16)
                for h in range(2)]

    def pair_lanes(halves, p):
        return jnp.concatenate([halves[h][:, j * LANES:(j + 1) * LANES]
                                for j in (2 * p, 2 * p + 1) for h in range(2)], axis=-1)

    pc = chunk_rows(uc_ref, cc)
    px = chunk_rows(ux_ref, xc)
    for p in range(S5_PAIRS):
        s_scr[0:cc, p * 512:(p + 1) * 512] = _dot(pair_lanes(pc, p), bp_ref[p])
        s_scr[cc:nc, p * 512:(p + 1) * 512] = _dot(pair_lanes(px, p), bp_ref[p])

    av = av_ref[...]
    sub = lax.broadcasted_iota(I32, (8, 128), 0)

    def cmul(ar, ai, br, bi):
        return ar * br - ai * bi, ar * bi + ai * br

    for p in range(S5_PAIRS):
        for d in range(2):
            a1 = (av[p, 2 * d:2 * d + 1], av[p, 2 * d + 1:2 * d + 2])
            a2 = cmul(*a1, *a1)
            a4 = cmul(*a2, *a2)
            pw_re, pw_im = jnp.zeros((8, 128), F32), jnp.zeros((8, 128), F32)
            cur = a1
            for k in range(8):
                r = k if d == 0 else 7 - k
                pw_re = jnp.where(sub == r, cur[0], pw_re)
                pw_im = jnp.where(sub == r, cur[1], pw_im)
                cur = cmul(*cur, *a1)
            for q, z in enumerate((a1, a2, a4)):
                tab_scr[p, d, 2 * q] = jnp.broadcast_to(z[0], (8, 128))
                tab_scr[p, d, 2 * q + 1] = jnp.broadcast_to(z[1], (8, 128))
            tab_scr[p, d, 6] = pw_re
            tab_scr[p, d, 7] = pw_im

    def scan8(s_re, s_im, h_re, h_im, p, d):
        back = d == 1
        x_re, x_im = s_re, s_im
        for q, k in enumerate((1, 2, 4)):
            keep = (sub < 8 - k) if back else (sub >= k)
            sh = (8 - k) if back else k
            y_re = jnp.where(keep, pltpu.roll(x_re, sh, axis=0), 0.0)
            y_im = jnp.where(keep, pltpu.roll(x_im, sh, axis=0), 0.0)
            m_re, m_im = cmul(tab_scr[p, d, 2 * q], tab_scr[p, d, 2 * q + 1], y_re, y_im)
            x_re, x_im = x_re + m_re, x_im + m_im
        c_re, c_im = cmul(tab_scr[p, d, 6], tab_scr[p, d, 7], h_re, h_im)
        x_re, x_im = x_re + c_re, x_im + c_im
        edge = 7 if back else 0
        sh = 7 if back else 1
        e_re = jnp.where(sub == edge, h_re, pltpu.roll(x_re, sh, axis=0))
        e_im = jnp.where(sub == edge, h_im, pltpu.roll(x_im, sh, axis=0))
        last = 0 if back else 7
        return e_re, e_im, x_re[last:last + 1], x_im[last:last + 1]

    def step(row_f, row_b, carry):
        rf = pl.multiple_of(row_f, 8)
        rb = pl.multiple_of(row_b, 8)
        new = []
        for p in range(S5_PAIRS):
            hr, hi, gr, gi = carry[4 * p:4 * p + 4]
            c0 = p * 512
            p_re, p_im, hr, hi = scan8(s_scr[pl.ds(rf, 8), c0:c0 + 128], s_scr[pl.ds(rf, 8), c0 + 128:c0 + 256],
                                       hr, hi, p, 0)
            hp_scr[pl.ds(rf, 8), c0:c0 + 128] = p_re
            hp_scr[pl.ds(rf, 8), c0 + 128:c0 + 256] = p_im
            p_re, p_im, gr, gi = scan8(s_scr[pl.ds(rb, 8), c0 + 256:c0 + 384], s_scr[pl.ds(rb, 8), c0 + 384:c0 + 512],
                                       gr, gi, p, 1)
            hp_scr[pl.ds(rb, 8), c0 + 256:c0 + 384] = p_re
            hp_scr[pl.ds(rb, 8), c0 + 384:c0 + 512] = p_im
            new += [hr, hi, gr, gi]
        return tuple(new)

    z = jnp.zeros((1, 128), F32)
    carry = lax.fori_loop(0, cc // 8, lambda i, c: step(8 * i, cc - 8 - 8 * i, c), (z,) * (4 * S5_PAIRS))
    lax.fori_loop(0, xc // 8, lambda i, c: step(cc + 8 * i, nc - 8 - 8 * i, c), carry)

    ys = []
    for p in range(S5_PAIRS):
        ux = pair_lanes(px, p)
        y = _dot_nt(hp_scr[cc:nc, p * 512:(p + 1) * 512].astype(BF16), cp_ref[p])
        ys.append(y + jnp.concatenate([_dot(ux[:, 0:256], m_ref[p, 0]), _dot(ux[:, 256:512], m_ref[p, 1])],
                                      axis=-1))
    for h in range(2):
        yh = jnp.concatenate([ys[j // 2][:, (j % 2) * 256 + h * LANES:(j % 2) * 256 + (h + 1) * LANES]
                              for j in range(8)], axis=-1).astype(BF16)
        r = _dot(yh, e1t_ref[...])
        for t8 in range(8):
            y_ref[0, pl.ds(8 * h + t8, xc, stride=S5_T), :] = r[:, t8 * LANES:(t8 + 1) * LANES]


def _s5_scan(ux, uc, m, bp, cp, avec):
    bsz, seq, ds5 = ux.shape
    lctx = uc.shape[1]
    cc, xc = lctx // S5_T, seq // S5_T
    assert cc % 8 == 0 and xc % 8 == 0
    e1, e1t = _s5_permutation()
    kern = functools.partial(_s5_kernel, cc=cc, xc=xc)
    nblk = ds5 // LANES
    const = lambda j, b: (0, 0)
    return pl.pallas_call(
        kern,
        grid=(nblk, bsz),
        in_specs=[pl.BlockSpec((1, seq, LANES), lambda j, b: (b, 0, j)),
                  pl.BlockSpec((1, lctx, LANES), lambda j, b: (b, 0, j)),
                  pl.BlockSpec(e1.shape, const),
                  pl.BlockSpec(e1.shape, const),
                  pl.BlockSpec((S5_PAIRS, 2, 256, 256), lambda j, b: (j, 0, 0, 0)),
                  pl.BlockSpec((S5_PAIRS, 512, 512), lambda j, b: (j, 0, 0)),
                  pl.BlockSpec((S5_PAIRS, 512, 512), lambda j, b: (j, 0, 0)),
                  pl.BlockSpec((S5_PAIRS, 8, 128), lambda j, b: (j, 0, 0))],
        out_specs=pl.BlockSpec((1, seq, LANES), lambda j, b: (b, 0, j)),
        out_shape=jax.ShapeDtypeStruct((bsz, seq, ds5), F32),
        scratch_shapes=[pltpu.VMEM((cc + xc, S5_PAIRS * 512), F32),
                        pltpu.VMEM((cc + xc, S5_PAIRS * 512), F32),
                        pltpu.VMEM((S5_PAIRS, 2, 8, 8, 128), F32)],
        compiler_params=_cparams(("parallel", "parallel")),
        name="s5_scan",
    )(ux, uc, e1, e1t, m, bp, cp, avec)


def _conv_kernel(x_ref, p_ref, n_ref, w_ref, b_ref, sc_ref, o_ref, scr, *, rows, seq_tiles):
    i = pl.program_id(1)
    first = functools.reduce(jnp.logical_or, [i == s for s in seq_tiles[0]])
    last = functools.reduce(jnp.logical_or, [i == s for s in seq_tiles[1]])
    pm = jnp.where(first, 0.0, 1.0)
    nm = jnp.where(last, 0.0, 1.0)
    scr[0:8, :] = p_ref[0, 8:16, :].astype(F32) * pm
    scr[8:8 + rows, :] = x_ref[0].astype(F32)
    scr[8 + rows:16 + rows, :] = n_ref[0, 0:8, :].astype(F32) * nm
    w = w_ref[...]
    acc = scr[6:6 + rows, :] * w[0:1]
    for j in range(1, CONV_K):
        acc = acc + scr[6 + j:6 + j + rows, :] * w[j:j + 1]
    z = acc + b_ref[...]
    o_ref[0] = (_silu(z) * sc_ref[...]).astype(o_ref.dtype)


def _conv_silu(ml, conv_w8, conv_b, scale, seq_starts, seq_ends):
    bsz, ltot, _ = ml.shape
    rows = 256
    ct = 1024
    nch = conv_b.shape[-1]
    nt = ltot // rows
    hb = rows // 16
    nhb = ltot // 16
    seq_tiles = (tuple(s // rows for s in seq_starts), tuple(e // rows - 1 for e in seq_ends))
    kern = functools.partial(_conv_kernel, rows=rows, seq_tiles=seq_tiles)
    return pl.pallas_call(
        kern,
        grid=(bsz, nt, nch // ct),
        in_specs=[pl.BlockSpec((1, rows, ct), lambda b, i, c: (b, i, c)),
                  pl.BlockSpec((1, 16, ct), lambda b, i, c: (b, jnp.maximum(i * hb - 1, 0), c)),
                  pl.BlockSpec((1, 16, ct), lambda b, i, c: (b, jnp.minimum((i + 1) * hb, nhb - 1), c)),
                  pl.BlockSpec((8, ct), lambda b, i, c: (0, c)),
                  pl.BlockSpec((1, ct), lambda b, i, c: (0, c)),
                  pl.BlockSpec((1, ct), lambda b, i, c: (0, c))],
        out_specs=pl.BlockSpec((1, rows, ct), lambda b, i, c: (b, i, c)),
        out_shape=jax.ShapeDtypeStruct((bsz, ltot, nch), BF16),
        scratch_shapes=[pltpu.VMEM((rows + 16, ct), F32)],
        compiler_params=_cparams(("parallel", "parallel", "parallel")),
        name="conv_silu",
    )(ml, ml, ml, conv_w8, conv_b, scale)


def _mlstm_chain_step(q, k, v, i_col, f_col, i_row, b_col, b_row, mask, c_ref, n_ref, m_ref, idx):
    m_st = m_ref[idx]
    d = jnp.where(mask, b_col - b_row + i_row, -jnp.inf)
    inter = b_col + m_st
    m_t = jnp.maximum(inter, jnp.max(d, axis=-1, keepdims=True))
    w = jnp.exp(d - m_t)
    s_inter = jnp.exp(inter - m_t)
    s = _dot_nt(q, k) * w
    c_st = c_ref[idx]
    n_st = n_ref[idx]
    num = _dot(s.astype(BF16), v) + s_inter * _dot_nt(q, c_st.astype(BF16))
    qf = q.astype(F32)
    den = jnp.sum(s, axis=-1, keepdims=True) + s_inter * jnp.sum(qf * n_st, axis=-1, keepdims=True)
    h = num / jnp.maximum(jnp.abs(den), jnp.exp(-m_t))
    total = jnp.sum(f_col, axis=0, keepdims=True)
    d_end = total - b_col + i_col
    m_new = jnp.maximum(total + m_st, jnp.max(d_end, axis=0, keepdims=True))
    w_end = jnp.exp(d_end - m_new)
    dec = jnp.exp(total + m_st - m_new)
    vw = (v.astype(F32) * w_end).astype(BF16)
    c_ref[idx] = dec * c_st + _dot_tn(vw, k)
    n_ref[idx] = dec * n_st + jnp.sum(k.astype(F32) * w_end, axis=0, keepdims=True)
    m_ref[idx] = m_new
    return h


def _mlstm_kernel(qf_ref, kf_ref, vf_ref, gf_ref, gtf_ref, qb_ref, kb_ref, vb_ref, gb_ref, gtb_ref,
                  gbias_ref, gbias_t_ref, hf_ref, hb_ref, c_scr, n_scr, m_scr, *, dh):
    @pl.when(pl.program_id(1) == 0)
    def _():
        c_scr[...] = jnp.zeros_like(c_scr)
        n_scr[...] = jnp.zeros_like(n_scr)
        m_scr[...] = jnp.zeros_like(m_scr)

    bsz, t = qf_ref.shape[0], qf_ref.shape[1]
    rid = lax.broadcasted_iota(I32, (t, t), 0)
    cid = lax.broadcasted_iota(I32, (t, t), 1)
    lower = jnp.where(rid >= cid, 1.0, 0.0).astype(BF16)
    upper = jnp.where(cid >= rid, 1.0, 0.0).astype(BF16)

    for bb in range(bsz):
        for d, (q_ref, k_ref, v_ref, g_ref, gt_ref, h_ref) in enumerate(
                [(qf_ref, kf_ref, vf_ref, gf_ref, gtf_ref, hf_ref),
                 (qb_ref, kb_ref, vb_ref, gb_ref, gtb_ref, hb_ref)]):
            backward = d == 1
            g = g_ref[bb, :, 0:16] + gbias_ref[...]
            gt = gt_ref[bb] + gbias_t_ref[...]
            lf = _log_sigmoid(g)
            lft = _log_sigmoid(gt)
            cum_col = _dot3_right(upper if backward else lower, lf)
            cum_row = _dot3_left(lft, lower if backward else upper)
            for hd in range(ML_HEADS):
                ci = 2 * ML_HEADS * d + hd
                cf = ci + ML_HEADS
                sl = slice(hd * dh, (hd + 1) * dh)
                h = _mlstm_chain_step(
                    q_ref[bb, :, sl], k_ref[bb, :, sl], v_ref[bb, :, sl],
                    g[:, ci:ci + 1], lf[:, cf:cf + 1], gt[ci:ci + 1, :],
                    cum_col[:, cf:cf + 1], cum_row[cf:cf + 1, :],
                    (cid >= rid) if backward else (rid >= cid),
                    c_scr, n_scr, m_scr, (bb * 2 + d) * ML_HEADS + hd)
                h_ref[bb, :, sl] = h.astype(h_ref.dtype)


def _mlstm_scan(qk, ml, gates, gates_t, gbias, gbias_t, n_ctx_chunks):
    bsz, ltot, dml2 = qk.shape
    dml = dml2 // 2
    dh = dml // ML_HEADS
    t = ML_T
    nchunk = ltot // t
    cc = n_ctx_chunks
    nx = nchunk - cc

    def fwd(s):
        return jnp.where(s < cc, nx + s, s - cc)

    def bwd(s):
        return jnp.where(s < cc, nx + cc - 1 - s, nx - 1 - (s - cc))

    bps = 1

    def specs(cmap):
        return [pl.BlockSpec((bps, t, dml), lambda b, s: (b, cmap(s), 0)),
                pl.BlockSpec((bps, t, dml), lambda b, s: (b, cmap(s), 1)),
                pl.BlockSpec((bps, t, dml), lambda b, s: (b, cmap(s), 2)),
                pl.BlockSpec((bps, t, LANES), lambda b, s: (b, cmap(s), 0)),
                pl.BlockSpec((bps, 16, t), lambda b, s: (b, 0, cmap(s)))]

    kern = functools.partial(_mlstm_kernel, dh=dh)
    nchain = bps * 2 * ML_HEADS
    return pl.pallas_call(
        kern,
        grid=(bsz // bps, nchunk),
        in_specs=specs(fwd) + specs(bwd) + [pl.BlockSpec((1, 16), lambda b, s: (0, 0)),
                                            pl.BlockSpec((16, 1), lambda b, s: (0, 0))],
        out_specs=[pl.BlockSpec((bps, t, dml), lambda b, s: (b, fwd(s), 0)),
                   pl.BlockSpec((bps, t, dml), lambda b, s: (b, bwd(s), 0))],
        out_shape=[jax.ShapeDtypeStruct((bsz, ltot, dml), BF16)] * 2,
        scratch_shapes=[pltpu.VMEM((nchain, dh, dh), F32),
                        pltpu.VMEM((nchain, 1, dh), F32),
                        pltpu.VMEM((nchain, 1, 1), F32)],
        compiler_params=_cparams(("parallel", "arbitrary")),
        name="mlstm_scan",
    )(qk, qk, ml, gates, gates_t, qk, qk, ml, gates, gates_t, gbias, gbias_t)


COL_PITCH = ML_RB + 8


def _mlpost_kernel(hf_ref, hb_ref, o_ref, mg_ref, out_ref, scr, *, dh):
    nw, _, rb, dml = hf_ref.shape[1:]
    h = (hf_ref[0].astype(F32) + hb_ref[0].astype(F32)).reshape(nw * rb, dml)
    parts = []
    for hd in range(ML_HEADS):
        hh = h[:, hd * dh:(hd + 1) * dh]
        parts.append(hh * lax.rsqrt(jnp.mean(hh * hh, axis=-1, keepdims=True) + EPS))
    hn = jnp.concatenate(parts, axis=-1) * mg_ref[...]
    res = hn * _sigmoid(o_ref[0].astype(F32).reshape(nw * rb, dml))
    nlb = dml // LANES
    for c in range(nlb):
        for w in range(nw):
            scr[c, w * COL_PITCH:w * COL_PITCH + rb, :] = res[w * rb:(w + 1) * rb, c * LANES:(c + 1) * LANES]
    for r in range(rb):
        row = jnp.concatenate([scr[c, pl.ds(r, nw, stride=COL_PITCH), :] for c in range(nlb)], axis=-1)
        out_ref[0, r * nw:(r + 1) * nw, :] = row.astype(out_ref.dtype)


def _ml_post(hf5, hb5, ml5, ml_g, seq):
    bsz, _, rblocks, rb, dml = hf5.shape
    kern = functools.partial(_mlpost_kernel, dh=dml // ML_HEADS)
    blk = (1, GRID_W, 1, rb, dml)
    return pl.pallas_call(
        kern,
        grid=(bsz, rblocks),
        in_specs=[pl.BlockSpec(blk, lambda b, i: (b, 0, i, 0, 0)),
                  pl.BlockSpec(blk, lambda b, i: (b, 0, i, 0, 0)),
                  pl.BlockSpec(blk, lambda b, i: (b, 0, i, 0, 3)),
                  pl.BlockSpec((1, dml), lambda b, i: (0, 0))],
        out_specs=pl.BlockSpec((1, rb * GRID_W, dml), lambda b, i: (b, i, 0)),
        out_shape=jax.ShapeDtypeStruct((bsz, seq, dml), BF16),
        scratch_shapes=[pltpu.VMEM((dml // LANES, GRID_W * COL_PITCH, LANES), F32)],
        compiler_params=_cparams(("parallel", "parallel")),
        name="mlstm_post",
    )(hf5, hb5, ml5, ml_g)


def _gelu_tanh(y):
    return 0.5 * y * (1.0 + jnp.tanh(math.sqrt(2.0 / math.pi) * (y + 0.044715 * (y * y * y))))


def _mixout_kernel(ys_ref, u_ref, ml_ref, x_ref, dsk_ref, gw_ref, gb_ref, wo_ref, ng_ref, mod_ref,
                   rw_ref, x1_ref, hx_ref, aff_ref, *, n_exp):
    y = ys_ref[...] + dsk_ref[...] * u_ref[...]
    ge = _gelu_tanh(y)
    z = _dot(ge.astype(BF16), gw_ref[...]) + gb_ref[...]
    s5 = (ge * _sigmoid(z)).astype(BF16)
    yx = _dot(jnp.concatenate([s5, ml_ref[...]], axis=-1), wo_ref[...])
    ng = ng_ref[...]
    mod = mod_ref[0]
    r1 = yx * lax.rsqrt(jnp.mean(yx * yx, axis=-1, keepdims=True) + EPS) * ng[1:2]
    x1 = x_ref[...] + mod[0:1] * r1
    x1_ref[...] = x1
    hx = x1 * lax.rsqrt(jnp.mean(x1 * x1, axis=-1, keepdims=True) + EPS) * ng[2:3]
    hx = hx * (1.0 + mod[1:2]) + mod[2:3]
    nslab = hx.shape[1] // LANES
    for cb in range(nslab):
        hx_ref[pl.ds(cb, hx.shape[0], stride=nslab), :] = hx[:, cb * LANES:(cb + 1) * LANES]
    h1, h2, _ = _split3(hx)
    rw2 = rw_ref[...]
    l1 = _dot(h1, rw2)
    logits = (l1[:, 0:LANES] + l1[:, LANES:2 * LANES]) + _dot(h2, rw2[:, 0:LANES])
    lane = lax.broadcasted_iota(I32, logits.shape, 1)
    logits = jnp.where(lane < n_exp, logits, -jnp.inf)
    e = jnp.exp(logits - jnp.max(logits, axis=-1, keepdims=True))
    aff_ref[...] = e / jnp.sum(e, axis=-1, keepdims=True)


def _mixer_out(ys, u, mlo, x2d, rows_per_batch, dsk, glu_w, glu_b, w_out, norm_g, mod3, rw2):
    n, d = x2d.shape
    ds5 = ys.shape[1]
    dml = mlo.shape[1]
    tm = 256
    tiles_per_batch = rows_per_batch // tm
    kern = functools.partial(_mixout_kernel, n_exp=N_EXPERTS)
    row = lambda i: (i, 0)
    const = lambda i: (0, 0)
    return pl.pallas_call(
        kern,
        grid=(n // tm,),
        in_specs=[pl.BlockSpec((tm, ds5), row),
                  pl.BlockSpec((tm, ds5), row),
                  pl.BlockSpec((tm, dml), row),
                  pl.BlockSpec((tm, d), row),
                  pl.BlockSpec((1, ds5), const),
                  pl.BlockSpec((ds5, ds5), const),
                  pl.BlockSpec((1, ds5), const),
                  pl.BlockSpec((d, d), const),
                  pl.BlockSpec((4, d), const),
                  pl.BlockSpec((1, 8, d), lambda i: (i // tiles_per_batch, 0, 0)),
                  pl.BlockSpec((d, 2 * LANES), const)],
        out_specs=[pl.BlockSpec((tm, d), row),
                   pl.BlockSpec((tm * (d // LANES), LANES), row),
                   pl.BlockSpec((tm, LANES), row)],
        out_shape=[jax.ShapeDtypeStruct((n, d), F32),
                   jax.ShapeDtypeStruct((n * (d // LANES), LANES), F32),
                   jax.ShapeDtypeStruct((n, LANES), F32)],
        compiler_params=_cparams(("parallel",)),
        name="mixer_out",
    )(ys, u, mlo, x2d, dsk, glu_w, glu_b, w_out, norm_g, mod3, rw2)


def _count(mask):
    c = jnp.sum(jnp.where(mask, 1.0, 0.0), axis=2, keepdims=True)
    return jnp.sum(c, axis=1, keepdims=True)


SELECT_BISECTIONS = 48


def _pow2_neg(n):
    p = jnp.ones(n.shape, F32)
    for i in range(7):
        p = p * jnp.where(((n >> i) & 1) == 1, 2.0 ** -(2 ** i), 1.0)
    return p


def _select_kernel(aff_ref, off_ref, idx_ref, gate_ref, *, cap, n_tok):
    a = aff_ref[0]
    e, nb, _ = a.shape

    def enough(thr):
        return _count(a >= thr) >= cap

    ex_lo = jnp.full((e, 1, 1), -127, I32)
    ex_hi = jnp.full((e, 1, 1), 1, I32)
    for _ in range(7):
        ex_mid = (ex_lo + ex_hi) >> 1
        ok = enough(_pow2_neg(-ex_mid))
        ex_lo = jnp.where(ok, ex_mid, ex_lo)
        ex_hi = jnp.where(ok, ex_hi, ex_mid)
    lo = jnp.where(ex_lo == -127, 0.0, _pow2_neg(jnp.minimum(-ex_lo, 126)))
    hi = jnp.where(ex_hi == 1, 2.0, _pow2_neg(jnp.maximum(-ex_hi, 0)))
    for _ in range(SELECT_BISECTIONS):
        mid = lo + 0.5 * (hi - lo)
        ok = enough(mid)
        lo = jnp.where(ok, mid, lo)
        hi = jnp.where(ok, hi, mid)
    gt = a >= hi
    tie = (a >= lo) & (a < hi)
    need = cap - _count(gt)
    tok = (lax.broadcasted_iota(I32, a.shape, 1) * LANES + lax.broadcasted_iota(I32, a.shape, 2))
    cut = jnp.zeros((e, 1, 1), I32)
    for bit in range(n_tok.bit_length() - 1, -1, -1):
        cand = cut | (1 << bit)
        cut = jnp.where(_count(tie & (tok < cand)) < need, cand, cut)
    sel = gt | (tie & (tok <= cut))
    self32 = jnp.where(sel, 1.0, 0.0).reshape(e * nb, LANES)
    r = lax.broadcasted_iota(I32, (LANES, LANES), 0)
    c = lax.broadcasted_iota(I32, (LANES, LANES), 1)
    incl = _dot(self32.astype(BF16), jnp.where(r <= c, 1.0, 0.0).astype(BF16))
    tot = jnp.broadcast_to(incl[:, LANES - 1:LANES], (e * nb, LANES)).astype(BF16)
    rr = lax.broadcasted_iota(I32, (e * nb, e * nb), 0)
    cc = lax.broadcasted_iota(I32, (e * nb, e * nb), 1)
    same = (rr // nb) == (cc // nb)
    before = jnp.where(same & (cc < rr), 1.0, 0.0).astype(BF16)
    off = _dot(before, tot)
    off_ref[0] = off.astype(I32).reshape(e, nb, LANES)

    kcol = lax.broadcasted_iota(I32, (cap, nb), 0).astype(F32)
    jrow = lax.broadcasted_iota(I32, (cap, nb), 1).astype(F32)
    lane = lax.broadcasted_iota(I32, (cap, LANES), 1)
    lane_f = lane.astype(F32)
    ones8 = jnp.ones((8, LANES), BF16)
    jr = lax.broadcasted_iota(I32, (nb, nb), 0)
    jc = lax.broadcasted_iota(I32, (nb, nb), 1)
    upper = jnp.where(jr <= jc, 1.0, 0.0).astype(BF16)
    sel3 = self32.reshape(e, nb, LANES)
    incl3 = incl.reshape(e, nb, LANES)

    def to_row(col):
        return _dot_nt(ones8, jnp.where(lane == 0, col, 0.0).astype(BF16))

    for ex in range(e):
        tot_row = _dot_nt(ones8, sel3[ex].astype(BF16))[0:1]
        end_row = _dot(jnp.broadcast_to(tot_row, (8, nb)).astype(BF16), upper)[0:1]
        start_row = end_row - tot_row
        jk = jnp.sum(jnp.where(end_row <= kcol, 1.0, 0.0), axis=1, keepdims=True)
        onehot_j = jnp.where(jrow == jk, 1.0, 0.0)
        kk = kcol[:, 0:1] - jnp.sum(onehot_j * start_row, axis=1, keepdims=True)
        d = _dot(onehot_j.astype(BF16), incl3[ex].astype(BF16))
        tl = jnp.sum(jnp.where(d <= kk, 1.0, 0.0), axis=1, keepdims=True)
        dg = _dot3_right(onehot_j.astype(BF16), a[ex])
        gate = jnp.sum(jnp.where(lane_f == tl, dg, 0.0), axis=1, keepdims=True)
        idx_ref[0, ex] = (to_row(jk) * float(LANES) + to_row(tl)).astype(I32)
        g1, g2, g3 = _split3(gate)
        gate_ref[0, ex] = (to_row(g1.astype(F32)) + to_row(g2.astype(F32))) + to_row(g3.astype(F32))


def _select(aff_t, cap):
    bsz, e, nb, _ = aff_t.shape
    kern = functools.partial(_select_kernel, cap=cap, n_tok=nb * LANES)
    blk = pl.BlockSpec((1, e, nb, LANES), lambda b: (b, 0, 0, 0))
    slot = pl.BlockSpec((1, e, 8, cap), lambda b: (b, 0, 0, 0))
    return pl.pallas_call(
        kern,
        grid=(bsz,),
        in_specs=[blk],
        out_specs=[blk, slot, slot],
        out_shape=[jax.ShapeDtypeStruct(aff_t.shape, I32),
                   jax.ShapeDtypeStruct((bsz, e, 8, cap), I32),
                   jax.ShapeDtypeStruct((bsz, e, 8, cap), F32)],
        compiler_params=_cparams(("parallel",)),
        name="ec_select",
    )(aff_t)


SLAB_PITCH = 24


def _row_copy(hx_hbm, xbuf, sem, src_row, dst_row, nslab):
    return pltpu.make_async_copy(hx_hbm.at[pl.ds(pl.multiple_of(src_row * nslab, nslab), nslab)],
                                 xbuf.at[pl.ds(pl.multiple_of(dst_row * SLAB_PITCH, 8), nslab)], sem)


def _dispatch_kernel(idx_ref, hx_hbm, o_ref, xbuf, sems, *, cap, n_tok, n_exp, nslab, n_steps):
    step = pl.program_id(0) * n_exp + pl.program_id(1)

    def issue_all(s):
        half = s % 2
        row0 = (s // n_exp) * n_tok

        def issue(k, _):
            _row_copy(hx_hbm, xbuf, sems.at[half], row0 + idx_ref[s * cap + k], half * cap + k, nslab).start()
            return 0

        lax.fori_loop(0, cap, issue, 0, unroll=8)

    @pl.when(step == 0)
    def _():
        issue_all(step)

    @pl.when(step + 1 < n_steps)
    def _():
        issue_all(step + 1)

    half = step % 2

    def drain(k, _):
        _row_copy(hx_hbm, xbuf, sems.at[half], 0, half * cap + k, nslab).wait()
        return 0

    lax.fori_loop(0, cap, drain, 0, unroll=8)
    first = pl.multiple_of(half * cap * SLAB_PITCH, 8)
    for c in range(nslab):
        o_ref[0, 0, :, c * LANES:(c + 1) * LANES] = (
            xbuf[pl.ds(first + c, cap, stride=SLAB_PITCH), :].astype(o_ref.dtype))


def _dispatch(idx_flat, hx_slabs, bsz, n_exp, cap, nslab):
    n_tok = hx_slabs.shape[0] // (bsz * nslab)
    d = nslab * LANES
    kern = functools.partial(_dispatch_kernel, cap=cap, n_tok=n_tok, n_exp=n_exp, nslab=nslab,
                             n_steps=bsz * n_exp)
    return pl.pallas_call(
        kern,
        grid_spec=pltpu.PrefetchScalarGridSpec(
            num_scalar_prefetch=1,
            grid=(bsz, n_exp),
            in_specs=[pl.BlockSpec(memory_space=pl.ANY)],
            out_specs=pl.BlockSpec((1, 1, cap, d), lambda b, e, idx: (b, e, 0, 0)),
            scratch_shapes=[pltpu.VMEM((2 * cap * SLAB_PITCH, LANES), F32),
                            pltpu.SemaphoreType.DMA((2,))]),
        out_shape=jax.ShapeDtypeStruct((bsz, n_exp, cap, d), BF16),
        compiler_params=_cparams(("arbitrary", "arbitrary")),
        name="ec_dispatch",
    )(idx_flat, hx_slabs)


FFN_COL_BLOCKS = 4


def _ffn_kernel(x_ref, g_ref, wg_ref, wu_ref, wd_ref, o_ref, acc):
    f = pl.program_id(2)

    @pl.when(f == 0)
    def _():
        acc[...] = jnp.zeros_like(acc)

    x = x_ref[0, 0]
    a = _dot(x, wg_ref[0].astype(BF16))
    u = _dot(x, wu_ref[0].astype(BF16))
    hmid = (_silu(a) * u).astype(BF16)
    wd = wd_ref[0].astype(BF16)
    cw = acc.shape[1] // FFN_COL_BLOCKS
    for cb in range(FFN_COL_BLOCKS):
        acc[:, cb * cw:(cb + 1) * cw] += _dot(hmid, wd[:, cb * cw:(cb + 1) * cw])

    @pl.when(f == pl.num_programs(2) - 1)
    def _():
        g = g_ref[0, 0]
        cap = g.shape[0]
        for c in range(o_ref.shape[2]):
            o_ref[0, 0, c, 0:cap, :] = acc[:, c * LANES:(c + 1) * LANES] * g
            o_ref[0, 0, c, cap:, :] = jnp.zeros((o_ref.shape[3] - cap, LANES), F32)


PLANE_PAD = 8


def _expert_ffn(xs, gates, w_gate, w_up, w_down):
    bsz, n_exp, cap, d = xs.shape
    ff = w_gate.shape[2]
    tf = 256
    rows = cap + PLANE_PAD
    return pl.pallas_call(
        _ffn_kernel,
        grid=(n_exp, bsz, ff // tf),
        in_specs=[pl.BlockSpec((1, 1, cap, d), lambda e, b, f: (b, e, 0, 0)),
                  pl.BlockSpec((1, 1, cap, 1), lambda e, b, f: (b, e, 0, 0)),
                  pl.BlockSpec((1, d, tf), lambda e, b, f: (e, 0, f)),
                  pl.BlockSpec((1, d, tf), lambda e, b, f: (e, 0, f)),
                  pl.BlockSpec((1, tf, d), lambda e, b, f: (e, f, 0))],
        out_specs=pl.BlockSpec((1, 1, d // LANES, rows, LANES), lambda e, b, f: (b, e, 0, 0, 0)),
        out_shape=jax.ShapeDtypeStruct((bsz, n_exp, d // LANES, rows, LANES), F32),
        scratch_shapes=[pltpu.VMEM((cap, d), F32)],
        compiler_params=_cparams(("parallel", "parallel", "arbitrary")),
        name="expert_ffn",
    )(xs, gates, w_gate, w_up, w_down)


COMB_UNROLL = 8


def _combine_kernel(idx_ref, off_ref, y_ref, o_ref, *, cap, nb, n_exp, tok_part, n_part):
    b = pl.program_id(0)
    part = pl.program_id(1)
    ex = pl.program_id(3)

    @pl.when(ex == 0)
    def _():
        o_ref[...] = jnp.zeros_like(o_ref)

    be = b * n_exp + ex
    blocks_per_part = nb // n_part
    k_lo = off_ref[be * nb + part * blocks_per_part]
    k_hi = jnp.where(part == n_part - 1, cap,
                     off_ref[be * nb + jnp.minimum((part + 1) * blocks_per_part, nb - 1)])
    t0 = part * tok_part

    def add_rows(k, n):
        toks = [idx_ref[be * cap + k + i] - t0 for i in range(n)]
        rows = [o_ref[0, toks[i]] + y_ref[0, 0, :, k + i, :] for i in range(n)]
        for i in range(n):
            o_ref[0, toks[i]] = rows[i]

    n_full = (k_hi - k_lo) // COMB_UNROLL

    def body(i, _):
        add_rows(k_lo + i * COMB_UNROLL, COMB_UNROLL)
        return 0

    lax.fori_loop(0, n_full, body, 0)

    def tail(k, _):
        add_rows(k, 1)
        return 0

    lax.fori_loop(k_lo + n_full * COMB_UNROLL, k_hi, tail, 0)


def _combine(idx_flat, off_flat, y, n_tok, cap):
    bsz, n_exp, nslab, rows, _ = y.shape
    nb = n_tok // TOK_BLK
    n_part = 2
    sblk = 8
    kern = functools.partial(_combine_kernel, cap=cap, nb=nb, n_exp=n_exp, tok_part=n_tok // n_part,
                             n_part=n_part)
    return pl.pallas_call(
        kern,
        grid_spec=pltpu.PrefetchScalarGridSpec(
            num_scalar_prefetch=2,
            grid=(bsz, n_part, nslab // sblk, n_exp),
            in_specs=[pl.BlockSpec((1, 1, sblk, rows, LANES), lambda b, p, c, e, idx, off: (b, e, c, 0, 0))],
            out_specs=pl.BlockSpec((1, n_tok // n_part, sblk, LANES),
                                   lambda b, p, c, e, idx, off: (b, p, c, 0))),
        out_shape=jax.ShapeDtypeStruct((bsz, n_tok, nslab, LANES), F32),
        compiler_params=_cparams(("parallel", "parallel", "parallel", "arbitrary")),
        name="ec_combine",
    )(idx_flat, off_flat, y)


def _final_kernel(x1_ref, moe_ref, g_ref, mod_ref, o_ref):
    tm = x1_ref.shape[0]
    nslab = moe_ref.shape[0] // tm
    m = jnp.concatenate([moe_ref[pl.ds(c, tm, stride=nslab), :] for c in range(nslab)], axis=-1)
    r = m * lax.rsqrt(jnp.mean(m * m, axis=-1, keepdims=True) + EPS) * g_ref[...]
    o_ref[...] = x1_ref[...] + mod_ref[0] * r


def _final(x1, moe, g3, g2, rows_per_batch):
    n, d = x1.shape
    tm = 512
    tiles_per_batch = rows_per_batch // tm
    return pl.pallas_call(
        _final_kernel,
        grid=(n // tm,),
        in_specs=[pl.BlockSpec((tm, d), lambda i: (i, 0)),
                  pl.BlockSpec((tm * (d // LANES), LANES), lambda i: (i, 0)),
                  pl.BlockSpec((1, d), lambda i: (0, 0)),
                  pl.BlockSpec((1, 1, d), lambda i: (i // tiles_per_batch, 0, 0))],
        out_specs=pl.BlockSpec((tm, d), lambda i: (i, 0)),
        out_shape=jax.ShapeDtypeStruct((n, d), F32),
        compiler_params=_cparams(("parallel",)),
        name="final_residual",
    )(x1, moe, g3, g2)


def _layer(x, c, ctx, c_ctx, ada_w, ada_b, norm_g, w_in, s5_a_re, s5_a_im, s5_log_dt, s5_b_re,
           s5_b_im, s5_c_re, s5_c_im, s5_d, s5_glu_w, s5_glu_b, ml_conv_w, ml_conv_b, ml_gate_b,
           ml_norm_g, w_out, router_w, exp_w_gate, exp_w_up, exp_w_down):
    bsz, seq, d = x.shape
    lctx = ctx.shape[1]
    ds5 = s5_d.shape[0]
    dml = ml_norm_g.shape[0]
    ngroups = ds5 // S5_GROUP
    rows = seq // GRID_W
    n = bsz * seq
    cap = EC_FACTOR * seq // N_EXPERTS
    assert cap >= TOK_BLK + 16 and seq % DISP_TOK == 0 and lctx % ML_T == 0 and seq % ML_T == 0

    cond = jnp.concatenate([c, c_ctx[None], jnp.zeros((8 - bsz - 1, d), F32)], axis=0)
    mod = _modulation(cond, ada_w, ada_b).reshape(8, N_MOD, d)
    sh1, sc1, g1, sh2, sc2, g2 = [mod[:, i] for i in range(N_MOD)]

    nmain = ds5 + 4 * dml
    w_s5 = w_in[:, :ds5].astype(BF16)
    w_ml = w_in[:, ds5:nmain].astype(BF16)
    w_gate = jnp.pad(w_in[:, nmain:], ((0, 0), (0, LANES - 4 * ML_HEADS))).astype(BF16)
    g0 = norm_g[0:1]
    sc, sh = sc1[:, None], sh1[:, None]
    of_sample = lambda b: b
    of_ctx = lambda b: bsz
    uc = _in_projection_s5(ctx, g0, sc, sh, of_ctx, w_s5)
    ltot = seq + lctx
    ml5, gates5, ux = _in_projection_ml(x, ctx, g0, sc, sh, bsz, w_ml, w_gate, w_s5)

    m, bp, cpt, avec = _s5_operators(s5_a_re, s5_a_im, s5_log_dt, s5_b_re, s5_b_im, s5_c_re, s5_c_im)
    ys = _s5_scan(ux, uc, m, bp, cpt, avec).reshape(n, ds5)

    ml = ml5.reshape(bsz, ltot, 4 * dml)
    gates = gates5.reshape(bsz, ltot, LANES)
    conv_w8 = jnp.pad(ml_conv_w, ((0, 8 - CONV_K), (0, 0)))
    qscale = jnp.concatenate([jnp.full((1, dml), (dml // ML_HEADS) ** -0.5, F32),
                              jnp.ones((1, dml), F32)], axis=1)
    qk = _conv_silu(ml, conv_w8, ml_conv_b[None], qscale, (0, seq), (seq, ltot))
    gbias = ml_gate_b.reshape(1, 4 * ML_HEADS)
    gates_t = gates[..., :4 * ML_HEADS].swapaxes(1, 2)
    hf, hb = _mlstm_scan(qk, ml, gates, gates_t, gbias, gbias.reshape(4 * ML_HEADS, 1), lctx // ML_T)
    mlo = _ml_post(hf.reshape(ml5.shape[:4] + (dml,)), hb.reshape(ml5.shape[:4] + (dml,)), ml5,
                   ml_norm_g[None], seq).reshape(n, dml)

    mod3 = jnp.stack([g1[:bsz], sc2[:bsz], sh2[:bsz]], axis=1)
    mod3 = jnp.concatenate([mod3, jnp.zeros((bsz, 5, d), F32)], axis=1)
    rw = jnp.pad(router_w, ((0, 0), (0, LANES - N_EXPERTS)))
    rw_hi = rw.astype(BF16)
    rw_lo = (rw - rw_hi.astype(F32)).astype(BF16)
    x1, hx2, aff = _mixer_out(ys, ux.reshape(n, ds5), mlo, x.reshape(n, d), seq, s5_d[None],
                              s5_glu_w.astype(BF16), s5_glu_b[None], w_out.astype(BF16), norm_g, mod3,
                              jnp.concatenate([rw_hi, rw_lo], axis=1))

    nb = seq // TOK_BLK
    aff16 = aff[:, :N_EXPERTS].reshape(bsz, seq, N_EXPERTS)
    aff_t = aff16.swapaxes(1, 2).reshape(bsz, N_EXPERTS, nb, LANES)
    off, idx, gates = _select(aff_t, cap)
    off_flat = off[..., 0].reshape(-1)
    idx_flat = idx[:, :, 0].reshape(-1)
    xs = _dispatch(idx_flat, hx2, bsz, N_EXPERTS, cap, d // LANES)
    ye = _expert_ffn(xs, gates[:, :, 0, :, None], exp_w_gate, exp_w_up, exp_w_down)
    moe = _combine(idx_flat, off_flat, ye, seq, cap)
    out = _final(x1, moe.reshape(n * (d // LANES), LANES), norm_g[3:4], g2[:bsz, None], seq)
    return out.reshape(bsz, seq, d)


def kernel(x, c, ctx, c_ctx, ada_w, ada_b, norm_g, w_in, s5_a_re, s5_a_im, s5_log_dt, s5_b_re, s5_b_im, s5_c_re, s5_c_im, s5_d, s5_glu_w, s5_glu_b, ml_conv_w, ml_conv_b, ml_gate_b, ml_norm_g, w_out, router_w, exp_w_gate, exp_w_up, exp_w_down):
    depth = ada_w.shape[0]
    assert depth == 1, "context outputs are only produced when another layer follows"
    li = 0
    return _layer(x, c, ctx, c_ctx, ada_w[li], ada_b[li], norm_g[li], w_in[li], s5_a_re[li],
                  s5_a_im[li], s5_log_dt[li], s5_b_re[li], s5_b_im[li], s5_c_re[li], s5_c_im[li],
                  s5_d[li], s5_glu_w[li], s5_glu_b[li], ml_conv_w[li], ml_conv_b[li], ml_gate_b[li],
                  ml_norm_g[li], w_out[li], router_w[li], exp_w_gate[li], exp_w_up[li],
                  exp_w_down[li])
```

```python
import functools
import math

import jax
import jax.numpy as jnp
from jax import lax
from jax.experimental import pallas as pl
from jax.experimental.pallas import tpu as pltpu

F32 = jnp.float32
BF16 = jnp.bfloat16
I32 = jnp.int32

EPS = 1e-6
GRID_W = 64
S5_GROUP = 16
S5_STATE = 64
S5_T = 16
ML_HEADS = 4
ML_T = 256
CONV_K = 5
N_EXPERTS = 16
EC_FACTOR = 2
N_MOD = 6
LANES = 128
TOK_BLK = 128
DISP_TOK = 512
DISP_W = DISP_TOK + 8
VMEM_LIMIT = 56 * 1024 * 1024


def _cparams(sem):
    return pltpu.CompilerParams(dimension_semantics=sem, vmem_limit_bytes=VMEM_LIMIT)


def _sigmoid(x):
    return 1.0 / (1.0 + jnp.exp(-x))


def _silu(x):
    return x * _sigmoid(x)


def _log_sigmoid(x):
    return jnp.minimum(x, 0.0) - jnp.log(1.0 + jnp.exp(-jnp.abs(x)))


def _split3(x):
    x1 = x.astype(BF16)
    r = x - x1.astype(F32)
    x2 = r.astype(BF16)
    x3 = (r - x2.astype(F32)).astype(BF16)
    return x1, x2, x3


def _dot(a, b):
    return jnp.dot(a, b, preferred_element_type=F32)


def _dot_nt(a, b):
    return lax.dot_general(a, b, (((1,), (1,)), ((), ())), preferred_element_type=F32)


def _dot_tn(a, b):
    return lax.dot_general(a, b, (((0,), (0,)), ((), ())), preferred_element_type=F32)


def _dot3_left(a_f32, b_bf16):
    a1, a2, a3 = _split3(a_f32)
    return _dot(a1, b_bf16) + _dot(a2, b_bf16) + _dot(a3, b_bf16)


def _dot3_right(a_bf16, b_f32):
    b1, b2, b3 = _split3(b_f32)
    return _dot(a_bf16, b1) + _dot(a_bf16, b2) + _dot(a_bf16, b3)


def _mod_kernel(c_ref, w_ref, b_ref, o_ref):
    c = c_ref[...]
    s = _silu(c).astype(BF16)
    o_ref[...] = _dot(s, w_ref[...].astype(BF16)) + b_ref[...]


def _modulation(cond8, ada_w, ada_b):
    d, n = ada_w.shape
    tn = 1024
    return pl.pallas_call(
        _mod_kernel,
        grid=(n // tn,),
        in_specs=[pl.BlockSpec((8, d), lambda j: (0, 0)),
                  pl.BlockSpec((d, tn), lambda j: (0, j)),
                  pl.BlockSpec((1, tn), lambda j: (0, j))],
        out_specs=pl.BlockSpec((8, tn), lambda j: (0, j)),
        out_shape=jax.ShapeDtypeStruct((8, n), F32),
        compiler_params=_cparams(("parallel",)),
        name="modulation",
    )(cond8, ada_w, ada_b.reshape(1, n))


def _norm_mod_f32(x, g, sc, sh):
    ms = jnp.mean(x * x, axis=-1, keepdims=True)
    h = x * lax.rsqrt(ms + EPS) * g
    return h * (1.0 + sc) + sh


def _norm_mod(x, g, sc, sh):
    return _norm_mod_f32(x, g, sc, sh).astype(BF16)


def _inproj_s5_kernel(x_ref, g_ref, sc_ref, sh_ref, w_ref, o_ref, h_scr):
    @pl.when(pl.program_id(2) == 0)
    def _():
        h_scr[...] = _norm_mod(x_ref[0], g_ref[...], sc_ref[0], sh_ref[0])

    o_ref[0] = _dot(h_scr[...], w_ref[...])


def _in_projection_s5(x, g0, sc, sh, mod_of_batch, w_s5):
    bsz, length, d = x.shape
    ds5 = w_s5.shape[1]
    tm = min(1024, length)
    tn = 512
    return pl.pallas_call(
        _inproj_s5_kernel,
        grid=(bsz, length // tm, ds5 // tn),
        in_specs=[pl.BlockSpec((1, tm, d), lambda b, i, j: (b, i, 0)),
                  pl.BlockSpec((1, d), lambda b, i, j: (0, 0)),
                  pl.BlockSpec((1, 1, d), lambda b, i, j: (mod_of_batch(b), 0, 0)),
                  pl.BlockSpec((1, 1, d), lambda b, i, j: (mod_of_batch(b), 0, 0)),
                  pl.BlockSpec((d, tn), lambda b, i, j: (0, j))],
        out_specs=pl.BlockSpec((1, tm, tn), lambda b, i, j: (b, i, j)),
        out_shape=jax.ShapeDtypeStruct((bsz, length, ds5), F32),
        scratch_shapes=[pltpu.VMEM((tm, d), BF16)],
        compiler_params=_cparams(("parallel", "parallel", "arbitrary")),
        name="in_projection_s5",
    )(x, g0, sc, sh, w_s5)


def _inproj_ml_kernel(x_ref, c_ref, g_ref, sc_ref, sh_ref, csc_ref, csh_ref, w_ref, wg_ref, ws_ref,
                      o_ref, og_ref, u_ref, h_scr, x_scr, hr_scr, *, n_ml_blocks):
    rb = o_ref.shape[3]
    nlb = x_ref.shape[2] // LANES
    j = pl.program_id(2)

    @pl.when(j == 0)
    def _():
        for r in range(rb):
            h = _norm_mod_f32(x_ref[0, r * GRID_W:(r + 1) * GRID_W, :], g_ref[...], sc_ref[0], sh_ref[0])
            hr_scr[r * GRID_W:(r + 1) * GRID_W, :] = h.astype(BF16)
            for c in range(nlb):
                x_scr[c, r * ROW_PITCH:r * ROW_PITCH + GRID_W, :] = h[:, c * LANES:(c + 1) * LANES]
        for w in range(GRID_W):
            hw = jnp.concatenate([x_scr[c, pl.ds(w, rb, stride=ROW_PITCH), :] for c in range(nlb)],
                                 axis=-1)
            h_scr[w * rb:(w + 1) * rb, :] = hw.astype(BF16)
        for s in range(c_ref.shape[1]):
            h_scr[(GRID_W + s) * rb:(GRID_W + s + 1) * rb, :] = _norm_mod(
                c_ref[0, s, 0], g_ref[...], csc_ref[0], csh_ref[0])
        og_ref[0] = _dot(h_scr[...], wg_ref[...]).reshape(og_ref.shape[1:])

    @pl.when(j < n_ml_blocks)
    def _():
        o_ref[0] = _dot(h_scr[...], w_ref[...]).reshape(o_ref.shape[1:]).astype(o_ref.dtype)

    @pl.when(j >= n_ml_blocks)
    def _():
        u_ref[0] = _dot(hr_scr[...], ws_ref[...])


ML_RB = 16
ROW_PITCH = GRID_W + 8


def _in_projection_ml(x, ctx, g0, sc, sh, ctx_mod, w_ml, w_gate, w_s5):
    bsz, length, d = x.shape
    lctx = ctx.shape[1]
    nml = w_ml.shape[1]
    ds5 = w_s5.shape[1]
    tn = 512
    tns = 256
    nmb, nsb = nml // tn, ds5 // tns
    rows = length // GRID_W
    rblocks = rows // ML_RB
    nslot_c = lctx // rows
    assert nslot_c * rows == lctx and rblocks * ML_RB == rows
    n_slots = GRID_W + nslot_c
    tm = ML_RB * GRID_W
    ctx5 = ctx.reshape(bsz, nslot_c, rblocks, ML_RB, d)
    kern = functools.partial(_inproj_ml_kernel, n_ml_blocks=nmb)
    return pl.pallas_call(
        kern,
        grid=(bsz, rblocks, nmb + nsb),
        in_specs=[pl.BlockSpec((1, tm, d), lambda b, i, j: (b, i, 0)),
                  pl.BlockSpec((1, nslot_c, 1, ML_RB, d), lambda b, i, j: (b, 0, i, 0, 0)),
                  pl.BlockSpec((1, d), lambda b, i, j: (0, 0)),
                  pl.BlockSpec((1, 1, d), lambda b, i, j: (b, 0, 0)),
                  pl.BlockSpec((1, 1, d), lambda b, i, j: (b, 0, 0)),
                  pl.BlockSpec((1, 1, d), lambda b, i, j: (ctx_mod, 0, 0)),
                  pl.BlockSpec((1, 1, d), lambda b, i, j: (ctx_mod, 0, 0)),
                  pl.BlockSpec((d, tn), lambda b, i, j: (0, jnp.minimum(j, nmb - 1))),
                  pl.BlockSpec((d, LANES), lambda b, i, j: (0, 0)),
                  pl.BlockSpec((d, tns), lambda b, i, j: (0, jnp.maximum(j - nmb, 0)))],
        out_specs=[pl.BlockSpec((1, n_slots, 1, ML_RB, tn),
                                lambda b, i, j: (b, 0, i, 0, jnp.minimum(j, nmb - 1))),
                   pl.BlockSpec((1, n_slots, 1, ML_RB, LANES), lambda b, i, j: (b, 0, i, 0, 0)),
                   pl.BlockSpec((1, tm, tns), lambda b, i, j: (b, i, jnp.maximum(j - nmb, 0)))],
        out_shape=[jax.ShapeDtypeStruct((bsz, n_slots, rblocks, ML_RB, nml), BF16),
                   jax.ShapeDtypeStruct((bsz, n_slots, rblocks, ML_RB, LANES), F32),
                   jax.ShapeDtypeStruct((bsz, length, ds5), F32)],
        scratch_shapes=[pltpu.VMEM((n_slots * ML_RB, d), BF16),
                        pltpu.VMEM((d // LANES, ML_RB * ROW_PITCH, LANES), F32),
                        pltpu.VMEM((tm, d), BF16)],
        compiler_params=_cparams(("parallel", "parallel", "arbitrary")),
        name="in_projection_ml",
    )(x, ctx5, g0, sc, sh, sc, sh, w_ml, w_gate, w_s5)


def _dot_nt6(a, b):
    a1, a2, a3 = _split3(a)
    b1, b2, b3 = _split3(b)
    return (_dot_nt(a1, b1) + (_dot_nt(a1, b2) + _dot_nt(a2, b1))
            + (_dot_nt(a1, b3) + _dot_nt(a3, b1) + _dot_nt(a2, b2)))


S5_LANES = 4 * S5_STATE


def _s5ops_kernel(prm_ref, bt_ref, c_ref, rexp_ref, rc_ref, place_ref, m_ref, bp_ref, cpt_ref, av_ref):
    t = S5_T
    prm = prm_ref[0]
    lr, li = jnp.minimum(prm[0:1], -1e-4), prm[1:2]
    dt = jnp.exp(prm[2:3])
    zr, zi = lr * dt, li * dt
    tau = lax.broadcasted_iota(I32, (32, S5_LANES), 0).astype(F32)
    mag, ang = jnp.exp(tau * zr), tau * zi
    pwr, pwi = mag * jnp.cos(ang), mag * jnp.sin(ang)
    xr, xi = pwr[1:2] - 1.0, pwi[1:2]
    den = lr * lr + li * li
    qr, qi = (xr * lr + xi * li) / den, (xi * lr - xr * li) / den
    btr, bti = bt_ref[0, 0], bt_ref[0, 1]
    bbr, bbi = qr * btr - qi * bti, qr * bti + qi * btr
    cr, ci = c_ref[0, 0], c_ref[0, 1]

    lane = lax.broadcasted_iota(I32, (t * S5_GROUP, S5_LANES), 1)
    fwd = ((lane // S5_STATE) % 2) == 0

    def expand(r, x):
        return _dot3_right(r, x)

    e_t = [expand(rexp_ref[0], z) for z in (pwr, pwi)]
    e_r = [expand(rexp_ref[1], z) for z in (pwr, pwi)]
    e_p = [expand(rexp_ref[2], z) for z in (pwr, pwi)]
    e_q = [expand(rexp_ref[3], z) for z in (pwr, pwi)]
    pk = [jnp.where(fwd, e_t[i], e_r[i]) for i in range(2)]
    pb = [jnp.where(fwd, e_r[i], e_t[i]) for i in range(2)]
    pc = [jnp.where(fwd, e_p[i], e_q[i]) for i in range(2)]
    cxr, cxi = expand(rc_ref[...], cr), expand(rc_ref[...], ci)
    bxr, bxi = expand(rc_ref[...], bbr), expand(rc_ref[...], bbi)

    wkr, wki = pk[0] * cxr - pk[1] * cxi, pk[0] * cxi + pk[1] * cxr
    row_blk = lax.broadcasted_iota(I32, (4 * S5_GROUP, S5_LANES), 0) // S5_GROUP
    lane_blk = lax.broadcasted_iota(I32, (4 * S5_GROUP, S5_LANES), 1) // S5_STATE
    own = row_blk == lane_blk
    bmr = jnp.where(own, jnp.concatenate([bbr] * 4, axis=0), 0.0)
    bmi = jnp.where(own, jnp.concatenate([bbi] * 4, axis=0), 0.0)
    kt = _dot_nt6(bmr, wkr) - _dot_nt6(bmi, wki)

    ln = lax.broadcasted_iota(I32, (S5_GROUP, t * S5_GROUP), 1)
    for gi in range(2):
        ktf = kt[(2 * gi) * S5_GROUP:(2 * gi + 1) * S5_GROUP]
        ktb = kt[(2 * gi + 1) * S5_GROUP:(2 * gi + 2) * S5_GROUP]
        blocks = []
        for s in range(t):
            f = ktf if s == 0 else pltpu.roll(ktf, S5_GROUP * s, axis=1)
            sh = (t * S5_GROUP - S5_GROUP * (t - 1 - s)) % (t * S5_GROUP)
            bk = ktb if sh == 0 else pltpu.roll(ktb, sh, axis=1)
            blocks.append(jnp.where(ln >= S5_GROUP * s, f, 0.0)
                          + jnp.where(ln < S5_GROUP * (s + 1), bk, 0.0))
        m_ref[0, gi] = jnp.concatenate(blocks, axis=0).astype(BF16)

    inj = jnp.concatenate([pb[0] * bxr - pb[1] * bxi, pb[0] * bxi + pb[1] * bxr], axis=-1).astype(BF16)
    out = jnp.concatenate([pc[0] * cxr - pc[1] * cxi, -(pc[0] * cxi + pc[1] * cxr)], axis=-1).astype(BF16)
    for gi in range(2):
        bp_ref[0, gi * t * S5_GROUP:(gi + 1) * t * S5_GROUP, :] = _dot(inj, place_ref[gi]).astype(BF16)
        cpt_ref[0, gi * t * S5_GROUP:(gi + 1) * t * S5_GROUP, :] = _dot(out, place_ref[gi]).astype(BF16)
    a_t = jnp.concatenate([pwr[t:t + 8], pwi[t:t + 8]], axis=-1)
    av_ref[0] = _dot3_left(a_t, place_ref[0]) + _dot3_left(a_t, place_ref[1])


def _s5_operators(a_re, a_im, log_dt, b_re, b_im, c_re, c_im):
    t = S5_T
    g, p = a_re.shape[1], a_re.shape[2]
    npair = g // 2

    def lanes(z):
        return z.reshape(2, npair, 2, p).transpose(1, 2, 0, 3).reshape(npair, 4 * p)

    prm = jnp.stack([lanes(a_re.astype(F32)), lanes(a_im.astype(F32)),
                     lanes(jnp.broadcast_to(log_dt.astype(F32)[..., None], a_re.shape))], axis=1)
    prm = jnp.concatenate([prm, jnp.zeros((npair, 5, 4 * p), F32)], axis=1)

    def tile_d(z):
        z = z.reshape(npair, 2, S5_GROUP, p).transpose(0, 2, 1, 3)
        return jnp.broadcast_to(z[:, :, :, None, :], (npair, S5_GROUP, 2, 2, p)).reshape(npair, S5_GROUP, 4 * p)

    bt = jnp.stack([tile_d(b_re.astype(F32).swapaxes(1, 2)), tile_d(b_im.astype(F32).swapaxes(1, 2))], axis=1)
    ct = jnp.stack([tile_d(c_re.astype(F32)), tile_d(c_im.astype(F32))], axis=1)

    row = jnp.arange(t * S5_GROUP)
    tau, ch = row // S5_GROUP, row % S5_GROUP
    e = jnp.arange(32)
    rexp = jnp.stack([(expo[:, None] == e[None, :]) for expo in (tau, t - 1 - tau, tau + 1, t - tau)]
                     ).astype(BF16)
    rc = (ch[:, None] == jnp.arange(S5_GROUP)[None, :]).astype(BF16)
    src = jnp.arange(2 * 4 * p)
    reim, gi_s, d_s, p_s = src // (4 * p), (src // (2 * p)) % 2, (src // p) % 2, src % p
    dst = d_s * (4 * p) + reim * (2 * p) + gi_s * p + p_s
    col = jnp.arange(8 * p)
    place = jnp.stack([((dst[:, None] == col[None, :]) & (gi_s[:, None] == gi)) for gi in range(2)]
                      ).astype(BF16)
    blk3 = lambda shape: pl.BlockSpec((1,) + shape, lambda i: (i,) + (0,) * len(shape))
    full = lambda arr: pl.BlockSpec(arr.shape, lambda i: (0,) * arr.ndim)
    m, bp, cpt, av = pl.pallas_call(
        _s5ops_kernel,
        grid=(npair,),
        in_specs=[blk3((8, 4 * p)), blk3((2, S5_GROUP, 4 * p)), blk3((2, S5_GROUP, 4 * p)),
                  full(rexp), full(rc), full(place)],
        out_specs=[blk3((2, t * S5_GROUP, t * S5_GROUP)), blk3((2 * t * S5_GROUP, 8 * p)),
                   blk3((2 * t * S5_GROUP, 8 * p)), blk3((8, 8 * p))],
        out_shape=[jax.ShapeDtypeStruct((npair, 2, t * S5_GROUP, t * S5_GROUP), BF16),
                   jax.ShapeDtypeStruct((npair, 2 * t * S5_GROUP, 8 * p), BF16),
                   jax.ShapeDtypeStruct((npair, 2 * t * S5_GROUP, 8 * p), BF16),
                   jax.ShapeDtypeStruct((npair, 8, 8 * p), F32)],
        compiler_params=_cparams(("parallel",)),
        name="s5_operators",
    )(prm, bt, ct, rexp, rc, place)
    avec = av[:, 0].reshape(npair, 4, 2 * p)
    return m, bp, cpt, jnp.concatenate([avec, jnp.zeros_like(avec)], axis=1)


def _s5_permutation():
    i = jnp.arange(8 * 8 * S5_GROUP)
    t8, j, c = i // (8 * S5_GROUP), (i // S5_GROUP) % 8, i % S5_GROUP
    dst = j * (8 * S5_GROUP) + t8 * S5_GROUP + c
    e1 = (dst[:, None] == i[None, :]).astype(BF16)
    return e1, e1.T


S5_PAIRS = 4


def _s5_kernel(ux_ref, uc_ref, e1_ref, e1t_ref, m_ref, bp_ref, cp_ref, av_ref, y_ref, s_scr, hp_scr,
               tab_scr, *, cc, xc):
    nc = cc + xc

    def chunk_rows(ref, nchunks):
        cols = [ref[0, pl.ds(t, nchunks, stride=S5_T), :].astype(BF16) for t in range(S5_T)]
        return [_dot(jnp.concatenate(cols[8 * h:8 * h + 8], axis=-1), e1_ref[...]).astype(BF16)
                for h in range(2)]

    def pair_lanes(halves, p):
        return jnp.concatenate([halves[h][:, j * LANES:(j + 1) * LANES]
                                for j in (2 * p, 2 * p + 1) for h in range(2)], axis=-1)

    pc = chunk_rows(uc_ref, cc)
    px = chunk_rows(ux_ref, xc)
    for p in range(S5_PAIRS):
        s_scr[0:cc, p * 512:(p + 1) * 512] = _dot(pair_lanes(pc, p), bp_ref[p])
        s_scr[cc:nc, p * 512:(p + 1) * 512] = _dot(pair_lanes(px, p), bp_ref[p])

    av = av_ref[...]
    sub = lax.broadcasted_iota(I32, (8, 128), 0)

    def cmul(ar, ai, br, bi):
        return ar * br - ai * bi, ar * bi + ai * br

    for p in range(S5_PAIRS):
        for d in range(2):
            a1 = (av[p, 2 * d:2 * d + 1], av[p, 2 * d + 1:2 * d + 2])
            a2 = cmul(*a1, *a1)
            a4 = cmul(*a2, *a2)
            pw_re, pw_im = jnp.zeros((8, 128), F32), jnp.zeros((8, 128), F32)
            cur = a1
            for k in range(8):
                r = k if d == 0 else 7 - k
                pw_re = jnp.where(sub == r, cur[0], pw_re)
                pw_im = jnp.where(sub == r, cur[1], pw_im)
                cur = cmul(*cur, *a1)
            for q, z in enumerate((a1, a2, a4)):
                tab_scr[p, d, 2 * q] = jnp.broadcast_to(z[0], (8, 128))
                tab_scr[p, d, 2 * q + 1] = jnp.broadcast_to(z[1], (8, 128))
            tab_scr[p, d, 6] = pw_re
            tab_scr[p, d, 7] = pw_im

    def scan8(s_re, s_im, h_re, h_im, p, d):
        back = d == 1
        x_re, x_im = s_re, s_im
        for q, k in enumerate((1, 2, 4)):
            keep = (sub < 8 - k) if back else (sub >= k)
            sh = (8 - k) if back else k
            y_re = jnp.where(keep, pltpu.roll(x_re, sh, axis=0), 0.0)
            y_im = jnp.where(keep, pltpu.roll(x_im, sh, axis=0), 0.0)
            m_re, m_im = cmul(tab_scr[p, d, 2 * q], tab_scr[p, d, 2 * q + 1], y_re, y_im)
            x_re, x_im = x_re + m_re, x_im + m_im
        c_re, c_im = cmul(tab_scr[p, d, 6], tab_scr[p, d, 7], h_re, h_im)
        x_re, x_im = x_re + c_re, x_im + c_im
        edge = 7 if back else 0
        sh = 7 if back else 1
        e_re = jnp.where(sub == edge, h_re, pltpu.roll(x_re, sh, axis=0))
        e_im = jnp.where(sub == edge, h_im, pltpu.roll(x_im, sh, axis=0))
        last = 0 if back else 7
        return e_re, e_im, x_re[last:last + 1], x_im[last:last + 1]

    def step(row_f, row_b, carry):
        rf = pl.multiple_of(row_f, 8)
        rb = pl.multiple_of(row_b, 8)
        new = []
        for p in range(S5_PAIRS):
            hr, hi, gr, gi = carry[4 * p:4 * p + 4]
            c0 = p * 512
            p_re, p_im, hr, hi = scan8(s_scr[pl.ds(rf, 8), c0:c0 + 128], s_scr[pl.ds(rf, 8), c0 + 128:c0 + 256],
                                       hr, hi, p, 0)
            hp_scr[pl.ds(rf, 8), c0:c0 + 128] = p_re
            hp_scr[pl.ds(rf, 8), c0 + 128:c0 + 256] = p_im
            p_re, p_im, gr, gi = scan8(s_scr[pl.ds(rb, 8), c0 + 256:c0 + 384], s_scr[pl.ds(rb, 8), c0 + 384:c0 + 512],
                                       gr, gi, p, 1)
            hp_scr[pl.ds(rb, 8), c0 + 256:c0 + 384] = p_re
            hp_scr[pl.ds(rb, 8), c0 + 384:c0 + 512] = p_im
            new += [hr, hi, gr, gi]
        return tuple(new)

    z = jnp.zeros((1, 128), F32)
    carry = lax.fori_loop(0, cc // 8, lambda i, c: step(8 * i, cc - 8 - 8 * i, c), (z,) * (4 * S5_PAIRS))
    lax.fori_loop(0, xc // 8, lambda i, c: step(cc + 8 * i, nc - 8 - 8 * i, c), carry)

    ys = []
    for p in range(S5_PAIRS):
        ux = pair_lanes(px, p)
        y = _dot_nt(hp_scr[cc:nc, p * 512:(p + 1) * 512].astype(BF16), cp_ref[p])
        ys.append(y + jnp.concatenate([_dot(ux[:, 0:256], m_ref[p, 0]), _dot(ux[:, 256:512], m_ref[p, 1])],
                                      axis=-1))
    for h in range(2):
        yh = jnp.concatenate([ys[j // 2][:, (j % 2) * 256 + h * LANES:(j % 2) * 256 + (h + 1) * LANES]
                              for j in range(8)], axis=-1).astype(BF16)
        r = _dot(yh, e1t_ref[...])
        for t8 in range(8):
            y_ref[0, pl.ds(8 * h + t8, xc, stride=S5_T), :] = r[:, t8 * LANES:(t8 + 1) * LANES]


def _s5_scan(ux, uc, m, bp, cp, avec):
    bsz, seq, ds5 = ux.shape
    lctx = uc.shape[1]
    cc, xc = lctx // S5_T, seq // S5_T
    assert cc % 8 == 0 and xc % 8 == 0
    e1, e1t = _s5_permutation()
    kern = functools.partial(_s5_kernel, cc=cc, xc=xc)
    nblk = ds5 // LANES
    const = lambda j, b: (0, 0)
    return pl.pallas_call(
        kern,
        grid=(nblk, bsz),
        in_specs=[pl.BlockSpec((1, seq, LANES), lambda j, b: (b, 0, j)),
                  pl.BlockSpec((1, lctx, LANES), lambda j, b: (b, 0, j)),
                  pl.BlockSpec(e1.shape, const),
                  pl.BlockSpec(e1.shape, const),
                  pl.BlockSpec((S5_PAIRS, 2, 256, 256), lambda j, b: (j, 0, 0, 0)),
                  pl.BlockSpec((S5_PAIRS, 512, 512), lambda j, b: (j, 0, 0)),
                  pl.BlockSpec((S5_PAIRS, 512, 512), lambda j, b: (j, 0, 0)),
                  pl.BlockSpec((S5_PAIRS, 8, 128), lambda j, b: (j, 0, 0))],
        out_specs=pl.BlockSpec((1, seq, LANES), lambda j, b: (b, 0, j)),
        out_shape=jax.ShapeDtypeStruct((bsz, seq, ds5), F32),
        scratch_shapes=[pltpu.VMEM((cc + xc, S5_PAIRS * 512), F32),
                        pltpu.VMEM((cc + xc, S5_PAIRS * 512), F32),
                        pltpu.VMEM((S5_PAIRS, 2, 8, 8, 128), F32)],
        compiler_params=_cparams(("parallel", "parallel")),
        name="s5_scan",
    )(ux, uc, e1, e1t, m, bp, cp, avec)


def _conv_kernel(x_ref, p_ref, n_ref, w_ref, b_ref, sc_ref, o_ref, scr, *, rows, seq_tiles):
    i = pl.program_id(1)
    first = functools.reduce(jnp.logical_or, [i == s for s in seq_tiles[0]])
    last = functools.reduce(jnp.logical_or, [i == s for s in seq_tiles[1]])
    pm = jnp.where(first, 0.0, 1.0)
    nm = jnp.where(last, 0.0, 1.0)
    scr[0:8, :] = p_ref[0, 8:16, :].astype(F32) * pm
    scr[8:8 + rows, :] = x_ref[0].astype(F32)
    scr[8 + rows:16 + rows, :] = n_ref[0, 0:8, :].astype(F32) * nm
    w = w_ref[...]
    acc = scr[6:6 + rows, :] * w[0:1]
    for j in range(1, CONV_K):
        acc = acc + scr[6 + j:6 + j + rows, :] * w[j:j + 1]
    z = acc + b_ref[...]
    o_ref[0] = (_silu(z) * sc_ref[...]).astype(o_ref.dtype)


def _conv_silu(ml, conv_w8, conv_b, scale, seq_starts, seq_ends):
    bsz, ltot, _ = ml.shape
    rows = 256
    ct = 1024
    nch = conv_b.shape[-1]
    nt = ltot // rows
    hb = rows // 16
    nhb = ltot // 16
    seq_tiles = (tuple(s // rows for s in seq_starts), tuple(e // rows - 1 for e in seq_ends))
    kern = functools.partial(_conv_kernel, rows=rows, seq_tiles=seq_tiles)
    return pl.pallas_call(
        kern,
        grid=(bsz, nt, nch // ct),
        in_specs=[pl.BlockSpec((1, rows, ct), lambda b, i, c: (b, i, c)),
                  pl.BlockSpec((1, 16, ct), lambda b, i, c: (b, jnp.maximum(i * hb - 1, 0), c)),
                  pl.BlockSpec((1, 16, ct), lambda b, i, c: (b, jnp.minimum((i + 1) * hb, nhb - 1), c)),
                  pl.BlockSpec((8, ct), lambda b, i, c: (0, c)),
                  pl.BlockSpec((1, ct), lambda b, i, c: (0, c)),
                  pl.BlockSpec((1, ct), lambda b, i, c: (0, c))],
        out_specs=pl.BlockSpec((1, rows, ct), lambda b, i, c: (b, i, c)),
        out_shape=jax.ShapeDtypeStruct((bsz, ltot, nch), BF16),
        scratch_shapes=[pltpu.VMEM((rows + 16, ct), F32)],
        compiler_params=_cparams(("parallel", "parallel", "parallel")),
        name="conv_silu",
    )(ml, ml, ml, conv_w8, conv_b, scale)


def _mlstm_chain_step(q, k, v, i_col, f_col, i_row, b_col, b_row, mask, c_ref, n_ref, m_ref, idx):
    m_st = m_ref[idx]
    d = jnp.where(mask, b_col - b_row + i_row, -jnp.inf)
    inter = b_col + m_st
    m_t = jnp.maximum(inter, jnp.max(d, axis=-1, keepdims=True))
    w = jnp.exp(d - m_t)
    s_inter = jnp.exp(inter - m_t)
    s = _dot_nt(q, k) * w
    c_st = c_ref[idx]
    n_st = n_ref[idx]
    num = _dot(s.astype(BF16), v) + s_inter * _dot_nt(q, c_st.astype(BF16))
    qf = q.astype(F32)
    den = jnp.sum(s, axis=-1, keepdims=True) + s_inter * jnp.sum(qf * n_st, axis=-1, keepdims=True)
    h = num / jnp.maximum(jnp.abs(den), jnp.exp(-m_t))
    total = jnp.sum(f_col, axis=0, keepdims=True)
    d_end = total - b_col + i_col
    m_new = jnp.maximum(total + m_st, jnp.max(d_end, axis=0, keepdims=True))
    w_end = jnp.exp(d_end - m_new)
    dec = jnp.exp(total + m_st - m_new)
    vw = (v.astype(F32) * w_end).astype(BF16)
    c_ref[idx] = dec * c_st + _dot_tn(vw, k)
    n_ref[idx] = dec * n_st + jnp.sum(k.astype(F32) * w_end, axis=0, keepdims=True)
    m_ref[idx] = m_new
    return h


def _mlstm_kernel(qf_ref, kf_ref, vf_ref, gf_ref, gtf_ref, qb_ref, kb_ref, vb_ref, gb_ref, gtb_ref,
                  gbias_ref, gbias_t_ref, hf_ref, hb_ref, c_scr, n_scr, m_scr, *, dh):
    @pl.when(pl.program_id(1) == 0)
    def _():
        c_scr[...] = jnp.zeros_like(c_scr)
        n_scr[...] = jnp.zeros_like(n_scr)
        m_scr[...] = jnp.zeros_like(m_scr)

    bsz, t = qf_ref.shape[0], qf_ref.shape[1]
    rid = lax.broadcasted_iota(I32, (t, t), 0)
    cid = lax.broadcasted_iota(I32, (t, t), 1)
    lower = jnp.where(rid >= cid, 1.0, 0.0).astype(BF16)
    upper = jnp.where(cid >= rid, 1.0, 0.0).astype(BF16)

    for bb in range(bsz):
        for d, (q_ref, k_ref, v_ref, g_ref, gt_ref, h_ref) in enumerate(
                [(qf_ref, kf_ref, vf_ref, gf_ref, gtf_ref, hf_ref),
                 (qb_ref, kb_ref, vb_ref, gb_ref, gtb_ref, hb_ref)]):
            backward = d == 1
            g = g_ref[bb, :, 0:16] + gbias_ref[...]
            gt = gt_ref[bb] + gbias_t_ref[...]
            lf = _log_sigmoid(g)
            lft = _log_sigmoid(gt)
            cum_col = _dot3_right(upper if backward else lower, lf)
            cum_row = _dot3_left(lft, lower if backward else upper)
            for hd in range(ML_HEADS):
                ci = 2 * ML_HEADS * d + hd
                cf = ci + ML_HEADS
                sl = slice(hd * dh, (hd + 1) * dh)
                h = _mlstm_chain_step(
                    q_ref[bb, :, sl], k_ref[bb, :, sl], v_ref[bb, :, sl],
                    g[:, ci:ci + 1], lf[:, cf:cf + 1], gt[ci:ci + 1, :],
                    cum_col[:, cf:cf + 1], cum_row[cf:cf + 1, :],
                    (cid >= rid) if backward else (rid >= cid),
                    c_scr, n_scr, m_scr, (bb * 2 + d) * ML_HEADS + hd)
                h_ref[bb, :, sl] = h.astype(h_ref.dtype)


def _mlstm_scan(qk, ml, gates, gates_t, gbias, gbias_t, n_ctx_chunks):
    bsz, ltot, dml2 = qk.shape
    dml = dml2 // 2
    dh = dml // ML_HEADS
    t = ML_T
    nchunk = ltot // t
    cc = n_ctx_chunks
    nx = nchunk - cc

    def fwd(s):
        return jnp.where(s < cc, nx + s, s - cc)

    def bwd(s):
        return jnp.where(s < cc, nx + cc - 1 - s, nx - 1 - (s - cc))

    bps = 1

    def specs(cmap):
        return [pl.BlockSpec((bps, t, dml), lambda b, s: (b, cmap(s), 0)),
                pl.BlockSpec((bps, t, dml), lambda b, s: (b, cmap(s), 1)),
                pl.BlockSpec((bps, t, dml), lambda b, s: (b, cmap(s), 2)),
                pl.BlockSpec((bps, t, LANES), lambda b, s: (b, cmap(s), 0)),
                pl.BlockSpec((bps, 16, t), lambda b, s: (b, 0, cmap(s)))]

    kern = functools.partial(_mlstm_kernel, dh=dh)
    nchain = bps * 2 * ML_HEADS
    return pl.pallas_call(
        kern,
        grid=(bsz // bps, nchunk),
        in_specs=specs(fwd) + specs(bwd) + [pl.BlockSpec((1, 16), lambda b, s: (0, 0)),
                                            pl.BlockSpec((16, 1), lambda b, s: (0, 0))],
        out_specs=[pl.BlockSpec((bps, t, dml), lambda b, s: (b, fwd(s), 0)),
                   pl.BlockSpec((bps, t, dml), lambda b, s: (b, bwd(s), 0))],
        out_shape=[jax.ShapeDtypeStruct((bsz, ltot, dml), BF16)] * 2,
        scratch_shapes=[pltpu.VMEM((nchain, dh, dh), F32),
                        pltpu.VMEM((nchain, 1, dh), F32),
                        pltpu.VMEM((nchain, 1, 1), F32)],
        compiler_params=_cparams(("parallel", "arbitrary")),
        name="mlstm_scan",
    )(qk, qk, ml, gates, gates_t, qk, qk, ml, gates, gates_t, gbias, gbias_t)


COL_PITCH = ML_RB + 8


def _mlpost_kernel(hf_ref, hb_ref, o_ref, mg_ref, out_ref, scr, *, dh):
    nw, _, rb, dml = hf_ref.shape[1:]
    h = (hf_ref[0].astype(F32) + hb_ref[0].astype(F32)).reshape(nw * rb, dml)
    parts = []
    for hd in range(ML_HEADS):
        hh = h[:, hd * dh:(hd + 1) * dh]
        parts.append(hh * lax.rsqrt(jnp.mean(hh * hh, axis=-1, keepdims=True) + EPS))
    hn = jnp.concatenate(parts, axis=-1) * mg_ref[...]
    res = hn * _sigmoid(o_ref[0].astype(F32).reshape(nw * rb, dml))
    nlb = dml // LANES
    for c in range(nlb):
        for w in range(nw):
            scr[c, w * COL_PITCH:w * COL_PITCH + rb, :] = res[w * rb:(w + 1) * rb, c * LANES:(c + 1) * LANES]
    for r in range(rb):
        row = jnp.concatenate([scr[c, pl.ds(r, nw, stride=COL_PITCH), :] for c in range(nlb)], axis=-1)
        out_ref[0, r * nw:(r + 1) * nw, :] = row.astype(out_ref.dtype)


def _ml_post(hf5, hb5, ml5, ml_g, seq):
    bsz, _, rblocks, rb, dml = hf5.shape
    kern = functools.partial(_mlpost_kernel, dh=dml // ML_HEADS)
    blk = (1, GRID_W, 1, rb, dml)
    return pl.pallas_call(
        kern,
        grid=(bsz, rblocks),
        in_specs=[pl.BlockSpec(blk, lambda b, i: (b, 0, i, 0, 0)),
                  pl.BlockSpec(blk, lambda b, i: (b, 0, i, 0, 0)),
                  pl.BlockSpec(blk, lambda b, i: (b, 0, i, 0, 3)),
                  pl.BlockSpec((1, dml), lambda b, i: (0, 0))],
        out_specs=pl.BlockSpec((1, rb * GRID_W, dml), lambda b, i: (b, i, 0)),
        out_shape=jax.ShapeDtypeStruct((bsz, seq, dml), BF16),
        scratch_shapes=[pltpu.VMEM((dml // LANES, GRID_W * COL_PITCH, LANES), F32)],
        compiler_params=_cparams(("parallel", "parallel")),
        name="mlstm_post",
    )(hf5, hb5, ml5, ml_g)


def _gelu_tanh(y):
    return 0.5 * y * (1.0 + jnp.tanh(math.sqrt(2.0 / math.pi) * (y + 0.044715 * (y * y * y))))


def _mixout_kernel(ys_ref, u_ref, ml_ref, x_ref, dsk_ref, gw_ref, gb_ref, wo_ref, ng_ref, mod_ref,
                   rw_ref, x1_ref, hx_ref, aff_ref, *, n_exp):
    y = ys_ref[...] + dsk_ref[...] * u_ref[...]
    ge = _gelu_tanh(y)
    z = _dot(ge.astype(BF16), gw_ref[...]) + gb_ref[...]
    s5 = (ge * _sigmoid(z)).astype(BF16)
    yx = _dot(jnp.concatenate([s5, ml_ref[...]], axis=-1), wo_ref[...])
    ng = ng_ref[...]
    mod = mod_ref[0]
    r1 = yx * lax.rsqrt(jnp.mean(yx * yx, axis=-1, keepdims=True) + EPS) * ng[1:2]
    x1 = x_ref[...] + mod[0:1] * r1
    x1_ref[...] = x1
    hx = x1 * lax.rsqrt(jnp.mean(x1 * x1, axis=-1, keepdims=True) + EPS) * ng[2:3]
    hx = hx * (1.0 + mod[1:2]) + mod[2:3]
    nslab = hx.shape[1] // LANES
    for cb in range(nslab):
        hx_ref[pl.ds(cb, hx.shape[0], stride=nslab), :] = hx[:, cb * LANES:(cb + 1) * LANES]
    h1, h2, _ = _split3(hx)
    rw2 = rw_ref[...]
    l1 = _dot(h1, rw2)
    logits = (l1[:, 0:LANES] + l1[:, LANES:2 * LANES]) + _dot(h2, rw2[:, 0:LANES])
    lane = lax.broadcasted_iota(I32, logits.shape, 1)
    logits = jnp.where(lane < n_exp, logits, -jnp.inf)
    e = jnp.exp(logits - jnp.max(logits, axis=-1, keepdims=True))
    aff_ref[...] = e / jnp.sum(e, axis=-1, keepdims=True)


def _mixer_out(ys, u, mlo, x2d, rows_per_batch, dsk, glu_w, glu_b, w_out, norm_g, mod3, rw2):
    n, d = x2d.shape
    ds5 = ys.shape[1]
    dml = mlo.shape[1]
    tm = 256
    tiles_per_batch = rows_per_batch // tm
    kern = functools.partial(_mixout_kernel, n_exp=N_EXPERTS)
    row = lambda i: (i, 0)
    const = lambda i: (0, 0)
    return pl.pallas_call(
        kern,
        grid=(n // tm,),
        in_specs=[pl.BlockSpec((tm, ds5), row),
                  pl.BlockSpec((tm, ds5), row),
                  pl.BlockSpec((tm, dml), row),
                  pl.BlockSpec((tm, d), row),
                  pl.BlockSpec((1, ds5), const),
                  pl.BlockSpec((ds5, ds5), const),
                  pl.BlockSpec((1, ds5), const),
                  pl.BlockSpec((d, d), const),
                  pl.BlockSpec((4, d), const),
                  pl.BlockSpec((1, 8, d), lambda i: (i // tiles_per_batch, 0, 0)),
                  pl.BlockSpec((d, 2 * LANES), const)],
        out_specs=[pl.BlockSpec((tm, d), row),
                   pl.BlockSpec((tm * (d // LANES), LANES), row),
                   pl.BlockSpec((tm, LANES), row)],
        out_shape=[jax.ShapeDtypeStruct((n, d), F32),
                   jax.ShapeDtypeStruct((n * (d // LANES), LANES), F32),
                   jax.ShapeDtypeStruct((n, LANES), F32)],
        compiler_params=_cparams(("parallel",)),
        name="mixer_out",
    )(ys, u, mlo, x2d, dsk, glu_w, glu_b, w_out, norm_g, mod3, rw2)


def _count(mask):
    c = jnp.sum(jnp.where(mask, 1.0, 0.0), axis=2, keepdims=True)
    return jnp.sum(c, axis=1, keepdims=True)


SELECT_BISECTIONS = 48


def _pow2_neg(n):
    p = jnp.ones(n.shape, F32)
    for i in range(7):
        p = p * jnp.where(((n >> i) & 1) == 1, 2.0 ** -(2 ** i), 1.0)
    return p


def _select_kernel(aff_ref, off_ref, idx_ref, gate_ref, *, cap, n_tok):
    a = aff_ref[0]
    e, nb, _ = a.shape

    def enough(thr):
        return _count(a >= thr) >= cap

    ex_lo = jnp.full((e, 1, 1), -127, I32)
    ex_hi = jnp.full((e, 1, 1), 1, I32)
    for _ in range(7):
        ex_mid = (ex_lo + ex_hi) >> 1
        ok = enough(_pow2_neg(-ex_mid))
        ex_lo = jnp.where(ok, ex_mid, ex_lo)
        ex_hi = jnp.where(ok, ex_hi, ex_mid)
    lo = jnp.where(ex_lo == -127, 0.0, _pow2_neg(jnp.minimum(-ex_lo, 126)))
    hi = jnp.where(ex_hi == 1, 2.0, _pow2_neg(jnp.maximum(-ex_hi, 0)))
    for _ in range(SELECT_BISECTIONS):
        mid = lo + 0.5 * (hi - lo)
        ok = enough(mid)
        lo = jnp.where(ok, mid, lo)
        hi = jnp.where(ok, hi, mid)
    gt = a >= hi
    tie = (a >= lo) & (a < hi)
    need = cap - _count(gt)
    tok = (lax.broadcasted_iota(I32, a.shape, 1) * LANES + lax.broadcasted_iota(I32, a.shape, 2))
    cut = jnp.zeros((e, 1, 1), I32)
    for bit in range(n_tok.bit_length() - 1, -1, -1):
        cand = cut | (1 << bit)
        cut = jnp.where(_count(tie & (tok < cand)) < need, cand, cut)
    sel = gt | (tie & (tok <= cut))
    self32 = jnp.where(sel, 1.0, 0.0).reshape(e * nb, LANES)
    r = lax.broadcasted_iota(I32, (LANES, LANES), 0)
    c = lax.broadcasted_iota(I32, (LANES, LANES), 1)
    incl = _dot(self32.astype(BF16), jnp.where(r <= c, 1.0, 0.0).astype(BF16))
    tot = jnp.broadcast_to(incl[:, LANES - 1:LANES], (e * nb, LANES)).astype(BF16)
    rr = lax.broadcasted_iota(I32, (e * nb, e * nb), 0)
    cc = lax.broadcasted_iota(I32, (e * nb, e * nb), 1)
    same = (rr // nb) == (cc // nb)
    before = jnp.where(same & (cc < rr), 1.0, 0.0).astype(BF16)
    off = _dot(before, tot)
    off_ref[0] = off.astype(I32).reshape(e, nb, LANES)

    kcol = lax.broadcasted_iota(I32, (cap, nb), 0).astype(F32)
    jrow = lax.broadcasted_iota(I32, (cap, nb), 1).astype(F32)
    lane = lax.broadcasted_iota(I32, (cap, LANES), 1)
    lane_f = lane.astype(F32)
    ones8 = jnp.ones((8, LANES), BF16)
    jr = lax.broadcasted_iota(I32, (nb, nb), 0)
    jc = lax.broadcasted_iota(I32, (nb, nb), 1)
    upper = jnp.where(jr <= jc, 1.0, 0.0).astype(BF16)
    sel3 = self32.reshape(e, nb, LANES)
    incl3 = incl.reshape(e, nb, LANES)

    def to_row(col):
        return _dot_nt(ones8, jnp.where(lane == 0, col, 0.0).astype(BF16))

    for ex in range(e):
        tot_row = _dot_nt(ones8, sel3[ex].astype(BF16))[0:1]
        end_row = _dot(jnp.broadcast_to(tot_row, (8, nb)).astype(BF16), upper)[0:1]
        start_row = end_row - tot_row
        jk = jnp.sum(jnp.where(end_row <= kcol, 1.0, 0.0), axis=1, keepdims=True)
        onehot_j = jnp.where(jrow == jk, 1.0, 0.0)
        kk = kcol[:, 0:1] - jnp.sum(onehot_j * start_row, axis=1, keepdims=True)
        d = _dot(onehot_j.astype(BF16), incl3[ex].astype(BF16))
        tl = jnp.sum(jnp.where(d <= kk, 1.0, 0.0), axis=1, keepdims=True)
        dg = _dot3_right(onehot_j.astype(BF16), a[ex])
        gate = jnp.sum(jnp.where(lane_f == tl, dg, 0.0), axis=1, keepdims=True)
        idx_ref[0, ex] = (to_row(jk) * float(LANES) + to_row(tl)).astype(I32)
        g1, g2, g3 = _split3(gate)
        gate_ref[0, ex] = (to_row(g1.astype(F32)) + to_row(g2.astype(F32))) + to_row(g3.astype(F32))


def _select(aff_t, cap):
    bsz, e, nb, _ = aff_t.shape
    kern = functools.partial(_select_kernel, cap=cap, n_tok=nb * LANES)
    blk = pl.BlockSpec((1, e, nb, LANES), lambda b: (b, 0, 0, 0))
    slot = pl.BlockSpec((1, e, 8, cap), lambda b: (b, 0, 0, 0))
    return pl.pallas_call(
        kern,
        grid=(bsz,),
        in_specs=[blk],
        out_specs=[blk, slot, slot],
        out_shape=[jax.ShapeDtypeStruct(aff_t.shape, I32),
                   jax.ShapeDtypeStruct((bsz, e, 8, cap), I32),
                   jax.ShapeDtypeStruct((bsz, e, 8, cap), F32)],
        compiler_params=_cparams(("parallel",)),
        name="ec_select",
    )(aff_t)


SLAB_PITCH = 24


def _row_copy(hx_hbm, xbuf, sem, src_start, dst_row, nslab):
    return pltpu.make_async_copy(hx_hbm.at[pl.ds(pl.multiple_of(src_start, nslab), nslab)],
                                 xbuf.at[pl.ds(pl.multiple_of(dst_row * SLAB_PITCH, 8), nslab)], sem)


def _dispatch_kernel(idx_ref, hx_hbm, o_ref, xbuf, sems, *, cap, n_tok, n_exp, nslab, n_steps):
    step = pl.program_id(0) * n_exp + pl.program_id(1)

    def issue_all(s):
        half = s % 2

        def issue(k, _):
            _row_copy(hx_hbm, xbuf, sems.at[half], idx_ref[s * cap + k], half * cap + k, nslab).start()
            return 0

        lax.fori_loop(0, cap, issue, 0, unroll=8)

    @pl.when(step == 0)
    def _():
        issue_all(step)

    @pl.when(step + 1 < n_steps)
    def _():
        issue_all(step + 1)

    half = step % 2

    def drain(k, _):
        _row_copy(hx_hbm, xbuf, sems.at[half], 0, half * cap + k, nslab).wait()
        return 0

    lax.fori_loop(0, cap, drain, 0, unroll=8)
    first = pl.multiple_of(half * cap * SLAB_PITCH, 8)
    for c in range(nslab):
        o_ref[0, 0, :, c * LANES:(c + 1) * LANES] = (
            xbuf[pl.ds(first + c, cap, stride=SLAB_PITCH), :].astype(o_ref.dtype))


def _dispatch(idx_flat, hx_slabs, bsz, n_exp, cap, nslab):
    n_tok = hx_slabs.shape[0] // (bsz * nslab)
    d = nslab * LANES
    kern = functools.partial(_dispatch_kernel, cap=cap, n_tok=n_tok, n_exp=n_exp, nslab=nslab,
                             n_steps=bsz * n_exp)
    return pl.pallas_call(
        kern,
        grid_spec=pltpu.PrefetchScalarGridSpec(
            num_scalar_prefetch=1,
            grid=(bsz, n_exp),
            in_specs=[pl.BlockSpec(memory_space=pl.ANY)],
            out_specs=pl.BlockSpec((1, 1, cap, d), lambda b, e, idx: (b, e, 0, 0)),
            scratch_shapes=[pltpu.VMEM((2 * cap * SLAB_PITCH, LANES), F32),
                            pltpu.SemaphoreType.DMA((2,))]),
        out_shape=jax.ShapeDtypeStruct((bsz, n_exp, cap, d), BF16),
        compiler_params=_cparams(("arbitrary", "arbitrary")),
        name="ec_dispatch",
    )(idx_flat, hx_slabs)


FFN_COL_BLOCKS = 4


def _ffn_kernel(x_ref, g_ref, wg_ref, wu_ref, wd_ref, o_ref, acc):
    f = pl.program_id(2)

    @pl.when(f == 0)
    def _():
        acc[...] = jnp.zeros_like(acc)

    x = x_ref[0, 0]
    a = _dot(x, wg_ref[0].astype(BF16))
    u = _dot(x, wu_ref[0].astype(BF16))
    hmid = (_silu(a) * u).astype(BF16)
    wd = wd_ref[0].astype(BF16)
    cw = acc.shape[1] // FFN_COL_BLOCKS
    for cb in range(FFN_COL_BLOCKS):
        acc[:, cb * cw:(cb + 1) * cw] += _dot(hmid, wd[:, cb * cw:(cb + 1) * cw])

    @pl.when(f == pl.num_programs(2) - 1)
    def _():
        g = g_ref[0, 0]
        cap = g.shape[0]
        for c in range(o_ref.shape[2]):
            o_ref[0, 0, c, 0:cap, :] = acc[:, c * LANES:(c + 1) * LANES] * g
            o_ref[0, 0, c, cap:, :] = jnp.zeros((o_ref.shape[3] - cap, LANES), F32)


PLANE_PAD = 8


def _expert_ffn(xs, gates, w_gate, w_up, w_down):
    bsz, n_exp, cap, d = xs.shape
    ff = w_gate.shape[2]
    tf = 256
    rows = cap + PLANE_PAD
    return pl.pallas_call(
        _ffn_kernel,
        grid=(n_exp, bsz, ff // tf),
        in_specs=[pl.BlockSpec((1, 1, cap, d), lambda e, b, f: (b, e, 0, 0)),
                  pl.BlockSpec((1, 1, cap, 1), lambda e, b, f: (b, e, 0, 0)),
                  pl.BlockSpec((1, d, tf), lambda e, b, f: (e, 0, f)),
                  pl.BlockSpec((1, d, tf), lambda e, b, f: (e, 0, f)),
                  pl.BlockSpec((1, tf, d), lambda e, b, f: (e, f, 0))],
        out_specs=pl.BlockSpec((1, 1, d // LANES, rows, LANES), lambda e, b, f: (b, e, 0, 0, 0)),
        out_shape=jax.ShapeDtypeStruct((bsz, n_exp, d // LANES, rows, LANES), F32),
        scratch_shapes=[pltpu.VMEM((cap, d), F32)],
        compiler_params=_cparams(("parallel", "parallel", "arbitrary")),
        name="expert_ffn",
    )(xs, gates, w_gate, w_up, w_down)


COMB_UNROLL = 8


def _combine_kernel(idx_ref, off_ref, y_ref, o_ref, *, cap, nb, n_exp, tok_part, n_part):
    b = pl.program_id(0)
    part = pl.program_id(1)
    ex = pl.program_id(3)

    @pl.when(ex == 0)
    def _():
        o_ref[...] = jnp.zeros_like(o_ref)

    be = b * n_exp + ex
    blocks_per_part = nb // n_part
    k_lo = off_ref[be * nb + part * blocks_per_part]
    k_hi = jnp.where(part == n_part - 1, cap,
                     off_ref[be * nb + jnp.minimum((part + 1) * blocks_per_part, nb - 1)])
    t0 = part * tok_part

    def add_rows(k, n):
        toks = [idx_ref[be * cap + k + i] - t0 for i in range(n)]
        rows = [o_ref[0, toks[i]] + y_ref[0, 0, :, k + i, :] for i in range(n)]
        for i in range(n):
            o_ref[0, toks[i]] = rows[i]

    n_full = (k_hi - k_lo) // COMB_UNROLL

    def body(i, _):
        add_rows(k_lo + i * COMB_UNROLL, COMB_UNROLL)
        return 0

    lax.fori_loop(0, n_full, body, 0)

    def tail(k, _):
        add_rows(k, 1)
        return 0

    lax.fori_loop(k_lo + n_full * COMB_UNROLL, k_hi, tail, 0)


def _combine(idx_flat, off_flat, y, n_tok, cap):
    bsz, n_exp, nslab, rows, _ = y.shape
    nb = n_tok // TOK_BLK
    n_part = 2
    sblk = 8
    kern = functools.partial(_combine_kernel, cap=cap, nb=nb, n_exp=n_exp, tok_part=n_tok // n_part,
                             n_part=n_part)
    return pl.pallas_call(
        kern,
        grid_spec=pltpu.PrefetchScalarGridSpec(
            num_scalar_prefetch=2,
            grid=(bsz, n_part, nslab // sblk, n_exp),
            in_specs=[pl.BlockSpec((1, 1, sblk, rows, LANES), lambda b, p, c, e, idx, off: (b, e, c, 0, 0))],
            out_specs=pl.BlockSpec((1, n_tok // n_part, sblk, LANES),
                                   lambda b, p, c, e, idx, off: (b, p, c, 0))),
        out_shape=jax.ShapeDtypeStruct((bsz, n_tok, nslab, LANES), F32),
        compiler_params=_cparams(("parallel", "parallel", "parallel", "arbitrary")),
        name="ec_combine",
    )(idx_flat, off_flat, y)


def _final_kernel(x1_ref, moe_ref, g_ref, mod_ref, o_ref):
    tm = x1_ref.shape[0]
    nslab = moe_ref.shape[0] // tm
    m = jnp.concatenate([moe_ref[pl.ds(c, tm, stride=nslab), :] for c in range(nslab)], axis=-1)
    r = m * lax.rsqrt(jnp.mean(m * m, axis=-1, keepdims=True) + EPS) * g_ref[...]
    o_ref[...] = x1_ref[...] + mod_ref[0] * r


def _final(x1, moe, g3, g2, rows_per_batch):
    n, d = x1.shape
    tm = 512
    tiles_per_batch = rows_per_batch // tm
    return pl.pallas_call(
        _final_kernel,
        grid=(n // tm,),
        in_specs=[pl.BlockSpec((tm, d), lambda i: (i, 0)),
                  pl.BlockSpec((tm * (d // LANES), LANES), lambda i: (i, 0)),
                  pl.BlockSpec((1, d), lambda i: (0, 0)),
                  pl.BlockSpec((1, 1, d), lambda i: (i // tiles_per_batch, 0, 0))],
        out_specs=pl.BlockSpec((tm, d), lambda i: (i, 0)),
        out_shape=jax.ShapeDtypeStruct((n, d), F32),
        compiler_params=_cparams(("parallel",)),
        name="final_residual",
    )(x1, moe, g3, g2)


def _layer(x, c, ctx, c_ctx, ada_w, ada_b, norm_g, w_in, s5_a_re, s5_a_im, s5_log_dt, s5_b_re,
           s5_b_im, s5_c_re, s5_c_im, s5_d, s5_glu_w, s5_glu_b, ml_conv_w, ml_conv_b, ml_gate_b,
           ml_norm_g, w_out, router_w, exp_w_gate, exp_w_up, exp_w_down):
    bsz, seq, d = x.shape
    lctx = ctx.shape[1]
    ds5 = s5_d.shape[0]
    dml = ml_norm_g.shape[0]
    rows = seq // GRID_W
    n = bsz * seq
    cap = EC_FACTOR * seq // N_EXPERTS
    assert cap >= TOK_BLK + 16 and seq % DISP_TOK == 0 and lctx % ML_T == 0 and seq % ML_T == 0

    cond = jnp.concatenate([c, c_ctx[None], jnp.zeros((8 - bsz - 1, d), F32)], axis=0)
    mod = _modulation(cond, ada_w, ada_b).reshape(8, N_MOD, d)
    sh1, sc1, g1, sh2, sc2, g2 = [mod[:, i] for i in range(N_MOD)]

    nmain = ds5 + 4 * dml
    w_s5 = w_in[:, :ds5].astype(BF16)
    w_ml = w_in[:, ds5:nmain].astype(BF16)
    w_gate = jnp.pad(w_in[:, nmain:], ((0, 0), (0, LANES - 4 * ML_HEADS))).astype(BF16)
    g0 = norm_g[0:1]
    sc, sh = sc1[:, None], sh1[:, None]
    uc = _in_projection_s5(ctx, g0, sc, sh, lambda b: bsz, w_s5)
    ltot = seq + lctx
    ml5, gates5, ux = _in_projection_ml(x, ctx, g0, sc, sh, bsz, w_ml, w_gate, w_s5)

    m, bp, cpt, avec = _s5_operators(s5_a_re, s5_a_im, s5_log_dt, s5_b_re, s5_b_im, s5_c_re, s5_c_im)
    ys = _s5_scan(ux, uc, m, bp, cpt, avec).reshape(n, ds5)

    ml = ml5.reshape(bsz, ltot, 4 * dml)
    gates = gates5.reshape(bsz, ltot, LANES)
    conv_w8 = jnp.pad(ml_conv_w, ((0, 8 - CONV_K), (0, 0)))
    qscale = jnp.concatenate([jnp.full((1, dml), (dml // ML_HEADS) ** -0.5, F32),
                              jnp.ones((1, dml), F32)], axis=1)
    qk = _conv_silu(ml, conv_w8, ml_conv_b[None], qscale, (0, seq), (seq, ltot))
    gbias = ml_gate_b.reshape(1, 4 * ML_HEADS)
    gates_t = gates[..., :4 * ML_HEADS].swapaxes(1, 2)
    hf, hb = _mlstm_scan(qk, ml, gates, gates_t, gbias, gbias.reshape(4 * ML_HEADS, 1), lctx // ML_T)
    mlo = _ml_post(hf.reshape(ml5.shape[:4] + (dml,)), hb.reshape(ml5.shape[:4] + (dml,)), ml5,
                   ml_norm_g[None], seq).reshape(n, dml)

    mod3 = jnp.stack([g1[:bsz], sc2[:bsz], sh2[:bsz]], axis=1)
    mod3 = jnp.concatenate([mod3, jnp.zeros((bsz, 5, d), F32)], axis=1)
    rw = jnp.pad(router_w, ((0, 0), (0, LANES - N_EXPERTS)))
    rw_hi = rw.astype(BF16)
    rw_lo = (rw - rw_hi.astype(F32)).astype(BF16)
    x1, hx2, aff = _mixer_out(ys, ux.reshape(n, ds5), mlo, x.reshape(n, d), seq, s5_d[None],
                              s5_glu_w.astype(BF16), s5_glu_b[None], w_out.astype(BF16), norm_g, mod3,
                              jnp.concatenate([rw_hi, rw_lo], axis=1))

    nb = seq // TOK_BLK
    aff16 = aff[:, :N_EXPERTS].reshape(bsz, seq, N_EXPERTS)
    aff_t = aff16.swapaxes(1, 2).reshape(bsz, N_EXPERTS, nb, LANES)
    off, idx, gates = _select(aff_t, cap)
    off_flat = off[..., 0].reshape(-1)
    idx_flat = idx[:, :, 0].reshape(-1)
    nslab = d // LANES
    src_start = ((idx[:, :, 0] + (jnp.arange(bsz, dtype=I32) * seq)[:, None, None]) * nslab).reshape(-1)
    xs = _dispatch(src_start, hx2, bsz, N_EXPERTS, cap, nslab)
    ye = _expert_ffn(xs, gates[:, :, 0, :, None], exp_w_gate, exp_w_up, exp_w_down)
    moe = _combine(idx_flat, off_flat, ye, seq, cap)
    out = _final(x1, moe.reshape(n * (d // LANES), LANES), norm_g[3:4], g2[:bsz, None], seq)
    return out.reshape(bsz, seq, d)


def kernel(x, c, ctx, c_ctx, ada_w, ada_b, norm_g, w_in, s5_a_re, s5_a_im, s5_log_dt, s5_b_re, s5_b_im, s5_c_re, s5_c_im, s5_d, s5_glu_w, s5_glu_b, ml_conv_w, ml_conv_b, ml_gate_b, ml_norm_g, w_out, router_w, exp_w_gate, exp_w_up, exp_w_down):
    depth = ada_w.shape[0]
    assert depth == 1, "context outputs are only produced when another layer follows"
    li = 0
    return _layer(x, c, ctx, c_ctx, ada_w[li], ada_b[li], norm_g[li], w_in[li], s5_a_re[li],
                  s5_a_im[li], s5_log_dt[li], s5_b_re[li], s5_b_im[li], s5_c_re[li], s5_c_im[li],
                  s5_d[li], s5_glu_w[li], s5_glu_b[li], ml_conv_w[li], ml_conv_b[li], ml_gate_b[li],
                  ml_norm_g[li], w_out[li], router_w[li], exp_w_gate[li], exp_w_up[li],
                  exp_w_down[li])
```

```python
import functools
import math

import jax
import jax.numpy as jnp
from jax import lax
from jax.experimental import pallas as pl
from jax.experimental.pallas import tpu as pltpu

F32 = jnp.float32
BF16 = jnp.bfloat16
I32 = jnp.int32

EPS = 1e-6
GRID_W = 64
S5_GROUP = 16
S5_STATE = 64
S5_T = 16
ML_HEADS = 4
ML_T = 256
CONV_K = 5
N_EXPERTS = 16
EC_FACTOR = 2
N_MOD = 6
LANES = 128
TOK_BLK = 128
DISP_TOK = 512
DISP_W = DISP_TOK + 8
VMEM_LIMIT = 56 * 1024 * 1024


def _cparams(sem):
    return pltpu.CompilerParams(dimension_semantics=sem, vmem_limit_bytes=VMEM_LIMIT)


def _sigmoid(x):
    return 1.0 / (1.0 + jnp.exp(-x))


def _silu(x):
    return x * _sigmoid(x)


def _log_sigmoid(x):
    return jnp.minimum(x, 0.0) - jnp.log(1.0 + jnp.exp(-jnp.abs(x)))


def _split3(x):
    x1 = x.astype(BF16)
    r = x - x1.astype(F32)
    x2 = r.astype(BF16)
    x3 = (r - x2.astype(F32)).astype(BF16)
    return x1, x2, x3


def _dot(a, b):
    return jnp.dot(a, b, preferred_element_type=F32)


def _dot_nt(a, b):
    return lax.dot_general(a, b, (((1,), (1,)), ((), ())), preferred_element_type=F32)


def _dot_tn(a, b):
    return lax.dot_general(a, b, (((0,), (0,)), ((), ())), preferred_element_type=F32)


def _dot3_left(a_f32, b_bf16):
    a1, a2, a3 = _split3(a_f32)
    return _dot(a1, b_bf16) + _dot(a2, b_bf16) + _dot(a3, b_bf16)


def _dot3_right(a_bf16, b_f32):
    b1, b2, b3 = _split3(b_f32)
    return _dot(a_bf16, b1) + _dot(a_bf16, b2) + _dot(a_bf16, b3)


def _mod_kernel(c_ref, w_ref, b_ref, o_ref):
    c = c_ref[...]
    s = _silu(c).astype(BF16)
    o_ref[...] = _dot(s, w_ref[...].astype(BF16)) + b_ref[...]


def _modulation(cond8, ada_w, ada_b):
    d, n = ada_w.shape
    tn = 1024
    return pl.pallas_call(
        _mod_kernel,
        grid=(n // tn,),
        in_specs=[pl.BlockSpec((8, d), lambda j: (0, 0)),
                  pl.BlockSpec((d, tn), lambda j: (0, j)),
                  pl.BlockSpec((1, tn), lambda j: (0, j))],
        out_specs=pl.BlockSpec((8, tn), lambda j: (0, j)),
        out_shape=jax.ShapeDtypeStruct((8, n), F32),
        compiler_params=_cparams(("parallel",)),
        name="modulation",
    )(cond8, ada_w, ada_b.reshape(1, n))


def _norm_mod_f32(x, g, sc, sh):
    ms = jnp.mean(x * x, axis=-1, keepdims=True)
    h = x * lax.rsqrt(ms + EPS) * g
    return h * (1.0 + sc) + sh


def _norm_mod(x, g, sc, sh):
    return _norm_mod_f32(x, g, sc, sh).astype(BF16)


def _inproj_s5_kernel(x_ref, g_ref, sc_ref, sh_ref, w_ref, o_ref, h_scr):
    @pl.when(pl.program_id(2) == 0)
    def _():
        h_scr[...] = _norm_mod(x_ref[0], g_ref[...], sc_ref[0], sh_ref[0])

    o_ref[0] = _dot(h_scr[...], w_ref[...])


def _in_projection_s5(x, g0, sc, sh, mod_of_batch, w_s5):
    bsz, length, d = x.shape
    ds5 = w_s5.shape[1]
    tm = min(1024, length)
    tn = 512
    return pl.pallas_call(
        _inproj_s5_kernel,
        grid=(bsz, length // tm, ds5 // tn),
        in_specs=[pl.BlockSpec((1, tm, d), lambda b, i, j: (b, i, 0)),
                  pl.BlockSpec((1, d), lambda b, i, j: (0, 0)),
                  pl.BlockSpec((1, 1, d), lambda b, i, j: (mod_of_batch(b), 0, 0)),
                  pl.BlockSpec((1, 1, d), lambda b, i, j: (mod_of_batch(b), 0, 0)),
                  pl.BlockSpec((d, tn), lambda b, i, j: (0, j))],
        out_specs=pl.BlockSpec((1, tm, tn), lambda b, i, j: (b, i, j)),
        out_shape=jax.ShapeDtypeStruct((bsz, length, ds5), F32),
        scratch_shapes=[pltpu.VMEM((tm, d), BF16)],
        compiler_params=_cparams(("parallel", "parallel", "arbitrary")),
        name="in_projection_s5",
    )(x, g0, sc, sh, w_s5)


def _inproj_ml_kernel(x_ref, c_ref, g_ref, sc_ref, sh_ref, csc_ref, csh_ref, w_ref, wg_ref, ws_ref,
                      o_ref, og_ref, u_ref, h_scr, x_scr, hr_scr, *, n_ml_blocks):
    rb = o_ref.shape[3]
    nlb = x_ref.shape[2] // LANES
    j = pl.program_id(2)

    @pl.when(j == 0)
    def _():
        for r in range(rb):
            h = _norm_mod_f32(x_ref[0, r * GRID_W:(r + 1) * GRID_W, :], g_ref[...], sc_ref[0], sh_ref[0])
            hr_scr[r * GRID_W:(r + 1) * GRID_W, :] = h.astype(BF16)
            for c in range(nlb):
                x_scr[c, r * ROW_PITCH:r * ROW_PITCH + GRID_W, :] = h[:, c * LANES:(c + 1) * LANES]
        for w in range(GRID_W):
            hw = jnp.concatenate([x_scr[c, pl.ds(w, rb, stride=ROW_PITCH), :] for c in range(nlb)],
                                 axis=-1)
            h_scr[w * rb:(w + 1) * rb, :] = hw.astype(BF16)
        for s in range(c_ref.shape[1]):
            h_scr[(GRID_W + s) * rb:(GRID_W + s + 1) * rb, :] = _norm_mod(
                c_ref[0, s, 0], g_ref[...], csc_ref[0], csh_ref[0])
        og_ref[0] = _dot(h_scr[...], wg_ref[...]).reshape(og_ref.shape[1:])

    @pl.when(j < n_ml_blocks)
    def _():
        o_ref[0] = _dot(h_scr[...], w_ref[...]).reshape(o_ref.shape[1:]).astype(o_ref.dtype)

    @pl.when(j >= n_ml_blocks)
    def _():
        u_ref[0] = _dot(hr_scr[...], ws_ref[...])


ML_RB = 16
ROW_PITCH = GRID_W + 8


def _in_projection_ml(x, ctx, g0, sc, sh, ctx_mod, w_ml, w_gate, w_s5):
    bsz, length, d = x.shape
    lctx = ctx.shape[1]
    nml = w_ml.shape[1]
    ds5 = w_s5.shape[1]
    tn = 512
    tns = 256
    nmb, nsb = nml // tn, ds5 // tns
    rows = length // GRID_W
    rblocks = rows // ML_RB
    nslot_c = lctx // rows
    assert nslot_c * rows == lctx and rblocks * ML_RB == rows
    n_slots = GRID_W + nslot_c
    tm = ML_RB * GRID_W
    ctx5 = ctx.reshape(bsz, nslot_c, rblocks, ML_RB, d)
    kern = functools.partial(_inproj_ml_kernel, n_ml_blocks=nmb)
    return pl.pallas_call(
        kern,
        grid=(bsz, rblocks, nmb + nsb),
        in_specs=[pl.BlockSpec((1, tm, d), lambda b, i, j: (b, i, 0)),
                  pl.BlockSpec((1, nslot_c, 1, ML_RB, d), lambda b, i, j: (b, 0, i, 0, 0)),
                  pl.BlockSpec((1, d), lambda b, i, j: (0, 0)),
                  pl.BlockSpec((1, 1, d), lambda b, i, j: (b, 0, 0)),
                  pl.BlockSpec((1, 1, d), lambda b, i, j: (b, 0, 0)),
                  pl.BlockSpec((1, 1, d), lambda b, i, j: (ctx_mod, 0, 0)),
                  pl.BlockSpec((1, 1, d), lambda b, i, j: (ctx_mod, 0, 0)),
                  pl.BlockSpec((d, tn), lambda b, i, j: (0, jnp.minimum(j, nmb - 1))),
                  pl.BlockSpec((d, LANES), lambda b, i, j: (0, 0)),
                  pl.BlockSpec((d, tns), lambda b, i, j: (0, jnp.maximum(j - nmb, 0)))],
        out_specs=[pl.BlockSpec((1, n_slots, 1, ML_RB, tn),
                                lambda b, i, j: (b, 0, i, 0, jnp.minimum(j, nmb - 1))),
                   pl.BlockSpec((1, n_slots, 1, ML_RB, LANES), lambda b, i, j: (b, 0, i, 0, 0)),
                   pl.BlockSpec((1, tm, tns), lambda b, i, j: (b, i, jnp.maximum(j - nmb, 0)))],
        out_shape=[jax.ShapeDtypeStruct((bsz, n_slots, rblocks, ML_RB, nml), BF16),
                   jax.ShapeDtypeStruct((bsz, n_slots, rblocks, ML_RB, LANES), F32),
                   jax.ShapeDtypeStruct((bsz, length, ds5), F32)],
        scratch_shapes=[pltpu.VMEM((n_slots * ML_RB, d), BF16),
                        pltpu.VMEM((d // LANES, ML_RB * ROW_PITCH, LANES), F32),
                        pltpu.VMEM((tm, d), BF16)],
        compiler_params=_cparams(("parallel", "parallel", "arbitrary")),
        name="in_projection_ml",
    )(x, ctx5, g0, sc, sh, sc, sh, w_ml, w_gate, w_s5)


def _dot_nt6(a, b):
    a1, a2, a3 = _split3(a)
    b1, b2, b3 = _split3(b)
    return (_dot_nt(a1, b1) + (_dot_nt(a1, b2) + _dot_nt(a2, b1))
            + (_dot_nt(a1, b3) + _dot_nt(a3, b1) + _dot_nt(a2, b2)))


S5_LANES = 4 * S5_STATE


def _s5ops_kernel(prm_ref, bt_ref, c_ref, rexp_ref, rc_ref, place_ref, m_ref, bp_ref, cpt_ref, av_ref):
    t = S5_T
    prm = prm_ref[0]
    lr, li = jnp.minimum(prm[0:1], -1e-4), prm[1:2]
    dt = jnp.exp(prm[2:3])
    zr, zi = lr * dt, li * dt
    tau = lax.broadcasted_iota(I32, (32, S5_LANES), 0).astype(F32)
    mag, ang = jnp.exp(tau * zr), tau * zi
    pwr, pwi = mag * jnp.cos(ang), mag * jnp.sin(ang)
    xr, xi = pwr[1:2] - 1.0, pwi[1:2]
    den = lr * lr + li * li
    qr, qi = (xr * lr + xi * li) / den, (xi * lr - xr * li) / den
    btr, bti = bt_ref[0, 0], bt_ref[0, 1]
    bbr, bbi = qr * btr - qi * bti, qr * bti + qi * btr
    cr, ci = c_ref[0, 0], c_ref[0, 1]

    lane = lax.broadcasted_iota(I32, (t * S5_GROUP, S5_LANES), 1)
    fwd = ((lane // S5_STATE) % 2) == 0

    def expand(r, x):
        return _dot3_right(r, x)

    e_t = [expand(rexp_ref[0], z) for z in (pwr, pwi)]
    e_r = [expand(rexp_ref[1], z) for z in (pwr, pwi)]
    e_p = [expand(rexp_ref[2], z) for z in (pwr, pwi)]
    e_q = [expand(rexp_ref[3], z) for z in (pwr, pwi)]
    pk = [jnp.where(fwd, e_t[i], e_r[i]) for i in range(2)]
    pb = [jnp.where(fwd, e_r[i], e_t[i]) for i in range(2)]
    pc = [jnp.where(fwd, e_p[i], e_q[i]) for i in range(2)]
    cxr, cxi = expand(rc_ref[...], cr), expand(rc_ref[...], ci)
    bxr, bxi = expand(rc_ref[...], bbr), expand(rc_ref[...], bbi)

    wkr, wki = pk[0] * cxr - pk[1] * cxi, pk[0] * cxi + pk[1] * cxr
    row_blk = lax.broadcasted_iota(I32, (4 * S5_GROUP, S5_LANES), 0) // S5_GROUP
    lane_blk = lax.broadcasted_iota(I32, (4 * S5_GROUP, S5_LANES), 1) // S5_STATE
    own = row_blk == lane_blk
    bmr = jnp.where(own, jnp.concatenate([bbr] * 4, axis=0), 0.0)
    bmi = jnp.where(own, jnp.concatenate([bbi] * 4, axis=0), 0.0)
    kt = _dot_nt6(bmr, wkr) - _dot_nt6(bmi, wki)

    ln = lax.broadcasted_iota(I32, (S5_GROUP, t * S5_GROUP), 1)
    for gi in range(2):
        ktf = kt[(2 * gi) * S5_GROUP:(2 * gi + 1) * S5_GROUP]
        ktb = kt[(2 * gi + 1) * S5_GROUP:(2 * gi + 2) * S5_GROUP]
        blocks = []
        for s in range(t):
            f = ktf if s == 0 else pltpu.roll(ktf, S5_GROUP * s, axis=1)
            sh = (t * S5_GROUP - S5_GROUP * (t - 1 - s)) % (t * S5_GROUP)
            bk = ktb if sh == 0 else pltpu.roll(ktb, sh, axis=1)
            blocks.append(jnp.where(ln >= S5_GROUP * s, f, 0.0)
                          + jnp.where(ln < S5_GROUP * (s + 1), bk, 0.0))
        m_ref[0, gi] = jnp.concatenate(blocks, axis=0).astype(BF16)

    inj = jnp.concatenate([pb[0] * bxr - pb[1] * bxi, pb[0] * bxi + pb[1] * bxr], axis=-1).astype(BF16)
    out = jnp.concatenate([pc[0] * cxr - pc[1] * cxi, -(pc[0] * cxi + pc[1] * cxr)], axis=-1).astype(BF16)
    for gi in range(2):
        bp_ref[0, gi * t * S5_GROUP:(gi + 1) * t * S5_GROUP, :] = _dot(inj, place_ref[gi]).astype(BF16)
        cpt_ref[0, gi * t * S5_GROUP:(gi + 1) * t * S5_GROUP, :] = _dot(out, place_ref[gi]).astype(BF16)
    a_t = jnp.concatenate([pwr[t:t + 8], pwi[t:t + 8]], axis=-1)
    av_ref[0] = _dot3_left(a_t, place_ref[0]) + _dot3_left(a_t, place_ref[1])


def _s5_operators(a_re, a_im, log_dt, b_re, b_im, c_re, c_im):
    t = S5_T
    g, p = a_re.shape[1], a_re.shape[2]
    npair = g // 2

    def lanes(z):
        return z.reshape(2, npair, 2, p).transpose(1, 2, 0, 3).reshape(npair, 4 * p)

    prm = jnp.stack([lanes(a_re.astype(F32)), lanes(a_im.astype(F32)),
                     lanes(jnp.broadcast_to(log_dt.astype(F32)[..., None], a_re.shape))], axis=1)
    prm = jnp.concatenate([prm, jnp.zeros((npair, 5, 4 * p), F32)], axis=1)

    def tile_d(z):
        z = z.reshape(npair, 2, S5_GROUP, p).transpose(0, 2, 1, 3)
        return jnp.broadcast_to(z[:, :, :, None, :], (npair, S5_GROUP, 2, 2, p)).reshape(npair, S5_GROUP, 4 * p)

    bt = jnp.stack([tile_d(b_re.astype(F32).swapaxes(1, 2)), tile_d(b_im.astype(F32).swapaxes(1, 2))], axis=1)
    ct = jnp.stack([tile_d(c_re.astype(F32)), tile_d(c_im.astype(F32))], axis=1)

    row = jnp.arange(t * S5_GROUP)
    tau, ch = row // S5_GROUP, row % S5_GROUP
    e = jnp.arange(32)
    rexp = jnp.stack([(expo[:, None] == e[None, :]) for expo in (tau, t - 1 - tau, tau + 1, t - tau)]
                     ).astype(BF16)
    rc = (ch[:, None] == jnp.arange(S5_GROUP)[None, :]).astype(BF16)
    src = jnp.arange(2 * 4 * p)
    reim, gi_s, d_s, p_s = src // (4 * p), (src // (2 * p)) % 2, (src // p) % 2, src % p
    dst = d_s * (4 * p) + reim * (2 * p) + gi_s * p + p_s
    col = jnp.arange(8 * p)
    place = jnp.stack([((dst[:, None] == col[None, :]) & (gi_s[:, None] == gi)) for gi in range(2)]
                      ).astype(BF16)
    blk3 = lambda shape: pl.BlockSpec((1,) + shape, lambda i: (i,) + (0,) * len(shape))
    full = lambda arr: pl.BlockSpec(arr.shape, lambda i: (0,) * arr.ndim)
    m, bp, cpt, av = pl.pallas_call(
        _s5ops_kernel,
        grid=(npair,),
        in_specs=[blk3((8, 4 * p)), blk3((2, S5_GROUP, 4 * p)), blk3((2, S5_GROUP, 4 * p)),
                  full(rexp), full(rc), full(place)],
        out_specs=[blk3((2, t * S5_GROUP, t * S5_GROUP)), blk3((2 * t * S5_GROUP, 8 * p)),
                   blk3((2 * t * S5_GROUP, 8 * p)), blk3((8, 8 * p))],
        out_shape=[jax.ShapeDtypeStruct((npair, 2, t * S5_GROUP, t * S5_GROUP), BF16),
                   jax.ShapeDtypeStruct((npair, 2 * t * S5_GROUP, 8 * p), BF16),
                   jax.ShapeDtypeStruct((npair, 2 * t * S5_GROUP, 8 * p), BF16),
                   jax.ShapeDtypeStruct((npair, 8, 8 * p), F32)],
        compiler_params=_cparams(("parallel",)),
        name="s5_operators",
    )(prm, bt, ct, rexp, rc, place)
    avec = av[:, 0].reshape(npair, 4, 2 * p)
    return m, bp, cpt, jnp.concatenate([avec, jnp.zeros_like(avec)], axis=1)


def _s5_permutation():
    i = jnp.arange(8 * 8 * S5_GROUP)
    t8, j, c = i // (8 * S5_GROUP), (i // S5_GROUP) % 8, i % S5_GROUP
    dst = j * (8 * S5_GROUP) + t8 * S5_GROUP + c
    e1 = (dst[:, None] == i[None, :]).astype(BF16)
    return e1, e1.T


S5_PAIRS = 4


def _s5_kernel(ux_ref, uc_ref, e1_ref, e1t_ref, m_ref, bp_ref, cp_ref, av_ref, y_ref, s_scr, hp_scr,
               tab_scr, *, cc, xc):
    nc = cc + xc

    def chunk_rows(ref, nchunks):
        cols = [ref[0, pl.ds(t, nchunks, stride=S5_T), :].astype(BF16) for t in range(S5_T)]
        return [_dot(jnp.concatenate(cols[8 * h:8 * h + 8], axis=-1), e1_ref[...]).astype(BF16)
                for h in range(2)]

    def pair_lanes(halves, p):
        return jnp.concatenate([halves[h][:, j * LANES:(j + 1) * LANES]
                                for j in (2 * p, 2 * p + 1) for h in range(2)], axis=-1)

    pc = chunk_rows(uc_ref, cc)
    px = chunk_rows(ux_ref, xc)
    for p in range(S5_PAIRS):
        s_scr[0:cc, p * 512:(p + 1) * 512] = _dot(pair_lanes(pc, p), bp_ref[p])
        s_scr[cc:nc, p * 512:(p + 1) * 512] = _dot(pair_lanes(px, p), bp_ref[p])

    av = av_ref[...]
    sub = lax.broadcasted_iota(I32, (8, 128), 0)

    def cmul(ar, ai, br, bi):
        return ar * br - ai * bi, ar * bi + ai * br

    for p in range(S5_PAIRS):
        for d in range(2):
            a1 = (av[p, 2 * d:2 * d + 1], av[p, 2 * d + 1:2 * d + 2])
            a2 = cmul(*a1, *a1)
            a4 = cmul(*a2, *a2)
            pw_re, pw_im = jnp.zeros((8, 128), F32), jnp.zeros((8, 128), F32)
            cur = a1
            for k in range(8):
                r = k if d == 0 else 7 - k
                pw_re = jnp.where(sub == r, cur[0], pw_re)
                pw_im = jnp.where(sub == r, cur[1], pw_im)
                cur = cmul(*cur, *a1)
            for q, z in enumerate((a1, a2, a4)):
                tab_scr[p, d, 2 * q] = jnp.broadcast_to(z[0], (8, 128))
                tab_scr[p, d, 2 * q + 1] = jnp.broadcast_to(z[1], (8, 128))
            tab_scr[p, d, 6] = pw_re
            tab_scr[p, d, 7] = pw_im

    def scan8(s_re, s_im, h_re, h_im, p, d):
        back = d == 1
        x_re, x_im = s_re, s_im
        for q, k in enumerate((1, 2, 4)):
            keep = (sub < 8 - k) if back else (sub >= k)
            sh = (8 - k) if back else k
            y_re = jnp.where(keep, pltpu.roll(x_re, sh, axis=0), 0.0)
            y_im = jnp.where(keep, pltpu.roll(x_im, sh, axis=0), 0.0)
            m_re, m_im = cmul(tab_scr[p, d, 2 * q], tab_scr[p, d, 2 * q + 1], y_re, y_im)
            x_re, x_im = x_re + m_re, x_im + m_im
        c_re, c_im = cmul(tab_scr[p, d, 6], tab_scr[p, d, 7], h_re, h_im)
        x_re, x_im = x_re + c_re, x_im + c_im
        edge = 7 if back else 0
        sh = 7 if back else 1
        e_re = jnp.where(sub == edge, h_re, pltpu.roll(x_re, sh, axis=0))
        e_im = jnp.where(sub == edge, h_im, pltpu.roll(x_im, sh, axis=0))
        last = 0 if back else 7
        return e_re, e_im, x_re[last:last + 1], x_im[last:last + 1]

    def step(row_f, row_b, carry):
        rf = pl.multiple_of(row_f, 8)
        rb = pl.multiple_of(row_b, 8)
        new = []
        for p in range(S5_PAIRS):
            hr, hi, gr, gi = carry[4 * p:4 * p + 4]
            c0 = p * 512
            p_re, p_im, hr, hi = scan8(s_scr[pl.ds(rf, 8), c0:c0 + 128], s_scr[pl.ds(rf, 8), c0 + 128:c0 + 256],
                                       hr, hi, p, 0)
            hp_scr[pl.ds(rf, 8), c0:c0 + 128] = p_re
            hp_scr[pl.ds(rf, 8), c0 + 128:c0 + 256] = p_im
            p_re, p_im, gr, gi = scan8(s_scr[pl.ds(rb, 8), c0 + 256:c0 + 384], s_scr[pl.ds(rb, 8), c0 + 384:c0 + 512],
                                       gr, gi, p, 1)
            hp_scr[pl.ds(rb, 8), c0 + 256:c0 + 384] = p_re
            hp_scr[pl.ds(rb, 8), c0 + 384:c0 + 512] = p_im
            new += [hr, hi, gr, gi]
        return tuple(new)

    z = jnp.zeros((1, 128), F32)
    carry = lax.fori_loop(0, cc // 8, lambda i, c: step(8 * i, cc - 8 - 8 * i, c), (z,) * (4 * S5_PAIRS))
    lax.fori_loop(0, xc // 8, lambda i, c: step(cc + 8 * i, nc - 8 - 8 * i, c), carry)

    ys = []
    for p in range(S5_PAIRS):
        ux = pair_lanes(px, p)
        y = _dot_nt(hp_scr[cc:nc, p * 512:(p + 1) * 512].astype(BF16), cp_ref[p])
        ys.append(y + jnp.concatenate([_dot(ux[:, 0:256], m_ref[p, 0]), _dot(ux[:, 256:512], m_ref[p, 1])],
                                      axis=-1))
    for h in range(2):
        yh = jnp.concatenate([ys[j // 2][:, (j % 2) * 256 + h * LANES:(j % 2) * 256 + (h + 1) * LANES]
                              for j in range(8)], axis=-1).astype(BF16)
        r = _dot(yh, e1t_ref[...])
        for t8 in range(8):
            y_ref[0, pl.ds(8 * h + t8, xc, stride=S5_T), :] = r[:, t8 * LANES:(t8 + 1) * LANES]


def _s5_scan(ux, uc, m, bp, cp, avec):
    bsz, seq, ds5 = ux.shape
    lctx = uc.shape[1]
    cc, xc = lctx // S5_T, seq // S5_T
    assert cc % 8 == 0 and xc % 8 == 0
    e1, e1t = _s5_permutation()
    kern = functools.partial(_s5_kernel, cc=cc, xc=xc)
    nblk = ds5 // LANES
    const = lambda j, b: (0, 0)
    return pl.pallas_call(
        kern,
        grid=(nblk, bsz),
        in_specs=[pl.BlockSpec((1, seq, LANES), lambda j, b: (b, 0, j)),
                  pl.BlockSpec((1, lctx, LANES), lambda j, b: (b, 0, j)),
                  pl.BlockSpec(e1.shape, const),
                  pl.BlockSpec(e1.shape, const),
                  pl.BlockSpec((S5_PAIRS, 2, 256, 256), lambda j, b: (j, 0, 0, 0)),
                  pl.BlockSpec((S5_PAIRS, 512, 512), lambda j, b: (j, 0, 0)),
                  pl.BlockSpec((S5_PAIRS, 512, 512), lambda j, b: (j, 0, 0)),
                  pl.BlockSpec((S5_PAIRS, 8, 128), lambda j, b: (j, 0, 0))],
        out_specs=pl.BlockSpec((1, seq, LANES), lambda j, b: (b, 0, j)),
        out_shape=jax.ShapeDtypeStruct((bsz, seq, ds5), F32),
        scratch_shapes=[pltpu.VMEM((cc + xc, S5_PAIRS * 512), F32),
                        pltpu.VMEM((cc + xc, S5_PAIRS * 512), F32),
                        pltpu.VMEM((S5_PAIRS, 2, 8, 8, 128), F32)],
        compiler_params=_cparams(("parallel", "parallel")),
        name="s5_scan",
    )(ux, uc, e1, e1t, m, bp, cp, avec)


def _conv_kernel(x_ref, p_ref, n_ref, w_ref, b_ref, sc_ref, o_ref, scr, *, rows, seq_tiles):
    i = pl.program_id(1)
    first = functools.reduce(jnp.logical_or, [i == s for s in seq_tiles[0]])
    last = functools.reduce(jnp.logical_or, [i == s for s in seq_tiles[1]])
    pm = jnp.where(first, 0.0, 1.0)
    nm = jnp.where(last, 0.0, 1.0)
    scr[0:8, :] = p_ref[0, 8:16, :].astype(F32) * pm
    scr[8:8 + rows, :] = x_ref[0].astype(F32)
    scr[8 + rows:16 + rows, :] = n_ref[0, 0:8, :].astype(F32) * nm
    w = w_ref[...]
    acc = scr[6:6 + rows, :] * w[0:1]
    for j in range(1, CONV_K):
        acc = acc + scr[6 + j:6 + j + rows, :] * w[j:j + 1]
    z = acc + b_ref[...]
    o_ref[0] = (_silu(z) * sc_ref[...]).astype(o_ref.dtype)


def _conv_silu(ml, conv_w8, conv_b, scale, seq_starts, seq_ends):
    bsz, ltot, _ = ml.shape
    rows = 256
    ct = 1024
    nch = conv_b.shape[-1]
    nt = ltot // rows
    hb = rows // 16
    nhb = ltot // 16
    seq_tiles = (tuple(s // rows for s in seq_starts), tuple(e // rows - 1 for e in seq_ends))
    kern = functools.partial(_conv_kernel, rows=rows, seq_tiles=seq_tiles)
    return pl.pallas_call(
        kern,
        grid=(bsz, nt, nch // ct),
        in_specs=[pl.BlockSpec((1, rows, ct), lambda b, i, c: (b, i, c)),
                  pl.BlockSpec((1, 16, ct), lambda b, i, c: (b, jnp.maximum(i * hb - 1, 0), c)),
                  pl.BlockSpec((1, 16, ct), lambda b, i, c: (b, jnp.minimum((i + 1) * hb, nhb - 1), c)),
                  pl.BlockSpec((8, ct), lambda b, i, c: (0, c)),
                  pl.BlockSpec((1, ct), lambda b, i, c: (0, c)),
                  pl.BlockSpec((1, ct), lambda b, i, c: (0, c))],
        out_specs=pl.BlockSpec((1, rows, ct), lambda b, i, c: (b, i, c)),
        out_shape=jax.ShapeDtypeStruct((bsz, ltot, nch), BF16),
        scratch_shapes=[pltpu.VMEM((rows + 16, ct), F32)],
        compiler_params=_cparams(("parallel", "parallel", "parallel")),
        name="conv_silu",
    )(ml, ml, ml, conv_w8, conv_b, scale)


def _mlstm_chain_step(q, k, v, i_col, f_col, i_row, b_col, b_row, mask, c_ref, n_ref, m_ref, idx):
    m_st = m_ref[idx]
    d = jnp.where(mask, b_col - b_row + i_row, -jnp.inf)
    inter = b_col + m_st
    m_t = jnp.maximum(inter, jnp.max(d, axis=-1, keepdims=True))
    w = jnp.exp(d - m_t)
    s_inter = jnp.exp(inter - m_t)
    s = _dot_nt(q, k) * w
    c_st = c_ref[idx]
    n_st = n_ref[idx]
    num = _dot(s.astype(BF16), v) + s_inter * _dot_nt(q, c_st.astype(BF16))
    qf = q.astype(F32)
    den = jnp.sum(s, axis=-1, keepdims=True) + s_inter * jnp.sum(qf * n_st, axis=-1, keepdims=True)
    h = num / jnp.maximum(jnp.abs(den), jnp.exp(-m_t))
    total = jnp.sum(f_col, axis=0, keepdims=True)
    d_end = total - b_col + i_col
    m_new = jnp.maximum(total + m_st, jnp.max(d_end, axis=0, keepdims=True))
    w_end = jnp.exp(d_end - m_new)
    dec = jnp.exp(total + m_st - m_new)
    vw = (v.astype(F32) * w_end).astype(BF16)
    c_ref[idx] = dec * c_st + _dot_tn(vw, k)
    n_ref[idx] = dec * n_st + jnp.sum(k.astype(F32) * w_end, axis=0, keepdims=True)
    m_ref[idx] = m_new
    return h


def _mlstm_kernel(qf_ref, kf_ref, vf_ref, gf_ref, gtf_ref, qb_ref, kb_ref, vb_ref, gb_ref, gtb_ref,
                  gbias_ref, gbias_t_ref, hf_ref, hb_ref, c_scr, n_scr, m_scr, *, dh):
    @pl.when(pl.program_id(1) == 0)
    def _():
        c_scr[...] = jnp.zeros_like(c_scr)
        n_scr[...] = jnp.zeros_like(n_scr)
        m_scr[...] = jnp.zeros_like(m_scr)

    bsz, t = qf_ref.shape[0], qf_ref.shape[1]
    rid = lax.broadcasted_iota(I32, (t, t), 0)
    cid = lax.broadcasted_iota(I32, (t, t), 1)
    lower = jnp.where(rid >= cid, 1.0, 0.0).astype(BF16)
    upper = jnp.where(cid >= rid, 1.0, 0.0).astype(BF16)

    for bb in range(bsz):
        for d, (q_ref, k_ref, v_ref, g_ref, gt_ref, h_ref) in enumerate(
                [(qf_ref, kf_ref, vf_ref, gf_ref, gtf_ref, hf_ref),
                 (qb_ref, kb_ref, vb_ref, gb_ref, gtb_ref, hb_ref)]):
            backward = d == 1
            g = g_ref[bb, :, 0:16] + gbias_ref[...]
            gt = gt_ref[bb] + gbias_t_ref[...]
            lf = _log_sigmoid(g)
            lft = _log_sigmoid(gt)
            cum_col = _dot3_right(upper if backward else lower, lf)
            cum_row = _dot3_left(lft, lower if backward else upper)
            for hd in range(ML_HEADS):
                ci = 2 * ML_HEADS * d + hd
                cf = ci + ML_HEADS
                sl = slice(hd * dh, (hd + 1) * dh)
                h = _mlstm_chain_step(
                    q_ref[bb, :, sl], k_ref[bb, :, sl], v_ref[bb, :, sl],
                    g[:, ci:ci + 1], lf[:, cf:cf + 1], gt[ci:ci + 1, :],
                    cum_col[:, cf:cf + 1], cum_row[cf:cf + 1, :],
                    (cid >= rid) if backward else (rid >= cid),
                    c_scr, n_scr, m_scr, (bb * 2 + d) * ML_HEADS + hd)
                h_ref[bb, :, sl] = h.astype(h_ref.dtype)


def _mlstm_scan(qk, ml, gates, gates_t, gbias, gbias_t, n_ctx_chunks):
    bsz, ltot, dml2 = qk.shape
    dml = dml2 // 2
    dh = dml // ML_HEADS
    t = ML_T
    nchunk = ltot // t
    cc = n_ctx_chunks
    nx = nchunk - cc

    def fwd(s):
        return jnp.where(s < cc, nx + s, s - cc)

    def bwd(s):
        return jnp.where(s < cc, nx + cc - 1 - s, nx - 1 - (s - cc))

    bps = 1

    def specs(cmap):
        return [pl.BlockSpec((bps, t, dml), lambda b, s: (b, cmap(s), 0)),
                pl.BlockSpec((bps, t, dml), lambda b, s: (b, cmap(s), 1)),
                pl.BlockSpec((bps, t, dml), lambda b, s: (b, cmap(s), 2)),
                pl.BlockSpec((bps, t, LANES), lambda b, s: (b, cmap(s), 0)),
                pl.BlockSpec((bps, 16, t), lambda b, s: (b, 0, cmap(s)))]

    kern = functools.partial(_mlstm_kernel, dh=dh)
    nchain = bps * 2 * ML_HEADS
    return pl.pallas_call(
        kern,
        grid=(bsz // bps, nchunk),
        in_specs=specs(fwd) + specs(bwd) + [pl.BlockSpec((1, 16), lambda b, s: (0, 0)),
                                            pl.BlockSpec((16, 1), lambda b, s: (0, 0))],
        out_specs=[pl.BlockSpec((bps, t, dml), lambda b, s: (b, fwd(s), 0)),
                   pl.BlockSpec((bps, t, dml), lambda b, s: (b, bwd(s), 0))],
        out_shape=[jax.ShapeDtypeStruct((bsz, ltot, dml), BF16)] * 2,
        scratch_shapes=[pltpu.VMEM((nchain, dh, dh), F32),
                        pltpu.VMEM((nchain, 1, dh), F32),
                        pltpu.VMEM((nchain, 1, 1), F32)],
        compiler_params=_cparams(("parallel", "arbitrary")),
        name="mlstm_scan",
    )(qk, qk, ml, gates, gates_t, qk, qk, ml, gates, gates_t, gbias, gbias_t)


COL_PITCH = ML_RB + 8


def _mlpost_kernel(hf_ref, hb_ref, o_ref, mg_ref, out_ref, scr, *, dh):
    nw, _, rb, dml = hf_ref.shape[1:]
    h = (hf_ref[0].astype(F32) + hb_ref[0].astype(F32)).reshape(nw * rb, dml)
    parts = []
    for hd in range(ML_HEADS):
        hh = h[:, hd * dh:(hd + 1) * dh]
        parts.append(hh * lax.rsqrt(jnp.mean(hh * hh, axis=-1, keepdims=True) + EPS))
    hn = jnp.concatenate(parts, axis=-1) * mg_ref[...]
    res = hn * _sigmoid(o_ref[0].astype(F32).reshape(nw * rb, dml))
    nlb = dml // LANES
    for c in range(nlb):
        for w in range(nw):
            scr[c, w * COL_PITCH:w * COL_PITCH + rb, :] = res[w * rb:(w + 1) * rb, c * LANES:(c + 1) * LANES]
    for r in range(rb):
        row = jnp.concatenate([scr[c, pl.ds(r, nw, stride=COL_PITCH), :] for c in range(nlb)], axis=-1)
        out_ref[0, r * nw:(r + 1) * nw, :] = row.astype(out_ref.dtype)


def _ml_post(hf5, hb5, ml5, ml_g, seq):
    bsz, _, rblocks, rb, dml = hf5.shape
    kern = functools.partial(_mlpost_kernel, dh=dml // ML_HEADS)
    blk = (1, GRID_W, 1, rb, dml)
    return pl.pallas_call(
        kern,
        grid=(bsz, rblocks),
        in_specs=[pl.BlockSpec(blk, lambda b, i: (b, 0, i, 0, 0)),
                  pl.BlockSpec(blk, lambda b, i: (b, 0, i, 0, 0)),
                  pl.BlockSpec(blk, lambda b, i: (b, 0, i, 0, 3)),
                  pl.BlockSpec((1, dml), lambda b, i: (0, 0))],
        out_specs=pl.BlockSpec((1, rb * GRID_W, dml), lambda b, i: (b, i, 0)),
        out_shape=jax.ShapeDtypeStruct((bsz, seq, dml), BF16),
        scratch_shapes=[pltpu.VMEM((dml // LANES, GRID_W * COL_PITCH, LANES), F32)],
        compiler_params=_cparams(("parallel", "parallel")),
        name="mlstm_post",
    )(hf5, hb5, ml5, ml_g)


def _gelu_tanh(y):
    return 0.5 * y * (1.0 + jnp.tanh(math.sqrt(2.0 / math.pi) * (y + 0.044715 * (y * y * y))))


def _mixout_kernel(ys_ref, u_ref, ml_ref, x_ref, dsk_ref, gw_ref, gb_ref, wo_ref, ng_ref, mod_ref,
                   rw_ref, x1_ref, hx_ref, aff_ref, *, n_exp):
    y = ys_ref[...] + dsk_ref[...] * u_ref[...]
    ge = _gelu_tanh(y)
    z = _dot(ge.astype(BF16), gw_ref[...]) + gb_ref[...]
    s5 = (ge * _sigmoid(z)).astype(BF16)
    yx = _dot(jnp.concatenate([s5, ml_ref[...]], axis=-1), wo_ref[...])
    ng = ng_ref[...]
    mod = mod_ref[0]
    r1 = yx * lax.rsqrt(jnp.mean(yx * yx, axis=-1, keepdims=True) + EPS) * ng[1:2]
    x1 = x_ref[...] + mod[0:1] * r1
    x1_ref[...] = x1
    hx = x1 * lax.rsqrt(jnp.mean(x1 * x1, axis=-1, keepdims=True) + EPS) * ng[2:3]
    hx = hx * (1.0 + mod[1:2]) + mod[2:3]
    nslab = hx.shape[1] // LANES
    for cb in range(nslab):
        hx_ref[pl.ds(cb, hx.shape[0], stride=nslab), :] = hx[:, cb * LANES:(cb + 1) * LANES]
    h1, h2, _ = _split3(hx)
    rw2 = rw_ref[...]
    l1 = _dot(h1, rw2)
    logits = (l1[:, 0:LANES] + l1[:, LANES:2 * LANES]) + _dot(h2, rw2[:, 0:LANES])
    lane = lax.broadcasted_iota(I32, logits.shape, 1)
    logits = jnp.where(lane < n_exp, logits, -jnp.inf)
    e = jnp.exp(logits - jnp.max(logits, axis=-1, keepdims=True))
    aff_ref[...] = e / jnp.sum(e, axis=-1, keepdims=True)


def _mixer_out(ys, u, mlo, x2d, rows_per_batch, dsk, glu_w, glu_b, w_out, norm_g, mod3, rw2):
    n, d = x2d.shape
    ds5 = ys.shape[1]
    dml = mlo.shape[1]
    tm = 256
    tiles_per_batch = rows_per_batch // tm
    kern = functools.partial(_mixout_kernel, n_exp=N_EXPERTS)
    row = lambda i: (i, 0)
    const = lambda i: (0, 0)
    return pl.pallas_call(
        kern,
        grid=(n // tm,),
        in_specs=[pl.BlockSpec((tm, ds5), row),
                  pl.BlockSpec((tm, ds5), row),
                  pl.BlockSpec((tm, dml), row),
                  pl.BlockSpec((tm, d), row),
                  pl.BlockSpec((1, ds5), const),
                  pl.BlockSpec((ds5, ds5), const),
                  pl.BlockSpec((1, ds5), const),
                  pl.BlockSpec((d, d), const),
                  pl.BlockSpec((4, d), const),
                  pl.BlockSpec((1, 8, d), lambda i: (i // tiles_per_batch, 0, 0)),
                  pl.BlockSpec((d, 2 * LANES), const)],
        out_specs=[pl.BlockSpec((tm, d), row),
                   pl.BlockSpec((tm * (d // LANES), LANES), row),
                   pl.BlockSpec((tm, LANES), row)],
        out_shape=[jax.ShapeDtypeStruct((n, d), F32),
                   jax.ShapeDtypeStruct((n * (d // LANES), LANES), F32),
                   jax.ShapeDtypeStruct((n, LANES), F32)],
        compiler_params=_cparams(("parallel",)),
        name="mixer_out",
    )(ys, u, mlo, x2d, dsk, glu_w, glu_b, w_out, norm_g, mod3, rw2)


def _count(mask):
    c = jnp.sum(jnp.where(mask, 1.0, 0.0), axis=2, keepdims=True)
    return jnp.sum(c, axis=1, keepdims=True)


SELECT_BISECTIONS = 48


def _pow2_neg(n):
    p = jnp.ones(n.shape, F32)
    for i in range(7):
        p = p * jnp.where(((n >> i) & 1) == 1, 2.0 ** -(2 ** i), 1.0)
    return p


def _select_kernel(aff_ref, off_ref, idx_ref, gate_ref, *, cap, n_tok):
    a = aff_ref[0]
    e, nb, _ = a.shape

    def enough(thr):
        return _count(a >= thr) >= cap

    ex_lo = jnp.full((e, 1, 1), -127, I32)
    ex_hi = jnp.full((e, 1, 1), 1, I32)
    for _ in range(7):
        ex_mid = (ex_lo + ex_hi) >> 1
        ok = enough(_pow2_neg(-ex_mid))
        ex_lo = jnp.where(ok, ex_mid, ex_lo)
        ex_hi = jnp.where(ok, ex_hi, ex_mid)
    lo = jnp.where(ex_lo == -127, 0.0, _pow2_neg(jnp.minimum(-ex_lo, 126)))
    hi = jnp.where(ex_hi == 1, 2.0, _pow2_neg(jnp.maximum(-ex_hi, 0)))
    for _ in range(SELECT_BISECTIONS):
        mid = lo + 0.5 * (hi - lo)
        ok = enough(mid)
        lo = jnp.where(ok, mid, lo)
        hi = jnp.where(ok, hi, mid)
    gt = a >= hi
    tie = (a >= lo) & (a < hi)
    need = cap - _count(gt)
    tok = (lax.broadcasted_iota(I32, a.shape, 1) * LANES + lax.broadcasted_iota(I32, a.shape, 2))
    cut = jnp.zeros((e, 1, 1), I32)
    for bit in range(n_tok.bit_length() - 1, -1, -1):
        cand = cut | (1 << bit)
        cut = jnp.where(_count(tie & (tok < cand)) < need, cand, cut)
    sel = gt | (tie & (tok <= cut))
    self32 = jnp.where(sel, 1.0, 0.0).reshape(e * nb, LANES)
    r = lax.broadcasted_iota(I32, (LANES, LANES), 0)
    c = lax.broadcasted_iota(I32, (LANES, LANES), 1)
    incl = _dot(self32.astype(BF16), jnp.where(r <= c, 1.0, 0.0).astype(BF16))
    tot = jnp.broadcast_to(incl[:, LANES - 1:LANES], (e * nb, LANES)).astype(BF16)
    rr = lax.broadcasted_iota(I32, (e * nb, e * nb), 0)
    cc = lax.broadcasted_iota(I32, (e * nb, e * nb), 1)
    same = (rr // nb) == (cc // nb)
    before = jnp.where(same & (cc < rr), 1.0, 0.0).astype(BF16)
    off = _dot(before, tot)
    off_ref[0] = off.astype(I32).reshape(e, nb, LANES)

    kcol = lax.broadcasted_iota(I32, (cap, nb), 0).astype(F32)
    jrow = lax.broadcasted_iota(I32, (cap, nb), 1).astype(F32)
    lane = lax.broadcasted_iota(I32, (cap, LANES), 1)
    lane_f = lane.astype(F32)
    ones8 = jnp.ones((8, LANES), BF16)
    jr = lax.broadcasted_iota(I32, (nb, nb), 0)
    jc = lax.broadcasted_iota(I32, (nb, nb), 1)
    upper = jnp.where(jr <= jc, 1.0, 0.0).astype(BF16)
    sel3 = self32.reshape(e, nb, LANES)
    incl3 = incl.reshape(e, nb, LANES)

    def to_row(col):
        return _dot_nt(ones8, jnp.where(lane == 0, col, 0.0).astype(BF16))

    for ex in range(e):
        tot_row = _dot_nt(ones8, sel3[ex].astype(BF16))[0:1]
        end_row = _dot(jnp.broadcast_to(tot_row, (8, nb)).astype(BF16), upper)[0:1]
        start_row = end_row - tot_row
        jk = jnp.sum(jnp.where(end_row <= kcol, 1.0, 0.0), axis=1, keepdims=True)
        onehot_j = jnp.where(jrow == jk, 1.0, 0.0)
        kk = kcol[:, 0:1] - jnp.sum(onehot_j * start_row, axis=1, keepdims=True)
        d = _dot(onehot_j.astype(BF16), incl3[ex].astype(BF16))
        tl = jnp.sum(jnp.where(d <= kk, 1.0, 0.0), axis=1, keepdims=True)
        dg = _dot3_right(onehot_j.astype(BF16), a[ex])
        gate = jnp.sum(jnp.where(lane_f == tl, dg, 0.0), axis=1, keepdims=True)
        idx_ref[0, ex] = (to_row(jk) * float(LANES) + to_row(tl)).astype(I32)
        g1, g2, g3 = _split3(gate)
        gate_ref[0, ex] = (to_row(g1.astype(F32)) + to_row(g2.astype(F32))) + to_row(g3.astype(F32))


def _select(aff_t, cap):
    bsz, e, nb, _ = aff_t.shape
    kern = functools.partial(_select_kernel, cap=cap, n_tok=nb * LANES)
    blk = pl.BlockSpec((1, e, nb, LANES), lambda b: (b, 0, 0, 0))
    slot = pl.BlockSpec((1, e, 8, cap), lambda b: (b, 0, 0, 0))
    return pl.pallas_call(
        kern,
        grid=(bsz,),
        in_specs=[blk],
        out_specs=[blk, slot, slot],
        out_shape=[jax.ShapeDtypeStruct(aff_t.shape, I32),
                   jax.ShapeDtypeStruct((bsz, e, 8, cap), I32),
                   jax.ShapeDtypeStruct((bsz, e, 8, cap), F32)],
        compiler_params=_cparams(("parallel",)),
        name="ec_select",
    )(aff_t)


SLAB_PITCH = 24


def _row_copy(hx_hbm, xbuf, sem, src_start, dst_row, nslab):
    return pltpu.make_async_copy(hx_hbm.at[pl.ds(pl.multiple_of(src_start, nslab), nslab)],
                                 xbuf.at[pl.ds(pl.multiple_of(dst_row * SLAB_PITCH, 8), nslab)], sem)


def _dispatch_kernel(idx_ref, hx_hbm, o_ref, xbuf, sems, *, cap, n_tok, n_exp, nslab, n_steps):
    step = pl.program_id(0) * n_exp + pl.program_id(1)

    def issue_all(s):
        half = s % 2

        def issue(k, _):
            _row_copy(hx_hbm, xbuf, sems.at[half], idx_ref[s * cap + k], half * cap + k, nslab).start()
            return 0

        lax.fori_loop(0, cap, issue, 0, unroll=8)

    @pl.when(step == 0)
    def _():
        issue_all(step)

    @pl.when(step + 1 < n_steps)
    def _():
        issue_all(step + 1)

    half = step % 2

    def drain(k, _):
        _row_copy(hx_hbm, xbuf, sems.at[half], 0, half * cap + k, nslab).wait()
        return 0

    lax.fori_loop(0, cap, drain, 0, unroll=8)
    first = pl.multiple_of(half * cap * SLAB_PITCH, 8)
    for c in range(nslab):
        o_ref[0, 0, :, c * LANES:(c + 1) * LANES] = (
            xbuf[pl.ds(first + c, cap, stride=SLAB_PITCH), :].astype(o_ref.dtype))


def _dispatch(idx_flat, hx_slabs, bsz, n_exp, cap, nslab):
    n_tok = hx_slabs.shape[0] // (bsz * nslab)
    d = nslab * LANES
    kern = functools.partial(_dispatch_kernel, cap=cap, n_tok=n_tok, n_exp=n_exp, nslab=nslab,
                             n_steps=bsz * n_exp)
    return pl.pallas_call(
        kern,
        grid_spec=pltpu.PrefetchScalarGridSpec(
            num_scalar_prefetch=1,
            grid=(bsz, n_exp),
            in_specs=[pl.BlockSpec(memory_space=pl.ANY)],
            out_specs=pl.BlockSpec((1, 1, cap, d), lambda b, e, idx: (b, e, 0, 0)),
            scratch_shapes=[pltpu.VMEM((2 * cap * SLAB_PITCH, LANES), F32),
                            pltpu.SemaphoreType.DMA((2,))]),
        out_shape=jax.ShapeDtypeStruct((bsz, n_exp, cap, d), BF16),
        compiler_params=_cparams(("arbitrary", "arbitrary")),
        name="ec_dispatch",
    )(idx_flat, hx_slabs)


FFN_COL_BLOCKS = 4


def _ffn_kernel(x_ref, g_ref, wg_ref, wu_ref, wd_ref, o_ref, acc):
    f = pl.program_id(2)

    @pl.when(f == 0)
    def _():
        acc[...] = jnp.zeros_like(acc)

    x = x_ref[0, 0]
    a = _dot(x, wg_ref[0].astype(BF16))
    u = _dot(x, wu_ref[0].astype(BF16))
    hmid = (_silu(a) * u).astype(BF16)
    wd = wd_ref[0].astype(BF16)
    cw = acc.shape[1] // FFN_COL_BLOCKS
    for cb in range(FFN_COL_BLOCKS):
        acc[:, cb * cw:(cb + 1) * cw] += _dot(hmid, wd[:, cb * cw:(cb + 1) * cw])

    @pl.when(f == pl.num_programs(2) - 1)
    def _():
        g = g_ref[0, 0]
        cap = g.shape[0]
        for c in range(acc.shape[1] // LANES):
            o_ref[0, 0, c // 8, pl.ds(c % 8, cap, stride=8), :] = acc[:, c * LANES:(c + 1) * LANES] * g


def _expert_ffn(xs, gates, w_gate, w_up, w_down):
    bsz, n_exp, cap, d = xs.shape
    ff = w_gate.shape[2]
    tf = 256
    nhalf = d // (8 * LANES)
    return pl.pallas_call(
        _ffn_kernel,
        grid=(n_exp, bsz, ff // tf),
        in_specs=[pl.BlockSpec((1, 1, cap, d), lambda e, b, f: (b, e, 0, 0)),
                  pl.BlockSpec((1, 1, cap, 1), lambda e, b, f: (b, e, 0, 0)),
                  pl.BlockSpec((1, d, tf), lambda e, b, f: (e, 0, f)),
                  pl.BlockSpec((1, d, tf), lambda e, b, f: (e, 0, f)),
                  pl.BlockSpec((1, tf, d), lambda e, b, f: (e, f, 0))],
        out_specs=pl.BlockSpec((1, 1, nhalf, cap * 8, LANES), lambda e, b, f: (b, e, 0, 0, 0)),
        out_shape=jax.ShapeDtypeStruct((bsz, n_exp, nhalf, cap * 8, LANES), F32),
        scratch_shapes=[pltpu.VMEM((cap, d), F32)],
        compiler_params=_cparams(("parallel", "parallel", "arbitrary")),
        name="expert_ffn",
    )(xs, gates, w_gate, w_up, w_down)


COMB_UNROLL = 8


def _combine_kernel(idx_ref, off_ref, y_ref, o_ref, *, cap, nb, n_exp, tok_part, n_part):
    b = pl.program_id(0)
    part = pl.program_id(1)
    ex = pl.program_id(3)

    @pl.when(ex == 0)
    def _():
        o_ref[...] = jnp.zeros_like(o_ref)

    be = b * n_exp + ex
    blocks_per_part = nb // n_part
    k_lo = off_ref[be * nb + part * blocks_per_part]
    k_hi = jnp.where(part == n_part - 1, cap,
                     off_ref[be * nb + jnp.minimum((part + 1) * blocks_per_part, nb - 1)])
    t0 = part * tok_part

    def add_rows(k, n):
        toks = [idx_ref[be * cap + k + i] - t0 for i in range(n)]
        rows = [o_ref[0, toks[i]] + y_ref[0, 0, 0, pl.ds(pl.multiple_of((k + i) * 8, 8), 8), :]
                for i in range(n)]
        for i in range(n):
            o_ref[0, toks[i]] = rows[i]

    n_full = (k_hi - k_lo) // COMB_UNROLL

    def body(i, _):
        add_rows(k_lo + i * COMB_UNROLL, COMB_UNROLL)
        return 0

    lax.fori_loop(0, n_full, body, 0)

    def tail(k, _):
        add_rows(k, 1)
        return 0

    lax.fori_loop(k_lo + n_full * COMB_UNROLL, k_hi, tail, 0)


def _combine(idx_flat, off_flat, y, n_tok, cap):
    bsz, n_exp, nhalf, rows, _ = y.shape
    nb = n_tok // TOK_BLK
    n_part = 2
    sblk = 8
    nslab = nhalf * sblk
    kern = functools.partial(_combine_kernel, cap=cap, nb=nb, n_exp=n_exp, tok_part=n_tok // n_part,
                             n_part=n_part)
    return pl.pallas_call(
        kern,
        grid_spec=pltpu.PrefetchScalarGridSpec(
            num_scalar_prefetch=2,
            grid=(bsz, n_part, nslab // sblk, n_exp),
            in_specs=[pl.BlockSpec((1, 1, 1, rows, LANES), lambda b, p, c, e, idx, off: (b, e, c, 0, 0))],
            out_specs=pl.BlockSpec((1, n_tok // n_part, sblk, LANES),
                                   lambda b, p, c, e, idx, off: (b, p, c, 0))),
        out_shape=jax.ShapeDtypeStruct((bsz, n_tok, nslab, LANES), F32),
        compiler_params=_cparams(("parallel", "parallel", "parallel", "arbitrary")),
        name="ec_combine",
    )(idx_flat, off_flat, y)


def _final_kernel(x1_ref, moe_ref, g_ref, mod_ref, o_ref):
    tm = x1_ref.shape[0]
    nslab = moe_ref.shape[0] // tm
    m = jnp.concatenate([moe_ref[pl.ds(c, tm, stride=nslab), :] for c in range(nslab)], axis=-1)
    r = m * lax.rsqrt(jnp.mean(m * m, axis=-1, keepdims=True) + EPS) * g_ref[...]
    o_ref[...] = x1_ref[...] + mod_ref[0] * r


def _final(x1, moe, g3, g2, rows_per_batch):
    n, d = x1.shape
    tm = 512
    tiles_per_batch = rows_per_batch // tm
    return pl.pallas_call(
        _final_kernel,
        grid=(n // tm,),
        in_specs=[pl.BlockSpec((tm, d), lambda i: (i, 0)),
                  pl.BlockSpec((tm * (d // LANES), LANES), lambda i: (i, 0)),
                  pl.BlockSpec((1, d), lambda i: (0, 0)),
                  pl.BlockSpec((1, 1, d), lambda i: (i // tiles_per_batch, 0, 0))],
        out_specs=pl.BlockSpec((tm, d), lambda i: (i, 0)),
        out_shape=jax.ShapeDtypeStruct((n, d), F32),
        compiler_params=_cparams(("parallel",)),
        name="final_residual",
    )(x1, moe, g3, g2)


def _layer(x, c, ctx, c_ctx, ada_w, ada_b, norm_g, w_in, s5_a_re, s5_a_im, s5_log_dt, s5_b_re,
           s5_b_im, s5_c_re, s5_c_im, s5_d, s5_glu_w, s5_glu_b, ml_conv_w, ml_conv_b, ml_gate_b,
           ml_norm_g, w_out, router_w, exp_w_gate, exp_w_up, exp_w_down):
    bsz, seq, d = x.shape
    lctx = ctx.shape[1]
    ds5 = s5_d.shape[0]
    dml = ml_norm_g.shape[0]
    rows = seq // GRID_W
    n = bsz * seq
    cap = EC_FACTOR * seq // N_EXPERTS
    assert cap >= TOK_BLK + 16 and seq % DISP_TOK == 0 and lctx % ML_T == 0 and seq % ML_T == 0

    cond = jnp.concatenate([c, c_ctx[None], jnp.zeros((8 - bsz - 1, d), F32)], axis=0)
    mod = _modulation(cond, ada_w, ada_b).reshape(8, N_MOD, d)
    sh1, sc1, g1, sh2, sc2, g2 = [mod[:, i] for i in range(N_MOD)]

    nmain = ds5 + 4 * dml
    w_s5 = w_in[:, :ds5].astype(BF16)
    w_ml = w_in[:, ds5:nmain].astype(BF16)
    w_gate = jnp.pad(w_in[:, nmain:], ((0, 0), (0, LANES - 4 * ML_HEADS))).astype(BF16)
    g0 = norm_g[0:1]
    sc, sh = sc1[:, None], sh1[:, None]
    uc = _in_projection_s5(ctx, g0, sc, sh, lambda b: bsz, w_s5)
    ltot = seq + lctx
    ml5, gates5, ux = _in_projection_ml(x, ctx, g0, sc, sh, bsz, w_ml, w_gate, w_s5)

    m, bp, cpt, avec = _s5_operators(s5_a_re, s5_a_im, s5_log_dt, s5_b_re, s5_b_im, s5_c_re, s5_c_im)
    ys = _s5_scan(ux, uc, m, bp, cpt, avec).reshape(n, ds5)

    ml = ml5.reshape(bsz, ltot, 4 * dml)
    gates = gates5.reshape(bsz, ltot, LANES)
    conv_w8 = jnp.pad(ml_conv_w, ((0, 8 - CONV_K), (0, 0)))
    qscale = jnp.concatenate([jnp.full((1, dml), (dml // ML_HEADS) ** -0.5, F32),
                              jnp.ones((1, dml), F32)], axis=1)
    qk = _conv_silu(ml, conv_w8, ml_conv_b[None], qscale, (0, seq), (seq, ltot))
    gbias = ml_gate_b.reshape(1, 4 * ML_HEADS)
    gates_t = gates[..., :4 * ML_HEADS].swapaxes(1, 2)
    hf, hb = _mlstm_scan(qk, ml, gates, gates_t, gbias, gbias.reshape(4 * ML_HEADS, 1), lctx // ML_T)
    mlo = _ml_post(hf.reshape(ml5.shape[:4] + (dml,)), hb.reshape(ml5.shape[:4] + (dml,)), ml5,
                   ml_norm_g[None], seq).reshape(n, dml)

    mod3 = jnp.stack([g1[:bsz], sc2[:bsz], sh2[:bsz]], axis=1)
    mod3 = jnp.concatenate([mod3, jnp.zeros((bsz, 5, d), F32)], axis=1)
    rw = jnp.pad(router_w, ((0, 0), (0, LANES - N_EXPERTS)))
    rw_hi = rw.astype(BF16)
    rw_lo = (rw - rw_hi.astype(F32)).astype(BF16)
    x1, hx2, aff = _mixer_out(ys, ux.reshape(n, ds5), mlo, x.reshape(n, d), seq, s5_d[None],
                              s5_glu_w.astype(BF16), s5_glu_b[None], w_out.astype(BF16), norm_g, mod3,
                              jnp.concatenate([rw_hi, rw_lo], axis=1))

    nb = seq // TOK_BLK
    aff16 = aff[:, :N_EXPERTS].reshape(bsz, seq, N_EXPERTS)
    aff_t = aff16.swapaxes(1, 2).reshape(bsz, N_EXPERTS, nb, LANES)
    off, idx, gates = _select(aff_t, cap)
    off_flat = off[..., 0].reshape(-1)
    idx_flat = idx[:, :, 0].reshape(-1)
    nslab = d // LANES
    src_start = ((idx[:, :, 0] + (jnp.arange(bsz, dtype=I32) * seq)[:, None, None]) * nslab).reshape(-1)
    xs = _dispatch(src_start, hx2, bsz, N_EXPERTS, cap, nslab)
    ye = _expert_ffn(xs, gates[:, :, 0, :, None], exp_w_gate, exp_w_up, exp_w_down)
    moe = _combine(idx_flat, off_flat, ye, seq, cap)
    out = _final(x1, moe.reshape(n * (d // LANES), LANES), norm_g[3:4], g2[:bsz, None], seq)
    return out.reshape(bsz, seq, d)


def kernel(x, c, ctx, c_ctx, ada_w, ada_b, norm_g, w_in, s5_a_re, s5_a_im, s5_log_dt, s5_b_re, s5_b_im, s5_c_re, s5_c_im, s5_d, s5_glu_w, s5_glu_b, ml_conv_w, ml_conv_b, ml_gate_b, ml_norm_g, w_out, router_w, exp_w_gate, exp_w_up, exp_w_down):
    depth = ada_w.shape[0]
    assert depth == 1, "context outputs are only produced when another layer follows"
    li = 0
    return _layer(x, c, ctx, c_ctx, ada_w[li], ada_b[li], norm_g[li], w_in[li], s5_a_re[li],
                  s5_a_im[li], s5_log_dt[li], s5_b_re[li], s5_b_im[li], s5_c_re[li], s5_c_im[li],
                  s5_d[li], s5_glu_w[li], s5_glu_b[li], ml_conv_w[li], ml_conv_b[li], ml_gate_b[li],
                  ml_norm_g[li], w_out[li], router_w[li], exp_w_gate[li], exp_w_up[li],
                  exp_w_down[li])
```

```python
import functools
import math

import jax
import jax.numpy as jnp
from jax import lax
from jax.experimental import pallas as pl
from jax.experimental.pallas import tpu as pltpu

F32 = jnp.float32
BF16 = jnp.bfloat16
I32 = jnp.int32

EPS = 1e-6
GRID_W = 64
S5_GROUP = 16
S5_STATE = 64
S5_T = 16
ML_HEADS = 4
ML_T = 256
CONV_K = 5
N_EXPERTS = 16
EC_FACTOR = 2
N_MOD = 6
LANES = 128
TOK_BLK = 128
DISP_TOK = 512
DISP_W = DISP_TOK + 8
VMEM_LIMIT = 56 * 1024 * 1024


def _cparams(sem):
    return pltpu.CompilerParams(dimension_semantics=sem, vmem_limit_bytes=VMEM_LIMIT)


def _sigmoid(x):
    return 1.0 / (1.0 + jnp.exp(-x))


def _silu(x):
    return x * _sigmoid(x)


def _log_sigmoid(x):
    return jnp.minimum(x, 0.0) - jnp.log(1.0 + jnp.exp(-jnp.abs(x)))


def _split3(x):
    x1 = x.astype(BF16)
    r = x - x1.astype(F32)
    x2 = r.astype(BF16)
    x3 = (r - x2.astype(F32)).astype(BF16)
    return x1, x2, x3


def _dot(a, b):
    return jnp.dot(a, b, preferred_element_type=F32)


def _dot_nt(a, b):
    return lax.dot_general(a, b, (((1,), (1,)), ((), ())), preferred_element_type=F32)


def _dot_tn(a, b):
    return lax.dot_general(a, b, (((0,), (0,)), ((), ())), preferred_element_type=F32)


def _dot3_left(a_f32, b_bf16):
    a1, a2, a3 = _split3(a_f32)
    return _dot(a1, b_bf16) + _dot(a2, b_bf16) + _dot(a3, b_bf16)


def _dot3_right(a_bf16, b_f32):
    b1, b2, b3 = _split3(b_f32)
    return _dot(a_bf16, b1) + _dot(a_bf16, b2) + _dot(a_bf16, b3)


def _mod_kernel(c_ref, w_ref, b_ref, o_ref):
    c = c_ref[...]
    s = _silu(c).astype(BF16)
    o_ref[...] = _dot(s, w_ref[...].astype(BF16)) + b_ref[...]


def _modulation(cond8, ada_w, ada_b):
    d, n = ada_w.shape
    tn = 1024
    return pl.pallas_call(
        _mod_kernel,
        grid=(n // tn,),
        in_specs=[pl.BlockSpec((8, d), lambda j: (0, 0)),
                  pl.BlockSpec((d, tn), lambda j: (0, j)),
                  pl.BlockSpec((1, tn), lambda j: (0, j))],
        out_specs=pl.BlockSpec((8, tn), lambda j: (0, j)),
        out_shape=jax.ShapeDtypeStruct((8, n), F32),
        compiler_params=_cparams(("parallel",)),
        name="modulation",
    )(cond8, ada_w, ada_b.reshape(1, n))


def _norm_mod_f32(x, g, sc, sh):
    ms = jnp.mean(x * x, axis=-1, keepdims=True)
    h = x * lax.rsqrt(ms + EPS) * g
    return h * (1.0 + sc) + sh


def _norm_mod(x, g, sc, sh):
    return _norm_mod_f32(x, g, sc, sh).astype(BF16)


def _inproj_s5_kernel(x_ref, g_ref, sc_ref, sh_ref, w_ref, o_ref, h_scr):
    @pl.when(pl.program_id(2) == 0)
    def _():
        h_scr[...] = _norm_mod(x_ref[0], g_ref[...], sc_ref[0], sh_ref[0])

    o_ref[0] = _dot(h_scr[...], w_ref[...])


def _in_projection_s5(x, g0, sc, sh, mod_of_batch, w_s5):
    bsz, length, d = x.shape
    ds5 = w_s5.shape[1]
    tm = min(1024, length)
    tn = 512
    return pl.pallas_call(
        _inproj_s5_kernel,
        grid=(bsz, length // tm, ds5 // tn),
        in_specs=[pl.BlockSpec((1, tm, d), lambda b, i, j: (b, i, 0)),
                  pl.BlockSpec((1, d), lambda b, i, j: (0, 0)),
                  pl.BlockSpec((1, 1, d), lambda b, i, j: (mod_of_batch(b), 0, 0)),
                  pl.BlockSpec((1, 1, d), lambda b, i, j: (mod_of_batch(b), 0, 0)),
                  pl.BlockSpec((d, tn), lambda b, i, j: (0, j))],
        out_specs=pl.BlockSpec((1, tm, tn), lambda b, i, j: (b, i, j)),
        out_shape=jax.ShapeDtypeStruct((bsz, length, ds5), F32),
        scratch_shapes=[pltpu.VMEM((tm, d), BF16)],
        compiler_params=_cparams(("parallel", "parallel", "arbitrary")),
        name="in_projection_s5",
    )(x, g0, sc, sh, w_s5)


def _inproj_ml_kernel(x_ref, c_ref, g_ref, sc_ref, sh_ref, csc_ref, csh_ref, w_ref, wg_ref, ws_ref,
                      o_ref, og_ref, u_ref, h_scr, x_scr, hr_scr, *, n_ml_blocks):
    rb = o_ref.shape[3]
    nlb = x_ref.shape[2] // LANES
    j = pl.program_id(2)

    @pl.when(j == 0)
    def _():
        for r in range(rb):
            h = _norm_mod_f32(x_ref[0, r * GRID_W:(r + 1) * GRID_W, :], g_ref[...], sc_ref[0], sh_ref[0])
            hr_scr[r * GRID_W:(r + 1) * GRID_W, :] = h.astype(BF16)
            for c in range(nlb):
                x_scr[c, r * ROW_PITCH:r * ROW_PITCH + GRID_W, :] = h[:, c * LANES:(c + 1) * LANES]
        for w in range(GRID_W):
            hw = jnp.concatenate([x_scr[c, pl.ds(w, rb, stride=ROW_PITCH), :] for c in range(nlb)],
                                 axis=-1)
            h_scr[w * rb:(w + 1) * rb, :] = hw.astype(BF16)
        for s in range(c_ref.shape[1]):
            h_scr[(GRID_W + s) * rb:(GRID_W + s + 1) * rb, :] = _norm_mod(
                c_ref[0, s, 0], g_ref[...], csc_ref[0], csh_ref[0])
        og_ref[0] = _dot(h_scr[...], wg_ref[...]).reshape(og_ref.shape[1:])

    @pl.when(j < n_ml_blocks)
    def _():
        o_ref[0] = _dot(h_scr[...], w_ref[...]).reshape(o_ref.shape[1:]).astype(o_ref.dtype)

    @pl.when(j >= n_ml_blocks)
    def _():
        u_ref[0] = _dot(hr_scr[...], ws_ref[...])


ML_RB = 16
ROW_PITCH = GRID_W + 8


def _in_projection_ml(x, ctx, g0, sc, sh, ctx_mod, w_ml, w_gate, w_s5):
    bsz, length, d = x.shape
    lctx = ctx.shape[1]
    nml = w_ml.shape[1]
    ds5 = w_s5.shape[1]
    tn = 512
    tns = 256
    nmb, nsb = nml // tn, ds5 // tns
    rows = length // GRID_W
    rblocks = rows // ML_RB
    nslot_c = lctx // rows
    assert nslot_c * rows == lctx and rblocks * ML_RB == rows
    n_slots = GRID_W + nslot_c
    tm = ML_RB * GRID_W
    ctx5 = ctx.reshape(bsz, nslot_c, rblocks, ML_RB, d)
    kern = functools.partial(_inproj_ml_kernel, n_ml_blocks=nmb)
    return pl.pallas_call(
        kern,
        grid=(bsz, rblocks, nmb + nsb),
        in_specs=[pl.BlockSpec((1, tm, d), lambda b, i, j: (b, i, 0)),
                  pl.BlockSpec((1, nslot_c, 1, ML_RB, d), lambda b, i, j: (b, 0, i, 0, 0)),
                  pl.BlockSpec((1, d), lambda b, i, j: (0, 0)),
                  pl.BlockSpec((1, 1, d), lambda b, i, j: (b, 0, 0)),
                  pl.BlockSpec((1, 1, d), lambda b, i, j: (b, 0, 0)),
                  pl.BlockSpec((1, 1, d), lambda b, i, j: (ctx_mod, 0, 0)),
                  pl.BlockSpec((1, 1, d), lambda b, i, j: (ctx_mod, 0, 0)),
                  pl.BlockSpec((d, tn), lambda b, i, j: (0, jnp.minimum(j, nmb - 1))),
                  pl.BlockSpec((d, LANES), lambda b, i, j: (0, 0)),
                  pl.BlockSpec((d, tns), lambda b, i, j: (0, jnp.maximum(j - nmb, 0)))],
        out_specs=[pl.BlockSpec((1, n_slots, 1, ML_RB, tn),
                                lambda b, i, j: (b, 0, i, 0, jnp.minimum(j, nmb - 1))),
                   pl.BlockSpec((1, n_slots, 1, ML_RB, LANES), lambda b, i, j: (b, 0, i, 0, 0)),
                   pl.BlockSpec((1, tm, tns), lambda b, i, j: (b, i, jnp.maximum(j - nmb, 0)))],
        out_shape=[jax.ShapeDtypeStruct((bsz, n_slots, rblocks, ML_RB, nml), BF16),
                   jax.ShapeDtypeStruct((bsz, n_slots, rblocks, ML_RB, LANES), F32),
                   jax.ShapeDtypeStruct((bsz, length, ds5), F32)],
        scratch_shapes=[pltpu.VMEM((n_slots * ML_RB, d), BF16),
                        pltpu.VMEM((d // LANES, ML_RB * ROW_PITCH, LANES), F32),
                        pltpu.VMEM((tm, d), BF16)],
        compiler_params=_cparams(("parallel", "parallel", "arbitrary")),
        name="in_projection_ml",
    )(x, ctx5, g0, sc, sh, sc, sh, w_ml, w_gate, w_s5)


def _dot_nt6(a, b):
    a1, a2, a3 = _split3(a)
    b1, b2, b3 = _split3(b)
    return (_dot_nt(a1, b1) + (_dot_nt(a1, b2) + _dot_nt(a2, b1))
            + (_dot_nt(a1, b3) + _dot_nt(a3, b1) + _dot_nt(a2, b2)))


S5_LANES = 4 * S5_STATE


def _s5ops_kernel(prm_ref, bt_ref, c_ref, rexp_ref, rc_ref, place_ref, m_ref, bp_ref, cpt_ref, av_ref):
    t = S5_T
    prm = prm_ref[0]
    lr, li = jnp.minimum(prm[0:1], -1e-4), prm[1:2]
    dt = jnp.exp(prm[2:3])
    zr, zi = lr * dt, li * dt
    tau = lax.broadcasted_iota(I32, (32, S5_LANES), 0).astype(F32)
    mag, ang = jnp.exp(tau * zr), tau * zi
    pwr, pwi = mag * jnp.cos(ang), mag * jnp.sin(ang)
    xr, xi = pwr[1:2] - 1.0, pwi[1:2]
    den = lr * lr + li * li
    qr, qi = (xr * lr + xi * li) / den, (xi * lr - xr * li) / den
    btr, bti = bt_ref[0, 0], bt_ref[0, 1]
    bbr, bbi = qr * btr - qi * bti, qr * bti + qi * btr
    cr, ci = c_ref[0, 0], c_ref[0, 1]

    lane = lax.broadcasted_iota(I32, (t * S5_GROUP, S5_LANES), 1)
    fwd = ((lane // S5_STATE) % 2) == 0

    def expand(r, x):
        return _dot3_right(r, x)

    e_t = [expand(rexp_ref[0], z) for z in (pwr, pwi)]
    e_r = [expand(rexp_ref[1], z) for z in (pwr, pwi)]
    e_p = [expand(rexp_ref[2], z) for z in (pwr, pwi)]
    e_q = [expand(rexp_ref[3], z) for z in (pwr, pwi)]
    pk = [jnp.where(fwd, e_t[i], e_r[i]) for i in range(2)]
    pb = [jnp.where(fwd, e_r[i], e_t[i]) for i in range(2)]
    pc = [jnp.where(fwd, e_p[i], e_q[i]) for i in range(2)]
    cxr, cxi = expand(rc_ref[...], cr), expand(rc_ref[...], ci)
    bxr, bxi = expand(rc_ref[...], bbr), expand(rc_ref[...], bbi)

    wkr, wki = pk[0] * cxr - pk[1] * cxi, pk[0] * cxi + pk[1] * cxr
    row_blk = lax.broadcasted_iota(I32, (4 * S5_GROUP, S5_LANES), 0) // S5_GROUP
    lane_blk = lax.broadcasted_iota(I32, (4 * S5_GROUP, S5_LANES), 1) // S5_STATE
    own = row_blk == lane_blk
    bmr = jnp.where(own, jnp.concatenate([bbr] * 4, axis=0), 0.0)
    bmi = jnp.where(own, jnp.concatenate([bbi] * 4, axis=0), 0.0)
    kt = _dot_nt6(bmr, wkr) - _dot_nt6(bmi, wki)

    ln = lax.broadcasted_iota(I32, (S5_GROUP, t * S5_GROUP), 1)
    for gi in range(2):
        ktf = kt[(2 * gi) * S5_GROUP:(2 * gi + 1) * S5_GROUP]
        ktb = kt[(2 * gi + 1) * S5_GROUP:(2 * gi + 2) * S5_GROUP]
        blocks = []
        for s in range(t):
            f = ktf if s == 0 else pltpu.roll(ktf, S5_GROUP * s, axis=1)
            sh = (t * S5_GROUP - S5_GROUP * (t - 1 - s)) % (t * S5_GROUP)
            bk = ktb if sh == 0 else pltpu.roll(ktb, sh, axis=1)
            blocks.append(jnp.where(ln >= S5_GROUP * s, f, 0.0)
                          + jnp.where(ln < S5_GROUP * (s + 1), bk, 0.0))
        m_ref[0, gi] = jnp.concatenate(blocks, axis=0).astype(BF16)

    inj = jnp.concatenate([pb[0] * bxr - pb[1] * bxi, pb[0] * bxi + pb[1] * bxr], axis=-1).astype(BF16)
    out = jnp.concatenate([pc[0] * cxr - pc[1] * cxi, -(pc[0] * cxi + pc[1] * cxr)], axis=-1).astype(BF16)
    for gi in range(2):
        bp_ref[0, gi * t * S5_GROUP:(gi + 1) * t * S5_GROUP, :] = _dot(inj, place_ref[gi]).astype(BF16)
        cpt_ref[0, gi * t * S5_GROUP:(gi + 1) * t * S5_GROUP, :] = _dot(out, place_ref[gi]).astype(BF16)
    a_t = jnp.concatenate([pwr[t:t + 8], pwi[t:t + 8]], axis=-1)
    av_ref[0] = _dot3_left(a_t, place_ref[0]) + _dot3_left(a_t, place_ref[1])


def _s5_operators(a_re, a_im, log_dt, b_re, b_im, c_re, c_im):
    t = S5_T
    g, p = a_re.shape[1], a_re.shape[2]
    npair = g // 2

    def lanes(z):
        return z.reshape(2, npair, 2, p).transpose(1, 2, 0, 3).reshape(npair, 4 * p)

    prm = jnp.stack([lanes(a_re.astype(F32)), lanes(a_im.astype(F32)),
                     lanes(jnp.broadcast_to(log_dt.astype(F32)[..., None], a_re.shape))], axis=1)
    prm = jnp.concatenate([prm, jnp.zeros((npair, 5, 4 * p), F32)], axis=1)

    def tile_d(z):
        z = z.reshape(npair, 2, S5_GROUP, p).transpose(0, 2, 1, 3)
        return jnp.broadcast_to(z[:, :, :, None, :], (npair, S5_GROUP, 2, 2, p)).reshape(npair, S5_GROUP, 4 * p)

    bt = jnp.stack([tile_d(b_re.astype(F32).swapaxes(1, 2)), tile_d(b_im.astype(F32).swapaxes(1, 2))], axis=1)
    ct = jnp.stack([tile_d(c_re.astype(F32)), tile_d(c_im.astype(F32))], axis=1)

    row = jnp.arange(t * S5_GROUP)
    tau, ch = row // S5_GROUP, row % S5_GROUP
    e = jnp.arange(32)
    rexp = jnp.stack([(expo[:, None] == e[None, :]) for expo in (tau, t - 1 - tau, tau + 1, t - tau)]
                     ).astype(BF16)
    rc = (ch[:, None] == jnp.arange(S5_GROUP)[None, :]).astype(BF16)
    src = jnp.arange(2 * 4 * p)
    reim, gi_s, d_s, p_s = src // (4 * p), (src // (2 * p)) % 2, (src // p) % 2, src % p
    dst = d_s * (4 * p) + reim * (2 * p) + gi_s * p + p_s
    col = jnp.arange(8 * p)
    place = jnp.stack([((dst[:, None] == col[None, :]) & (gi_s[:, None] == gi)) for gi in range(2)]
                      ).astype(BF16)
    blk3 = lambda shape: pl.BlockSpec((1,) + shape, lambda i: (i,) + (0,) * len(shape))
    full = lambda arr: pl.BlockSpec(arr.shape, lambda i: (0,) * arr.ndim)
    m, bp, cpt, av = pl.pallas_call(
        _s5ops_kernel,
        grid=(npair,),
        in_specs=[blk3((8, 4 * p)), blk3((2, S5_GROUP, 4 * p)), blk3((2, S5_GROUP, 4 * p)),
                  full(rexp), full(rc), full(place)],
        out_specs=[blk3((2, t * S5_GROUP, t * S5_GROUP)), blk3((2 * t * S5_GROUP, 8 * p)),
                   blk3((2 * t * S5_GROUP, 8 * p)), blk3((8, 8 * p))],
        out_shape=[jax.ShapeDtypeStruct((npair, 2, t * S5_GROUP, t * S5_GROUP), BF16),
                   jax.ShapeDtypeStruct((npair, 2 * t * S5_GROUP, 8 * p), BF16),
                   jax.ShapeDtypeStruct((npair, 2 * t * S5_GROUP, 8 * p), BF16),
                   jax.ShapeDtypeStruct((npair, 8, 8 * p), F32)],
        compiler_params=_cparams(("parallel",)),
        name="s5_operators",
    )(prm, bt, ct, rexp, rc, place)
    avec = av[:, 0].reshape(npair, 4, 2 * p)
    return m, bp, cpt, jnp.concatenate([avec, jnp.zeros_like(avec)], axis=1)


def _s5_permutation():
    i = jnp.arange(8 * 8 * S5_GROUP)
    t8, j, c = i // (8 * S5_GROUP), (i // S5_GROUP) % 8, i % S5_GROUP
    dst = j * (8 * S5_GROUP) + t8 * S5_GROUP + c
    e1 = (dst[:, None] == i[None, :]).astype(BF16)
    return e1, e1.T


S5_PAIRS = 4


def _s5_kernel(ux_ref, uc_ref, e1_ref, e1t_ref, m_ref, bp_ref, cp_ref, av_ref, y_ref, s_scr, hp_scr,
               tab_scr, *, cc, xc):
    nc = cc + xc

    def chunk_rows(ref, nchunks):
        cols = [ref[0, pl.ds(t, nchunks, stride=S5_T), :].astype(BF16) for t in range(S5_T)]
        return [_dot(jnp.concatenate(cols[8 * h:8 * h + 8], axis=-1), e1_ref[...]).astype(BF16)
                for h in range(2)]

    def pair_lanes(halves, p):
        return jnp.concatenate([halves[h][:, j * LANES:(j + 1) * LANES]
                                for j in (2 * p, 2 * p + 1) for h in range(2)], axis=-1)

    pc = chunk_rows(uc_ref, cc)
    px = chunk_rows(ux_ref, xc)
    for p in range(S5_PAIRS):
        s_scr[0:cc, p * 512:(p + 1) * 512] = _dot(pair_lanes(pc, p), bp_ref[p])
        s_scr[cc:nc, p * 512:(p + 1) * 512] = _dot(pair_lanes(px, p), bp_ref[p])

    av = av_ref[...]
    sub = lax.broadcasted_iota(I32, (8, 128), 0)

    def cmul(ar, ai, br, bi):
        return ar * br - ai * bi, ar * bi + ai * br

    for p in range(S5_PAIRS):
        for d in range(2):
            a1 = (av[p, 2 * d:2 * d + 1], av[p, 2 * d + 1:2 * d + 2])
            a2 = cmul(*a1, *a1)
            a4 = cmul(*a2, *a2)
            pw_re, pw_im = jnp.zeros((8, 128), F32), jnp.zeros((8, 128), F32)
            cur = a1
            for k in range(8):
                r = k if d == 0 else 7 - k
                pw_re = jnp.where(sub == r, cur[0], pw_re)
                pw_im = jnp.where(sub == r, cur[1], pw_im)
                cur = cmul(*cur, *a1)
            for q, z in enumerate((a1, a2, a4)):
                tab_scr[p, d, 2 * q] = jnp.broadcast_to(z[0], (8, 128))
                tab_scr[p, d, 2 * q + 1] = jnp.broadcast_to(z[1], (8, 128))
            tab_scr[p, d, 6] = pw_re
            tab_scr[p, d, 7] = pw_im

    def scan8(s_re, s_im, h_re, h_im, p, d):
        back = d == 1
        x_re, x_im = s_re, s_im
        for q, k in enumerate((1, 2, 4)):
            keep = (sub < 8 - k) if back else (sub >= k)
            sh = (8 - k) if back else k
            y_re = jnp.where(keep, pltpu.roll(x_re, sh, axis=0), 0.0)
            y_im = jnp.where(keep, pltpu.roll(x_im, sh, axis=0), 0.0)
            m_re, m_im = cmul(tab_scr[p, d, 2 * q], tab_scr[p, d, 2 * q + 1], y_re, y_im)
            x_re, x_im = x_re + m_re, x_im + m_im
        c_re, c_im = cmul(tab_scr[p, d, 6], tab_scr[p, d, 7], h_re, h_im)
        x_re, x_im = x_re + c_re, x_im + c_im
        edge = 7 if back else 0
        sh = 7 if back else 1
        e_re = jnp.where(sub == edge, h_re, pltpu.roll(x_re, sh, axis=0))
        e_im = jnp.where(sub == edge, h_im, pltpu.roll(x_im, sh, axis=0))
        last = 0 if back else 7
        return e_re, e_im, x_re[last:last + 1], x_im[last:last + 1]

    def step(row_f, row_b, carry):
        rf = pl.multiple_of(row_f, 8)
        rb = pl.multiple_of(row_b, 8)
        new = []
        for p in range(S5_PAIRS):
            hr, hi, gr, gi = carry[4 * p:4 * p + 4]
            c0 = p * 512
            p_re, p_im, hr, hi = scan8(s_scr[pl.ds(rf, 8), c0:c0 + 128], s_scr[pl.ds(rf, 8), c0 + 128:c0 + 256],
                                       hr, hi, p, 0)
            hp_scr[pl.ds(rf, 8), c0:c0 + 128] = p_re
            hp_scr[pl.ds(rf, 8), c0 + 128:c0 + 256] = p_im
            p_re, p_im, gr, gi = scan8(s_scr[pl.ds(rb, 8), c0 + 256:c0 + 384], s_scr[pl.ds(rb, 8), c0 + 384:c0 + 512],
                                       gr, gi, p, 1)
            hp_scr[pl.ds(rb, 8), c0 + 256:c0 + 384] = p_re
            hp_scr[pl.ds(rb, 8), c0 + 384:c0 + 512] = p_im
            new += [hr, hi, gr, gi]
        return tuple(new)

    z = jnp.zeros((1, 128), F32)
    carry = lax.fori_loop(0, cc // 8, lambda i, c: step(8 * i, cc - 8 - 8 * i, c), (z,) * (4 * S5_PAIRS))
    lax.fori_loop(0, xc // 8, lambda i, c: step(cc + 8 * i, nc - 8 - 8 * i, c), carry)

    ys = []
    for p in range(S5_PAIRS):
        ux = pair_lanes(px, p)
        y = _dot_nt(hp_scr[cc:nc, p * 512:(p + 1) * 512].astype(BF16), cp_ref[p])
        ys.append(y + jnp.concatenate([_dot(ux[:, 0:256], m_ref[p, 0]), _dot(ux[:, 256:512], m_ref[p, 1])],
                                      axis=-1))
    for h in range(2):
        yh = jnp.concatenate([ys[j // 2][:, (j % 2) * 256 + h * LANES:(j % 2) * 256 + (h + 1) * LANES]
                              for j in range(8)], axis=-1).astype(BF16)
        r = _dot(yh, e1t_ref[...])
        for t8 in range(8):
            y_ref[0, pl.ds(8 * h + t8, xc, stride=S5_T), :] = r[:, t8 * LANES:(t8 + 1) * LANES]


def _s5_scan(ux, uc, m, bp, cp, avec):
    bsz, seq, ds5 = ux.shape
    lctx = uc.shape[1]
    cc, xc = lctx // S5_T, seq // S5_T
    assert cc % 8 == 0 and xc % 8 == 0
    e1, e1t = _s5_permutation()
    kern = functools.partial(_s5_kernel, cc=cc, xc=xc)
    nblk = ds5 // LANES
    const = lambda j, b: (0, 0)
    return pl.pallas_call(
        kern,
        grid=(nblk, bsz),
        in_specs=[pl.BlockSpec((1, seq, LANES), lambda j, b: (b, 0, j)),
                  pl.BlockSpec((1, lctx, LANES), lambda j, b: (b, 0, j)),
                  pl.BlockSpec(e1.shape, const),
                  pl.BlockSpec(e1.shape, const),
                  pl.BlockSpec((S5_PAIRS, 2, 256, 256), lambda j, b: (j, 0, 0, 0)),
                  pl.BlockSpec((S5_PAIRS, 512, 512), lambda j, b: (j, 0, 0)),
                  pl.BlockSpec((S5_PAIRS, 512, 512), lambda j, b: (j, 0, 0)),
                  pl.BlockSpec((S5_PAIRS, 8, 128), lambda j, b: (j, 0, 0))],
        out_specs=pl.BlockSpec((1, seq, LANES), lambda j, b: (b, 0, j)),
        out_shape=jax.ShapeDtypeStruct((bsz, seq, ds5), F32),
        scratch_shapes=[pltpu.VMEM((cc + xc, S5_PAIRS * 512), F32),
                        pltpu.VMEM((cc + xc, S5_PAIRS * 512), F32),
                        pltpu.VMEM((S5_PAIRS, 2, 8, 8, 128), F32)],
        compiler_params=_cparams(("parallel", "parallel")),
        name="s5_scan",
    )(ux, uc, e1, e1t, m, bp, cp, avec)


def _conv_kernel(x_ref, p_ref, n_ref, w_ref, b_ref, sc_ref, o_ref, scr, *, rows, seq_tiles):
    i = pl.program_id(1)
    first = functools.reduce(jnp.logical_or, [i == s for s in seq_tiles[0]])
    last = functools.reduce(jnp.logical_or, [i == s for s in seq_tiles[1]])
    pm = jnp.where(first, 0.0, 1.0)
    nm = jnp.where(last, 0.0, 1.0)
    scr[0:8, :] = p_ref[0, 8:16, :].astype(F32) * pm
    scr[8:8 + rows, :] = x_ref[0].astype(F32)
    scr[8 + rows:16 + rows, :] = n_ref[0, 0:8, :].astype(F32) * nm
    w = w_ref[...]
    acc = scr[6:6 + rows, :] * w[0:1]
    for j in range(1, CONV_K):
        acc = acc + scr[6 + j:6 + j + rows, :] * w[j:j + 1]
    z = acc + b_ref[...]
    o_ref[0] = (_silu(z) * sc_ref[...]).astype(o_ref.dtype)


def _conv_silu(ml, conv_w8, conv_b, scale, seq_starts, seq_ends):
    bsz, ltot, _ = ml.shape
    rows = 256
    ct = 1024
    nch = conv_b.shape[-1]
    nt = ltot // rows
    hb = rows // 16
    nhb = ltot // 16
    seq_tiles = (tuple(s // rows for s in seq_starts), tuple(e // rows - 1 for e in seq_ends))
    kern = functools.partial(_conv_kernel, rows=rows, seq_tiles=seq_tiles)
    return pl.pallas_call(
        kern,
        grid=(bsz, nt, nch // ct),
        in_specs=[pl.BlockSpec((1, rows, ct), lambda b, i, c: (b, i, c)),
                  pl.BlockSpec((1, 16, ct), lambda b, i, c: (b, jnp.maximum(i * hb - 1, 0), c)),
                  pl.BlockSpec((1, 16, ct), lambda b, i, c: (b, jnp.minimum((i + 1) * hb, nhb - 1), c)),
                  pl.BlockSpec((8, ct), lambda b, i, c: (0, c)),
                  pl.BlockSpec((1, ct), lambda b, i, c: (0, c)),
                  pl.BlockSpec((1, ct), lambda b, i, c: (0, c))],
        out_specs=pl.BlockSpec((1, rows, ct), lambda b, i, c: (b, i, c)),
        out_shape=jax.ShapeDtypeStruct((bsz, ltot, nch), BF16),
        scratch_shapes=[pltpu.VMEM((rows + 16, ct), F32)],
        compiler_params=_cparams(("parallel", "parallel", "parallel")),
        name="conv_silu",
    )(ml, ml, ml, conv_w8, conv_b, scale)


def _mlstm_chain_step(q, k, v, i_col, f_col, i_row, b_col, b_row, mask, c_ref, n_ref, m_ref, idx):
    m_st = m_ref[idx]
    d = jnp.where(mask, b_col - b_row + i_row, -jnp.inf)
    inter = b_col + m_st
    m_t = jnp.maximum(inter, jnp.max(d, axis=-1, keepdims=True))
    w = jnp.exp(d - m_t)
    s_inter = jnp.exp(inter - m_t)
    s = _dot_nt(q, k) * w
    c_st = c_ref[idx]
    n_st = n_ref[idx]
    num = _dot(s.astype(BF16), v) + s_inter * _dot_nt(q, c_st.astype(BF16))
    qf = q.astype(F32)
    den = jnp.sum(s, axis=-1, keepdims=True) + s_inter * jnp.sum(qf * n_st, axis=-1, keepdims=True)
    h = num / jnp.maximum(jnp.abs(den), jnp.exp(-m_t))
    total = jnp.sum(f_col, axis=0, keepdims=True)
    d_end = total - b_col + i_col
    m_new = jnp.maximum(total + m_st, jnp.max(d_end, axis=0, keepdims=True))
    w_end = jnp.exp(d_end - m_new)
    dec = jnp.exp(total + m_st - m_new)
    vw = (v.astype(F32) * w_end).astype(BF16)
    c_ref[idx] = dec * c_st + _dot_tn(vw, k)
    n_ref[idx] = dec * n_st + jnp.sum(k.astype(F32) * w_end, axis=0, keepdims=True)
    m_ref[idx] = m_new
    return h


def _mlstm_kernel(qf_ref, kf_ref, vf_ref, gf_ref, gtf_ref, qb_ref, kb_ref, vb_ref, gb_ref, gtb_ref,
                  gbias_ref, gbias_t_ref, hf_ref, hb_ref, c_scr, n_scr, m_scr, *, dh):
    @pl.when(pl.program_id(1) == 0)
    def _():
        c_scr[...] = jnp.zeros_like(c_scr)
        n_scr[...] = jnp.zeros_like(n_scr)
        m_scr[...] = jnp.zeros_like(m_scr)

    bsz, t = qf_ref.shape[0], qf_ref.shape[1]
    rid = lax.broadcasted_iota(I32, (t, t), 0)
    cid = lax.broadcasted_iota(I32, (t, t), 1)
    lower = jnp.where(rid >= cid, 1.0, 0.0).astype(BF16)
    upper = jnp.where(cid >= rid, 1.0, 0.0).astype(BF16)

    for bb in range(bsz):
        for d, (q_ref, k_ref, v_ref, g_ref, gt_ref, h_ref) in enumerate(
                [(qf_ref, kf_ref, vf_ref, gf_ref, gtf_ref, hf_ref),
                 (qb_ref, kb_ref, vb_ref, gb_ref, gtb_ref, hb_ref)]):
            backward = d == 1
            g = g_ref[bb, :, 0:16] + gbias_ref[...]
            gt = gt_ref[bb] + gbias_t_ref[...]
            lf = _log_sigmoid(g)
            lft = _log_sigmoid(gt)
            cum_col = _dot3_right(upper if backward else lower, lf)
            cum_row = _dot3_left(lft, lower if backward else upper)
            for hd in range(ML_HEADS):
                ci = 2 * ML_HEADS * d + hd
                cf = ci + ML_HEADS
                sl = slice(hd * dh, (hd + 1) * dh)
                h = _mlstm_chain_step(
                    q_ref[bb, :, sl], k_ref[bb, :, sl], v_ref[bb, :, sl],
                    g[:, ci:ci + 1], lf[:, cf:cf + 1], gt[ci:ci + 1, :],
                    cum_col[:, cf:cf + 1], cum_row[cf:cf + 1, :],
                    (cid >= rid) if backward else (rid >= cid),
                    c_scr, n_scr, m_scr, (bb * 2 + d) * ML_HEADS + hd)
                h_ref[bb, :, sl] = h.astype(h_ref.dtype)


def _mlstm_scan(qk, ml, gates, gates_t, gbias, gbias_t, n_ctx_chunks):
    bsz, ltot, dml2 = qk.shape
    dml = dml2 // 2
    dh = dml // ML_HEADS
    t = ML_T
    nchunk = ltot // t
    cc = n_ctx_chunks
    nx = nchunk - cc

    def fwd(s):
        return jnp.where(s < cc, nx + s, s - cc)

    def bwd(s):
        return jnp.where(s < cc, nx + cc - 1 - s, nx - 1 - (s - cc))

    bps = 1

    def specs(cmap):
        return [pl.BlockSpec((bps, t, dml), lambda b, s: (b, cmap(s), 0)),
                pl.BlockSpec((bps, t, dml), lambda b, s: (b, cmap(s), 1)),
                pl.BlockSpec((bps, t, dml), lambda b, s: (b, cmap(s), 2)),
                pl.BlockSpec((bps, t, LANES), lambda b, s: (b, cmap(s), 0)),
                pl.BlockSpec((bps, 16, t), lambda b, s: (b, 0, cmap(s)))]

    kern = functools.partial(_mlstm_kernel, dh=dh)
    nchain = bps * 2 * ML_HEADS
    return pl.pallas_call(
        kern,
        grid=(bsz // bps, nchunk),
        in_specs=specs(fwd) + specs(bwd) + [pl.BlockSpec((1, 16), lambda b, s: (0, 0)),
                                            pl.BlockSpec((16, 1), lambda b, s: (0, 0))],
        out_specs=[pl.BlockSpec((bps, t, dml), lambda b, s: (b, fwd(s), 0)),
                   pl.BlockSpec((bps, t, dml), lambda b, s: (b, bwd(s), 0))],
        out_shape=[jax.ShapeDtypeStruct((bsz, ltot, dml), BF16)] * 2,
        scratch_shapes=[pltpu.VMEM((nchain, dh, dh), F32),
                        pltpu.VMEM((nchain, 1, dh), F32),
                        pltpu.VMEM((nchain, 1, 1), F32)],
        compiler_params=_cparams(("parallel", "arbitrary")),
        name="mlstm_scan",
    )(qk, qk, ml, gates, gates_t, qk, qk, ml, gates, gates_t, gbias, gbias_t)


COL_PITCH = ML_RB + 8


def _mlpost_kernel(hf_ref, hb_ref, o_ref, mg_ref, out_ref, scr, *, dh):
    nw, _, rb, dml = hf_ref.shape[1:]
    h = (hf_ref[0].astype(F32) + hb_ref[0].astype(F32)).reshape(nw * rb, dml)
    parts = []
    for hd in range(ML_HEADS):
        hh = h[:, hd * dh:(hd + 1) * dh]
        parts.append(hh * lax.rsqrt(jnp.mean(hh * hh, axis=-1, keepdims=True) + EPS))
    hn = jnp.concatenate(parts, axis=-1) * mg_ref[...]
    res = hn * _sigmoid(o_ref[0].astype(F32).reshape(nw * rb, dml))
    nlb = dml // LANES
    for c in range(nlb):
        for w in range(nw):
            scr[c, w * COL_PITCH:w * COL_PITCH + rb, :] = res[w * rb:(w + 1) * rb, c * LANES:(c + 1) * LANES]
    for r in range(rb):
        row = jnp.concatenate([scr[c, pl.ds(r, nw, stride=COL_PITCH), :] for c in range(nlb)], axis=-1)
        out_ref[0, r * nw:(r + 1) * nw, :] = row.astype(out_ref.dtype)


def _ml_post(hf5, hb5, ml5, ml_g, seq):
    bsz, _, rblocks, rb, dml = hf5.shape
    kern = functools.partial(_mlpost_kernel, dh=dml // ML_HEADS)
    blk = (1, GRID_W, 1, rb, dml)
    return pl.pallas_call(
        kern,
        grid=(bsz, rblocks),
        in_specs=[pl.BlockSpec(blk, lambda b, i: (b, 0, i, 0, 0)),
                  pl.BlockSpec(blk, lambda b, i: (b, 0, i, 0, 0)),
                  pl.BlockSpec(blk, lambda b, i: (b, 0, i, 0, 3)),
                  pl.BlockSpec((1, dml), lambda b, i: (0, 0))],
        out_specs=pl.BlockSpec((1, rb * GRID_W, dml), lambda b, i: (b, i, 0)),
        out_shape=jax.ShapeDtypeStruct((bsz, seq, dml), BF16),
        scratch_shapes=[pltpu.VMEM((dml // LANES, GRID_W * COL_PITCH, LANES), F32)],
        compiler_params=_cparams(("parallel", "parallel")),
        name="mlstm_post",
    )(hf5, hb5, ml5, ml_g)


def _gelu_tanh(y):
    return 0.5 * y * (1.0 + jnp.tanh(math.sqrt(2.0 / math.pi) * (y + 0.044715 * (y * y * y))))


def _mixout_kernel(ys_ref, u_ref, ml_ref, x_ref, dsk_ref, gw_ref, gb_ref, wo_ref, ng_ref, mod_ref,
                   rw_ref, x1_ref, hx_ref, aff_ref, *, n_exp):
    y = ys_ref[...] + dsk_ref[...] * u_ref[...]
    ge = _gelu_tanh(y)
    z = _dot(ge.astype(BF16), gw_ref[...]) + gb_ref[...]
    s5 = (ge * _sigmoid(z)).astype(BF16)
    yx = _dot(jnp.concatenate([s5, ml_ref[...]], axis=-1), wo_ref[...])
    ng = ng_ref[...]
    mod = mod_ref[0]
    r1 = yx * lax.rsqrt(jnp.mean(yx * yx, axis=-1, keepdims=True) + EPS) * ng[1:2]
    x1 = x_ref[...] + mod[0:1] * r1
    x1_ref[...] = x1
    hx = x1 * lax.rsqrt(jnp.mean(x1 * x1, axis=-1, keepdims=True) + EPS) * ng[2:3]
    hx = hx * (1.0 + mod[1:2]) + mod[2:3]
    nslab = hx.shape[1] // LANES
    for cb in range(nslab):
        hx_ref[pl.ds(cb, hx.shape[0], stride=nslab), :] = hx[:, cb * LANES:(cb + 1) * LANES]
    h1, h2, _ = _split3(hx)
    rw2 = rw_ref[...]
    l1 = _dot(h1, rw2)
    logits = (l1[:, 0:LANES] + l1[:, LANES:2 * LANES]) + _dot(h2, rw2[:, 0:LANES])
    lane = lax.broadcasted_iota(I32, logits.shape, 1)
    logits = jnp.where(lane < n_exp, logits, -jnp.inf)
    e = jnp.exp(logits - jnp.max(logits, axis=-1, keepdims=True))
    aff_ref[...] = e / jnp.sum(e, axis=-1, keepdims=True)


def _mixer_out(ys, u, mlo, x2d, rows_per_batch, dsk, glu_w, glu_b, w_out, norm_g, mod3, rw2):
    n, d = x2d.shape
    ds5 = ys.shape[1]
    dml = mlo.shape[1]
    tm = 256
    tiles_per_batch = rows_per_batch // tm
    kern = functools.partial(_mixout_kernel, n_exp=N_EXPERTS)
    row = lambda i: (i, 0)
    const = lambda i: (0, 0)
    return pl.pallas_call(
        kern,
        grid=(n // tm,),
        in_specs=[pl.BlockSpec((tm, ds5), row),
                  pl.BlockSpec((tm, ds5), row),
                  pl.BlockSpec((tm, dml), row),
                  pl.BlockSpec((tm, d), row),
                  pl.BlockSpec((1, ds5), const),
                  pl.BlockSpec((ds5, ds5), const),
                  pl.BlockSpec((1, ds5), const),
                  pl.BlockSpec((d, d), const),
                  pl.BlockSpec((4, d), const),
                  pl.BlockSpec((1, 8, d), lambda i: (i // tiles_per_batch, 0, 0)),
                  pl.BlockSpec((d, 2 * LANES), const)],
        out_specs=[pl.BlockSpec((tm, d), row),
                   pl.BlockSpec((tm * (d // LANES), LANES), row),
                   pl.BlockSpec((tm, LANES), row)],
        out_shape=[jax.ShapeDtypeStruct((n, d), F32),
                   jax.ShapeDtypeStruct((n * (d // LANES), LANES), F32),
                   jax.ShapeDtypeStruct((n, LANES), F32)],
        compiler_params=_cparams(("parallel",)),
        name="mixer_out",
    )(ys, u, mlo, x2d, dsk, glu_w, glu_b, w_out, norm_g, mod3, rw2)


def _count(mask):
    c = jnp.sum(jnp.where(mask, 1.0, 0.0), axis=2, keepdims=True)
    return jnp.sum(c, axis=1, keepdims=True)


SELECT_BISECTIONS = 48


def _pow2_neg(n):
    p = jnp.ones(n.shape, F32)
    for i in range(7):
        p = p * jnp.where(((n >> i) & 1) == 1, 2.0 ** -(2 ** i), 1.0)
    return p


def _select_kernel(aff_ref, off_ref, idx_ref, gate_ref, *, cap, n_tok):
    a = aff_ref[0]
    e, nb, _ = a.shape

    def enough(thr):
        return _count(a >= thr) >= cap

    ex_lo = jnp.full((e, 1, 1), -127, I32)
    ex_hi = jnp.full((e, 1, 1), 1, I32)
    for _ in range(7):
        ex_mid = (ex_lo + ex_hi) >> 1
        ok = enough(_pow2_neg(-ex_mid))
        ex_lo = jnp.where(ok, ex_mid, ex_lo)
        ex_hi = jnp.where(ok, ex_hi, ex_mid)
    lo = jnp.where(ex_lo == -127, 0.0, _pow2_neg(jnp.minimum(-ex_lo, 126)))
    hi = jnp.where(ex_hi == 1, 2.0, _pow2_neg(jnp.maximum(-ex_hi, 0)))
    for _ in range(SELECT_BISECTIONS):
        mid = lo + 0.5 * (hi - lo)
        ok = enough(mid)
        lo = jnp.where(ok, mid, lo)
        hi = jnp.where(ok, hi, mid)
    gt = a >= hi
    tie = (a >= lo) & (a < hi)
    need = cap - _count(gt)
    tok = (lax.broadcasted_iota(I32, a.shape, 1) * LANES + lax.broadcasted_iota(I32, a.shape, 2))
    cut = jnp.zeros((e, 1, 1), I32)
    for bit in range(n_tok.bit_length() - 1, -1, -1):
        cand = cut | (1 << bit)
        cut = jnp.where(_count(tie & (tok < cand)) < need, cand, cut)
    sel = gt | (tie & (tok <= cut))
    self32 = jnp.where(sel, 1.0, 0.0).reshape(e * nb, LANES)
    r = lax.broadcasted_iota(I32, (LANES, LANES), 0)
    c = lax.broadcasted_iota(I32, (LANES, LANES), 1)
    incl = _dot(self32.astype(BF16), jnp.where(r <= c, 1.0, 0.0).astype(BF16))
    tot = jnp.broadcast_to(incl[:, LANES - 1:LANES], (e * nb, LANES)).astype(BF16)
    rr = lax.broadcasted_iota(I32, (e * nb, e * nb), 0)
    cc = lax.broadcasted_iota(I32, (e * nb, e * nb), 1)
    same = (rr // nb) == (cc // nb)
    before = jnp.where(same & (cc < rr), 1.0, 0.0).astype(BF16)
    off = _dot(before, tot)
    off_ref[0] = off.astype(I32).reshape(e, nb, LANES)

    kcol = lax.broadcasted_iota(I32, (cap, nb), 0).astype(F32)
    jrow = lax.broadcasted_iota(I32, (cap, nb), 1).astype(F32)
    lane = lax.broadcasted_iota(I32, (cap, LANES), 1)
    lane_f = lane.astype(F32)
    ones8 = jnp.ones((8, LANES), BF16)
    jr = lax.broadcasted_iota(I32, (nb, nb), 0)
    jc = lax.broadcasted_iota(I32, (nb, nb), 1)
    upper = jnp.where(jr <= jc, 1.0, 0.0).astype(BF16)
    sel3 = self32.reshape(e, nb, LANES)
    incl3 = incl.reshape(e, nb, LANES)

    def to_row(col):
        return _dot_nt(ones8, jnp.where(lane == 0, col, 0.0).astype(BF16))

    for ex in range(e):
        tot_row = _dot_nt(ones8, sel3[ex].astype(BF16))[0:1]
        end_row = _dot(jnp.broadcast_to(tot_row, (8, nb)).astype(BF16), upper)[0:1]
        start_row = end_row - tot_row
        jk = jnp.sum(jnp.where(end_row <= kcol, 1.0, 0.0), axis=1, keepdims=True)
        onehot_j = jnp.where(jrow == jk, 1.0, 0.0)
        kk = kcol[:, 0:1] - jnp.sum(onehot_j * start_row, axis=1, keepdims=True)
        d = _dot(onehot_j.astype(BF16), incl3[ex].astype(BF16))
        tl = jnp.sum(jnp.where(d <= kk, 1.0, 0.0), axis=1, keepdims=True)
        dg = _dot3_right(onehot_j.astype(BF16), a[ex])
        gate = jnp.sum(jnp.where(lane_f == tl, dg, 0.0), axis=1, keepdims=True)
        idx_ref[0, ex] = (to_row(jk) * float(LANES) + to_row(tl)).astype(I32)
        g1, g2, g3 = _split3(gate)
        gate_ref[0, ex] = (to_row(g1.astype(F32)) + to_row(g2.astype(F32))) + to_row(g3.astype(F32))


def _select(aff_t, cap):
    bsz, e, nb, _ = aff_t.shape
    kern = functools.partial(_select_kernel, cap=cap, n_tok=nb * LANES)
    blk = pl.BlockSpec((1, e, nb, LANES), lambda b: (b, 0, 0, 0))
    slot = pl.BlockSpec((1, e, 8, cap), lambda b: (b, 0, 0, 0))
    return pl.pallas_call(
        kern,
        grid=(bsz,),
        in_specs=[blk],
        out_specs=[blk, slot, slot],
        out_shape=[jax.ShapeDtypeStruct(aff_t.shape, I32),
                   jax.ShapeDtypeStruct((bsz, e, 8, cap), I32),
                   jax.ShapeDtypeStruct((bsz, e, 8, cap), F32)],
        compiler_params=_cparams(("parallel",)),
        name="ec_select",
    )(aff_t)


SLAB_PITCH = 24


def _row_copy(hx_hbm, xbuf, sem, src_start, dst_row, nslab):
    return pltpu.make_async_copy(hx_hbm.at[pl.ds(pl.multiple_of(src_start, nslab), nslab)],
                                 xbuf.at[pl.ds(pl.multiple_of(dst_row * SLAB_PITCH, 8), nslab)], sem)


def _dispatch_kernel(idx_ref, hx_hbm, o_ref, xbuf, sems, *, cap, n_tok, n_exp, nslab, n_steps):
    step = pl.program_id(0) * n_exp + pl.program_id(1)

    def issue_all(s):
        half = s % 2

        def issue(k2, _):
            for pr in range(2):
                k = 2 * k2 + pr
                _row_copy(hx_hbm, xbuf, sems.at[half], idx_ref[s * cap + k], half * cap + k, nslab).start(
                    priority=pr)
            return 0

        lax.fori_loop(0, cap // 2, issue, 0, unroll=4)

    @pl.when(step == 0)
    def _():
        issue_all(step)

    @pl.when(step + 1 < n_steps)
    def _():
        issue_all(step + 1)

    half = step % 2

    def drain(k, _):
        _row_copy(hx_hbm, xbuf, sems.at[half], 0, half * cap + k, nslab).wait()
        return 0

    lax.fori_loop(0, cap, drain, 0, unroll=8)
    first = pl.multiple_of(half * cap * SLAB_PITCH, 8)
    for c in range(nslab):
        o_ref[0, 0, :, c * LANES:(c + 1) * LANES] = (
            xbuf[pl.ds(first + c, cap, stride=SLAB_PITCH), :].astype(o_ref.dtype))


def _dispatch(idx_flat, hx_slabs, bsz, n_exp, cap, nslab):
    n_tok = hx_slabs.shape[0] // (bsz * nslab)
    d = nslab * LANES
    kern = functools.partial(_dispatch_kernel, cap=cap, n_tok=n_tok, n_exp=n_exp, nslab=nslab,
                             n_steps=bsz * n_exp)
    return pl.pallas_call(
        kern,
        grid_spec=pltpu.PrefetchScalarGridSpec(
            num_scalar_prefetch=1,
            grid=(bsz, n_exp),
            in_specs=[pl.BlockSpec(memory_space=pl.ANY)],
            out_specs=pl.BlockSpec((1, 1, cap, d), lambda b, e, idx: (b, e, 0, 0)),
            scratch_shapes=[pltpu.VMEM((2 * cap * SLAB_PITCH, LANES), F32),
                            pltpu.SemaphoreType.DMA((2,))]),
        out_shape=jax.ShapeDtypeStruct((bsz, n_exp, cap, d), BF16),
        compiler_params=_cparams(("arbitrary", "arbitrary")),
        name="ec_dispatch",
    )(idx_flat, hx_slabs)


FFN_COL_BLOCKS = 4


def _ffn_kernel(x_ref, g_ref, wg_ref, wu_ref, wd_ref, o_ref, acc):
    f = pl.program_id(2)

    @pl.when(f == 0)
    def _():
        acc[...] = jnp.zeros_like(acc)

    x = x_ref[0, 0]
    a = _dot(x, wg_ref[0].astype(BF16))
    u = _dot(x, wu_ref[0].astype(BF16))
    hmid = (_silu(a) * u).astype(BF16)
    wd = wd_ref[0].astype(BF16)
    cw = acc.shape[1] // FFN_COL_BLOCKS
    for cb in range(FFN_COL_BLOCKS):
        acc[:, cb * cw:(cb + 1) * cw] += _dot(hmid, wd[:, cb * cw:(cb + 1) * cw])

    @pl.when(f == pl.num_programs(2) - 1)
    def _():
        g = g_ref[0, 0]
        cap = g.shape[0]
        for c in range(acc.shape[1] // LANES):
            o_ref[0, 0, c // 8, pl.ds(c % 8, cap, stride=8), :] = acc[:, c * LANES:(c + 1) * LANES] * g


def _expert_ffn(xs, gates, w_gate, w_up, w_down):
    bsz, n_exp, cap, d = xs.shape
    ff = w_gate.shape[2]
    tf = 256
    nhalf = d // (8 * LANES)
    return pl.pallas_call(
        _ffn_kernel,
        grid=(n_exp, bsz, ff // tf),
        in_specs=[pl.BlockSpec((1, 1, cap, d), lambda e, b, f: (b, e, 0, 0)),
                  pl.BlockSpec((1, 1, cap, 1), lambda e, b, f: (b, e, 0, 0)),
                  pl.BlockSpec((1, d, tf), lambda e, b, f: (e, 0, f)),
                  pl.BlockSpec((1, d, tf), lambda e, b, f: (e, 0, f)),
                  pl.BlockSpec((1, tf, d), lambda e, b, f: (e, f, 0))],
        out_specs=pl.BlockSpec((1, 1, nhalf, cap * 8, LANES), lambda e, b, f: (b, e, 0, 0, 0)),
        out_shape=jax.ShapeDtypeStruct((bsz, n_exp, nhalf, cap * 8, LANES), F32),
        scratch_shapes=[pltpu.VMEM((cap, d), F32)],
        compiler_params=_cparams(("parallel", "parallel", "arbitrary")),
        name="expert_ffn",
    )(xs, gates, w_gate, w_up, w_down)


COMB_UNROLL = 8


def _combine_kernel(idx_ref, off_ref, y_ref, o_ref, *, cap, nb, n_exp, tok_part, n_part):
    b = pl.program_id(0)
    part = pl.program_id(1)
    ex = pl.program_id(3)

    @pl.when(ex == 0)
    def _():
        o_ref[...] = jnp.zeros_like(o_ref)

    be = b * n_exp + ex
    blocks_per_part = nb // n_part
    k_lo = off_ref[be * nb + part * blocks_per_part]
    k_hi = jnp.where(part == n_part - 1, cap,
                     off_ref[be * nb + jnp.minimum((part + 1) * blocks_per_part, nb - 1)])
    t0 = part * tok_part

    def add_rows(k, n):
        toks = [idx_ref[be * cap + k + i] - t0 for i in range(n)]
        rows = [o_ref[0, toks[i]] + y_ref[0, 0, 0, pl.ds(pl.multiple_of((k + i) * 8, 8), 8), :]
                for i in range(n)]
        for i in range(n):
            o_ref[0, toks[i]] = rows[i]

    n_full = (k_hi - k_lo) // COMB_UNROLL

    def body(i, _):
        add_rows(k_lo + i * COMB_UNROLL, COMB_UNROLL)
        return 0

    lax.fori_loop(0, n_full, body, 0)

    def tail(k, _):
        add_rows(k, 1)
        return 0

    lax.fori_loop(k_lo + n_full * COMB_UNROLL, k_hi, tail, 0)


def _combine(idx_flat, off_flat, y, n_tok, cap):
    bsz, n_exp, nhalf, rows, _ = y.shape
    nb = n_tok // TOK_BLK
    n_part = 2
    sblk = 8
    nslab = nhalf * sblk
    kern = functools.partial(_combine_kernel, cap=cap, nb=nb, n_exp=n_exp, tok_part=n_tok // n_part,
                             n_part=n_part)
    return pl.pallas_call(
        kern,
        grid_spec=pltpu.PrefetchScalarGridSpec(
            num_scalar_prefetch=2,
            grid=(bsz, n_part, nslab // sblk, n_exp),
            in_specs=[pl.BlockSpec((1, 1, 1, rows, LANES), lambda b, p, c, e, idx, off: (b, e, c, 0, 0))],
            out_specs=pl.BlockSpec((1, n_tok // n_part, sblk, LANES),
                                   lambda b, p, c, e, idx, off: (b, p, c, 0))),
        out_shape=jax.ShapeDtypeStruct((bsz, n_tok, nslab, LANES), F32),
        compiler_params=_cparams(("parallel", "parallel", "parallel", "arbitrary")),
        name="ec_combine",
    )(idx_flat, off_flat, y)


def _final_kernel(x1_ref, moe_ref, g_ref, mod_ref, o_ref):
    tm = x1_ref.shape[0]
    nslab = moe_ref.shape[0] // tm
    m = jnp.concatenate([moe_ref[pl.ds(c, tm, stride=nslab), :] for c in range(nslab)], axis=-1)
    r = m * lax.rsqrt(jnp.mean(m * m, axis=-1, keepdims=True) + EPS) * g_ref[...]
    o_ref[...] = x1_ref[...] + mod_ref[0] * r


def _final(x1, moe, g3, g2, rows_per_batch):
    n, d = x1.shape
    tm = 512
    tiles_per_batch = rows_per_batch // tm
    return pl.pallas_call(
        _final_kernel,
        grid=(n // tm,),
        in_specs=[pl.BlockSpec((tm, d), lambda i: (i, 0)),
                  pl.BlockSpec((tm * (d // LANES), LANES), lambda i: (i, 0)),
                  pl.BlockSpec((1, d), lambda i: (0, 0)),
                  pl.BlockSpec((1, 1, d), lambda i: (i // tiles_per_batch, 0, 0))],
        out_specs=pl.BlockSpec((tm, d), lambda i: (i, 0)),
        out_shape=jax.ShapeDtypeStruct((n, d), F32),
        compiler_params=_cparams(("parallel",)),
        name="final_residual",
    )(x1, moe, g3, g2)


def _layer(x, c, ctx, c_ctx, ada_w, ada_b, norm_g, w_in, s5_a_re, s5_a_im, s5_log_dt, s5_b_re,
           s5_b_im, s5_c_re, s5_c_im, s5_d, s5_glu_w, s5_glu_b, ml_conv_w, ml_conv_b, ml_gate_b,
           ml_norm_g, w_out, router_w, exp_w_gate, exp_w_up, exp_w_down):
    bsz, seq, d = x.shape
    lctx = ctx.shape[1]
    ds5 = s5_d.shape[0]
    dml = ml_norm_g.shape[0]
    rows = seq // GRID_W
    n = bsz * seq
    cap = EC_FACTOR * seq // N_EXPERTS
    assert cap >= TOK_BLK + 16 and seq % DISP_TOK == 0 and lctx % ML_T == 0 and seq % ML_T == 0

    cond = jnp.concatenate([c, c_ctx[None], jnp.zeros((8 - bsz - 1, d), F32)], axis=0)
    mod = _modulation(cond, ada_w, ada_b).reshape(8, N_MOD, d)
    sh1, sc1, g1, sh2, sc2, g2 = [mod[:, i] for i in range(N_MOD)]

    nmain = ds5 + 4 * dml
    w_s5 = w_in[:, :ds5].astype(BF16)
    w_ml = w_in[:, ds5:nmain].astype(BF16)
    w_gate = jnp.pad(w_in[:, nmain:], ((0, 0), (0, LANES - 4 * ML_HEADS))).astype(BF16)
    g0 = norm_g[0:1]
    sc, sh = sc1[:, None], sh1[:, None]
    uc = _in_projection_s5(ctx, g0, sc, sh, lambda b: bsz, w_s5)
    ltot = seq + lctx
    ml5, gates5, ux = _in_projection_ml(x, ctx, g0, sc, sh, bsz, w_ml, w_gate, w_s5)

    m, bp, cpt, avec = _s5_operators(s5_a_re, s5_a_im, s5_log_dt, s5_b_re, s5_b_im, s5_c_re, s5_c_im)
    ys = _s5_scan(ux, uc, m, bp, cpt, avec).reshape(n, ds5)

    ml = ml5.reshape(bsz, ltot, 4 * dml)
    gates = gates5.reshape(bsz, ltot, LANES)
    conv_w8 = jnp.pad(ml_conv_w, ((0, 8 - CONV_K), (0, 0)))
    qscale = jnp.concatenate([jnp.full((1, dml), (dml // ML_HEADS) ** -0.5, F32),
                              jnp.ones((1, dml), F32)], axis=1)
    qk = _conv_silu(ml, conv_w8, ml_conv_b[None], qscale, (0, seq), (seq, ltot))
    gbias = ml_gate_b.reshape(1, 4 * ML_HEADS)
    gates_t = gates[..., :4 * ML_HEADS].swapaxes(1, 2)
    hf, hb = _mlstm_scan(qk, ml, gates, gates_t, gbias, gbias.reshape(4 * ML_HEADS, 1), lctx // ML_T)
    mlo = _ml_post(hf.reshape(ml5.shape[:4] + (dml,)), hb.reshape(ml5.shape[:4] + (dml,)), ml5,
                   ml_norm_g[None], seq).reshape(n, dml)

    mod3 = jnp.stack([g1[:bsz], sc2[:bsz], sh2[:bsz]], axis=1)
    mod3 = jnp.concatenate([mod3, jnp.zeros((bsz, 5, d), F32)], axis=1)
    rw = jnp.pad(router_w, ((0, 0), (0, LANES - N_EXPERTS)))
    rw_hi = rw.astype(BF16)
    rw_lo = (rw - rw_hi.astype(F32)).astype(BF16)
    x1, hx2, aff = _mixer_out(ys, ux.reshape(n, ds5), mlo, x.reshape(n, d), seq, s5_d[None],
                              s5_glu_w.astype(BF16), s5_glu_b[None], w_out.astype(BF16), norm_g, mod3,
                              jnp.concatenate([rw_hi, rw_lo], axis=1))

    nb = seq // TOK_BLK
    aff16 = aff[:, :N_EXPERTS].reshape(bsz, seq, N_EXPERTS)
    aff_t = aff16.swapaxes(1, 2).reshape(bsz, N_EXPERTS, nb, LANES)
    off, idx, gates = _select(aff_t, cap)
    off_flat = off[..., 0].reshape(-1)
    idx_flat = idx[:, :, 0].reshape(-1)
    nslab = d // LANES
    src_start = ((idx[:, :, 0] + (jnp.arange(bsz, dtype=I32) * seq)[:, None, None]) * nslab).reshape(-1)
    xs = _dispatch(src_start, hx2, bsz, N_EXPERTS, cap, nslab)
    ye = _expert_ffn(xs, gates[:, :, 0, :, None], exp_w_gate, exp_w_up, exp_w_down)
    moe = _combine(idx_flat, off_flat, ye, seq, cap)
    out = _final(x1, moe.reshape(n * (d // LANES), LANES), norm_g[3:4], g2[:bsz, None], seq)
    return out.reshape(bsz, seq, d)


def kernel(x, c, ctx, c_ctx, ada_w, ada_b, norm_g, w_in, s5_a_re, s5_a_im, s5_log_dt, s5_b_re, s5_b_im, s5_c_re, s5_c_im, s5_d, s5_glu_w, s5_glu_b, ml_conv_w, ml_conv_b, ml_gate_b, ml_norm_g, w_out, router_w, exp_w_gate, exp_w_up, exp_w_down):
    depth = ada_w.shape[0]
    assert depth == 1, "context outputs are only produced when another layer follows"
    li = 0
    return _layer(x, c, ctx, c_ctx, ada_w[li], ada_b[li], norm_g[li], w_in[li], s5_a_re[li],
                  s5_a_im[li], s5_log_dt[li], s5_b_re[li], s5_b_im[li], s5_c_re[li], s5_c_im[li],
                  s5_d[li], s5_glu_w[li], s5_glu_b[li], ml_conv_w[li], ml_conv_b[li], ml_gate_b[li],
                  ml_norm_g[li], w_out[li], router_w[li], exp_w_gate[li], exp_w_up[li],
                  exp_w_down[li])
```
